```python
import math
import jax, jax.numpy as jnp
from jax import lax
import numpy as np

D_MODEL = 1024
BATCH = 8
SEQ = 4096
DEPTH = 2
DEC_BATCH = 128
DEC_SEQ = 4
PAST_LEN = 16384
PAGE_SIZE = 128

N_MIXERS = 2
M_HEADS = 4
M_DK = D_MODEL // (2 * M_HEADS)
M_DV = D_MODEL // M_HEADS
M_CHUNK = 128
FORGET_BIAS_INIT = 3.0
A_HEADS_Q = 16
A_HEADS_KV = 4
A_GROUP = A_HEADS_Q // A_HEADS_KV
A_HEAD_DIM = D_MODEL // A_HEADS_Q
WINDOW = 128
A_BLOCK = WINDOW
NUM_BUCKETS = 32
MAX_DISTANCE = 128
D_FF = 2816
FFN_RESIDUAL = 0.5
RMS_EPS = 1e-6

kernel_name = 'hybrid_mlstm_swa_macaron_step'


def rms_norm(x, g):
    xf = x.astype(jnp.float32)
    y = xf * lax.rsqrt(jnp.mean(jnp.square(xf), axis=-1, keepdims=True) + RMS_EPS)
    return (y * g.astype(jnp.float32)).astype(x.dtype)


def swiglu_ffn(x, w_gate_up, w_down):
    gate, up = jnp.split(x @ w_gate_up, 2, axis=-1)
    return (jax.nn.silu(gate) * up) @ w_down


def mlstm_project(x, w_in, b_gates):
    B, S, _ = x.shape
    qk, vd = M_HEADS * M_DK, M_HEADS * M_DV
    q, k, v, o, gates = jnp.split(x @ w_in, [qk, 2 * qk, 2 * qk + vd, 2 * qk + 2 * vd], axis=-1)
    gates = (gates + b_gates).astype(jnp.float32)
    i_pre = jnp.transpose(gates[..., :M_HEADS], (0, 2, 1))
    log_f = jnp.transpose(jax.nn.log_sigmoid(gates[..., M_HEADS:]), (0, 2, 1))
    def heads(a, d):
        return jnp.transpose(a.reshape(B, S, M_HEADS, d), (0, 2, 1, 3)).astype(jnp.float32)
    q = heads(q, M_DK)
    k = heads(k, M_DK) * (M_DK ** -0.5)
    v = heads(v, M_DV)
    return q, k, v, o, i_pre, log_f


def mlstm_chunk(state, inputs):
    C, n, m = state
    q, k, v, i_pre, log_f = inputs
    L = q.shape[2]
    b = jnp.cumsum(log_f, axis=-1)
    causal = jnp.tril(jnp.ones((L, L), dtype=bool))
    log_d = jnp.where(causal, b[..., :, None] - b[..., None, :] + i_pre[..., None, :], -jnp.inf)
    log_inter = b + m[..., None]
    m_t = jnp.maximum(log_inter, jnp.max(log_d, axis=-1))
    inter = jnp.exp(log_inter - m_t)
    s = jnp.einsum('bhtk,bhsk->bhts', q, k) * jnp.exp(log_d - m_t[..., None])
    num = inter[..., None] * jnp.einsum('bhtk,bhkv->bhtv', q, C) + jnp.einsum('bhts,bhsv->bhtv', s, v)
    den = inter * jnp.einsum('bhtk,bhk->bht', q, n) + jnp.sum(s, axis=-1)
    h = num / jnp.maximum(jnp.abs(den), jnp.exp(-m_t))[..., None]
    b_last = b[..., -1]
    log_w = b_last[..., None] - b + i_pre
    m_new = jnp.maximum(b_last + m, jnp.max(log_w, axis=-1))
    w = jnp.exp(log_w - m_new[..., None])
    decay = jnp.exp(b_last + m - m_new)
    C_new = decay[..., None, None] * C + jnp.einsum('bhs,bhsk,bhsv->bhkv', w, k, v)
    n_new = decay[..., None] * n + jnp.einsum('bhs,bhsk->bhk', w, k)
    return (C_new, n_new, m_new), h


def mlstm_output(h, o, norm_g, w_out, dtype):
    B, H, S, DV = h.shape
    h = jnp.transpose(h, (0, 2, 1, 3))
    h = h * lax.rsqrt(jnp.mean(jnp.square(h), axis=-1, keepdims=True) + RMS_EPS)
    h = h.reshape(B, S, H * DV) * norm_g.astype(jnp.float32)
    return (h * jax.nn.sigmoid(o.astype(jnp.float32))).astype(dtype) @ w_out


def mlstm_mixer(x, state, w_in, b_gates, norm_g, w_out):
    B, S, _ = x.shape
    q, k, v, o, i_pre, log_f = mlstm_project(x, w_in, b_gates)
    if state is None:
        nc = S // M_CHUNK
        def chunks(a):
            a = a.reshape(a.shape[:2] + (nc, M_CHUNK) + a.shape[3:])
            return jnp.moveaxis(a, 2, 0)
        init = (jnp.zeros((B, M_HEADS, M_DK, M_DV), jnp.float32),
                jnp.zeros((B, M_HEADS, M_DK), jnp.float32),
                jnp.zeros((B, M_HEADS), jnp.float32))
        new_state, h = lax.scan(mlstm_chunk, init, (chunks(q), chunks(k), chunks(v), chunks(i_pre), chunks(log_f)))
        h = jnp.moveaxis(h, 0, 2).reshape(B, M_HEADS, S, M_DV)
        out_dtype = x.dtype
    else:
        C, n, m = state
        out_dtype = C.dtype
        carried = (C.astype(jnp.float32), n.astype(jnp.float32), m.astype(jnp.float32))
        new_state, h = mlstm_chunk(carried, (q, k, v, i_pre, log_f))
    new_state = tuple(a.astype(out_dtype) for a in new_state)
    return mlstm_output(h, o, norm_g, w_out, x.dtype), new_state


def rel_position_bias(dist, rel_bias):
    max_exact = NUM_BUCKETS // 2
    d = jnp.maximum(dist, 0)
    log_ratio = jnp.log(jnp.maximum(d, 1).astype(jnp.float32) / max_exact) / math.log(MAX_DISTANCE / max_exact)
    large = jnp.minimum(max_exact + (log_ratio * (NUM_BUCKETS - max_exact)).astype(jnp.int32), NUM_BUCKETS - 1)
    bucket = jnp.where(d < max_exact, d, large)
    bias = rel_bias[bucket].astype(jnp.float32)
    return jnp.transpose(bias, (2, 0, 1)).reshape((A_HEADS_KV, A_GROUP) + dist.shape)


def sink_softmax(scores, sinks):
    s = sinks.astype(jnp.float32).reshape(A_HEADS_KV, A_GROUP)[:, :, None, None]
    mx = jnp.maximum(jnp.max(scores, axis=-1, keepdims=True), s)
    p = jnp.exp(scores - mx)
    return p / (jnp.sum(p, axis=-1, keepdims=True) + jnp.exp(s - mx))


def attn_project(x, w_in, q_norm, k_norm):
    B, S, _ = x.shape
    qd, kd = A_HEADS_Q * A_HEAD_DIM, A_HEADS_KV * A_HEAD_DIM
    q, k, v = jnp.split(x @ w_in, [qd, qd + kd], axis=-1)
    q = rms_norm(q.reshape(B, S, A_HEADS_KV, A_GROUP, A_HEAD_DIM), q_norm) * (A_HEAD_DIM ** -0.5)
    k = rms_norm(k.reshape(B, S, A_HEADS_KV, A_HEAD_DIM), k_norm)
    v = v.reshape(B, S, A_HEADS_KV, A_HEAD_DIM)
    return q, k, v


def swa_mixer(x, buffers, w_in, q_norm, k_norm, sinks, rel_bias, w_out):
    B, S, _ = x.shape
    q, k, v = attn_project(x, w_in, q_norm, k_norm)
    if buffers is None:
        nb = S // A_BLOCK
        qb = q.reshape(B, nb, A_BLOCK, A_HEADS_KV, A_GROUP, A_HEAD_DIM)
        def band(a):
            ab = a.reshape(B, nb, A_BLOCK, A_HEADS_KV, A_HEAD_DIM)
            prev = jnp.pad(ab, ((0, 0), (1, 0), (0, 0), (0, 0), (0, 0)))[:, :-1]
            return jnp.concatenate([prev, ab], axis=2)
        kb, vb = band(k), band(v)
        k_idx = jnp.arange(2 * A_BLOCK)
        dist = (jnp.arange(A_BLOCK) + A_BLOCK)[:, None] - k_idx[None, :]
        in_window = (dist >= 0) & (dist < WINDOW)
        has_key = (jnp.arange(nb) > 0)[:, None, None] | (k_idx >= A_BLOCK)[None, None, :]
        mask = in_window[None] & has_key
        scores = jnp.einsum('bnqhgd,bnkhd->bnhgqk', qb, kb).astype(jnp.float32) + rel_position_bias(dist, rel_bias)
        scores = jnp.where(mask[None, :, None, None], scores, -jnp.inf)
        probs = sink_softmax(scores, sinks).astype(vb.dtype)
        out = jnp.einsum('bnhgqk,bnkhd->bnqhgd', probs, vb).reshape(B, S, A_HEADS_Q * A_HEAD_DIM)
        new_k, new_v = k[:, S - WINDOW:], v[:, S - WINDOW:]
    else:
        k_buf, v_buf = buffers
        kk = jnp.concatenate([k_buf.astype(k.dtype), k], axis=1)
        vv = jnp.concatenate([v_buf.astype(v.dtype), v], axis=1)
        dist = (jnp.arange(S) + WINDOW)[:, None] - jnp.arange(WINDOW + S)[None, :]
        mask = (dist >= 0) & (dist < WINDOW)
        scores = jnp.einsum('bqhgd,bkhd->bhgqk', q, kk).astype(jnp.float32) + rel_position_bias(dist, rel_bias)
        scores = jnp.where(mask, scores, -jnp.inf)
        probs = sink_softmax(scores, sinks).astype(vv.dtype)
        out = jnp.einsum('bhgqk,bkhd->bqhgd', probs, vv).reshape(B, S, A_HEADS_Q * A_HEAD_DIM)
        new_k, new_v = kk[:, S:].astype(k_buf.dtype), vv[:, S:].astype(v_buf.dtype)
    return out @ w_out, (new_k, new_v)


def trunk(x, mlstm_state, swa_buffers, p):
    new_mlstm, new_swa = None, None
    for layer in range(DEPTH):
        x = x + FFN_RESIDUAL * swiglu_ffn(rms_norm(x, p['ffn1_norm'][layer]), p['ffn1_w_gate_up'][layer], p['ffn1_w_down'][layer])
        h = rms_norm(x, p['mix_norm'][layer])
        if layer % N_MIXERS == 0:
            y, new_mlstm = mlstm_mixer(h, mlstm_state, p['mlstm_w_in'], p['mlstm_b_gates'], p['mlstm_out_norm'], p['mlstm_w_out'])
        else:
            y, new_swa = swa_mixer(h, swa_buffers, p['attn_w_in'], p['attn_q_norm'], p['attn_k_norm'], p['attn_sinks'], p['rel_bias'], p['attn_w_out'])
        x = x + y
        x = x + FFN_RESIDUAL * swiglu_ffn(rms_norm(x, p['ffn2_norm'][layer]), p['ffn2_w_gate_up'][layer], p['ffn2_w_down'][layer])
    return x, new_mlstm, new_swa


def setup_inputs(seed: int = 0) -> dict:
    key = jax.random.key(seed)
    ks = jax.random.split(key, 26)
    f32 = jnp.float32
    def nrm(k, shape, scale):
        return jax.random.normal(k, shape, f32) * scale
    def gain(k, shape):
        return 1.0 + 0.05 * jax.random.normal(k, shape, f32)
    qk, vd = M_HEADS * M_DK, M_HEADS * M_DV
    m_in = 2 * qk + 2 * vd + 2 * M_HEADS
    a_in = (A_HEADS_Q + 2 * A_HEADS_KV) * A_HEAD_DIM
    b_gates = jnp.concatenate([0.1 * jax.random.normal(ks[0], (M_HEADS,), f32),
                               FORGET_BIAS_INIT + 3.0 * jax.random.uniform(ks[1], (M_HEADS,), f32)])
    return {
        'x_prompt': nrm(ks[2], (BATCH, SEQ, D_MODEL), 1.0),
        'x_sample': nrm(ks[3], (DEC_BATCH, DEC_SEQ, D_MODEL), 1.0),
        'state_mlstm_C': nrm(ks[4], (DEC_BATCH, M_HEADS, M_DK, M_DV), 0.5),
        'state_mlstm_n': nrm(ks[5], (DEC_BATCH, M_HEADS, M_DK), 1.0),
        'state_mlstm_m': nrm(ks[6], (DEC_BATCH, M_HEADS), 1.0),
        'cache_swa_k': nrm(ks[7], (DEC_BATCH, WINDOW, A_HEADS_KV, A_HEAD_DIM), 1.0),
        'cache_swa_v': nrm(ks[8], (DEC_BATCH, WINDOW, A_HEADS_KV, A_HEAD_DIM), 1.0),
        'ffn1_norm': gain(ks[9], (DEPTH, D_MODEL)),
        'ffn1_w_gate_up': nrm(ks[10], (DEPTH, D_MODEL, 2 * D_FF), D_MODEL ** -0.5),
        'ffn1_w_down': nrm(ks[11], (DEPTH, D_FF, D_MODEL), D_FF ** -0.5),
        'mix_norm': gain(ks[12], (DEPTH, D_MODEL)),
        'ffn2_norm': gain(ks[13], (DEPTH, D_MODEL)),
        'ffn2_w_gate_up': nrm(ks[14], (DEPTH, D_MODEL, 2 * D_FF), D_MODEL ** -0.5),
        'ffn2_w_down': nrm(ks[15], (DEPTH, D_FF, D_MODEL), D_FF ** -0.5),
        'mlstm_w_in': nrm(ks[16], (D_MODEL, m_in), D_MODEL ** -0.5),
        'mlstm_b_gates': b_gates,
        'mlstm_out_norm': gain(ks[17], (vd,)),
        'mlstm_w_out': nrm(ks[18], (vd, D_MODEL), vd ** -0.5),
        'attn_w_in': nrm(ks[19], (D_MODEL, a_in), D_MODEL ** -0.5),
        'attn_q_norm': gain(ks[20], (A_HEAD_DIM,)),
        'attn_k_norm': gain(ks[21], (A_HEAD_DIM,)),
        'attn_sinks': nrm(ks[22], (A_HEADS_Q,), 0.5),
        'rel_bias': nrm(ks[23], (NUM_BUCKETS, A_HEADS_Q), 0.5),
        'attn_w_out': nrm(ks[24], (A_HEADS_Q * A_HEAD_DIM, D_MODEL), (A_HEADS_Q * A_HEAD_DIM) ** -0.5),
    }


def reference(x_prompt, x_sample, state_mlstm_C, state_mlstm_n, state_mlstm_m, cache_swa_k, cache_swa_v,
              ffn1_norm, ffn1_w_gate_up, ffn1_w_down, mix_norm, ffn2_norm, ffn2_w_gate_up, ffn2_w_down,
              mlstm_w_in, mlstm_b_gates, mlstm_out_norm, mlstm_w_out,
              attn_w_in, attn_q_norm, attn_k_norm, attn_sinks, rel_bias, attn_w_out):
    p = {'ffn1_norm': ffn1_norm, 'ffn1_w_gate_up': ffn1_w_gate_up, 'ffn1_w_down': ffn1_w_down,
         'mix_norm': mix_norm, 'ffn2_norm': ffn2_norm, 'ffn2_w_gate_up': ffn2_w_gate_up, 'ffn2_w_down': ffn2_w_down,
         'mlstm_w_in': mlstm_w_in, 'mlstm_b_gates': mlstm_b_gates, 'mlstm_out_norm': mlstm_out_norm,
         'mlstm_w_out': mlstm_w_out, 'attn_w_in': attn_w_in, 'attn_q_norm': attn_q_norm,
         'attn_k_norm': attn_k_norm, 'attn_sinks': attn_sinks, 'rel_bias': rel_bias, 'attn_w_out': attn_w_out}
    y_prompt, (C_p, n_p, m_p), (k_p, v_p) = trunk(x_prompt, None, None, p)
    y_sample, (C_s, n_s, m_s), (k_s, v_s) = trunk(
        x_sample, (state_mlstm_C, state_mlstm_n, state_mlstm_m), (cache_swa_k, cache_swa_v), p)
    return (y_prompt, y_sample, C_p, n_p, m_p, k_p, v_p, C_s, n_s, m_s, k_s, v_s)
```

```python
import functools
import math

import jax
import jax.numpy as jnp
from jax import lax
from jax.experimental import pallas as pl
from jax.experimental.pallas import tpu as pltpu

F32 = jnp.float32
BF16 = jnp.bfloat16

M_HEADS = 4
A_HEADS_Q = 16
A_HEADS_KV = 4
A_GROUP = A_HEADS_Q // A_HEADS_KV
WINDOW = 128
NUM_BUCKETS = 32
MAX_DISTANCE = 128
FFN_RESIDUAL = 0.5
RMS_EPS = 1e-6
LANES = 128
VMEM_LIMIT = 56 * 1024 * 1024


def _cparams(n_axes):
    return pltpu.CompilerParams(
        dimension_semantics=("arbitrary",) * n_axes, vmem_limit_bytes=VMEM_LIMIT)


def _resident(shape):
    nd = len(shape)
    return pl.BlockSpec(shape, lambda *_: (0,) * nd, pipeline_mode=pl.Buffered(1))


def _rms(x, g):
    return x * lax.rsqrt(jnp.mean(x * x, axis=-1, keepdims=True) + RMS_EPS) * g


def _dot(a, b):
    return jnp.dot(a, b, preferred_element_type=F32)


def _dot_nt(a, b):
    return lax.dot_general(a, b, (((1,), (1,)), ((), ())), preferred_element_type=F32)


def _dot_tn(a, b):
    return lax.dot_general(a, b, (((0,), (0,)), ((), ())), preferred_element_type=F32)


def _row_tile(t, pref):
    tm = min(t, pref)
    assert t % tm == 0
    return tm


def _ffn_body(x_ref, g_ref, wgu_ref, wd_ref, o_ref, act_ref, *, d_ff, tf):
    x = x_ref[...]
    xn = _rms(x, g_ref[...]).astype(BF16)
    for c in range(d_ff // tf):
        lo = c * tf
        gate = _dot(xn, wgu_ref[:, lo:lo + tf])
        up = _dot(xn, wgu_ref[:, d_ff + lo:d_ff + lo + tf])
        act_ref[:, lo:lo + tf] = (gate * jax.nn.sigmoid(gate) * up).astype(BF16)
    o_ref[...] = x + FFN_RESIDUAL * _dot(act_ref[...], wd_ref[...])


def _ffn(x, g, wgu, wd):
    t, d = x.shape
    d_ff = wd.shape[0]
    tm = _row_tile(t, 512)
    tf = 256
    assert d_ff % tf == 0
    return pl.pallas_call(
        functools.partial(_ffn_body, d_ff=d_ff, tf=tf),
        grid=(t // tm,),
        in_specs=[pl.BlockSpec((tm, d), lambda i: (i, 0)),
                  _resident((1, d)), _resident((d, 2 * d_ff)), _resident((d_ff, d))],
        out_specs=pl.BlockSpec((tm, d), lambda i: (i, 0)),
        out_shape=jax.ShapeDtypeStruct((t, d), F32),
        scratch_shapes=[pltpu.VMEM((tm, d_ff), BF16)],
        compiler_params=_cparams(1),
        name="ffn",
    )(x, g.reshape(1, d), wgu, wd)


def _matmul_res_body(a_ref, w_ref, x_ref, o_ref):
    o_ref[...] = x_ref[...] + _dot(a_ref[...], w_ref[...])


def _matmul_res(a, w, x):
    t, k = a.shape
    d = w.shape[1]
    tm = _row_tile(t, 1024)
    return pl.pallas_call(
        _matmul_res_body,
        grid=(t // tm,),
        in_specs=[pl.BlockSpec((tm, k), lambda i: (i, 0)), _resident((k, d)),
                  pl.BlockSpec((tm, d), lambda i: (i, 0))],
        out_specs=pl.BlockSpec((tm, d), lambda i: (i, 0)),
        out_shape=jax.ShapeDtypeStruct((t, d), F32),
        compiler_params=_cparams(1),
        name="matmul_res",
    )(a, w, x)


def _mlstm_proj_body(x_ref, g_ref, w_ref, wg_ref, bg_ref,
                     q_ref, k_ref, v_ref, o_ref, gates_ref, *, qk, vd, dk):
    xn = _rms(x_ref[...], g_ref[...]).astype(BF16)
    q_ref[...] = _dot(xn, w_ref[:, 0:qk]).astype(BF16)
    k_ref[...] = (_dot(xn, w_ref[:, qk:2 * qk]) * (dk ** -0.5)).astype(BF16)
    v_ref[...] = _dot(xn, w_ref[:, 2 * qk:2 * qk + vd]).astype(BF16)
    o_ref[...] = _dot(xn, w_ref[:, 2 * qk + vd:2 * qk + 2 * vd])
    gates_ref[...] = _dot(xn, wg_ref[...]) + bg_ref[...]


def _mlstm_proj(x, g, w_main, w_gates, b_gates, qk, vd):
    t, d = x.shape
    tm = _row_tile(t, 512)
    row = lambda n: pl.BlockSpec((tm, n), lambda i: (i, 0))
    return pl.pallas_call(
        functools.partial(_mlstm_proj_body, qk=qk, vd=vd, dk=qk // M_HEADS),
        grid=(t // tm,),
        in_specs=[row(d), _resident((1, d)), _resident(w_main.shape),
                  _resident(w_gates.shape), _resident((1, LANES))],
        out_specs=[row(qk), row(qk), row(vd), row(vd), row(LANES)],
        out_shape=[jax.ShapeDtypeStruct((t, qk), BF16), jax.ShapeDtypeStruct((t, qk), BF16),
                   jax.ShapeDtypeStruct((t, vd), BF16), jax.ShapeDtypeStruct((t, vd), F32),
                   jax.ShapeDtypeStruct((t, LANES), F32)],
        compiler_params=_cparams(1),
        name="mlstm_proj",
    )(x, g.reshape(1, d), w_main, w_gates, b_gates)


def _split3(x):
    hi = x.astype(BF16)
    r1 = x - hi.astype(F32)
    mid = r1.astype(BF16)
    lo = (r1 - mid.astype(F32)).astype(BF16)
    return hi, mid, lo


def _mlstm_cell_body(q_ref, k_ref, v_ref, o_ref, gt_ref, ng_ref, c0_ref, n0_ref, m0_ref,
                     hg_ref, c_ref, n_ref, m_ref, *, seg, dk, dv):
    h_n = M_HEADS

    @pl.when(pl.program_id(1) == 0)
    def _():
        c_ref[...] = c0_ref[...]
        n_ref[...] = n0_ref[...]
        m_ref[...] = m0_ref[...]

    gates = gt_ref[0]
    rows = gates.shape[0]
    lane = lax.broadcasted_iota(jnp.int32, (rows, LANES), 1)
    log_f = jnp.where((lane >= h_n) & (lane < 2 * h_n), jax.nn.log_sigmoid(gates), 0.0)
    r_i = lax.broadcasted_iota(jnp.int32, (rows, rows), 0)
    c_i = lax.broadcasted_iota(jnp.int32, (rows, rows), 1)
    causal = r_i >= c_i
    tri = jnp.where(causal, 1.0, 0.0).astype(BF16)
    hi, mid, lo = _split3(log_f)
    b_all = _dot(tri, hi) + _dot(tri, mid) + _dot(tri, lo)
    ib_t = jnp.where(lane < h_n, gates, b_all).T

    for h in range(h_n):
        q = q_ref[0, :, h * dk:(h + 1) * dk]
        k = k_ref[0, :, h * dk:(h + 1) * dk]
        v = v_ref[0, :, h * dv:(h + 1) * dv]
        c_prev = c_ref[0, h]
        n_prev = n_ref[0, h:h + 1, :]
        m_prev = m_ref[0, :, h:h + 1]
        b_c = b_all[:, h_n + h:h_n + h + 1]
        i_c = gates[:, h:h + 1]
        i_minus_b = ib_t[h:h + 1, :] - ib_t[h_n + h:h_n + h + 1, :]

        log_d = jnp.where(causal, b_c + i_minus_b, -jnp.inf)
        log_inter = b_c + m_prev
        m_t = jnp.maximum(log_inter, jnp.max(log_d, axis=-1, keepdims=True))
        inter = jnp.exp(log_inter - m_t)
        s = _dot_nt(q, k) * jnp.exp(log_d - m_t)
        qf = q.astype(F32)
        num = inter * _dot(q, c_prev.astype(BF16)) + _dot(s.astype(BF16), v)
        den = inter * jnp.sum(qf * n_prev, axis=-1, keepdims=True) + jnp.sum(s, axis=-1, keepdims=True)
        hid = num / jnp.maximum(jnp.abs(den), jnp.exp(-m_t))

        b_last = b_c[rows - 1:rows, :]
        log_w = b_last - b_c + i_c
        m_new = jnp.maximum(b_last + m_prev, jnp.max(log_w, axis=0, keepdims=True))
        w = jnp.exp(log_w - m_new)
        decay = jnp.exp(b_last + m_prev - m_new)
        kw = w * k.astype(F32)
        c_ref[0, h] = decay * c_prev + _dot_tn(kw.astype(BF16), v)
        n_ref[0, h:h + 1, :] = decay * n_prev + jnp.sum(kw, axis=0, keepdims=True)
        m_ref[0, :, h:h + 1] = m_new

        hid = hid * lax.rsqrt(jnp.mean(hid * hid, axis=-1, keepdims=True) + RMS_EPS)
        gate = jax.nn.sigmoid(o_ref[0, :, h * dv:(h + 1) * dv])
        hg_ref[0, :, h * dv:(h + 1) * dv] = (hid * ng_ref[:, h * dv:(h + 1) * dv] * gate).astype(BF16)


def _mlstm_cell(q, k, v, o, gates, norm_g, c0, n0, m0, chunk):
    b, s, qk = q.shape
    vd = v.shape[-1]
    dk, dv = qk // M_HEADS, vd // M_HEADS
    nc = s // chunk
    tok = lambda n: pl.BlockSpec((1, chunk, n), lambda i, j: (i, j, 0))
    st_c = pl.BlockSpec((1, M_HEADS, dk, dv), lambda i, j: (i, 0, 0, 0))
    st_n = pl.BlockSpec((1, M_HEADS, dk), lambda i, j: (i, 0, 0))
    st_m = pl.BlockSpec((1, 1, M_HEADS), lambda i, j: (i, 0, 0))
    return pl.pallas_call(
        functools.partial(_mlstm_cell_body, seg=chunk, dk=dk, dv=dv),
        grid=(b, nc),
        in_specs=[tok(qk), tok(qk), tok(vd), tok(vd), tok(LANES),
                  pl.BlockSpec((1, vd), lambda i, j: (0, 0)), st_c, st_n, st_m],
        out_specs=[tok(vd), st_c, st_n, st_m],
        out_shape=[jax.ShapeDtypeStruct((b, s, vd), BF16),
                   jax.ShapeDtypeStruct((b, M_HEADS, dk, dv), F32),
                   jax.ShapeDtypeStruct((b, M_HEADS, dk), F32),
                   jax.ShapeDtypeStruct((b, 1, M_HEADS), F32)],
        compiler_params=_cparams(2),
        name="mlstm_cell",
    )(q, k, v, o, gates, norm_g.reshape(1, vd), c0, n0, m0)


def _mlstm_layer(x3, state, p):
    b, s, d = x3.shape
    qk, vd = p["mlstm_qk"], p["mlstm_vd"]
    x = x3.reshape(b * s, d)
    q, k, v, o, gates = _mlstm_proj(x, p["mix_norm0"], p["mlstm_w_main"], p["mlstm_w_gates"],
                                    p["mlstm_b_gates"], qk, vd)
    sh = lambda a: a.reshape(b, s, a.shape[-1])
    if state is None:
        dk, dv = qk // M_HEADS, vd // M_HEADS
        c0 = jnp.zeros((b, M_HEADS, dk, dv), F32)
        n0 = jnp.zeros((b, M_HEADS, dk), F32)
        m0 = jnp.zeros((b, 1, M_HEADS), F32)
        chunk = min(s, 128)
    else:
        c0, n0, m0 = state
        m0 = m0.reshape(b, 1, M_HEADS)
        chunk = s
    hg, c1, n1, m1 = _mlstm_cell(sh(q), sh(k), sh(v), sh(o), sh(gates), p["mlstm_out_norm"],
                                 c0, n0, m0, chunk)
    y = _matmul_res(hg.reshape(b * s, vd), p["mlstm_w_out"], x)
    return y.reshape(b, s, d), (c1, n1, m1.reshape(b, M_HEADS))


def _attn_proj_body(x_ref, g_ref, w_ref, qg_ref, kg_ref, q_ref, k_ref, v_ref, *, dh):
    xn = _rms(x_ref[...], g_ref[...]).astype(BF16)
    qd = A_HEADS_Q * dh
    kd = A_HEADS_KV * dh
    q = _dot(xn, w_ref[:, 0:qd])
    k = _dot(xn, w_ref[:, qd:qd + kd])
    v_ref[...] = _dot(xn, w_ref[:, qd + kd:qd + 2 * kd])
    qg = qg_ref[...] * (dh ** -0.5)
    for h in range(A_HEADS_Q):
        qh = q[:, h * dh:(h + 1) * dh]
        q_ref[:, h * dh:(h + 1) * dh] = (_rms(qh, 1.0) * qg).astype(BF16)
    for h in range(A_HEADS_KV):
        kh = k[:, h * dh:(h + 1) * dh]
        k_ref[:, h * dh:(h + 1) * dh] = _rms(kh, kg_ref[...])


def _attn_proj(x, g, w, q_norm, k_norm, dh):
    t, d = x.shape
    qd, kd = A_HEADS_Q * dh, A_HEADS_KV * dh
    tm = _row_tile(t, 512)
    row = lambda n: pl.BlockSpec((tm, n), lambda i: (i, 0))
    return pl.pallas_call(
        functools.partial(_attn_proj_body, dh=dh),
        grid=(t // tm,),
        in_specs=[row(d), _resident((1, d)), _resident(w.shape),
                  _resident((1, dh)), _resident((1, dh))],
        out_specs=[row(qd), row(kd), row(kd)],
        out_shape=[jax.ShapeDtypeStruct((t, qd), BF16), jax.ShapeDtypeStruct((t, kd), F32),
                   jax.ShapeDtypeStruct((t, kd), F32)],
        compiler_params=_cparams(1),
        name="attn_proj",
    )(x, g.reshape(1, d), w, q_norm.reshape(1, dh), k_norm.reshape(1, dh))


def _bias_body(rel_ref, o_ref, *, tq):
    h = pl.program_id(0)
    nk = WINDOW + tq
    qi = lax.broadcasted_iota(jnp.int32, (tq, nk), 0)
    ki = lax.broadcasted_iota(jnp.int32, (tq, nk), 1)
    dist = qi + WINDOW - ki
    max_exact = NUM_BUCKETS // 2
    d = jnp.maximum(dist, 0)
    log_ratio = (jnp.log(jnp.maximum(d, 1).astype(F32) / max_exact)
                 / math.log(MAX_DISTANCE / max_exact))
    large = jnp.minimum(max_exact + (log_ratio * (NUM_BUCKETS - max_exact)).astype(jnp.int32),
                        NUM_BUCKETS - 1)
    bucket = jnp.where(d < max_exact, d, large)
    bias = jnp.zeros((tq, nk), F32)
    for bkt in range(NUM_BUCKETS):
        bias = jnp.where(bucket == bkt, rel_ref[bkt, h], bias)
    o_ref[0] = jnp.where((dist >= 0) & (dist < WINDOW), bias, -jnp.inf)


def _bias_table(rel_bias, tq):
    nk = WINDOW + tq
    return pl.pallas_call(
        functools.partial(_bias_body, tq=tq),
        grid=(A_HEADS_Q,),
        in_specs=[pl.BlockSpec(memory_space=pltpu.SMEM)],
        out_specs=pl.BlockSpec((1, tq, nk), lambda h: (h, 0, 0)),
        out_shape=jax.ShapeDtypeStruct((A_HEADS_Q, tq, nk), F32),
        compiler_params=_cparams(1),
        name="rel_bias_table",
    )(rel_bias)


def _attn_core(q_ref, kp, ko, vp, vo, bias_ref, sink_ref, o_ref, *, dh, has_prev):
    tq = ko.shape[0]
    for g in range(A_HEADS_KV):
        sl = slice(g * dh, (g + 1) * dh)
        qs = jnp.concatenate(
            [q_ref[0, :, (g * A_GROUP + j) * dh:(g * A_GROUP + j + 1) * dh] for j in range(A_GROUP)],
            axis=0)
        bias = jnp.concatenate([bias_ref[g * A_GROUP + j] for j in range(A_GROUP)], axis=0)
        sink = jnp.concatenate(
            [jnp.full((tq, 1), sink_ref[g * A_GROUP + j], F32) for j in range(A_GROUP)], axis=0)
        sp = _dot_nt(qs, kp[:, sl].astype(BF16)) + bias[:, :WINDOW]
        if has_prev is not None:
            sp = jnp.where(has_prev, sp, -jnp.inf)
        so = _dot_nt(qs, ko[:, sl].astype(BF16)) + bias[:, WINDOW:]
        mx = jnp.maximum(jnp.maximum(jnp.max(sp, axis=-1, keepdims=True),
                                     jnp.max(so, axis=-1, keepdims=True)), sink)
        pp = jnp.exp(sp - mx)
        po = jnp.exp(so - mx)
        den = (jnp.sum(pp, axis=-1, keepdims=True) + jnp.sum(po, axis=-1, keepdims=True)
               + jnp.exp(sink - mx))
        out = (_dot((pp / den).astype(BF16), vp[:, sl].astype(BF16))
               + _dot((po / den).astype(BF16), vo[:, sl].astype(BF16)))
        for j in range(A_GROUP):
            hq = g * A_GROUP + j
            o_ref[0, :, hq * dh:(hq + 1) * dh] = out[j * tq:(j + 1) * tq].astype(BF16)


def _attn_prompt_body(sink_ref, q_ref, kp_ref, ko_ref, vp_ref, vo_ref, bias_ref, o_ref, *, dh):
    has_prev = pl.program_id(1) > 0
    _attn_core(q_ref, kp_ref[0], ko_ref[0], vp_ref[0], vo_ref[0], bias_ref, sink_ref, o_ref,
               dh=dh, has_prev=has_prev)


def _attn_prompt(q, k, v, bias, sinks, dh):
    b, s, qd = q.shape
    kd = k.shape[-1]
    blk = WINDOW
    own = lambda n: pl.BlockSpec((1, blk, n), lambda i, j: (i, j, 0))
    prev = lambda n: pl.BlockSpec((1, blk, n), lambda i, j: (i, jnp.maximum(j - 1, 0), 0))
    return pl.pallas_call(
        functools.partial(_attn_prompt_body, dh=dh),
        grid=(b, s // blk),
        in_specs=[pl.BlockSpec(memory_space=pltpu.SMEM),
                  own(qd), prev(kd), own(kd), prev(kd), own(kd),
                  pl.BlockSpec(bias.shape, lambda i, j: (0, 0, 0))],
        out_specs=own(qd),
        out_shape=jax.ShapeDtypeStruct((b, s, qd), BF16),
        compiler_params=_cparams(2),
        name="attn_prompt",
    )(sinks, q, k, k, v, v, bias)


def _attn_sample_body(sink_ref, q_ref, kc_ref, ko_ref, vc_ref, vo_ref, bias_ref,
                      o_ref, kn_ref, vn_ref, *, dh):
    tq = ko_ref.shape[1]
    _attn_core(q_ref, kc_ref[0], ko_ref[0], vc_ref[0], vo_ref[0], bias_ref, sink_ref, o_ref,
               dh=dh, has_prev=None)
    kn_ref[0, 0:WINDOW - tq, :] = kc_ref[0, tq:WINDOW, :]
    kn_ref[0, WINDOW - tq:WINDOW, :] = ko_ref[0]
    vn_ref[0, 0:WINDOW - tq, :] = vc_ref[0, tq:WINDOW, :]
    vn_ref[0, WINDOW - tq:WINDOW, :] = vo_ref[0]


def _attn_sample(q, k, v, k_cache, v_cache, bias, sinks, dh):
    b, s, qd = q.shape
    kd = k.shape[-1]
    new = lambda n: pl.BlockSpec((1, s, n), lambda i: (i, 0, 0))
    win = pl.BlockSpec((1, WINDOW, kd), lambda i: (i, 0, 0))
    return pl.pallas_call(
        functools.partial(_attn_sample_body, dh=dh),
        grid=(b,),
        in_specs=[pl.BlockSpec(memory_space=pltpu.SMEM),
                  new(qd), win, new(kd), win, new(kd),
                  pl.BlockSpec(bias.shape, lambda i: (0, 0, 0))],
        out_specs=[new(qd), win, win],
        out_shape=[jax.ShapeDtypeStruct((b, s, qd), BF16),
                   jax.ShapeDtypeStruct((b, WINDOW, kd), F32),
                   jax.ShapeDtypeStruct((b, WINDOW, kd), F32)],
        compiler_params=_cparams(1),
        name="attn_sample",
    )(sinks, q, k_cache, k, v_cache, v, bias)


def _swa_layer(x3, buffers, p):
    b, s, d = x3.shape
    dh = p["attn_dh"]
    kd = A_HEADS_KV * dh
    x = x3.reshape(b * s, d)
    q, k, v = _attn_proj(x, p["mix_norm1"], p["attn_w_in"], p["attn_q_norm"], p["attn_k_norm"], dh)
    sh = lambda a: a.reshape(b, s, a.shape[-1])
    q, k, v = sh(q), sh(k), sh(v)
    if buffers is None:
        bias = _bias_table(p["rel_bias"], WINDOW)
        att = _attn_prompt(q, k, v, bias, p["attn_sinks"], dh)
        new_k, new_v = k[:, s - WINDOW:], v[:, s - WINDOW:]
    else:
        k_buf, v_buf = buffers
        bias = _bias_table(p["rel_bias"], s)
        att, new_k, new_v = _attn_sample(q, k, v, k_buf.reshape(b, WINDOW, kd),
                                         v_buf.reshape(b, WINDOW, kd), bias, p["attn_sinks"], dh)
    y = _matmul_res(att.reshape(b * s, A_HEADS_Q * dh), p["attn_w_out"], x)
    kv_shape = (b, WINDOW, A_HEADS_KV, dh)
    return y.reshape(b, s, d), (new_k.reshape(kv_shape), new_v.reshape(kv_shape))


def _ffn3(x3, g, wgu, wd):
    b, s, d = x3.shape
    return _ffn(x3.reshape(b * s, d), g, wgu, wd).reshape(b, s, d)


def _trunk(x3, mlstm_state, swa_buffers, p):
    x3 = _ffn3(x3, p["ffn1_norm"][0], p["ffn1_wgu"][0], p["ffn1_wd"][0])
    x3, new_mlstm = _mlstm_layer(x3, mlstm_state, p)
    x3 = _ffn3(x3, p["ffn2_norm"][0], p["ffn2_wgu"][0], p["ffn2_wd"][0])
    x3 = _ffn3(x3, p["ffn1_norm"][1], p["ffn1_wgu"][1], p["ffn1_wd"][1])
    x3, new_swa = _swa_layer(x3, swa_buffers, p)
    x3 = _ffn3(x3, p["ffn2_norm"][1], p["ffn2_wgu"][1], p["ffn2_wd"][1])
    return x3, new_mlstm, new_swa


def kernel(x_prompt, x_sample, state_mlstm_C, state_mlstm_n, state_mlstm_m, cache_swa_k, cache_swa_v,
           ffn1_norm, ffn1_w_gate_up, ffn1_w_down, mix_norm, ffn2_norm, ffn2_w_gate_up, ffn2_w_down,
           mlstm_w_in, mlstm_b_gates, mlstm_out_norm, mlstm_w_out,
           attn_w_in, attn_q_norm, attn_k_norm, attn_sinks, rel_bias, attn_w_out):
    d = x_prompt.shape[-1]
    vd = mlstm_w_out.shape[0]
    n_gates = 2 * M_HEADS
    qk = (mlstm_w_in.shape[1] - 2 * vd - n_gates) // 2
    w_gates = jnp.zeros((d, LANES), BF16).at[:, :n_gates].set(mlstm_w_in[:, 2 * qk + 2 * vd:].astype(BF16))
    b_gates = jnp.zeros((1, LANES), F32).at[0, :n_gates].set(mlstm_b_gates)
    p = {
        "ffn1_norm": ffn1_norm, "ffn2_norm": ffn2_norm,
        "ffn1_wgu": ffn1_w_gate_up.astype(BF16), "ffn1_wd": ffn1_w_down.astype(BF16),
        "ffn2_wgu": ffn2_w_gate_up.astype(BF16), "ffn2_wd": ffn2_w_down.astype(BF16),
        "mix_norm0": mix_norm[0], "mix_norm1": mix_norm[1],
        "mlstm_qk": qk, "mlstm_vd": vd,
        "mlstm_w_main": mlstm_w_in[:, :2 * qk + 2 * vd].astype(BF16),
        "mlstm_w_gates": w_gates, "mlstm_b_gates": b_gates,
        "mlstm_out_norm": mlstm_out_norm, "mlstm_w_out": mlstm_w_out.astype(BF16),
        "attn_dh": attn_q_norm.shape[0],
        "attn_w_in": attn_w_in.astype(BF16), "attn_q_norm": attn_q_norm, "attn_k_norm": attn_k_norm,
        "attn_sinks": attn_sinks, "rel_bias": rel_bias, "attn_w_out": attn_w_out.astype(BF16),
    }
    y_p, (c_p, n_p, m_p), (k_p, v_p) = _trunk(x_prompt, None, None, p)
    y_s, (c_s, n_s, m_s), (k_s, v_s) = _trunk(
        x_sample, (state_mlstm_C, state_mlstm_n, state_mlstm_m), (cache_swa_k, cache_swa_v), p)
    return (y_p, y_s, c_p, n_p, m_p, k_p, v_p, c_s, n_s, m_s, k_s, v_s)
```

```python
import functools
import math

import jax
import jax.numpy as jnp
from jax import lax
from jax.experimental import pallas as pl
from jax.experimental.pallas import tpu as pltpu

F32 = jnp.float32
BF16 = jnp.bfloat16

M_HEADS = 4
A_HEADS_Q = 16
A_HEADS_KV = 4
A_GROUP = A_HEADS_Q // A_HEADS_KV
WINDOW = 128
NUM_BUCKETS = 32
MAX_DISTANCE = 128
FFN_RESIDUAL = 0.5
RMS_EPS = 1e-6
LANES = 128
VMEM_LIMIT = 56 * 1024 * 1024


def _cparams(n_axes):
    return pltpu.CompilerParams(
        dimension_semantics=("arbitrary",) * n_axes, vmem_limit_bytes=VMEM_LIMIT)


def _resident(shape):
    nd = len(shape)
    return pl.BlockSpec(shape, lambda *_: (0,) * nd, pipeline_mode=pl.Buffered(1))


def _rms(x, g):
    return x * lax.rsqrt(jnp.mean(x * x, axis=-1, keepdims=True) + RMS_EPS) * g


def _dot(a, b):
    return jnp.dot(a, b, preferred_element_type=F32)


def _dot_nt(a, b):
    return lax.dot_general(a, b, (((1,), (1,)), ((), ())), preferred_element_type=F32)


def _dot_tn(a, b):
    return lax.dot_general(a, b, (((0,), (0,)), ((), ())), preferred_element_type=F32)


def _row_tile(t, pref):
    tm = min(t, pref)
    assert t % tm == 0
    return tm


def _ffn_body(x_ref, g_ref, wgu_ref, wd_ref, o_ref, act_ref, *, d_ff, tf):
    x = x_ref[...]
    xn = _rms(x, g_ref[...]).astype(BF16)
    for c in range(d_ff // tf):
        lo = c * tf
        gate = _dot(xn, wgu_ref[:, lo:lo + tf])
        up = _dot(xn, wgu_ref[:, d_ff + lo:d_ff + lo + tf])
        act_ref[:, lo:lo + tf] = (gate * jax.nn.sigmoid(gate) * up).astype(BF16)
    o_ref[...] = x + FFN_RESIDUAL * _dot(act_ref[...], wd_ref[...])


def _ffn(x, g, wgu, wd):
    t, d = x.shape
    d_ff = wd.shape[0]
    tm = _row_tile(t, 512)
    tf = 256
    assert d_ff % tf == 0
    return pl.pallas_call(
        functools.partial(_ffn_body, d_ff=d_ff, tf=tf),
        grid=(t // tm,),
        in_specs=[pl.BlockSpec((tm, d), lambda i: (i, 0)),
                  _resident((1, d)), _resident((d, 2 * d_ff)), _resident((d_ff, d))],
        out_specs=pl.BlockSpec((tm, d), lambda i: (i, 0)),
        out_shape=jax.ShapeDtypeStruct((t, d), F32),
        scratch_shapes=[pltpu.VMEM((tm, d_ff), BF16)],
        compiler_params=_cparams(1),
        name="ffn",
    )(x, g.reshape(1, d), wgu, wd)


def _matmul_res_body(a_ref, w_ref, x_ref, o_ref):
    o_ref[...] = x_ref[...] + _dot(a_ref[...], w_ref[...])


def _matmul_res(a, w, x):
    t, k = a.shape
    d = w.shape[1]
    tm = _row_tile(t, 1024)
    return pl.pallas_call(
        _matmul_res_body,
        grid=(t // tm,),
        in_specs=[pl.BlockSpec((tm, k), lambda i: (i, 0)), _resident((k, d)),
                  pl.BlockSpec((tm, d), lambda i: (i, 0))],
        out_specs=pl.BlockSpec((tm, d), lambda i: (i, 0)),
        out_shape=jax.ShapeDtypeStruct((t, d), F32),
        compiler_params=_cparams(1),
        name="matmul_res",
    )(a, w, x)


def _mlstm_proj_body(x_ref, g_ref, w_ref, wg_ref, bg_ref,
                     q_ref, k_ref, v_ref, o_ref, gates_ref, *, qk, vd, dk):
    xn = _rms(x_ref[...], g_ref[...]).astype(BF16)
    q_ref[...] = _dot(xn, w_ref[:, 0:qk]).astype(BF16)
    k_ref[...] = (_dot(xn, w_ref[:, qk:2 * qk]) * (dk ** -0.5)).astype(BF16)
    v_ref[...] = _dot(xn, w_ref[:, 2 * qk:2 * qk + vd]).astype(BF16)
    o_ref[...] = _dot(xn, w_ref[:, 2 * qk + vd:2 * qk + 2 * vd])
    gates_ref[...] = _dot(xn, wg_ref[...]) + bg_ref[...]


def _mlstm_proj(x, g, w_main, w_gates, b_gates, qk, vd):
    t, d = x.shape
    tm = _row_tile(t, 512)
    row = lambda n: pl.BlockSpec((tm, n), lambda i: (i, 0))
    return pl.pallas_call(
        functools.partial(_mlstm_proj_body, qk=qk, vd=vd, dk=qk // M_HEADS),
        grid=(t // tm,),
        in_specs=[row(d), _resident((1, d)), _resident(w_main.shape),
                  _resident(w_gates.shape), _resident((1, LANES))],
        out_specs=[row(qk), row(qk), row(vd), row(vd), row(LANES)],
        out_shape=[jax.ShapeDtypeStruct((t, qk), BF16), jax.ShapeDtypeStruct((t, qk), BF16),
                   jax.ShapeDtypeStruct((t, vd), BF16), jax.ShapeDtypeStruct((t, vd), F32),
                   jax.ShapeDtypeStruct((t, LANES), F32)],
        compiler_params=_cparams(1),
        name="mlstm_proj",
    )(x, g.reshape(1, d), w_main, w_gates, b_gates)


def _split3(x):
    hi = x.astype(BF16)
    r1 = x - hi.astype(F32)
    mid = r1.astype(BF16)
    lo = (r1 - mid.astype(F32)).astype(BF16)
    return hi, mid, lo


def _mlstm_cell_body(q_ref, k_ref, v_ref, o_ref, gt_ref, ng_ref, c0_ref, n0_ref, m0_ref,
                     hg_ref, c_ref, n_ref, m_ref, *, dk, dv):
    h_n = M_HEADS

    @pl.when(pl.program_id(1) == 0)
    def _():
        c_ref[...] = c0_ref[...]
        n_ref[...] = n0_ref[...]
        m_ref[...] = m0_ref[...]

    gates = gt_ref[0]
    rows = gates.shape[0]
    lane = lax.broadcasted_iota(jnp.int32, (rows, LANES), 1)
    log_f = jnp.where((lane >= h_n) & (lane < 2 * h_n), jax.nn.log_sigmoid(gates), 0.0)
    r_i = lax.broadcasted_iota(jnp.int32, (rows, rows), 0)
    c_i = lax.broadcasted_iota(jnp.int32, (rows, rows), 1)
    causal = r_i >= c_i
    tri = jnp.where(causal, 1.0, 0.0).astype(BF16)
    hi, mid, lo = _split3(log_f)
    b_all = _dot(tri, hi) + _dot(tri, mid) + _dot(tri, lo)
    ib_t = jnp.where(lane < h_n, gates, b_all).T

    for h in range(h_n):
        q = q_ref[0, :, h * dk:(h + 1) * dk]
        k = k_ref[0, :, h * dk:(h + 1) * dk]
        v = v_ref[0, :, h * dv:(h + 1) * dv]
        c_prev = c_ref[0, h]
        n_prev = n_ref[0, h:h + 1, :]
        m_prev = m_ref[0, :, h:h + 1]
        b_c = b_all[:, h_n + h:h_n + h + 1]
        i_c = gates[:, h:h + 1]
        i_minus_b = ib_t[h:h + 1, :] - ib_t[h_n + h:h_n + h + 1, :]

        log_d = jnp.where(causal, b_c + i_minus_b, -jnp.inf)
        log_inter = b_c + m_prev
        m_t = jnp.maximum(log_inter, jnp.max(log_d, axis=-1, keepdims=True))
        inter = jnp.exp(log_inter - m_t)
        s = _dot_nt(q, k) * jnp.exp(log_d - m_t)
        qf = q.astype(F32)
        num = inter * _dot(q, c_prev.astype(BF16)) + _dot(s.astype(BF16), v)
        den = inter * jnp.sum(qf * n_prev, axis=-1, keepdims=True) + jnp.sum(s, axis=-1, keepdims=True)
        hid = num / jnp.maximum(jnp.abs(den), jnp.exp(-m_t))

        b_last = b_c[rows - 1:rows, :]
        log_w = b_last - b_c + i_c
        m_new = jnp.maximum(b_last + m_prev, jnp.max(log_w, axis=0, keepdims=True))
        w = jnp.exp(log_w - m_new)
        decay = jnp.exp(b_last + m_prev - m_new)
        kw = w * k.astype(F32)
        c_ref[0, h] = decay * c_prev + _dot_tn(kw.astype(BF16), v)
        n_ref[0, h:h + 1, :] = decay * n_prev + jnp.sum(kw, axis=0, keepdims=True)
        m_ref[0, :, h:h + 1] = m_new

        hid = hid * lax.rsqrt(jnp.mean(hid * hid, axis=-1, keepdims=True) + RMS_EPS)
        gate = jax.nn.sigmoid(o_ref[0, :, h * dv:(h + 1) * dv])
        hg_ref[0, :, h * dv:(h + 1) * dv] = (hid * ng_ref[:, h * dv:(h + 1) * dv] * gate).astype(BF16)


def _mlstm_cell(q, k, v, o, gates, norm_g, c0, n0, m0, chunk):
    b, s, qk = q.shape
    vd = v.shape[-1]
    dk, dv = qk // M_HEADS, vd // M_HEADS
    nc = s // chunk
    tok = lambda n: pl.BlockSpec((1, chunk, n), lambda i, j: (i, j, 0))
    st_c = pl.BlockSpec((1, M_HEADS, dk, dv), lambda i, j: (i, 0, 0, 0))
    st_n = pl.BlockSpec((1, M_HEADS, dk), lambda i, j: (i, 0, 0))
    st_m = pl.BlockSpec((1, 1, M_HEADS), lambda i, j: (i, 0, 0))
    return pl.pallas_call(
        functools.partial(_mlstm_cell_body, dk=dk, dv=dv),
        grid=(b, nc),
        in_specs=[tok(qk), tok(qk), tok(vd), tok(vd), tok(LANES),
                  pl.BlockSpec((1, vd), lambda i, j: (0, 0)), st_c, st_n, st_m],
        out_specs=[tok(vd), st_c, st_n, st_m],
        out_shape=[jax.ShapeDtypeStruct((b, s, vd), BF16),
                   jax.ShapeDtypeStruct((b, M_HEADS, dk, dv), F32),
                   jax.ShapeDtypeStruct((b, M_HEADS, dk), F32),
                   jax.ShapeDtypeStruct((b, 1, M_HEADS), F32)],
        compiler_params=_cparams(2),
        name="mlstm_cell",
    )(q, k, v, o, gates, norm_g.reshape(1, vd), c0, n0, m0)


def _mlstm_layer(x3, state, p):
    b, s, d = x3.shape
    qk, vd = p["mlstm_qk"], p["mlstm_vd"]
    x = x3.reshape(b * s, d)
    q, k, v, o, gates = _mlstm_proj(x, p["mix_norm0"], p["mlstm_w_main"], p["mlstm_w_gates"],
                                    p["mlstm_b_gates"], qk, vd)
    sh = lambda a: a.reshape(b, s, a.shape[-1])
    if state is None:
        dk, dv = qk // M_HEADS, vd // M_HEADS
        c0 = jnp.zeros((b, M_HEADS, dk, dv), F32)
        n0 = jnp.zeros((b, M_HEADS, dk), F32)
        m0 = jnp.zeros((b, 1, M_HEADS), F32)
        chunk = min(s, 128)
    else:
        c0, n0, m0 = state
        m0 = m0.reshape(b, 1, M_HEADS)
        chunk = s
    hg, c1, n1, m1 = _mlstm_cell(sh(q), sh(k), sh(v), sh(o), sh(gates), p["mlstm_out_norm"],
                                 c0, n0, m0, chunk)
    y = _matmul_res(hg.reshape(b * s, vd), p["mlstm_w_out"], x)
    return y.reshape(b, s, d), (c1, n1, m1.reshape(b, M_HEADS))


def _attn_proj_body(x_ref, g_ref, w_ref, qg_ref, kg_ref, q_ref, k_ref, v_ref, *, dh, nq, nk):
    xn = _rms(x_ref[...], g_ref[...]).astype(BF16)
    r_i = lax.broadcasted_iota(jnp.int32, (LANES, LANES), 0)
    c_i = lax.broadcasted_iota(jnp.int32, (LANES, LANES), 1)
    same_half = jnp.where((r_i < dh) == (c_i < dh), 1.0, 0.0).astype(BF16)

    def half_norm(y, gain):
        sq = y * y
        hi = sq.astype(BF16)
        lo = (sq - hi.astype(F32)).astype(BF16)
        ms = (_dot(hi, same_half) + _dot(lo, same_half)) * (1.0 / dh)
        return y * lax.rsqrt(ms + RMS_EPS) * gain

    q = _dot(xn, w_ref[:, 0:nq])
    for s in range(nq // LANES):
        sl = slice(s * LANES, (s + 1) * LANES)
        q_ref[:, sl] = half_norm(q[:, sl], qg_ref[...]).astype(BF16)
    k = _dot(xn, w_ref[:, nq:nq + nk])
    for s in range(nk // LANES):
        sl = slice(s * LANES, (s + 1) * LANES)
        k_ref[:, sl] = half_norm(k[:, sl], kg_ref[...])
    v_ref[...] = _dot(xn, w_ref[:, nq + nk:])


def _attn_proj(x, g, w, qg, kg, dh, nq, nk):
    t, d = x.shape
    nv = w.shape[1] - nq - nk
    tm = _row_tile(t, 512)
    row = lambda n: pl.BlockSpec((tm, n), lambda i: (i, 0))
    return pl.pallas_call(
        functools.partial(_attn_proj_body, dh=dh, nq=nq, nk=nk),
        grid=(t // tm,),
        in_specs=[row(d), _resident((1, d)), _resident(w.shape),
                  _resident((1, LANES)), _resident((1, LANES))],
        out_specs=[row(nq), row(nk), row(nv)],
        out_shape=[jax.ShapeDtypeStruct((t, nq), BF16), jax.ShapeDtypeStruct((t, nk), F32),
                   jax.ShapeDtypeStruct((t, nv), F32)],
        compiler_params=_cparams(1),
        name="attn_proj",
    )(x, g.reshape(1, d), w, qg, kg)


def _bias_body(rel_ref, o_ref, *, tq):
    variant = pl.program_id(0)
    h = pl.program_id(1)
    nk = 2 * WINDOW
    qi = lax.broadcasted_iota(jnp.int32, (tq, nk), 0)
    ki = lax.broadcasted_iota(jnp.int32, (tq, nk), 1)
    dist = qi + WINDOW - ki
    max_exact = NUM_BUCKETS // 2
    d = jnp.maximum(dist, 0)
    log_ratio = (jnp.log(jnp.maximum(d, 1).astype(F32) / max_exact)
                 / math.log(MAX_DISTANCE / max_exact))
    large = jnp.minimum(max_exact + (log_ratio * (NUM_BUCKETS - max_exact)).astype(jnp.int32),
                        NUM_BUCKETS - 1)
    bucket = jnp.where(d < max_exact, d, large)
    bias = jnp.zeros((tq, nk), F32)
    for bkt in range(NUM_BUCKETS):
        bias = jnp.where(bucket == bkt, rel_ref[bkt, h], bias)
    visible = (dist >= 0) & (dist < WINDOW) & ((ki >= WINDOW) | (variant > 0))
    o_ref[0, 0] = jnp.where(visible, bias, -jnp.inf)


def _bias_table(rel_bias, tq):
    return pl.pallas_call(
        functools.partial(_bias_body, tq=tq),
        grid=(2, A_HEADS_Q),
        in_specs=[pl.BlockSpec(memory_space=pltpu.SMEM)],
        out_specs=pl.BlockSpec((1, 1, tq, 2 * WINDOW), lambda v, h: (v, h, 0, 0)),
        out_shape=jax.ShapeDtypeStruct((2, A_HEADS_Q, tq, 2 * WINDOW), F32),
        compiler_params=_cparams(2),
        name="rel_bias_table",
    )(rel_bias)


def _attn_prompt_body(sink_ref, q_ref, kp_ref, ko_ref, vp_ref, vo_ref, bias_ref, o_ref, *, dh):
    blk = WINDOW
    lane = lax.broadcasted_iota(jnp.int32, (blk, LANES), 1)
    lo_half = lane < dh
    ones = jnp.ones((2 * blk, LANES), BF16)
    res, den = [], []
    for g in range(A_HEADS_KV):
        sl = slice(g * LANES, (g + 1) * LANES)
        kk = jnp.concatenate([kp_ref[0, :, sl], ko_ref[0, :, sl]], axis=0).astype(BF16)
        vv = jnp.concatenate([vp_ref[0, :, sl], vo_ref[0, :, sl]], axis=0).astype(BF16)
        parts, sinks = [], []
        for j in range(A_GROUP):
            h = g * A_GROUP + j
            slot = q_ref[0, :, (h // 2) * LANES:(h // 2 + 1) * LANES]
            parts.append(jnp.where(lo_half == (h % 2 == 0), slot, jnp.zeros_like(slot)))
            sinks.append(jnp.full((blk, 1), sink_ref[h], F32))
        qs = jnp.concatenate(parts, axis=0)
        sink = jnp.concatenate(sinks, axis=0)
        s = _dot_nt(qs, kk) + bias_ref[0, g]
        m = jnp.maximum(jnp.max(s, axis=-1, keepdims=True), sink)
        p = jnp.exp(s - m).astype(BF16)
        res.append(_dot(p, vv))
        den.append(_dot(p, ones) + jnp.exp(sink - m))
    for slot in range(A_HEADS_Q // 2):
        h0 = 2 * slot
        g, j = h0 // A_GROUP, h0 % A_GROUP
        r0, r1 = slice(j * blk, (j + 1) * blk), slice((j + 1) * blk, (j + 2) * blk)
        out = jnp.where(lo_half, res[g][r0], res[g][r1]) / jnp.where(lo_half, den[g][r0], den[g][r1])
        o_ref[0, :, slot * LANES:(slot + 1) * LANES] = out.astype(BF16)


def _attn_prompt(q, k, v, bias, sinks, dh):
    b, s, qd = q.shape
    kd = k.shape[-1]
    blk = WINDOW
    own = lambda n: pl.BlockSpec((1, blk, n), lambda i, j: (i, j, 0))
    prev = lambda n: pl.BlockSpec((1, blk, n), lambda i, j: (i, jnp.maximum(j - 1, 0), 0))
    return pl.pallas_call(
        functools.partial(_attn_prompt_body, dh=dh),
        grid=(b, s // blk),
        in_specs=[pl.BlockSpec(memory_space=pltpu.SMEM),
                  own(qd), prev(kd), own(kd), prev(kd), own(kd),
                  pl.BlockSpec((1,) + bias.shape[1:], lambda i, j: (jnp.minimum(j, 1), 0, 0, 0))],
        out_specs=own(qd),
        out_shape=jax.ShapeDtypeStruct((b, s, qd), BF16),
        compiler_params=_cparams(2),
        name="attn_prompt",
    )(sinks, q, k, k, v, v, bias)


def _attn_sample_body(sink_ref, q_ref, kc_ref, kn_ref, vc_ref, vn_ref, bias_ref,
                      o_ref, ko_ref, vo_ref, *, dh, bt):
    tq = kn_ref.shape[1]
    heads_per_slab = 2 * A_GROUP
    n_slabs = A_HEADS_Q // heads_per_slab
    sinks = [jnp.concatenate([jnp.full((tq, 1), sink_ref[sl * heads_per_slab + hh], F32)
                              for hh in range(heads_per_slab)], axis=0) for sl in range(n_slabs)]

    def one(b, carry):
        for sl in range(n_slabs):
            lanes = slice(sl * LANES, (sl + 1) * LANES)
            kc = kc_ref[b, :, lanes].astype(BF16)
            vc = vc_ref[b, :, lanes].astype(BF16)
            kn = kn_ref[b, :, lanes].astype(BF16)
            vn = vn_ref[b, :, lanes].astype(BF16)
            h0 = sl * heads_per_slab
            qs = jnp.concatenate([q_ref[b, :, (h0 + hh) * LANES:(h0 + hh + 1) * LANES]
                                  for hh in range(heads_per_slab)], axis=0)
            bias = bias_ref[sl]
            sp = _dot_nt(qs, kc) + bias[:, :WINDOW]
            so = _dot_nt(qs, kn) + bias[:, WINDOW:WINDOW + tq]
            sink = sinks[sl]
            m = jnp.maximum(jnp.maximum(jnp.max(sp, axis=-1, keepdims=True),
                                        jnp.max(so, axis=-1, keepdims=True)), sink)
            pp = jnp.exp(sp - m)
            po = jnp.exp(so - m)
            den = (jnp.sum(pp, axis=-1, keepdims=True) + jnp.sum(po, axis=-1, keepdims=True)
                   + jnp.exp(sink - m))
            out = _dot((pp / den).astype(BF16), vc) + _dot((po / den).astype(BF16), vn)
            for hh in range(heads_per_slab):
                o_ref[b, :, (h0 + hh) * LANES:(h0 + hh + 1) * LANES] = (
                    out[hh * tq:(hh + 1) * tq].astype(BF16))
        ko_ref[b, 0:WINDOW - tq, :] = kc_ref[b, tq:WINDOW, :]
        ko_ref[b, WINDOW - tq:WINDOW, :] = kn_ref[b]
        vo_ref[b, 0:WINDOW - tq, :] = vc_ref[b, tq:WINDOW, :]
        vo_ref[b, WINDOW - tq:WINDOW, :] = vn_ref[b]
        return carry

    lax.fori_loop(0, bt, one, 0, unroll=True)


def _attn_sample(q, k, v, k_cache, v_cache, bias, sinks, dh):
    b, s, qd = q.shape
    kd = k.shape[-1]
    bt = math.gcd(b, 8)
    new = lambda n: pl.BlockSpec((bt, s, n), lambda i: (i, 0, 0))
    win = pl.BlockSpec((bt, WINDOW, kd), lambda i: (i, 0, 0))
    return pl.pallas_call(
        functools.partial(_attn_sample_body, dh=dh, bt=bt),
        grid=(b // bt,),
        in_specs=[pl.BlockSpec(memory_space=pltpu.SMEM),
                  new(qd), win, new(kd), win, new(kd),
                  pl.BlockSpec(bias.shape, lambda i: (0, 0, 0))],
        out_specs=[new(qd), win, win],
        out_shape=[jax.ShapeDtypeStruct((b, s, qd), BF16),
                   jax.ShapeDtypeStruct((b, WINDOW, kd), F32),
                   jax.ShapeDtypeStruct((b, WINDOW, kd), F32)],
        compiler_params=_cparams(1),
        name="attn_sample",
    )(sinks, q, k_cache, k, v_cache, v, bias)


def _swa_layer(x3, buffers, p):
    b, s, d = x3.shape
    dh = p["attn_dh"]
    kd = A_HEADS_KV * dh
    x = x3.reshape(b * s, d)
    sh = lambda a: a.reshape(b, s, a.shape[-1])
    kv_shape = (b, WINDOW, A_HEADS_KV, dh)
    if buffers is None:
        nq, nk = A_HEADS_Q * dh, A_HEADS_KV * LANES
        q, k, v = _attn_proj(x, p["mix_norm1"], p["attn_w_prompt"], p["attn_qg"], p["attn_kg"],
                             dh, nq, nk)
        bias = _bias_table(p["rel_bias"], WINDOW).reshape(2, A_HEADS_KV, A_GROUP * WINDOW, 2 * WINDOW)
        att = _attn_prompt(sh(q), sh(k), sh(v), bias, p["attn_sinks"], dh)
        tail = lambda a: a.reshape(b, s, A_HEADS_KV, LANES)[:, s - WINDOW:, :, :dh]
        new_k, new_v = tail(k), tail(v)
        y = _matmul_res(att.reshape(b * s, nq), p["attn_w_out"], x)
    else:
        k_buf, v_buf = buffers
        nq, nk = A_HEADS_Q * LANES, kd
        q, k, v = _attn_proj(x, p["mix_norm1"], p["attn_w_sample"], p["attn_qg"], p["attn_kg"],
                             dh, nq, nk)
        bias = _bias_table(p["rel_bias"], s)[1].reshape(2, 2 * A_GROUP * s, 2 * WINDOW)
        att, new_k, new_v = _attn_sample(sh(q), sh(k), sh(v), k_buf.reshape(b, WINDOW, kd),
                                         v_buf.reshape(b, WINDOW, kd), bias, p["attn_sinks"], dh)
        y = _matmul_res(att.reshape(b * s, nq), p["attn_w_out_slots"], x)
    return y.reshape(b, s, d), (new_k.reshape(kv_shape), new_v.reshape(kv_shape))


def _attn_weights(w_in, w_out, q_norm, k_norm, dh):
    d = w_in.shape[0]
    qd, kd = A_HEADS_Q * dh, A_HEADS_KV * dh
    wq = w_in[:, :qd].reshape(d, A_HEADS_Q, dh)
    wk = w_in[:, qd:qd + kd].reshape(d, A_HEADS_KV, dh)
    wv = w_in[:, qd + kd:].reshape(d, A_HEADS_KV, dh)
    dup = lambda a: jnp.concatenate([a, a], axis=-1).reshape(d, A_HEADS_KV * LANES)
    w_prompt = jnp.concatenate([w_in[:, :qd], dup(wk), dup(wv)], axis=1).astype(BF16)
    zero = jnp.zeros_like(wq)
    odd = ((jnp.arange(A_HEADS_Q) // A_GROUP) % 2 == 1)[None, :, None]
    wq_slots = jnp.concatenate([jnp.where(odd, zero, wq), jnp.where(odd, wq, zero)], axis=-1)
    w_sample = jnp.concatenate([wq_slots.reshape(d, A_HEADS_Q * LANES), w_in[:, qd:]], axis=1).astype(BF16)
    wo = w_out.reshape(A_HEADS_Q, dh, d)
    zo = jnp.zeros_like(wo)
    wo_slots = jnp.concatenate([jnp.where(odd.reshape(-1, 1, 1), zo, wo),
                                jnp.where(odd.reshape(-1, 1, 1), wo, zo)], axis=1)
    qg = jnp.concatenate([q_norm, q_norm]).reshape(1, LANES) * (dh ** -0.5)
    kg = jnp.concatenate([k_norm, k_norm]).reshape(1, LANES)
    return {"attn_w_prompt": w_prompt, "attn_w_sample": w_sample,
            "attn_w_out": w_out.astype(BF16),
            "attn_w_out_slots": wo_slots.reshape(A_HEADS_Q * LANES, d).astype(BF16),
            "attn_qg": qg, "attn_kg": kg}


def _ffn3(x3, g, wgu, wd):
    b, s, d = x3.shape
    return _ffn(x3.reshape(b * s, d), g, wgu, wd).reshape(b, s, d)


def _trunk(x3, mlstm_state, swa_buffers, p):
    x3 = _ffn3(x3, p["ffn1_norm"][0], p["ffn1_wgu"][0], p["ffn1_wd"][0])
    x3, new_mlstm = _mlstm_layer(x3, mlstm_state, p)
    x3 = _ffn3(x3, p["ffn2_norm"][0], p["ffn2_wgu"][0], p["ffn2_wd"][0])
    x3 = _ffn3(x3, p["ffn1_norm"][1], p["ffn1_wgu"][1], p["ffn1_wd"][1])
    x3, new_swa = _swa_layer(x3, swa_buffers, p)
    x3 = _ffn3(x3, p["ffn2_norm"][1], p["ffn2_wgu"][1], p["ffn2_wd"][1])
    return x3, new_mlstm, new_swa


def kernel(x_prompt, x_sample, state_mlstm_C, state_mlstm_n, state_mlstm_m, cache_swa_k, cache_swa_v,
           ffn1_norm, ffn1_w_gate_up, ffn1_w_down, mix_norm, ffn2_norm, ffn2_w_gate_up, ffn2_w_down,
           mlstm_w_in, mlstm_b_gates, mlstm_out_norm, mlstm_w_out,
           attn_w_in, attn_q_norm, attn_k_norm, attn_sinks, rel_bias, attn_w_out):
    d = x_prompt.shape[-1]
    vd = mlstm_w_out.shape[0]
    n_gates = 2 * M_HEADS
    qk = (mlstm_w_in.shape[1] - 2 * vd - n_gates) // 2
    dh = attn_q_norm.shape[0]
    assert 2 * dh == LANES
    w_gates = jnp.zeros((d, LANES), BF16).at[:, :n_gates].set(mlstm_w_in[:, 2 * qk + 2 * vd:].astype(BF16))
    b_gates = jnp.zeros((1, LANES), F32).at[0, :n_gates].set(mlstm_b_gates)
    p = {
        "ffn1_norm": ffn1_norm, "ffn2_norm": ffn2_norm,
        "ffn1_wgu": ffn1_w_gate_up.astype(BF16), "ffn1_wd": ffn1_w_down.astype(BF16),
        "ffn2_wgu": ffn2_w_gate_up.astype(BF16), "ffn2_wd": ffn2_w_down.astype(BF16),
        "mix_norm0": mix_norm[0], "mix_norm1": mix_norm[1],
        "mlstm_qk": qk, "mlstm_vd": vd,
        "mlstm_w_main": mlstm_w_in[:, :2 * qk + 2 * vd].astype(BF16),
        "mlstm_w_gates": w_gates, "mlstm_b_gates": b_gates,
        "mlstm_out_norm": mlstm_out_norm, "mlstm_w_out": mlstm_w_out.astype(BF16),
        "attn_dh": dh, "attn_sinks": attn_sinks, "rel_bias": rel_bias,
    }
    p.update(_attn_weights(attn_w_in, attn_w_out, attn_q_norm, attn_k_norm, dh))
    y_p, (c_p, n_p, m_p), (k_p, v_p) = _trunk(x_prompt, None, None, p)
    y_s, (c_s, n_s, m_s), (k_s, v_s) = _trunk(
        x_sample, (state_mlstm_C, state_mlstm_n, state_mlstm_m), (cache_swa_k, cache_swa_v), p)
    return (y_p, y_s, c_p, n_p, m_p, k_p, v_p, c_s, n_s, m_s, k_s, v_s)
```

```python
import functools
import math

import jax
import jax.numpy as jnp
from jax import lax
from jax.experimental import pallas as pl
from jax.experimental.pallas import tpu as pltpu

F32 = jnp.float32
BF16 = jnp.bfloat16

M_HEADS = 4
A_HEADS_Q = 16
A_HEADS_KV = 4
A_GROUP = A_HEADS_Q // A_HEADS_KV
WINDOW = 128
NUM_BUCKETS = 32
MAX_DISTANCE = 128
FFN_RESIDUAL = 0.5
RMS_EPS = 1e-6
LANES = 128
VMEM_LIMIT = 56 * 1024 * 1024


def _cparams(n_axes):
    return pltpu.CompilerParams(
        dimension_semantics=("arbitrary",) * n_axes, vmem_limit_bytes=VMEM_LIMIT)


def _resident(shape):
    nd = len(shape)
    return pl.BlockSpec(shape, lambda *_: (0,) * nd, pipeline_mode=pl.Buffered(1))


def _rms(x, g):
    return x * lax.rsqrt(jnp.mean(x * x, axis=-1, keepdims=True) + RMS_EPS) * g


def _dot(a, b):
    return jnp.dot(a, b, preferred_element_type=F32)


def _dot_nt(a, b):
    return lax.dot_general(a, b, (((1,), (1,)), ((), ())), preferred_element_type=F32)


def _dot_tn(a, b):
    return lax.dot_general(a, b, (((0,), (0,)), ((), ())), preferred_element_type=F32)


def _row_tile(t, pref):
    tm = min(t, pref)
    assert t % tm == 0
    return tm


def _ffn_body(x_ref, g_ref, wgu_ref, wd_ref, o_ref, act_ref, *, d_ff, tf):
    x = x_ref[...]
    xn = _rms(x, g_ref[...]).astype(BF16)
    for c in range(d_ff // tf):
        lo = c * tf
        gate = _dot(xn, wgu_ref[:, lo:lo + tf])
        up = _dot(xn, wgu_ref[:, d_ff + lo:d_ff + lo + tf])
        act_ref[:, lo:lo + tf] = (gate * jax.nn.sigmoid(gate) * up).astype(BF16)
    o_ref[...] = x + FFN_RESIDUAL * _dot(act_ref[...], wd_ref[...])


def _ffn(x, g, wgu, wd):
    t, d = x.shape
    d_ff = wd.shape[0]
    tm = _row_tile(t, 512)
    tf = 256
    assert d_ff % tf == 0
    return pl.pallas_call(
        functools.partial(_ffn_body, d_ff=d_ff, tf=tf),
        grid=(t // tm,),
        in_specs=[pl.BlockSpec((tm, d), lambda i: (i, 0)),
                  _resident((1, d)), _resident((d, 2 * d_ff)), _resident((d_ff, d))],
        out_specs=pl.BlockSpec((tm, d), lambda i: (i, 0)),
        out_shape=jax.ShapeDtypeStruct((t, d), F32),
        scratch_shapes=[pltpu.VMEM((tm, d_ff), BF16)],
        compiler_params=_cparams(1),
        name="ffn",
    )(x, g.reshape(1, d), wgu, wd)


def _matmul_res_body(a_ref, w_ref, x_ref, o_ref):
    o_ref[...] = x_ref[...] + _dot(a_ref[...], w_ref[...])


def _matmul_res(a, w, x):
    t, k = a.shape
    d = w.shape[1]
    tm = _row_tile(t, 1024)
    return pl.pallas_call(
        _matmul_res_body,
        grid=(t // tm,),
        in_specs=[pl.BlockSpec((tm, k), lambda i: (i, 0)), _resident((k, d)),
                  pl.BlockSpec((tm, d), lambda i: (i, 0))],
        out_specs=pl.BlockSpec((tm, d), lambda i: (i, 0)),
        out_shape=jax.ShapeDtypeStruct((t, d), F32),
        compiler_params=_cparams(1),
        name="matmul_res",
    )(a, w, x)


def _mlstm_proj_body(x_ref, g_ref, w_ref, wg_ref, bg_ref,
                     q_ref, k_ref, v_ref, o_ref, gates_ref, *, qk, vd, dk):
    xn = _rms(x_ref[...], g_ref[...]).astype(BF16)
    q_ref[...] = _dot(xn, w_ref[:, 0:qk]).astype(BF16)
    k_ref[...] = (_dot(xn, w_ref[:, qk:2 * qk]) * (dk ** -0.5)).astype(BF16)
    v_ref[...] = _dot(xn, w_ref[:, 2 * qk:2 * qk + vd]).astype(BF16)
    o_ref[...] = _dot(xn, w_ref[:, 2 * qk + vd:2 * qk + 2 * vd])
    gates_ref[...] = _dot(xn, wg_ref[...]) + bg_ref[...]


def _mlstm_proj(x, g, w_main, w_gates, b_gates, qk, vd):
    t, d = x.shape
    tm = _row_tile(t, 512)
    row = lambda n: pl.BlockSpec((tm, n), lambda i: (i, 0))
    return pl.pallas_call(
        functools.partial(_mlstm_proj_body, qk=qk, vd=vd, dk=qk // M_HEADS),
        grid=(t // tm,),
        in_specs=[row(d), _resident((1, d)), _resident(w_main.shape),
                  _resident(w_gates.shape), _resident((1, LANES))],
        out_specs=[row(qk), row(qk), row(vd), row(vd), row(LANES)],
        out_shape=[jax.ShapeDtypeStruct((t, qk), BF16), jax.ShapeDtypeStruct((t, qk), BF16),
                   jax.ShapeDtypeStruct((t, vd), BF16), jax.ShapeDtypeStruct((t, vd), F32),
                   jax.ShapeDtypeStruct((t, LANES), F32)],
        compiler_params=_cparams(1),
        name="mlstm_proj",
    )(x, g.reshape(1, d), w_main, w_gates, b_gates)


def _split3(x):
    hi = x.astype(BF16)
    r1 = x - hi.astype(F32)
    mid = r1.astype(BF16)
    lo = (r1 - mid.astype(F32)).astype(BF16)
    return hi, mid, lo


def _mlstm_cell_body(q_ref, k_ref, v_ref, o_ref, gt_ref, ng_ref, c0_ref, n0_ref, m0_ref,
                     hg_ref, c_ref, n_ref, m_ref, *, dk, dv):
    h_n = M_HEADS

    @pl.when(pl.program_id(1) == 0)
    def _():
        c_ref[...] = c0_ref[...]
        n_ref[...] = n0_ref[...]
        m_ref[...] = m0_ref[...]

    gates = gt_ref[0]
    rows = gates.shape[0]
    lane = lax.broadcasted_iota(jnp.int32, (rows, LANES), 1)
    log_f = jnp.where((lane >= h_n) & (lane < 2 * h_n), jax.nn.log_sigmoid(gates), 0.0)
    r_i = lax.broadcasted_iota(jnp.int32, (rows, rows), 0)
    c_i = lax.broadcasted_iota(jnp.int32, (rows, rows), 1)
    causal = r_i >= c_i
    tri = jnp.where(causal, 1.0, 0.0).astype(BF16)
    hi, mid, lo = _split3(log_f)
    b_all = _dot(tri, hi) + _dot(tri, mid) + _dot(tri, lo)
    ib_t = jnp.where(lane < h_n, gates, b_all).T

    hs = range(h_n)
    q = [q_ref[0, :, h * dk:(h + 1) * dk] for h in hs]
    k = [k_ref[0, :, h * dk:(h + 1) * dk] for h in hs]
    v = [v_ref[0, :, h * dv:(h + 1) * dv] for h in hs]
    c_prev = [c_ref[0, h] for h in hs]
    n_prev = [n_ref[0, h:h + 1, :] for h in hs]
    m_prev = [m_ref[0, :, h:h + 1] for h in hs]
    b_c = [b_all[:, h_n + h:h_n + h + 1] for h in hs]
    i_c = [gates[:, h:h + 1] for h in hs]
    log_d = [jnp.where(causal, b_c[h] + (ib_t[h:h + 1, :] - ib_t[h_n + h:h_n + h + 1, :]), -jnp.inf)
             for h in hs]
    log_inter = [b_c[h] + m_prev[h] for h in hs]
    m_t = [jnp.maximum(log_inter[h], jnp.max(log_d[h], axis=-1, keepdims=True)) for h in hs]
    inter = [jnp.exp(log_inter[h] - m_t[h]) for h in hs]
    qk_t = [_dot_nt(q[h], k[h]) for h in hs]
    q_c = [_dot(q[h], c_prev[h].astype(BF16)) for h in hs]
    s = [qk_t[h] * jnp.exp(log_d[h] - m_t[h]) for h in hs]
    s_v = [_dot(s[h].astype(BF16), v[h]) for h in hs]
    den = [inter[h] * jnp.sum(q[h].astype(F32) * n_prev[h], axis=-1, keepdims=True)
           + jnp.sum(s[h], axis=-1, keepdims=True) for h in hs]
    hid = [(inter[h] * q_c[h] + s_v[h]) / jnp.maximum(jnp.abs(den[h]), jnp.exp(-m_t[h])) for h in hs]

    b_last = [b_c[h][rows - 1:rows, :] for h in hs]
    log_w = [b_last[h] - b_c[h] + i_c[h] for h in hs]
    m_new = [jnp.maximum(b_last[h] + m_prev[h], jnp.max(log_w[h], axis=0, keepdims=True)) for h in hs]
    decay = [jnp.exp(b_last[h] + m_prev[h] - m_new[h]) for h in hs]
    kw = [jnp.exp(log_w[h] - m_new[h]) * k[h].astype(F32) for h in hs]
    kw_v = [_dot_tn(kw[h].astype(BF16), v[h]) for h in hs]
    for h in hs:
        c_ref[0, h] = decay[h] * c_prev[h] + kw_v[h]
        n_ref[0, h:h + 1, :] = decay[h] * n_prev[h] + jnp.sum(kw[h], axis=0, keepdims=True)
        m_ref[0, :, h:h + 1] = m_new[h]
    hid = [hid[h] * lax.rsqrt(jnp.mean(hid[h] * hid[h], axis=-1, keepdims=True) + RMS_EPS) for h in hs]
    for h in hs:
        gate = jax.nn.sigmoid(o_ref[0, :, h * dv:(h + 1) * dv])
        hg_ref[0, :, h * dv:(h + 1) * dv] = (hid[h] * ng_ref[:, h * dv:(h + 1) * dv] * gate).astype(BF16)


def _mlstm_cell(q, k, v, o, gates, norm_g, c0, n0, m0, chunk):
    b, s, qk = q.shape
    vd = v.shape[-1]
    dk, dv = qk // M_HEADS, vd // M_HEADS
    nc = s // chunk
    tok = lambda n: pl.BlockSpec((1, chunk, n), lambda i, j: (i, j, 0))
    st_c = pl.BlockSpec((1, M_HEADS, dk, dv), lambda i, j: (i, 0, 0, 0))
    st_n = pl.BlockSpec((1, M_HEADS, dk), lambda i, j: (i, 0, 0))
    st_m = pl.BlockSpec((1, 1, M_HEADS), lambda i, j: (i, 0, 0))
    return pl.pallas_call(
        functools.partial(_mlstm_cell_body, dk=dk, dv=dv),
        grid=(b, nc),
        in_specs=[tok(qk), tok(qk), tok(vd), tok(vd), tok(LANES),
                  pl.BlockSpec((1, vd), lambda i, j: (0, 0)), st_c, st_n, st_m],
        out_specs=[tok(vd), st_c, st_n, st_m],
        out_shape=[jax.ShapeDtypeStruct((b, s, vd), BF16),
                   jax.ShapeDtypeStruct((b, M_HEADS, dk, dv), F32),
                   jax.ShapeDtypeStruct((b, M_HEADS, dk), F32),
                   jax.ShapeDtypeStruct((b, 1, M_HEADS), F32)],
        compiler_params=_cparams(2),
        name="mlstm_cell",
    )(q, k, v, o, gates, norm_g.reshape(1, vd), c0, n0, m0)


def _mlstm_layer(x3, state, p):
    b, s, d = x3.shape
    qk, vd = p["mlstm_qk"], p["mlstm_vd"]
    x = x3.reshape(b * s, d)
    q, k, v, o, gates = _mlstm_proj(x, p["mix_norm0"], p["mlstm_w_main"], p["mlstm_w_gates"],
                                    p["mlstm_b_gates"], qk, vd)
    sh = lambda a: a.reshape(b, s, a.shape[-1])
    if state is None:
        dk, dv = qk // M_HEADS, vd // M_HEADS
        c0 = jnp.zeros((b, M_HEADS, dk, dv), F32)
        n0 = jnp.zeros((b, M_HEADS, dk), F32)
        m0 = jnp.zeros((b, 1, M_HEADS), F32)
        chunk = min(s, 128)
    else:
        c0, n0, m0 = state
        m0 = m0.reshape(b, 1, M_HEADS)
        chunk = s
    hg, c1, n1, m1 = _mlstm_cell(sh(q), sh(k), sh(v), sh(o), sh(gates), p["mlstm_out_norm"],
                                 c0, n0, m0, chunk)
    y = _matmul_res(hg.reshape(b * s, vd), p["mlstm_w_out"], x)
    return y.reshape(b, s, d), (c1, n1, m1.reshape(b, M_HEADS))


def _attn_proj_body(x_ref, g_ref, w_ref, wvt_ref, qg_ref, kg_ref, q_ref, k_ref, vt_ref, *, dh, nq, nk):
    xn = _rms(x_ref[...], g_ref[...]).astype(BF16)
    r_i = lax.broadcasted_iota(jnp.int32, (LANES, LANES), 0)
    c_i = lax.broadcasted_iota(jnp.int32, (LANES, LANES), 1)
    same_half = jnp.where((r_i < dh) == (c_i < dh), 1.0, 0.0).astype(BF16)

    def half_norm(y, gain):
        sq = y * y
        hi = sq.astype(BF16)
        lo = (sq - hi.astype(F32)).astype(BF16)
        ms = (_dot(hi, same_half) + _dot(lo, same_half)) * (1.0 / dh)
        return y * lax.rsqrt(ms + RMS_EPS) * gain

    q = _dot(xn, w_ref[:, 0:nq])
    for s in range(nq // LANES):
        sl = slice(s * LANES, (s + 1) * LANES)
        q_ref[:, sl] = half_norm(q[:, sl], qg_ref[...]).astype(BF16)
    k = _dot(xn, w_ref[:, nq:nq + nk])
    for s in range(nk // LANES):
        sl = slice(s * LANES, (s + 1) * LANES)
        k_ref[:, sl] = half_norm(k[:, sl], kg_ref[...])
    vt_ref[0] = _dot_nt(wvt_ref[...], xn)


def _attn_proj(x3, g, w, wvt, qg, kg, dh, nq, nk):
    b, s, d = x3.shape
    t = b * s
    nv = wvt.shape[0]
    tm = _row_tile(s, 512)
    nt = s // tm
    row = lambda n: pl.BlockSpec((tm, n), lambda i: (i, 0))
    return pl.pallas_call(
        functools.partial(_attn_proj_body, dh=dh, nq=nq, nk=nk),
        grid=(t // tm,),
        in_specs=[row(d), _resident((1, d)), _resident(w.shape), _resident(wvt.shape),
                  _resident((1, LANES)), _resident((1, LANES))],
        out_specs=[row(nq), row(nk), pl.BlockSpec((1, nv, tm), lambda i: (i // nt, 0, i % nt))],
        out_shape=[jax.ShapeDtypeStruct((t, nq), BF16), jax.ShapeDtypeStruct((t, nk), F32),
                   jax.ShapeDtypeStruct((b, nv, s), F32)],
        compiler_params=_cparams(1),
        name="attn_proj",
    )(x3.reshape(t, d), g.reshape(1, d), w, wvt, qg, kg)


def _bias_body(rel_ref, o_ref, *, tq, key_major):
    variant = pl.program_id(0)
    h = pl.program_id(1)
    nk = 2 * WINDOW
    shape, q_ax = ((nk, tq), 1) if key_major else ((tq, nk), 0)
    qi = lax.broadcasted_iota(jnp.int32, shape, q_ax)
    ki = lax.broadcasted_iota(jnp.int32, shape, 1 - q_ax)
    dist = qi + WINDOW - ki
    max_exact = NUM_BUCKETS // 2
    d = jnp.maximum(dist, 0)
    log_ratio = (jnp.log(jnp.maximum(d, 1).astype(F32) / max_exact)
                 / math.log(MAX_DISTANCE / max_exact))
    large = jnp.minimum(max_exact + (log_ratio * (NUM_BUCKETS - max_exact)).astype(jnp.int32),
                        NUM_BUCKETS - 1)
    bucket = jnp.where(d < max_exact, d, large)
    bias = jnp.zeros(shape, F32)
    for bkt in range(NUM_BUCKETS):
        bias = jnp.where(bucket == bkt, rel_ref[bkt, h], bias)
    visible = (dist >= 0) & (dist < WINDOW) & ((ki >= WINDOW) | (variant > 0))
    o_ref[0, 0] = jnp.where(visible, bias, -jnp.inf)


def _bias_table(rel_bias, tq, key_major):
    nk = 2 * WINDOW
    if key_major:
        spec = pl.BlockSpec((1, 1, nk, tq), lambda v, h: (v, h // 2, 0, h % 2))
        shape = (2, A_HEADS_Q // 2, nk, 2 * tq)
    else:
        spec = pl.BlockSpec((1, 1, tq, nk), lambda v, h: (v, h, 0, 0))
        shape = (2, A_HEADS_Q, tq, nk)
    return pl.pallas_call(
        functools.partial(_bias_body, tq=tq, key_major=key_major),
        grid=(2, A_HEADS_Q),
        in_specs=[pl.BlockSpec(memory_space=pltpu.SMEM)],
        out_specs=spec,
        out_shape=jax.ShapeDtypeStruct(shape, F32),
        compiler_params=_cparams(2),
        name="rel_bias_table",
    )(rel_bias)


def _attn_prompt_body(sink_ref, q_ref, kp_ref, ko_ref, vp_ref, vo_ref, bias_ref, o_ref, *, dh):
    blk = WINDOW
    lane = lax.broadcasted_iota(jnp.int32, (blk, LANES), 1)
    lo_half = lane < dh
    first = lax.broadcasted_iota(jnp.int32, (1, 2 * blk), 1) < blk
    ones = jnp.ones((dh, 2 * blk), BF16)
    slots_per_kv = A_GROUP // 2
    n_slots = A_HEADS_Q // 2
    vts, sts = [], []
    for g in range(A_HEADS_KV):
        sl = slice(g * LANES, (g + 1) * LANES)
        rows = slice(g * dh, (g + 1) * dh)
        kk = jnp.concatenate([kp_ref[0, :, sl], ko_ref[0, :, sl]], axis=0).astype(BF16)
        vt = jnp.concatenate([vp_ref[0, rows, :], vo_ref[0, rows, :]], axis=1).astype(BF16)
        vts.append(jnp.concatenate([vt, ones], axis=0))
        for pr in range(slots_per_kv):
            slot_i = g * slots_per_kv + pr
            slot = q_ref[0, :, slot_i * LANES:(slot_i + 1) * LANES]
            zero = jnp.zeros_like(slot)
            q2 = jnp.concatenate([jnp.where(lo_half, slot, zero), jnp.where(lo_half, zero, slot)],
                                 axis=0)
            sts.append(_dot_nt(kk, q2) + bias_ref[0, slot_i])
    sinks = [jnp.where(first, sink_ref[2 * i], sink_ref[2 * i + 1]) for i in range(n_slots)]
    ms = [jnp.maximum(jnp.max(sts[i], axis=0, keepdims=True), sinks[i]) for i in range(n_slots)]
    pts = [jnp.exp(sts[i] - ms[i]).astype(BF16) for i in range(n_slots)]
    oes = [_dot(vts[i // slots_per_kv], pts[i]) for i in range(n_slots)]
    heads_t = []
    for i in range(n_slots):
        ot = oes[i][0:dh, :] / (oes[i][dh:dh + 1, :] + jnp.exp(sinks[i] - ms[i]))
        heads_t += [ot[:, 0:blk], ot[:, blk:2 * blk]]
    o_ref[0] = jnp.concatenate(heads_t, axis=0).T.astype(BF16)


def _attn_prompt(q, k, vt, bias, sinks, dh):
    b, s, qd = q.shape
    kd = k.shape[-1]
    vd = vt.shape[1]
    blk = WINDOW
    own = lambda n: pl.BlockSpec((1, blk, n), lambda i, j: (i, j, 0))
    prev = lambda n: pl.BlockSpec((1, blk, n), lambda i, j: (i, jnp.maximum(j - 1, 0), 0))
    return pl.pallas_call(
        functools.partial(_attn_prompt_body, dh=dh),
        grid=(b, s // blk),
        in_specs=[pl.BlockSpec(memory_space=pltpu.SMEM),
                  own(qd), prev(kd), own(kd),
                  pl.BlockSpec((1, vd, blk), lambda i, j: (i, 0, jnp.maximum(j - 1, 0))),
                  pl.BlockSpec((1, vd, blk), lambda i, j: (i, 0, j)),
                  pl.BlockSpec((1,) + bias.shape[1:], lambda i, j: (jnp.minimum(j, 1), 0, 0, 0))],
        out_specs=own(qd),
        out_shape=jax.ShapeDtypeStruct((b, s, qd), BF16),
        compiler_params=_cparams(2),
        name="attn_prompt",
    )(sinks, q, k, k, vt, vt, bias)


def _attn_sample_body(sink_ref, q_ref, kc_ref, kn_ref, vc_ref, vn_ref, bias_ref,
                      o_ref, ko_ref, vo_ref, *, dh, bt):
    tq = kn_ref.shape[1]
    heads_per_slab = 2 * A_GROUP
    n_slabs = A_HEADS_Q // heads_per_slab
    sinks = [jnp.concatenate([jnp.full((tq, 1), sink_ref[sl * heads_per_slab + hh], F32)
                              for hh in range(heads_per_slab)], axis=0) for sl in range(n_slabs)]

    units = [(b, sl) for b in range(bt) for sl in range(n_slabs)]
    lanes = lambda sl: slice(sl * LANES, (sl + 1) * LANES)
    qs = [jnp.concatenate([q_ref[b, :, (sl * heads_per_slab + hh) * LANES:(sl * heads_per_slab + hh + 1) * LANES]
                           for hh in range(heads_per_slab)], axis=0) for b, sl in units]
    sp = [_dot_nt(qs[u], kc_ref[b, :, lanes(sl)].astype(BF16)) + bias_ref[sl][:, :WINDOW]
          for u, (b, sl) in enumerate(units)]
    so = [_dot_nt(qs[u], kn_ref[b, :, lanes(sl)].astype(BF16)) + bias_ref[sl][:, WINDOW:WINDOW + tq]
          for u, (b, sl) in enumerate(units)]
    m = [jnp.maximum(jnp.maximum(jnp.max(sp[u], axis=-1, keepdims=True),
                                 jnp.max(so[u], axis=-1, keepdims=True)), sinks[sl])
         for u, (b, sl) in enumerate(units)]
    pp = [jnp.exp(sp[u] - m[u]) for u in range(len(units))]
    po = [jnp.exp(so[u] - m[u]) for u in range(len(units))]
    den = [jnp.sum(pp[u], axis=-1, keepdims=True) + jnp.sum(po[u], axis=-1, keepdims=True)
           + jnp.exp(sinks[sl] - m[u]) for u, (b, sl) in enumerate(units)]
    out = [_dot((pp[u] / den[u]).astype(BF16), vc_ref[b, :, lanes(sl)].astype(BF16))
           + _dot((po[u] / den[u]).astype(BF16), vn_ref[b, :, lanes(sl)].astype(BF16))
           for u, (b, sl) in enumerate(units)]
    for u, (b, sl) in enumerate(units):
        for hh in range(heads_per_slab):
            h = sl * heads_per_slab + hh
            o_ref[b, :, h * LANES:(h + 1) * LANES] = out[u][hh * tq:(hh + 1) * tq].astype(BF16)
    ko_ref[:, 0:WINDOW - tq, :] = kc_ref[:, tq:WINDOW, :]
    ko_ref[:, WINDOW - tq:WINDOW, :] = kn_ref[...]
    vo_ref[:, 0:WINDOW - tq, :] = vc_ref[:, tq:WINDOW, :]
    vo_ref[:, WINDOW - tq:WINDOW, :] = vn_ref[...]


def _attn_sample(q, k, v, k_cache, v_cache, bias, sinks, dh):
    b, s, qd = q.shape
    kd = k.shape[-1]
    bt = math.gcd(b, 8)
    new = lambda n: pl.BlockSpec((bt, s, n), lambda i: (i, 0, 0))
    win = pl.BlockSpec((bt, WINDOW, kd), lambda i: (i, 0, 0))
    return pl.pallas_call(
        functools.partial(_attn_sample_body, dh=dh, bt=bt),
        grid=(b // bt,),
        in_specs=[pl.BlockSpec(memory_space=pltpu.SMEM),
                  new(qd), win, new(kd), win, new(kd),
                  pl.BlockSpec(bias.shape, lambda i: (0, 0, 0))],
        out_specs=[new(qd), win, win],
        out_shape=[jax.ShapeDtypeStruct((b, s, qd), BF16),
                   jax.ShapeDtypeStruct((b, WINDOW, kd), F32),
                   jax.ShapeDtypeStruct((b, WINDOW, kd), F32)],
        compiler_params=_cparams(1),
        name="attn_sample",
    )(sinks, q, k_cache, k, v_cache, v, bias)


def _swa_layer(x3, buffers, p):
    b, s, d = x3.shape
    dh = p["attn_dh"]
    kd = A_HEADS_KV * dh
    x = x3.reshape(b * s, d)
    kv_shape = (b, WINDOW, A_HEADS_KV, dh)
    if buffers is None:
        nq, nk = A_HEADS_Q * dh, A_HEADS_KV * LANES
        q, k, vt = _attn_proj(x3, p["mix_norm1"], p["attn_w_prompt"], p["attn_wvt"],
                              p["attn_qg"], p["attn_kg"], dh, nq, nk)
        k = k.reshape(b, s, nk)
        bias = _bias_table(p["rel_bias"], WINDOW, key_major=True)
        att = _attn_prompt(q.reshape(b, s, nq), k, vt, bias, p["attn_sinks"], dh)
        new_k = k[:, s - WINDOW:].reshape(b, WINDOW, A_HEADS_KV, LANES)[..., :dh]
        new_v = jnp.swapaxes(vt[:, :, s - WINDOW:], 1, 2)
        y = _matmul_res(att.reshape(b * s, nq), p["attn_w_out"], x)
    else:
        k_buf, v_buf = buffers
        nq, nk = A_HEADS_Q * LANES, kd
        q, k, vt = _attn_proj(x.reshape(1, b * s, d), p["mix_norm1"], p["attn_w_sample"], p["attn_wvt"],
                              p["attn_qg"], p["attn_kg"], dh, nq, nk)
        v = vt[0].T
        sh = lambda a: a.reshape(b, s, a.shape[-1])
        bias = _bias_table(p["rel_bias"], s, key_major=False)[1].reshape(2, 2 * A_GROUP * s, 2 * WINDOW)
        att, new_k, new_v = _attn_sample(sh(q), sh(k), sh(v), k_buf.reshape(b, WINDOW, kd),
                                         v_buf.reshape(b, WINDOW, kd), bias, p["attn_sinks"], dh)
        y = _matmul_res(att.reshape(b * s, nq), p["attn_w_out_slots"], x)
    return y.reshape(b, s, d), (new_k.reshape(kv_shape), new_v.reshape(kv_shape))


def _attn_weights(w_in, w_out, q_norm, k_norm, dh):
    d = w_in.shape[0]
    qd, kd = A_HEADS_Q * dh, A_HEADS_KV * dh
    wq = w_in[:, :qd].reshape(d, A_HEADS_Q, dh)
    wk = w_in[:, qd:qd + kd].reshape(d, A_HEADS_KV, dh)
    k_dup = jnp.concatenate([wk, wk], axis=-1).reshape(d, A_HEADS_KV * LANES)
    w_prompt = jnp.concatenate([w_in[:, :qd], k_dup], axis=1).astype(BF16)
    zero = jnp.zeros_like(wq)
    odd = ((jnp.arange(A_HEADS_Q) // A_GROUP) % 2 == 1)[None, :, None]
    wq_slots = jnp.concatenate([jnp.where(odd, zero, wq), jnp.where(odd, wq, zero)], axis=-1)
    w_sample = jnp.concatenate([wq_slots.reshape(d, A_HEADS_Q * LANES), w_in[:, qd:qd + kd]],
                               axis=1).astype(BF16)
    wo = w_out.reshape(A_HEADS_Q, dh, d)
    zo = jnp.zeros_like(wo)
    wo_slots = jnp.concatenate([jnp.where(odd.reshape(-1, 1, 1), zo, wo),
                                jnp.where(odd.reshape(-1, 1, 1), wo, zo)], axis=1)
    qg = jnp.concatenate([q_norm, q_norm]).reshape(1, LANES) * (dh ** -0.5)
    kg = jnp.concatenate([k_norm, k_norm]).reshape(1, LANES)
    return {"attn_w_prompt": w_prompt, "attn_w_sample": w_sample,
            "attn_wvt": w_in[:, qd + kd:].T.astype(BF16),
            "attn_w_out": w_out.astype(BF16),
            "attn_w_out_slots": wo_slots.reshape(A_HEADS_Q * LANES, d).astype(BF16),
            "attn_qg": qg, "attn_kg": kg}


def _ffn3(x3, g, wgu, wd):
    b, s, d = x3.shape
    return _ffn(x3.reshape(b * s, d), g, wgu, wd).reshape(b, s, d)


def _trunk(x3, mlstm_state, swa_buffers, p):
    x3 = _ffn3(x3, p["ffn1_norm"][0], p["ffn1_wgu"][0], p["ffn1_wd"][0])
    x3, new_mlstm = _mlstm_layer(x3, mlstm_state, p)
    x3 = _ffn3(x3, p["ffn2_norm"][0], p["ffn2_wgu"][0], p["ffn2_wd"][0])
    x3 = _ffn3(x3, p["ffn1_norm"][1], p["ffn1_wgu"][1], p["ffn1_wd"][1])
    x3, new_swa = _swa_layer(x3, swa_buffers, p)
    x3 = _ffn3(x3, p["ffn2_norm"][1], p["ffn2_wgu"][1], p["ffn2_wd"][1])
    return x3, new_mlstm, new_swa


def kernel(x_prompt, x_sample, state_mlstm_C, state_mlstm_n, state_mlstm_m, cache_swa_k, cache_swa_v,
           ffn1_norm, ffn1_w_gate_up, ffn1_w_down, mix_norm, ffn2_norm, ffn2_w_gate_up, ffn2_w_down,
           mlstm_w_in, mlstm_b_gates, mlstm_out_norm, mlstm_w_out,
           attn_w_in, attn_q_norm, attn_k_norm, attn_sinks, rel_bias, attn_w_out):
    d = x_prompt.shape[-1]
    vd = mlstm_w_out.shape[0]
    n_gates = 2 * M_HEADS
    qk = (mlstm_w_in.shape[1] - 2 * vd - n_gates) // 2
    dh = attn_q_norm.shape[0]
    assert 2 * dh == LANES
    w_gates = jnp.zeros((d, LANES), BF16).at[:, :n_gates].set(mlstm_w_in[:, 2 * qk + 2 * vd:].astype(BF16))
    b_gates = jnp.zeros((1, LANES), F32).at[0, :n_gates].set(mlstm_b_gates)
    p = {
        "ffn1_norm": ffn1_norm, "ffn2_norm": ffn2_norm,
        "ffn1_wgu": ffn1_w_gate_up.astype(BF16), "ffn1_wd": ffn1_w_down.astype(BF16),
        "ffn2_wgu": ffn2_w_gate_up.astype(BF16), "ffn2_wd": ffn2_w_down.astype(BF16),
        "mix_norm0": mix_norm[0], "mix_norm1": mix_norm[1],
        "mlstm_qk": qk, "mlstm_vd": vd,
        "mlstm_w_main": mlstm_w_in[:, :2 * qk + 2 * vd].astype(BF16),
        "mlstm_w_gates": w_gates, "mlstm_b_gates": b_gates,
        "mlstm_out_norm": mlstm_out_norm, "mlstm_w_out": mlstm_w_out.astype(BF16),
        "attn_dh": dh, "attn_sinks": attn_sinks, "rel_bias": rel_bias,
    }
    p.update(_attn_weights(attn_w_in, attn_w_out, attn_q_norm, attn_k_norm, dh))
    y_p, (c_p, n_p, m_p), (k_p, v_p) = _trunk(x_prompt, None, None, p)
    y_s, (c_s, n_s, m_s), (k_s, v_s) = _trunk(
        x_sample, (state_mlstm_C, state_mlstm_n, state_mlstm_m), (cache_swa_k, cache_swa_v), p)
    return (y_p, y_s, c_p, n_p, m_p, k_p, v_p, c_s, n_s, m_s, k_s, v_s)
```

```python
import functools
import math

import jax
import jax.numpy as jnp
from jax import lax
from jax.experimental import pallas as pl
from jax.experimental.pallas import tpu as pltpu

F32 = jnp.float32
BF16 = jnp.bfloat16

M_HEADS = 4
A_HEADS_Q = 16
A_HEADS_KV = 4
A_GROUP = A_HEADS_Q // A_HEADS_KV
WINDOW = 128
NUM_BUCKETS = 32
MAX_DISTANCE = 128
FFN_RESIDUAL = 0.5
RMS_EPS = 1e-6
LANES = 128
VMEM_LIMIT = 56 * 1024 * 1024


def _cparams(n_axes):
    return pltpu.CompilerParams(
        dimension_semantics=("arbitrary",) * n_axes, vmem_limit_bytes=VMEM_LIMIT)


def _resident(shape):
    nd = len(shape)
    return pl.BlockSpec(shape, lambda *_: (0,) * nd, pipeline_mode=pl.Buffered(1))


def _rms(x, g):
    return x * lax.rsqrt(jnp.mean(x * x, axis=-1, keepdims=True) + RMS_EPS) * g


def _dot(a, b):
    return jnp.dot(a, b, preferred_element_type=F32)


def _dot_nt(a, b):
    return lax.dot_general(a, b, (((1,), (1,)), ((), ())), preferred_element_type=F32)


def _dot_tn(a, b):
    return lax.dot_general(a, b, (((0,), (0,)), ((), ())), preferred_element_type=F32)


def _row_tile(t, pref):
    tm = min(t, pref)
    assert t % tm == 0
    return tm


def _ffn_body(*refs, d_ff, tf, has_pre):
    if has_pre:
        a_ref, wa_ref, x_ref, g_ref, wgu_ref, wd_ref, o_ref, act_ref = refs
        x = x_ref[...] + _dot(a_ref[...], wa_ref[...])
    else:
        x_ref, g_ref, wgu_ref, wd_ref, o_ref, act_ref = refs
        x = x_ref[...]
    xn = _rms(x, g_ref[...]).astype(BF16)
    for c in range(d_ff // tf):
        lo = c * tf
        gate = _dot(xn, wgu_ref[:, lo:lo + tf])
        up = _dot(xn, wgu_ref[:, d_ff + lo:d_ff + lo + tf])
        act_ref[:, lo:lo + tf] = (gate * jax.nn.sigmoid(gate) * up).astype(BF16)
    o_ref[...] = x + FFN_RESIDUAL * _dot(act_ref[...], wd_ref[...])


def _ffn(x, g, wgu, wd, pre=None):
    t, d = x.shape
    d_ff = wd.shape[0]
    tm = _row_tile(t, 512)
    tf = 256
    assert d_ff % tf == 0
    row = lambda n: pl.BlockSpec((tm, n), lambda i: (i, 0))
    pre_specs, pre_args = [], []
    if pre is not None:
        a, wa = pre
        pre_specs, pre_args = [row(a.shape[1]), _resident(wa.shape)], [a, wa]
    return pl.pallas_call(
        functools.partial(_ffn_body, d_ff=d_ff, tf=tf, has_pre=pre is not None),
        grid=(t // tm,),
        in_specs=pre_specs + [row(d), _resident((1, d)), _resident((d, 2 * d_ff)), _resident((d_ff, d))],
        out_specs=row(d),
        out_shape=jax.ShapeDtypeStruct((t, d), F32),
        scratch_shapes=[pltpu.VMEM((tm, d_ff), BF16)],
        compiler_params=_cparams(1),
        name="ffn",
    )(*pre_args, x, g.reshape(1, d), wgu, wd)


def _mlstm_proj_body(x_ref, g_ref, w_ref, wg_ref, bg_ref,
                     q_ref, k_ref, v_ref, og_ref, gcol_ref, grow_ref, tri_ref, *, qk, vd, dk, chunk):
    h_n = M_HEADS
    xn = _rms(x_ref[...], g_ref[...]).astype(BF16)
    tm = xn.shape[0]

    @pl.when(pl.program_id(0) == 0)
    def _():
        shift = chunk.bit_length() - 1
        r_i = lax.broadcasted_iota(jnp.int32, (tm, tm), 0)
        c_i = lax.broadcasted_iota(jnp.int32, (tm, tm), 1)
        same_chunk = lax.shift_right_logical(r_i, shift) == lax.shift_right_logical(c_i, shift)
        tri_ref[...] = jnp.where((r_i >= c_i) & same_chunk, 1.0, 0.0).astype(BF16)

    gates = _dot(xn, wg_ref[...]) + bg_ref[...]
    q_ref[...] = _dot(xn, w_ref[:, 0:qk]).astype(BF16)
    lane = lax.broadcasted_iota(jnp.int32, (tm, LANES), 1)
    log_f = jnp.where((lane >= h_n) & (lane < 2 * h_n), jax.nn.log_sigmoid(gates), 0.0)
    hi = log_f.astype(BF16)
    lo = (log_f - hi.astype(F32)).astype(BF16)
    k_ref[...] = (_dot(xn, w_ref[:, qk:2 * qk]) * (dk ** -0.5)).astype(BF16)
    csum = _dot(tri_ref[...], jnp.concatenate([hi, lo], axis=1))
    gcol = jnp.where(lane < h_n, gates, csum[:, :LANES] + csum[:, LANES:])
    gcol_ref[...] = gcol
    v_ref[...] = _dot(xn, w_ref[:, 2 * qk:2 * qk + vd]).astype(BF16)
    g_t = gcol.T
    grow_ref[0] = jnp.concatenate([g_t[0:h_n] - g_t[h_n:2 * h_n], g_t[h_n:2 * h_n]], axis=0)
    og_ref[...] = jax.nn.sigmoid(_dot(xn, w_ref[:, 2 * qk + vd:2 * qk + 2 * vd])).astype(BF16)


def _mlstm_proj(x3, g, w_main, w_gates, b_gates, qk, vd, chunk):
    b, s, d = x3.shape
    t = b * s
    tm = _row_tile(s, 512)
    nt = s // tm
    assert chunk & (chunk - 1) == 0 and tm % chunk == 0
    row = lambda n: pl.BlockSpec((tm, n), lambda i: (i, 0))
    return pl.pallas_call(
        functools.partial(_mlstm_proj_body, qk=qk, vd=vd, dk=qk // M_HEADS, chunk=chunk),
        grid=(t // tm,),
        in_specs=[row(d), _resident((1, d)), _resident(w_main.shape),
                  _resident(w_gates.shape), _resident((1, LANES))],
        out_specs=[row(qk), row(qk), row(vd), row(vd), row(LANES),
                   pl.BlockSpec((1, 2 * M_HEADS, tm), lambda i: (i // nt, 0, i % nt))],
        out_shape=[jax.ShapeDtypeStruct((t, qk), BF16), jax.ShapeDtypeStruct((t, qk), BF16),
                   jax.ShapeDtypeStruct((t, vd), BF16), jax.ShapeDtypeStruct((t, vd), BF16),
                   jax.ShapeDtypeStruct((t, LANES), F32),
                   jax.ShapeDtypeStruct((b, 2 * M_HEADS, s), F32)],
        scratch_shapes=[pltpu.VMEM((tm, tm), BF16)],
        compiler_params=_cparams(1),
        name="mlstm_proj",
    )(x3.reshape(t, d), g.reshape(1, d), w_main, w_gates, b_gates)


def _mlstm_cell_body(q_ref, k_ref, v_ref, og_ref, gcol_ref, grow_ref, ng_ref, c0_ref, n0_ref, m0_ref,
                     hg_ref, c_ref, n_ref, m_ref, *, dk, dv, bt):
    h_n = M_HEADS

    @pl.when(pl.program_id(1) == 0)
    def _():
        c_ref[...] = c0_ref[...]
        n_ref[...] = n0_ref[...]
        m_ref[...] = m0_ref[...]

    rows = q_ref.shape[1]
    r_i = lax.broadcasted_iota(jnp.int32, (rows, rows), 0)
    c_i = lax.broadcasted_iota(jnp.int32, (rows, rows), 1)
    causal = r_i >= c_i

    units = [(b, h) for b in range(bt) for h in range(h_n)]
    us = range(len(units))
    q = [q_ref[b, :, h * dk:(h + 1) * dk] for b, h in units]
    k = [k_ref[b, :, h * dk:(h + 1) * dk] for b, h in units]
    v = [v_ref[b, :, h * dv:(h + 1) * dv] for b, h in units]
    c_prev = [c_ref[b, h] for b, h in units]
    n_prev = [n_ref[b, h:h + 1, :] for b, h in units]
    m_prev = [m_ref[b, :, h:h + 1] for b, h in units]
    qk_t = [_dot_nt(q[u], k[u]) for u in us]
    q_c = [_dot(q[u], c_prev[u].astype(BF16)) for u in us]
    b_c = [gcol_ref[b, :, h_n + h:h_n + h + 1] for b, h in units]
    i_c = [gcol_ref[b, :, h:h + 1] for b, h in units]
    log_d = [jnp.where(causal, b_c[u] + grow_ref[b, h:h + 1, :], -jnp.inf)
             for u, (b, h) in enumerate(units)]
    log_inter = [b_c[u] + m_prev[u] for u in us]
    m_t = [jnp.maximum(log_inter[u], jnp.max(log_d[u], axis=-1, keepdims=True)) for u in us]
    inter = [jnp.exp(log_inter[u] - m_t[u]) for u in us]
    s = [qk_t[u] * jnp.exp(log_d[u] - m_t[u]) for u in us]
    s_v = [_dot(s[u].astype(BF16), v[u]) for u in us]
    den = [inter[u] * jnp.sum(q[u].astype(F32) * n_prev[u], axis=-1, keepdims=True)
           + jnp.sum(s[u], axis=-1, keepdims=True) for u in us]
    hid = [(inter[u] * q_c[u] + s_v[u]) / jnp.maximum(jnp.abs(den[u]), jnp.exp(-m_t[u])) for u in us]

    b_last = [b_c[u][rows - 1:rows, :] for u in us]
    log_w = [b_last[u] - b_c[u] + i_c[u] for u in us]
    m_new = [jnp.maximum(b_last[u] + m_prev[u], jnp.max(log_w[u], axis=0, keepdims=True)) for u in us]
    decay = [jnp.exp(b_last[u] + m_prev[u] - m_new[u]) for u in us]
    kw = [jnp.exp(log_w[u] - m_new[u]) * k[u].astype(F32) for u in us]
    kw_v = [_dot_tn(kw[u].astype(BF16), v[u]) for u in us]
    for u, (b, h) in enumerate(units):
        c_ref[b, h] = decay[u] * c_prev[u] + kw_v[u]
        n_ref[b, h:h + 1, :] = decay[u] * n_prev[u] + jnp.sum(kw[u], axis=0, keepdims=True)
        m_ref[b, :, h:h + 1] = m_new[u]
    hid = [hid[u] * lax.rsqrt(jnp.mean(hid[u] * hid[u], axis=-1, keepdims=True) + RMS_EPS) for u in us]
    for u, (b, h) in enumerate(units):
        cols = slice(h * dv, (h + 1) * dv)
        hg_ref[b, :, cols] = (hid[u] * ng_ref[:, cols] * og_ref[b, :, cols].astype(F32)).astype(BF16)


def _mlstm_cell(q, k, v, og, gcol, grow, norm_g, c0, n0, m0, chunk, bt):
    b, s, qk = q.shape
    vd = v.shape[-1]
    dk, dv = qk // M_HEADS, vd // M_HEADS
    nc = s // chunk
    assert b % bt == 0
    tok = lambda n: pl.BlockSpec((bt, chunk, n), lambda i, j: (i, j, 0))
    st_c = pl.BlockSpec((bt, M_HEADS, dk, dv), lambda i, j: (i, 0, 0, 0))
    st_n = pl.BlockSpec((bt, M_HEADS, dk), lambda i, j: (i, 0, 0))
    st_m = pl.BlockSpec((bt, 1, M_HEADS), lambda i, j: (i, 0, 0))
    return pl.pallas_call(
        functools.partial(_mlstm_cell_body, dk=dk, dv=dv, bt=bt),
        grid=(b // bt, nc),
        in_specs=[tok(qk), tok(qk), tok(vd), tok(vd), tok(LANES),
                  pl.BlockSpec((bt, 2 * M_HEADS, chunk), lambda i, j: (i, 0, j)),
                  pl.BlockSpec((1, vd), lambda i, j: (0, 0)), st_c, st_n, st_m],
        out_specs=[tok(vd), st_c, st_n, st_m],
        out_shape=[jax.ShapeDtypeStruct((b, s, vd), BF16),
                   jax.ShapeDtypeStruct((b, M_HEADS, dk, dv), F32),
                   jax.ShapeDtypeStruct((b, M_HEADS, dk), F32),
                   jax.ShapeDtypeStruct((b, 1, M_HEADS), F32)],
        compiler_params=_cparams(2),
        name="mlstm_cell",
    )(q, k, v, og, gcol, grow, norm_g.reshape(1, vd), c0, n0, m0)


def _mlstm_layer(x3, state, p):
    b, s, d = x3.shape
    qk, vd = p["mlstm_qk"], p["mlstm_vd"]
    x = x3.reshape(b * s, d)
    sh = lambda a: a.reshape(b, s, a.shape[-1])
    proj = functools.partial(_mlstm_proj, g=p["mix_norm0"], w_main=p["mlstm_w_main"],
                             w_gates=p["mlstm_w_gates"], b_gates=p["mlstm_b_gates"], qk=qk, vd=vd)
    if state is None:
        dk, dv = qk // M_HEADS, vd // M_HEADS
        c0 = jnp.zeros((b, M_HEADS, dk, dv), F32)
        n0 = jnp.zeros((b, M_HEADS, dk), F32)
        m0 = jnp.zeros((b, 1, M_HEADS), F32)
        chunk = min(s, 128)
        bt = 1
        q, k, v, og, gcol, grow = proj(x3, chunk=chunk)
    else:
        c0, n0, m0 = state
        m0 = m0.reshape(b, 1, M_HEADS)
        chunk = s
        bt = math.gcd(b, 8)
        q, k, v, og, gcol, grow = proj(x.reshape(1, b * s, d), chunk=chunk)
        grow = jnp.swapaxes(grow.reshape(2 * M_HEADS, b, s), 0, 1)
    hg, c1, n1, m1 = _mlstm_cell(sh(q), sh(k), sh(v), sh(og), sh(gcol), grow, p["mlstm_out_norm"],
                                 c0, n0, m0, chunk, bt)
    return (hg.reshape(b * s, vd), p["mlstm_w_out"]), (c1, n1, m1.reshape(b, M_HEADS))


def _attn_proj_body(x_ref, g_ref, w_ref, wvt_ref, qg_ref, kg_ref, q_ref, k_ref, vt_ref, *, dh, nq, nk):
    xn = _rms(x_ref[...], g_ref[...]).astype(BF16)
    r_i = lax.broadcasted_iota(jnp.int32, (LANES, LANES), 0)
    c_i = lax.broadcasted_iota(jnp.int32, (LANES, LANES), 1)
    same_half = jnp.where((r_i < dh) == (c_i < dh), 1.0, 0.0).astype(BF16)

    def half_norm(y, gain):
        sq = y * y
        hi = sq.astype(BF16)
        lo = (sq - hi.astype(F32)).astype(BF16)
        ms = (_dot(hi, same_half) + _dot(lo, same_half)) * (1.0 / dh)
        return y * lax.rsqrt(ms + RMS_EPS) * gain

    q = _dot(xn, w_ref[:, 0:nq])
    for s in range(nq // LANES):
        sl = slice(s * LANES, (s + 1) * LANES)
        q_ref[:, sl] = half_norm(q[:, sl], qg_ref[...]).astype(BF16)
    k = _dot(xn, w_ref[:, nq:nq + nk])
    for s in range(nk // LANES):
        sl = slice(s * LANES, (s + 1) * LANES)
        k_ref[:, sl] = half_norm(k[:, sl], kg_ref[...])
    vt_ref[0] = _dot_nt(wvt_ref[...], xn)


def _attn_proj(x3, g, w, wvt, qg, kg, dh, nq, nk):
    b, s, d = x3.shape
    t = b * s
    nv = wvt.shape[0]
    tm = _row_tile(s, 512)
    nt = s // tm
    row = lambda n: pl.BlockSpec((tm, n), lambda i: (i, 0))
    return pl.pallas_call(
        functools.partial(_attn_proj_body, dh=dh, nq=nq, nk=nk),
        grid=(t // tm,),
        in_specs=[row(d), _resident((1, d)), _resident(w.shape), _resident(wvt.shape),
                  _resident((1, LANES)), _resident((1, LANES))],
        out_specs=[row(nq), row(nk), pl.BlockSpec((1, nv, tm), lambda i: (i // nt, 0, i % nt))],
        out_shape=[jax.ShapeDtypeStruct((t, nq), BF16), jax.ShapeDtypeStruct((t, nk), F32),
                   jax.ShapeDtypeStruct((b, nv, s), F32)],
        compiler_params=_cparams(1),
        name="attn_proj",
    )(x3.reshape(t, d), g.reshape(1, d), w, wvt, qg, kg)


def _bias_body(rel_ref, o_ref, *, tq, key_major):
    variant = pl.program_id(0)
    h = pl.program_id(1)
    nk = 2 * WINDOW
    shape, q_ax = ((nk, tq), 1) if key_major else ((tq, nk), 0)
    qi = lax.broadcasted_iota(jnp.int32, shape, q_ax)
    ki = lax.broadcasted_iota(jnp.int32, shape, 1 - q_ax)
    dist = qi + WINDOW - ki
    max_exact = NUM_BUCKETS // 2
    d = jnp.maximum(dist, 0)
    log_ratio = (jnp.log(jnp.maximum(d, 1).astype(F32) / max_exact)
                 / math.log(MAX_DISTANCE / max_exact))
    large = jnp.minimum(max_exact + (log_ratio * (NUM_BUCKETS - max_exact)).astype(jnp.int32),
                        NUM_BUCKETS - 1)
    bucket = jnp.where(d < max_exact, d, large)
    bias = jnp.zeros(shape, F32)
    for bkt in range(NUM_BUCKETS):
        bias = jnp.where(bucket == bkt, rel_ref[bkt, h], bias)
    visible = (dist >= 0) & (dist < WINDOW) & ((ki >= WINDOW) | (variant > 0))
    o_ref[0, 0] = jnp.where(visible, bias, -jnp.inf)


def _bias_table(rel_bias, tq, key_major):
    nk = 2 * WINDOW
    if key_major:
        spec = pl.BlockSpec((1, 1, nk, tq), lambda v, h: (v, h // 2, 0, h % 2))
        shape = (2, A_HEADS_Q // 2, nk, 2 * tq)
    else:
        spec = pl.BlockSpec((1, 1, tq, nk), lambda v, h: (v, h, 0, 0))
        shape = (2, A_HEADS_Q, tq, nk)
    return pl.pallas_call(
        functools.partial(_bias_body, tq=tq, key_major=key_major),
        grid=(2, A_HEADS_Q),
        in_specs=[pl.BlockSpec(memory_space=pltpu.SMEM)],
        out_specs=spec,
        out_shape=jax.ShapeDtypeStruct(shape, F32),
        compiler_params=_cparams(2),
        name="rel_bias_table",
    )(rel_bias)


def _attn_prompt_body(sink_ref, q_ref, kp_ref, ko_ref, vp_ref, vo_ref, bias_ref, o_ref, *, dh):
    blk = WINDOW
    lane = lax.broadcasted_iota(jnp.int32, (blk, LANES), 1)
    lo_half = lane < dh
    first = lax.broadcasted_iota(jnp.int32, (1, 2 * blk), 1) < blk
    ones = jnp.ones((dh, 2 * blk), BF16)
    slots_per_kv = A_GROUP // 2
    n_slots = A_HEADS_Q // 2
    vts, sts = [], []
    for g in range(A_HEADS_KV):
        sl = slice(g * LANES, (g + 1) * LANES)
        rows = slice(g * dh, (g + 1) * dh)
        kk = jnp.concatenate([kp_ref[0, :, sl], ko_ref[0, :, sl]], axis=0).astype(BF16)
        vt = jnp.concatenate([vp_ref[0, rows, :], vo_ref[0, rows, :]], axis=1).astype(BF16)
        vts.append(jnp.concatenate([vt, ones], axis=0))
        for pr in range(slots_per_kv):
            slot_i = g * slots_per_kv + pr
            slot = q_ref[0, :, slot_i * LANES:(slot_i + 1) * LANES]
            zero = jnp.zeros_like(slot)
            q2 = jnp.concatenate([jnp.where(lo_half, slot, zero), jnp.where(lo_half, zero, slot)],
                                 axis=0)
            sts.append(_dot_nt(kk, q2) + bias_ref[0, slot_i])
    sinks = [jnp.where(first, sink_ref[2 * i], sink_ref[2 * i + 1]) for i in range(n_slots)]
    ms = [jnp.maximum(jnp.max(sts[i], axis=0, keepdims=True), sinks[i]) for i in range(n_slots)]
    pts = [jnp.exp(sts[i] - ms[i]).astype(BF16) for i in range(n_slots)]
    oes = [_dot(vts[i // slots_per_kv], pts[i]) for i in range(n_slots)]
    heads_t = []
    for i in range(n_slots):
        ot = oes[i][0:dh, :] / (oes[i][dh:dh + 1, :] + jnp.exp(sinks[i] - ms[i]))
        heads_t += [ot[:, 0:blk], ot[:, blk:2 * blk]]
    o_ref[0] = jnp.concatenate(heads_t, axis=0).T.astype(BF16)


def _attn_prompt(q, k, vt, bias, sinks, dh):
    b, s, qd = q.shape
    kd = k.shape[-1]
    vd = vt.shape[1]
    blk = WINDOW
    own = lambda n: pl.BlockSpec((1, blk, n), lambda i, j: (i, j, 0))
    prev = lambda n: pl.BlockSpec((1, blk, n), lambda i, j: (i, jnp.maximum(j - 1, 0), 0))
    return pl.pallas_call(
        functools.partial(_attn_prompt_body, dh=dh),
        grid=(b, s // blk),
        in_specs=[pl.BlockSpec(memory_space=pltpu.SMEM),
                  own(qd), prev(kd), own(kd),
                  pl.BlockSpec((1, vd, blk), lambda i, j: (i, 0, jnp.maximum(j - 1, 0))),
                  pl.BlockSpec((1, vd, blk), lambda i, j: (i, 0, j)),
                  pl.BlockSpec((1,) + bias.shape[1:], lambda i, j: (jnp.minimum(j, 1), 0, 0, 0))],
        out_specs=own(qd),
        out_shape=jax.ShapeDtypeStruct((b, s, qd), BF16),
        compiler_params=_cparams(2),
        name="attn_prompt",
    )(sinks, q, k, k, vt, vt, bias)


def _attn_sample_body(sink_ref, q_ref, kc_ref, kn_ref, vc_ref, vn_ref, bias_ref,
                      o_ref, ko_ref, vo_ref, *, dh, bt):
    tq = kn_ref.shape[1]
    heads_per_slab = 2 * A_GROUP
    n_slabs = A_HEADS_Q // heads_per_slab
    sinks = [jnp.concatenate([jnp.full((tq, 1), sink_ref[sl * heads_per_slab + hh], F32)
                              for hh in range(heads_per_slab)], axis=0) for sl in range(n_slabs)]

    units = [(b, sl) for b in range(bt) for sl in range(n_slabs)]
    lanes = lambda sl: slice(sl * LANES, (sl + 1) * LANES)
    qs = [jnp.concatenate([q_ref[b, :, (sl * heads_per_slab + hh) * LANES:(sl * heads_per_slab + hh + 1) * LANES]
                           for hh in range(heads_per_slab)], axis=0) for b, sl in units]
    sp = [_dot_nt(qs[u], kc_ref[b, :, lanes(sl)].astype(BF16)) + bias_ref[sl][:, :WINDOW]
          for u, (b, sl) in enumerate(units)]
    so = [_dot_nt(qs[u], kn_ref[b, :, lanes(sl)].astype(BF16)) + bias_ref[sl][:, WINDOW:WINDOW + tq]
          for u, (b, sl) in enumerate(units)]
    m = [jnp.maximum(jnp.maximum(jnp.max(sp[u], axis=-1, keepdims=True),
                                 jnp.max(so[u], axis=-1, keepdims=True)), sinks[sl])
         for u, (b, sl) in enumerate(units)]
    pp = [jnp.exp(sp[u] - m[u]) for u in range(len(units))]
    po = [jnp.exp(so[u] - m[u]) for u in range(len(units))]
    den = [jnp.sum(pp[u], axis=-1, keepdims=True) + jnp.sum(po[u], axis=-1, keepdims=True)
           + jnp.exp(sinks[sl] - m[u]) for u, (b, sl) in enumerate(units)]
    out = [_dot((pp[u] / den[u]).astype(BF16), vc_ref[b, :, lanes(sl)].astype(BF16))
           + _dot((po[u] / den[u]).astype(BF16), vn_ref[b, :, lanes(sl)].astype(BF16))
           for u, (b, sl) in enumerate(units)]
    for u, (b, sl) in enumerate(units):
        for hh in range(heads_per_slab):
            h = sl * heads_per_slab + hh
            o_ref[b, :, h * LANES:(h + 1) * LANES] = out[u][hh * tq:(hh + 1) * tq].astype(BF16)
    ko_ref[:, 0:WINDOW - tq, :] = kc_ref[:, tq:WINDOW, :]
    ko_ref[:, WINDOW - tq:WINDOW, :] = kn_ref[...]
    vo_ref[:, 0:WINDOW - tq, :] = vc_ref[:, tq:WINDOW, :]
    vo_ref[:, WINDOW - tq:WINDOW, :] = vn_ref[...]


def _attn_sample(q, k, v, k_cache, v_cache, bias, sinks, dh):
    b, s, qd = q.shape
    kd = k.shape[-1]
    bt = math.gcd(b, 8)
    new = lambda n: pl.BlockSpec((bt, s, n), lambda i: (i, 0, 0))
    win = pl.BlockSpec((bt, WINDOW, kd), lambda i: (i, 0, 0))
    return pl.pallas_call(
        functools.partial(_attn_sample_body, dh=dh, bt=bt),
        grid=(b // bt,),
        in_specs=[pl.BlockSpec(memory_space=pltpu.SMEM),
                  new(qd), win, new(kd), win, new(kd),
                  pl.BlockSpec(bias.shape, lambda i: (0, 0, 0))],
        out_specs=[new(qd), win, win],
        out_shape=[jax.ShapeDtypeStruct((b, s, qd), BF16),
                   jax.ShapeDtypeStruct((b, WINDOW, kd), F32),
                   jax.ShapeDtypeStruct((b, WINDOW, kd), F32)],
        compiler_params=_cparams(1),
        name="attn_sample",
    )(sinks, q, k_cache, k, v_cache, v, bias)


def _swa_layer(x3, buffers, p):
    b, s, d = x3.shape
    dh = p["attn_dh"]
    kd = A_HEADS_KV * dh
    x = x3.reshape(b * s, d)
    kv_shape = (b, WINDOW, A_HEADS_KV, dh)
    if buffers is None:
        nq, nk = A_HEADS_Q * dh, A_HEADS_KV * LANES
        q, k, vt = _attn_proj(x3, p["mix_norm1"], p["attn_w_prompt"], p["attn_wvt"],
                              p["attn_qg"], p["attn_kg"], dh, nq, nk)
        k = k.reshape(b, s, nk)
        bias = _bias_table(p["rel_bias"], WINDOW, key_major=True)
        att = _attn_prompt(q.reshape(b, s, nq), k, vt, bias, p["attn_sinks"], dh)
        new_k = k[:, s - WINDOW:].reshape(b, WINDOW, A_HEADS_KV, LANES)[..., :dh]
        new_v = jnp.swapaxes(vt[:, :, s - WINDOW:], 1, 2)
        w_out = p["attn_w_out"]
    else:
        k_buf, v_buf = buffers
        nq, nk = A_HEADS_Q * LANES, kd
        q, k, vt = _attn_proj(x.reshape(1, b * s, d), p["mix_norm1"], p["attn_w_sample"], p["attn_wvt"],
                              p["attn_qg"], p["attn_kg"], dh, nq, nk)
        v = vt[0].T
        sh = lambda a: a.reshape(b, s, a.shape[-1])
        bias = _bias_table(p["rel_bias"], s, key_major=False)[1].reshape(2, 2 * A_GROUP * s, 2 * WINDOW)
        att, new_k, new_v = _attn_sample(sh(q), sh(k), sh(v), k_buf.reshape(b, WINDOW, kd),
                                         v_buf.reshape(b, WINDOW, kd), bias, p["attn_sinks"], dh)
        w_out = p["attn_w_out_slots"]
    return (att.reshape(b * s, nq), w_out), (new_k.reshape(kv_shape), new_v.reshape(kv_shape))


def _attn_weights(w_in, w_out, q_norm, k_norm, dh):
    d = w_in.shape[0]
    qd, kd = A_HEADS_Q * dh, A_HEADS_KV * dh
    wq = w_in[:, :qd].reshape(d, A_HEADS_Q, dh)
    wk = w_in[:, qd:qd + kd].reshape(d, A_HEADS_KV, dh)
    k_dup = jnp.concatenate([wk, wk], axis=-1).reshape(d, A_HEADS_KV * LANES)
    w_prompt = jnp.concatenate([w_in[:, :qd], k_dup], axis=1).astype(BF16)
    zero = jnp.zeros_like(wq)
    odd = ((jnp.arange(A_HEADS_Q) // A_GROUP) % 2 == 1)[None, :, None]
    wq_slots = jnp.concatenate([jnp.where(odd, zero, wq), jnp.where(odd, wq, zero)], axis=-1)
    w_sample = jnp.concatenate([wq_slots.reshape(d, A_HEADS_Q * LANES), w_in[:, qd:qd + kd]],
                               axis=1).astype(BF16)
    wo = w_out.reshape(A_HEADS_Q, dh, d)
    zo = jnp.zeros_like(wo)
    wo_slots = jnp.concatenate([jnp.where(odd.reshape(-1, 1, 1), zo, wo),
                                jnp.where(odd.reshape(-1, 1, 1), wo, zo)], axis=1)
    qg = jnp.concatenate([q_norm, q_norm]).reshape(1, LANES) * (dh ** -0.5)
    kg = jnp.concatenate([k_norm, k_norm]).reshape(1, LANES)
    return {"attn_w_prompt": w_prompt, "attn_w_sample": w_sample,
            "attn_wvt": w_in[:, qd + kd:].T.astype(BF16),
            "attn_w_out": w_out.astype(BF16),
            "attn_w_out_slots": wo_slots.reshape(A_HEADS_Q * LANES, d).astype(BF16),
            "attn_qg": qg, "attn_kg": kg}


def _trunk(x3, mlstm_state, swa_buffers, p):
    b, s, d = x3.shape
    ffn = lambda x, name, layer, pre=None: _ffn(
        x, p[name + "_norm"][layer], p[name + "_wgu"][layer], p[name + "_wd"][layer], pre)
    x = ffn(x3.reshape(b * s, d), "ffn1", 0)
    mixed, new_mlstm = _mlstm_layer(x.reshape(b, s, d), mlstm_state, p)
    x = ffn(x, "ffn2", 0, mixed)
    x = ffn(x, "ffn1", 1)
    mixed, new_swa = _swa_layer(x.reshape(b, s, d), swa_buffers, p)
    x = ffn(x, "ffn2", 1, mixed)
    return x.reshape(b, s, d), new_mlstm, new_swa


def kernel(x_prompt, x_sample, state_mlstm_C, state_mlstm_n, state_mlstm_m, cache_swa_k, cache_swa_v,
           ffn1_norm, ffn1_w_gate_up, ffn1_w_down, mix_norm, ffn2_norm, ffn2_w_gate_up, ffn2_w_down,
           mlstm_w_in, mlstm_b_gates, mlstm_out_norm, mlstm_w_out,
           attn_w_in, attn_q_norm, attn_k_norm, attn_sinks, rel_bias, attn_w_out):
    d = x_prompt.shape[-1]
    vd = mlstm_w_out.shape[0]
    n_gates = 2 * M_HEADS
    qk = (mlstm_w_in.shape[1] - 2 * vd - n_gates) // 2
    dh = attn_q_norm.shape[0]
    assert 2 * dh == LANES
    w_gates = jnp.zeros((d, LANES), BF16).at[:, :n_gates].set(mlstm_w_in[:, 2 * qk + 2 * vd:].astype(BF16))
    b_gates = jnp.zeros((1, LANES), F32).at[0, :n_gates].set(mlstm_b_gates)
    p = {
        "ffn1_norm": ffn1_norm, "ffn2_norm": ffn2_norm,
        "ffn1_wgu": ffn1_w_gate_up.astype(BF16), "ffn1_wd": ffn1_w_down.astype(BF16),
        "ffn2_wgu": ffn2_w_gate_up.astype(BF16), "ffn2_wd": ffn2_w_down.astype(BF16),
        "mix_norm0": mix_norm[0], "mix_norm1": mix_norm[1],
        "mlstm_qk": qk, "mlstm_vd": vd,
        "mlstm_w_main": mlstm_w_in[:, :2 * qk + 2 * vd].astype(BF16),
        "mlstm_w_gates": w_gates, "mlstm_b_gates": b_gates,
        "mlstm_out_norm": mlstm_out_norm, "mlstm_w_out": mlstm_w_out.astype(BF16),
        "attn_dh": dh, "attn_sinks": attn_sinks, "rel_bias": rel_bias,
    }
    p.update(_attn_weights(attn_w_in, attn_w_out, attn_q_norm, attn_k_norm, dh))
    y_p, (c_p, n_p, m_p), (k_p, v_p) = _trunk(x_prompt, None, None, p)
    y_s, (c_s, n_s, m_s), (k_s, v_s) = _trunk(
        x_sample, (state_mlstm_C, state_mlstm_n, state_mlstm_m), (cache_swa_k, cache_swa_v), p)
    return (y_p, y_s, c_p, n_p, m_p, k_p, v_p, c_s, n_s, m_s, k_s, v_s)
```

```python
import functools
import math

import jax
import jax.numpy as jnp
from jax import lax
from jax.experimental import pallas as pl
from jax.experimental.pallas import tpu as pltpu

F32 = jnp.float32
BF16 = jnp.bfloat16

M_HEADS = 4
A_HEADS_Q = 16
A_HEADS_KV = 4
A_GROUP = A_HEADS_Q // A_HEADS_KV
WINDOW = 128
NUM_BUCKETS = 32
MAX_DISTANCE = 128
FFN_RESIDUAL = 0.5
RMS_EPS = 1e-6
LANES = 128
VMEM_LIMIT = 56 * 1024 * 1024


def _cparams(n_axes):
    return pltpu.CompilerParams(
        dimension_semantics=("arbitrary",) * n_axes, vmem_limit_bytes=VMEM_LIMIT)


def _resident(shape):
    nd = len(shape)
    return pl.BlockSpec(shape, lambda *_: (0,) * nd, pipeline_mode=pl.Buffered(1))


def _rms(x, g):
    return x * lax.rsqrt(jnp.mean(x * x, axis=-1, keepdims=True) + RMS_EPS) * g


def _dot(a, b):
    return jnp.dot(a, b, preferred_element_type=F32)


def _dot_nt(a, b):
    return lax.dot_general(a, b, (((1,), (1,)), ((), ())), preferred_element_type=F32)


def _dot_tn(a, b):
    return lax.dot_general(a, b, (((0,), (0,)), ((), ())), preferred_element_type=F32)


def _row_tile(t, pref):
    tm = min(t, pref)
    assert t % tm == 0
    return tm


def _interleave(main, side):
    done = 0
    for j, thunk in enumerate(main):
        thunk()
        upto = (j + 1) * len(side) // len(main)
        for other in side[done:upto]:
            other()
        done = upto


def _ffn_items(get_x, g_ref, wgu_ref, wd_ref, o_ref, act_ref, d_ff, tf):
    st = {}

    def head():
        y = get_x()
        o_ref[...] = y
        st["xn"] = _rms(y, g_ref[...]).astype(BF16)

    def chunk(lo):
        gate = _dot(st["xn"], wgu_ref[:, lo:lo + tf])
        up = _dot(st["xn"], wgu_ref[:, d_ff + lo:d_ff + lo + tf])
        act_ref[:, lo:lo + tf] = (gate * jax.nn.sigmoid(gate) * up).astype(BF16)

    def tail():
        o_ref[...] = o_ref[...] + FFN_RESIDUAL * _dot(act_ref[...], wd_ref[...])

    return [head] + [functools.partial(chunk, lo) for lo in range(0, d_ff, tf)] + [tail]


def _ffn_body(*refs, d_ff, tf, has_pre):
    if has_pre:
        a_ref, wa_ref, x_ref, g_ref, wgu_ref, wd_ref, o_ref, act_ref = refs
        get_x = lambda: x_ref[...] + _dot(a_ref[...], wa_ref[...])
    else:
        x_ref, g_ref, wgu_ref, wd_ref, o_ref, act_ref = refs
        get_x = lambda: x_ref[...]
    for thunk in _ffn_items(get_x, g_ref, wgu_ref, wd_ref, o_ref, act_ref, d_ff, tf):
        thunk()


FFN_TF = 256


def _ffn(x, g, wgu, wd, pre=None):
    t, d = x.shape
    d_ff = wd.shape[0]
    tm = _row_tile(t, 512)
    assert d_ff % FFN_TF == 0
    row = lambda n: pl.BlockSpec((tm, n), lambda i: (i, 0))
    pre_specs, pre_args = [], []
    if pre is not None:
        a, wa = pre
        pre_specs, pre_args = [row(a.shape[1]), _resident(wa.shape)], [a, wa]
    return pl.pallas_call(
        functools.partial(_ffn_body, d_ff=d_ff, tf=FFN_TF, has_pre=pre is not None),
        grid=(t // tm,),
        in_specs=pre_specs + [row(d), _resident((1, d)), _resident((d, 2 * d_ff)), _resident((d_ff, d))],
        out_specs=row(d),
        out_shape=jax.ShapeDtypeStruct((t, d), F32),
        scratch_shapes=[pltpu.VMEM((tm, d_ff), BF16)],
        compiler_params=_cparams(1),
        name="ffn",
    )(*pre_args, x, g.reshape(1, d), wgu, wd)


def _mlstm_proj_body(x_ref, g_ref, w_ref, wg_ref, bg_ref,
                     q_ref, k_ref, v_ref, og_ref, gcol_ref, grow_ref, tri_ref, *, qk, vd, dk, chunk):
    h_n = M_HEADS
    xn = _rms(x_ref[...], g_ref[...]).astype(BF16)
    tm = xn.shape[0]

    @pl.when(pl.program_id(0) == 0)
    def _():
        shift = chunk.bit_length() - 1
        r_i = lax.broadcasted_iota(jnp.int32, (tm, tm), 0)
        c_i = lax.broadcasted_iota(jnp.int32, (tm, tm), 1)
        same_chunk = lax.shift_right_logical(r_i, shift) == lax.shift_right_logical(c_i, shift)
        tri_ref[...] = jnp.where((r_i >= c_i) & same_chunk, 1.0, 0.0).astype(BF16)

    gates = _dot(xn, wg_ref[...]) + bg_ref[...]
    q_ref[...] = _dot(xn, w_ref[:, 0:qk]).astype(BF16)
    lane = lax.broadcasted_iota(jnp.int32, (tm, LANES), 1)
    log_f = jnp.where((lane >= h_n) & (lane < 2 * h_n), jax.nn.log_sigmoid(gates), 0.0)
    hi = log_f.astype(BF16)
    lo = (log_f - hi.astype(F32)).astype(BF16)
    k_ref[...] = (_dot(xn, w_ref[:, qk:2 * qk]) * (dk ** -0.5)).astype(BF16)
    csum = _dot(tri_ref[...], jnp.concatenate([hi, lo], axis=1))
    gcol = jnp.where(lane < h_n, gates, csum[:, :LANES] + csum[:, LANES:])
    gcol_ref[...] = gcol
    v_ref[...] = _dot(xn, w_ref[:, 2 * qk:2 * qk + vd]).astype(BF16)
    g_t = gcol.T
    grow_ref[0] = jnp.concatenate([g_t[0:h_n] - g_t[h_n:2 * h_n], g_t[h_n:2 * h_n]], axis=0)
    og_ref[...] = jax.nn.sigmoid(_dot(xn, w_ref[:, 2 * qk + vd:2 * qk + 2 * vd])).astype(BF16)


def _mlstm_proj(x3, g, w_main, w_gates, b_gates, qk, vd, chunk):
    b, s, d = x3.shape
    t = b * s
    tm = _row_tile(s, 512)
    nt = s // tm
    assert chunk & (chunk - 1) == 0 and tm % chunk == 0
    row = lambda n: pl.BlockSpec((tm, n), lambda i: (i, 0))
    return pl.pallas_call(
        functools.partial(_mlstm_proj_body, qk=qk, vd=vd, dk=qk // M_HEADS, chunk=chunk),
        grid=(t // tm,),
        in_specs=[row(d), _resident((1, d)), _resident(w_main.shape),
                  _resident(w_gates.shape), _resident((1, LANES))],
        out_specs=[row(qk), row(qk), row(vd), row(vd), row(LANES),
                   pl.BlockSpec((1, 2 * M_HEADS, tm), lambda i: (i // nt, 0, i % nt))],
        out_shape=[jax.ShapeDtypeStruct((t, qk), BF16), jax.ShapeDtypeStruct((t, qk), BF16),
                   jax.ShapeDtypeStruct((t, vd), BF16), jax.ShapeDtypeStruct((t, vd), BF16),
                   jax.ShapeDtypeStruct((t, LANES), F32),
                   jax.ShapeDtypeStruct((b, 2 * M_HEADS, s), F32)],
        scratch_shapes=[pltpu.VMEM((tm, tm), BF16)],
        compiler_params=_cparams(1),
        name="mlstm_proj",
    )(x3.reshape(t, d), g.reshape(1, d), w_main, w_gates, b_gates)


def _mlstm_cell_body(q_ref, k_ref, v_ref, og_ref, gcol_ref, grow_ref, ng_ref, c0_ref, n0_ref, m0_ref,
                     hg_ref, c_ref, n_ref, m_ref, *, dk, dv, bt):
    h_n = M_HEADS

    @pl.when(pl.program_id(1) == 0)
    def _():
        c_ref[...] = c0_ref[...]
        n_ref[...] = n0_ref[...]
        m_ref[...] = m0_ref[...]

    rows = q_ref.shape[1]
    r_i = lax.broadcasted_iota(jnp.int32, (rows, rows), 0)
    c_i = lax.broadcasted_iota(jnp.int32, (rows, rows), 1)
    causal = r_i >= c_i

    units = [(b, h) for b in range(bt) for h in range(h_n)]
    us = range(len(units))
    q = [q_ref[b, :, h * dk:(h + 1) * dk] for b, h in units]
    k = [k_ref[b, :, h * dk:(h + 1) * dk] for b, h in units]
    v = [v_ref[b, :, h * dv:(h + 1) * dv] for b, h in units]
    c_prev = [c_ref[b, h] for b, h in units]
    n_prev = [n_ref[b, h:h + 1, :] for b, h in units]
    m_prev = [m_ref[b, :, h:h + 1] for b, h in units]
    qk_t = [_dot_nt(q[u], k[u]) for u in us]
    q_c = [_dot(q[u], c_prev[u].astype(BF16)) for u in us]
    b_c = [gcol_ref[b, :, h_n + h:h_n + h + 1] for b, h in units]
    i_c = [gcol_ref[b, :, h:h + 1] for b, h in units]
    log_d = [jnp.where(causal, b_c[u] + grow_ref[b, h:h + 1, :], -jnp.inf)
             for u, (b, h) in enumerate(units)]
    log_inter = [b_c[u] + m_prev[u] for u in us]
    m_t = [jnp.maximum(log_inter[u], jnp.max(log_d[u], axis=-1, keepdims=True)) for u in us]
    inter = [jnp.exp(log_inter[u] - m_t[u]) for u in us]
    s = [qk_t[u] * jnp.exp(log_d[u] - m_t[u]) for u in us]
    s_v = [_dot(s[u].astype(BF16), v[u]) for u in us]
    den = [inter[u] * jnp.sum(q[u].astype(F32) * n_prev[u], axis=-1, keepdims=True)
           + jnp.sum(s[u], axis=-1, keepdims=True) for u in us]
    hid = [(inter[u] * q_c[u] + s_v[u]) / jnp.maximum(jnp.abs(den[u]), jnp.exp(-m_t[u])) for u in us]

    b_last = [b_c[u][rows - 1:rows, :] for u in us]
    log_w = [b_last[u] - b_c[u] + i_c[u] for u in us]
    m_new = [jnp.maximum(b_last[u] + m_prev[u], jnp.max(log_w[u], axis=0, keepdims=True)) for u in us]
    decay = [jnp.exp(b_last[u] + m_prev[u] - m_new[u]) for u in us]
    kw = [jnp.exp(log_w[u] - m_new[u]) * k[u].astype(F32) for u in us]
    kw_v = [_dot_tn(kw[u].astype(BF16), v[u]) for u in us]
    for u, (b, h) in enumerate(units):
        c_ref[b, h] = decay[u] * c_prev[u] + kw_v[u]
        n_ref[b, h:h + 1, :] = decay[u] * n_prev[u] + jnp.sum(kw[u], axis=0, keepdims=True)
        m_ref[b, :, h:h + 1] = m_new[u]
    hid = [hid[u] * lax.rsqrt(jnp.mean(hid[u] * hid[u], axis=-1, keepdims=True) + RMS_EPS) for u in us]
    for u, (b, h) in enumerate(units):
        cols = slice(h * dv, (h + 1) * dv)
        hg_ref[b, :, cols] = (hid[u] * ng_ref[:, cols] * og_ref[b, :, cols].astype(F32)).astype(BF16)


def _mlstm_cell(q, k, v, og, gcol, grow, norm_g, c0, n0, m0, chunk, bt):
    b, s, qk = q.shape
    vd = v.shape[-1]
    dk, dv = qk // M_HEADS, vd // M_HEADS
    nc = s // chunk
    assert b % bt == 0
    tok = lambda n: pl.BlockSpec((bt, chunk, n), lambda i, j: (i, j, 0))
    st_c = pl.BlockSpec((bt, M_HEADS, dk, dv), lambda i, j: (i, 0, 0, 0))
    st_n = pl.BlockSpec((bt, M_HEADS, dk), lambda i, j: (i, 0, 0))
    st_m = pl.BlockSpec((bt, 1, M_HEADS), lambda i, j: (i, 0, 0))
    return pl.pallas_call(
        functools.partial(_mlstm_cell_body, dk=dk, dv=dv, bt=bt),
        grid=(b // bt, nc),
        in_specs=[tok(qk), tok(qk), tok(vd), tok(vd), tok(LANES),
                  pl.BlockSpec((bt, 2 * M_HEADS, chunk), lambda i, j: (i, 0, j)),
                  pl.BlockSpec((1, vd), lambda i, j: (0, 0)), st_c, st_n, st_m],
        out_specs=[tok(vd), st_c, st_n, st_m],
        out_shape=[jax.ShapeDtypeStruct((b, s, vd), BF16),
                   jax.ShapeDtypeStruct((b, M_HEADS, dk, dv), F32),
                   jax.ShapeDtypeStruct((b, M_HEADS, dk), F32),
                   jax.ShapeDtypeStruct((b, 1, M_HEADS), F32)],
        compiler_params=_cparams(2),
        name="mlstm_cell",
    )(q, k, v, og, gcol, grow, norm_g.reshape(1, vd), c0, n0, m0)


def _mlstm_layer(x3, state, p):
    b, s, d = x3.shape
    qk, vd = p["mlstm_qk"], p["mlstm_vd"]
    x = x3.reshape(b * s, d)
    sh = lambda a: a.reshape(b, s, a.shape[-1])
    proj = functools.partial(_mlstm_proj, g=p["mix_norm0"], w_main=p["mlstm_w_main"],
                             w_gates=p["mlstm_w_gates"], b_gates=p["mlstm_b_gates"], qk=qk, vd=vd)
    if state is None:
        dk, dv = qk // M_HEADS, vd // M_HEADS
        c0 = jnp.zeros((b, M_HEADS, dk, dv), F32)
        n0 = jnp.zeros((b, M_HEADS, dk), F32)
        m0 = jnp.zeros((b, 1, M_HEADS), F32)
        chunk = min(s, 128)
        bt = 1
        q, k, v, og, gcol, grow = proj(x3, chunk=chunk)
    else:
        c0, n0, m0 = state
        m0 = m0.reshape(b, 1, M_HEADS)
        chunk = s
        bt = math.gcd(b, 8)
        q, k, v, og, gcol, grow = proj(x.reshape(1, b * s, d), chunk=chunk)
        grow = jnp.swapaxes(grow.reshape(2 * M_HEADS, b, s), 0, 1)
    hg, c1, n1, m1 = _mlstm_cell(sh(q), sh(k), sh(v), sh(og), sh(gcol), grow, p["mlstm_out_norm"],
                                 c0, n0, m0, chunk, bt)
    return (hg.reshape(b * s, vd), p["mlstm_w_out"]), (c1, n1, m1.reshape(b, M_HEADS))


def _attn_proj_body(x_ref, g_ref, w_ref, wvt_ref, qg_ref, kg_ref, q_ref, k_ref, vt_ref, *, dh, nq, nk):
    xn = _rms(x_ref[...], g_ref[...]).astype(BF16)
    r_i = lax.broadcasted_iota(jnp.int32, (LANES, LANES), 0)
    c_i = lax.broadcasted_iota(jnp.int32, (LANES, LANES), 1)
    same_half = jnp.where((r_i < dh) == (c_i < dh), 1.0, 0.0).astype(BF16)

    def half_norm(y, gain):
        sq = y * y
        hi = sq.astype(BF16)
        lo = (sq - hi.astype(F32)).astype(BF16)
        ms = (_dot(hi, same_half) + _dot(lo, same_half)) * (1.0 / dh)
        return y * lax.rsqrt(ms + RMS_EPS) * gain

    q = _dot(xn, w_ref[:, 0:nq])
    for s in range(nq // LANES):
        sl = slice(s * LANES, (s + 1) * LANES)
        q_ref[:, sl] = half_norm(q[:, sl], qg_ref[...]).astype(BF16)
    k = _dot(xn, w_ref[:, nq:nq + nk])
    for s in range(nk // LANES):
        sl = slice(s * LANES, (s + 1) * LANES)
        k_ref[:, sl] = half_norm(k[:, sl], kg_ref[...])
    vt_ref[0] = _dot_nt(wvt_ref[...], xn)


def _attn_proj(x3, g, w, wvt, qg, kg, dh, nq, nk):
    b, s, d = x3.shape
    t = b * s
    nv = wvt.shape[0]
    tm = _row_tile(s, 512)
    nt = s // tm
    row = lambda n: pl.BlockSpec((tm, n), lambda i: (i, 0))
    return pl.pallas_call(
        functools.partial(_attn_proj_body, dh=dh, nq=nq, nk=nk),
        grid=(t // tm,),
        in_specs=[row(d), _resident((1, d)), _resident(w.shape), _resident(wvt.shape),
                  _resident((1, LANES)), _resident((1, LANES))],
        out_specs=[row(nq), row(nk), pl.BlockSpec((1, nv, tm), lambda i: (i // nt, 0, i % nt))],
        out_shape=[jax.ShapeDtypeStruct((t, nq), BF16), jax.ShapeDtypeStruct((t, nk), F32),
                   jax.ShapeDtypeStruct((b, nv, s), F32)],
        compiler_params=_cparams(1),
        name="attn_proj",
    )(x3.reshape(t, d), g.reshape(1, d), w, wvt, qg, kg)


def _bias_body(rel_ref, o_ref, *, tq, key_major):
    variant = pl.program_id(0)
    h = pl.program_id(1)
    nk = 2 * WINDOW
    shape, q_ax = ((nk, tq), 1) if key_major else ((tq, nk), 0)
    qi = lax.broadcasted_iota(jnp.int32, shape, q_ax)
    ki = lax.broadcasted_iota(jnp.int32, shape, 1 - q_ax)
    dist = qi + WINDOW - ki
    max_exact = NUM_BUCKETS // 2
    d = jnp.maximum(dist, 0)
    log_ratio = (jnp.log(jnp.maximum(d, 1).astype(F32) / max_exact)
                 / math.log(MAX_DISTANCE / max_exact))
    large = jnp.minimum(max_exact + (log_ratio * (NUM_BUCKETS - max_exact)).astype(jnp.int32),
                        NUM_BUCKETS - 1)
    bucket = jnp.where(d < max_exact, d, large)
    bias = jnp.zeros(shape, F32)
    for bkt in range(NUM_BUCKETS):
        bias = jnp.where(bucket == bkt, rel_ref[bkt, h], bias)
    visible = (dist >= 0) & (dist < WINDOW) & ((ki >= WINDOW) | (variant > 0))
    o_ref[0, 0] = jnp.where(visible, bias, -jnp.inf)


def _bias_table(rel_bias, tq, key_major):
    nk = 2 * WINDOW
    if key_major:
        spec = pl.BlockSpec((1, 1, nk, tq), lambda v, h: (v, h // 2, 0, h % 2))
        shape = (2, A_HEADS_Q // 2, nk, 2 * tq)
    else:
        spec = pl.BlockSpec((1, 1, tq, nk), lambda v, h: (v, h, 0, 0))
        shape = (2, A_HEADS_Q, tq, nk)
    return pl.pallas_call(
        functools.partial(_bias_body, tq=tq, key_major=key_major),
        grid=(2, A_HEADS_Q),
        in_specs=[pl.BlockSpec(memory_space=pltpu.SMEM)],
        out_specs=spec,
        out_shape=jax.ShapeDtypeStruct(shape, F32),
        compiler_params=_cparams(2),
        name="rel_bias_table",
    )(rel_bias)


def _attn_tile_items(sink_ref, q_ref, kp_ref, ko_ref, vp_ref, vo_ref, bias_ref, first_of_seq, write, dh):
    blk = WINDOW
    n_blk = ko_ref.shape[0] // blk
    slots_per_kv = A_GROUP // 2
    n_slots = A_HEADS_Q // 2
    lane = lax.broadcasted_iota(jnp.int32, (blk, LANES), 1)
    lo_half = lane < dh
    left = lax.broadcasted_iota(jnp.int32, (1, 2 * blk), 1) < blk
    ones = jnp.ones((dh, 2 * blk), BF16)
    sinks = [jnp.where(left, sink_ref[2 * i], sink_ref[2 * i + 1]) for i in range(n_slots)]
    items = []
    for j in range(n_blk):
        st = {}
        own = slice(j * blk, (j + 1) * blk)
        before = slice((j - 1) * blk, j * blk)
        variant = jnp.where(first_of_seq, 0, 1) if j == 0 else 1

        def scores(st=st, j=j, own=own, before=before, variant=variant):
            st["vts"], st["sts"] = [], []
            for g in range(A_HEADS_KV):
                sl = slice(g * LANES, (g + 1) * LANES)
                rows = slice(g * dh, (g + 1) * dh)
                k_prev = kp_ref[:, sl] if j == 0 else ko_ref[before, sl]
                v_prev = vp_ref[0, rows, :] if j == 0 else vo_ref[0, rows, before]
                kk = jnp.concatenate([k_prev, ko_ref[own, sl]], axis=0).astype(BF16)
                vt = jnp.concatenate([v_prev, vo_ref[0, rows, own]], axis=1).astype(BF16)
                st["vts"].append(jnp.concatenate([vt, ones], axis=0))
                for pr in range(slots_per_kv):
                    slot_i = g * slots_per_kv + pr
                    slot = q_ref[own, slot_i * LANES:(slot_i + 1) * LANES]
                    zero = jnp.zeros_like(slot)
                    q2 = jnp.concatenate([jnp.where(lo_half, slot, zero), jnp.where(lo_half, zero, slot)],
                                         axis=0)
                    st["sts"].append(_dot_nt(kk, q2) + bias_ref[variant, slot_i])

        def softmax(st=st):
            st["ms"] = [jnp.maximum(jnp.max(st["sts"][i], axis=0, keepdims=True), sinks[i])
                        for i in range(n_slots)]
            st["pts"] = [jnp.exp(st["sts"][i] - st["ms"][i]).astype(BF16) for i in range(n_slots)]

        def values(st=st):
            st["oes"] = [_dot(st["vts"][i // slots_per_kv], st["pts"][i]) for i in range(n_slots)]

        def finish(st=st, j=j):
            heads_t = []
            for i in range(n_slots):
                oe = st["oes"][i]
                ot = oe[0:dh, :] / (oe[dh:dh + 1, :] + jnp.exp(sinks[i] - st["ms"][i]))
                heads_t += [ot[:, 0:blk], ot[:, blk:2 * blk]]
            write(j, jnp.concatenate(heads_t, axis=0).T.astype(BF16))

        items += [scores, softmax, values, finish]
    return items


def _attn_ffn_body(sink_ref, q_ref, kp_ref, ko_ref, vp_ref, vo_ref, bias_ref, wa_ref,
                   x_ref, g_ref, wgu_ref, wd_ref, o_ref, act_ref, att_ref, *, d_ff, tf, dh, tiles_per_seq):
    i = pl.program_id(0)

    @pl.when(i == 0)
    def _():
        att_ref[...] = jnp.zeros_like(att_ref)

    cur = lax.rem(i, 2)
    first_of_seq = lax.rem(jnp.minimum(i, pl.num_programs(0) - 2), tiles_per_seq) == 0

    def write(j, tile):
        att_ref[cur, j * WINDOW:(j + 1) * WINDOW, :] = tile

    side = _attn_tile_items(sink_ref, q_ref, kp_ref, ko_ref, vp_ref, vo_ref, bias_ref,
                            first_of_seq, write, dh)
    get_x = lambda: x_ref[...] + _dot(att_ref[1 - cur], wa_ref[...])
    _interleave(_ffn_items(get_x, g_ref, wgu_ref, wd_ref, o_ref, act_ref, d_ff, tf), side)


def _attn_ffn(x, q, k, vt, bias, sinks, wa, g, wgu, wd, dh):
    t, d = x.shape
    b, vd, s = vt.shape
    d_ff = wd.shape[0]
    qd, kd = q.shape[1], k.shape[1]
    blk = WINDOW
    tm = _row_tile(s, 512)
    nt, tps, bpt = t // tm, s // tm, tm // blk
    att_tile = lambda i: jnp.minimum(i, nt - 1)
    ffn_tile = lambda i: jnp.maximum(i - 1, 0)
    ffn_row = lambda n: pl.BlockSpec((tm, n), lambda i: (ffn_tile(i), 0))
    att_row = lambda n: pl.BlockSpec((tm, n), lambda i: (att_tile(i), 0))
    return pl.pallas_call(
        functools.partial(_attn_ffn_body, d_ff=d_ff, tf=FFN_TF, dh=dh, tiles_per_seq=tps),
        grid=(nt + 1,),
        in_specs=[pl.BlockSpec(memory_space=pltpu.SMEM),
                  att_row(qd),
                  pl.BlockSpec((blk, kd), lambda i: (jnp.maximum(att_tile(i) * bpt - 1, 0), 0)),
                  att_row(kd),
                  pl.BlockSpec((1, vd, blk), lambda i: (att_tile(i) // tps,
                                                        0, jnp.maximum(att_tile(i) % tps * bpt - 1, 0))),
                  pl.BlockSpec((1, vd, tm), lambda i: (att_tile(i) // tps, 0, att_tile(i) % tps)),
                  _resident(bias.shape), _resident(wa.shape),
                  ffn_row(d), _resident((1, d)), _resident((d, 2 * d_ff)), _resident((d_ff, d))],
        out_specs=ffn_row(d),
        out_shape=jax.ShapeDtypeStruct((t, d), F32),
        scratch_shapes=[pltpu.VMEM((tm, d_ff), BF16), pltpu.VMEM((2, tm, qd), BF16)],
        compiler_params=_cparams(1),
        name="attn_ffn",
    )(sinks, q, k, k, vt, vt, bias, wa, x, g.reshape(1, d), wgu, wd)


def _attn_sample_body(sink_ref, q_ref, kc_ref, kn_ref, vc_ref, vn_ref, bias_ref,
                      o_ref, ko_ref, vo_ref, *, dh, bt):
    tq = kn_ref.shape[1]
    heads_per_slab = 2 * A_GROUP
    n_slabs = A_HEADS_Q // heads_per_slab
    sinks = [jnp.concatenate([jnp.full((tq, 1), sink_ref[sl * heads_per_slab + hh], F32)
                              for hh in range(heads_per_slab)], axis=0) for sl in range(n_slabs)]

    units = [(b, sl) for b in range(bt) for sl in range(n_slabs)]
    lanes = lambda sl: slice(sl * LANES, (sl + 1) * LANES)
    qs = [jnp.concatenate([q_ref[b, :, (sl * heads_per_slab + hh) * LANES:(sl * heads_per_slab + hh + 1) * LANES]
                           for hh in range(heads_per_slab)], axis=0) for b, sl in units]
    sp = [_dot_nt(qs[u], kc_ref[b, :, lanes(sl)].astype(BF16)) + bias_ref[sl][:, :WINDOW]
          for u, (b, sl) in enumerate(units)]
    so = [_dot_nt(qs[u], kn_ref[b, :, lanes(sl)].astype(BF16)) + bias_ref[sl][:, WINDOW:WINDOW + tq]
          for u, (b, sl) in enumerate(units)]
    m = [jnp.maximum(jnp.maximum(jnp.max(sp[u], axis=-1, keepdims=True),
                                 jnp.max(so[u], axis=-1, keepdims=True)), sinks[sl])
         for u, (b, sl) in enumerate(units)]
    pp = [jnp.exp(sp[u] - m[u]) for u in range(len(units))]
    po = [jnp.exp(so[u] - m[u]) for u in range(len(units))]
    den = [jnp.sum(pp[u], axis=-1, keepdims=True) + jnp.sum(po[u], axis=-1, keepdims=True)
           + jnp.exp(sinks[sl] - m[u]) for u, (b, sl) in enumerate(units)]
    out = [_dot((pp[u] / den[u]).astype(BF16), vc_ref[b, :, lanes(sl)].astype(BF16))
           + _dot((po[u] / den[u]).astype(BF16), vn_ref[b, :, lanes(sl)].astype(BF16))
           for u, (b, sl) in enumerate(units)]
    for u, (b, sl) in enumerate(units):
        for hh in range(heads_per_slab):
            h = sl * heads_per_slab + hh
            o_ref[b, :, h * LANES:(h + 1) * LANES] = out[u][hh * tq:(hh + 1) * tq].astype(BF16)
    ko_ref[:, 0:WINDOW - tq, :] = kc_ref[:, tq:WINDOW, :]
    ko_ref[:, WINDOW - tq:WINDOW, :] = kn_ref[...]
    vo_ref[:, 0:WINDOW - tq, :] = vc_ref[:, tq:WINDOW, :]
    vo_ref[:, WINDOW - tq:WINDOW, :] = vn_ref[...]


def _attn_sample(q, k, v, k_cache, v_cache, bias, sinks, dh):
    b, s, qd = q.shape
    kd = k.shape[-1]
    bt = math.gcd(b, 8)
    new = lambda n: pl.BlockSpec((bt, s, n), lambda i: (i, 0, 0))
    win = pl.BlockSpec((bt, WINDOW, kd), lambda i: (i, 0, 0))
    return pl.pallas_call(
        functools.partial(_attn_sample_body, dh=dh, bt=bt),
        grid=(b // bt,),
        in_specs=[pl.BlockSpec(memory_space=pltpu.SMEM),
                  new(qd), win, new(kd), win, new(kd),
                  pl.BlockSpec(bias.shape, lambda i: (0, 0, 0))],
        out_specs=[new(qd), win, win],
        out_shape=[jax.ShapeDtypeStruct((b, s, qd), BF16),
                   jax.ShapeDtypeStruct((b, WINDOW, kd), F32),
                   jax.ShapeDtypeStruct((b, WINDOW, kd), F32)],
        compiler_params=_cparams(1),
        name="attn_sample",
    )(sinks, q, k_cache, k, v_cache, v, bias)


def _swa_layer(x3, buffers, p):
    b, s, d = x3.shape
    dh = p["attn_dh"]
    kd = A_HEADS_KV * dh
    x = x3.reshape(b * s, d)
    kv_shape = (b, WINDOW, A_HEADS_KV, dh)
    if buffers is None:
        nq, nk = A_HEADS_Q * dh, A_HEADS_KV * LANES
        q, k, vt = _attn_proj(x3, p["mix_norm1"], p["attn_w_prompt"], p["attn_wvt"],
                              p["attn_qg"], p["attn_kg"], dh, nq, nk)
        bias = _bias_table(p["rel_bias"], WINDOW, key_major=True)
        new_k = k.reshape(b, s, nk)[:, s - WINDOW:].reshape(b, WINDOW, A_HEADS_KV, LANES)[..., :dh]
        new_v = jnp.swapaxes(vt[:, :, s - WINDOW:], 1, 2)
        mixed = dict(q=q, k=k, vt=vt, bias=bias, sinks=p["attn_sinks"], wa=p["attn_w_out"], dh=dh)
    else:
        k_buf, v_buf = buffers
        nq, nk = A_HEADS_Q * LANES, kd
        q, k, vt = _attn_proj(x.reshape(1, b * s, d), p["mix_norm1"], p["attn_w_sample"], p["attn_wvt"],
                              p["attn_qg"], p["attn_kg"], dh, nq, nk)
        v = vt[0].T
        sh = lambda a: a.reshape(b, s, a.shape[-1])
        bias = _bias_table(p["rel_bias"], s, key_major=False)[1].reshape(2, 2 * A_GROUP * s, 2 * WINDOW)
        att, new_k, new_v = _attn_sample(sh(q), sh(k), sh(v), k_buf.reshape(b, WINDOW, kd),
                                         v_buf.reshape(b, WINDOW, kd), bias, p["attn_sinks"], dh)
        mixed = (att.reshape(b * s, nq), p["attn_w_out_slots"])
    return mixed, (new_k.reshape(kv_shape), new_v.reshape(kv_shape))


def _attn_weights(w_in, w_out, q_norm, k_norm, dh):
    d = w_in.shape[0]
    qd, kd = A_HEADS_Q * dh, A_HEADS_KV * dh
    wq = w_in[:, :qd].reshape(d, A_HEADS_Q, dh)
    wk = w_in[:, qd:qd + kd].reshape(d, A_HEADS_KV, dh)
    k_dup = jnp.concatenate([wk, wk], axis=-1).reshape(d, A_HEADS_KV * LANES)
    w_prompt = jnp.concatenate([w_in[:, :qd], k_dup], axis=1).astype(BF16)
    zero = jnp.zeros_like(wq)
    odd = ((jnp.arange(A_HEADS_Q) // A_GROUP) % 2 == 1)[None, :, None]
    wq_slots = jnp.concatenate([jnp.where(odd, zero, wq), jnp.where(odd, wq, zero)], axis=-1)
    w_sample = jnp.concatenate([wq_slots.reshape(d, A_HEADS_Q * LANES), w_in[:, qd:qd + kd]],
                               axis=1).astype(BF16)
    wo = w_out.reshape(A_HEADS_Q, dh, d)
    zo = jnp.zeros_like(wo)
    wo_slots = jnp.concatenate([jnp.where(odd.reshape(-1, 1, 1), zo, wo),
                                jnp.where(odd.reshape(-1, 1, 1), wo, zo)], axis=1)
    qg = jnp.concatenate([q_norm, q_norm]).reshape(1, LANES) * (dh ** -0.5)
    kg = jnp.concatenate([k_norm, k_norm]).reshape(1, LANES)
    return {"attn_w_prompt": w_prompt, "attn_w_sample": w_sample,
            "attn_wvt": w_in[:, qd + kd:].T.astype(BF16),
            "attn_w_out": w_out.astype(BF16),
            "attn_w_out_slots": wo_slots.reshape(A_HEADS_Q * LANES, d).astype(BF16),
            "attn_qg": qg, "attn_kg": kg}


def _trunk(x3, mlstm_state, swa_buffers, p):
    b, s, d = x3.shape

    def ffn(x, name, layer, mixed=None):
        w = (p[name + "_norm"][layer], p[name + "_wgu"][layer], p[name + "_wd"][layer])
        if isinstance(mixed, dict):
            return _attn_ffn(x, *(mixed[n] for n in ("q", "k", "vt", "bias", "sinks", "wa")), *w, mixed["dh"])
        return _ffn(x, *w, pre=mixed)

    x = ffn(x3.reshape(b * s, d), "ffn1", 0)
    mixed, new_mlstm = _mlstm_layer(x.reshape(b, s, d), mlstm_state, p)
    x = ffn(x, "ffn2", 0, mixed)
    x = ffn(x, "ffn1", 1)
    mixed, new_swa = _swa_layer(x.reshape(b, s, d), swa_buffers, p)
    x = ffn(x, "ffn2", 1, mixed)
    return x.reshape(b, s, d), new_mlstm, new_swa


def kernel(x_prompt, x_sample, state_mlstm_C, state_mlstm_n, state_mlstm_m, cache_swa_k, cache_swa_v,
           ffn1_norm, ffn1_w_gate_up, ffn1_w_down, mix_norm, ffn2_norm, ffn2_w_gate_up, ffn2_w_down,
           mlstm_w_in, mlstm_b_gates, mlstm_out_norm, mlstm_w_out,
           attn_w_in, attn_q_norm, attn_k_norm, attn_sinks, rel_bias, attn_w_out):
    d = x_prompt.shape[-1]
    vd = mlstm_w_out.shape[0]
    n_gates = 2 * M_HEADS
    qk = (mlstm_w_in.shape[1] - 2 * vd - n_gates) // 2
    dh = attn_q_norm.shape[0]
    assert 2 * dh == LANES
    w_gates = jnp.zeros((d, LANES), BF16).at[:, :n_gates].set(mlstm_w_in[:, 2 * qk + 2 * vd:].astype(BF16))
    b_gates = jnp.zeros((1, LANES), F32).at[0, :n_gates].set(mlstm_b_gates)
    p = {
        "ffn1_norm": ffn1_norm, "ffn2_norm": ffn2_norm,
        "ffn1_wgu": ffn1_w_gate_up.astype(BF16), "ffn1_wd": ffn1_w_down.astype(BF16),
        "ffn2_wgu": ffn2_w_gate_up.astype(BF16), "ffn2_wd": ffn2_w_down.astype(BF16),
        "mix_norm0": mix_norm[0], "mix_norm1": mix_norm[1],
        "mlstm_qk": qk, "mlstm_vd": vd,
        "mlstm_w_main": mlstm_w_in[:, :2 * qk + 2 * vd].astype(BF16),
        "mlstm_w_gates": w_gates, "mlstm_b_gates": b_gates,
        "mlstm_out_norm": mlstm_out_norm, "mlstm_w_out": mlstm_w_out.astype(BF16),
        "attn_dh": dh, "attn_sinks": attn_sinks, "rel_bias": rel_bias,
    }
    p.update(_attn_weights(attn_w_in, attn_w_out, attn_q_norm, attn_k_norm, dh))
    y_p, (c_p, n_p, m_p), (k_p, v_p) = _trunk(x_prompt, None, None, p)
    y_s, (c_s, n_s, m_s), (k_s, v_s) = _trunk(
        x_sample, (state_mlstm_C, state_mlstm_n, state_mlstm_m), (cache_swa_k, cache_swa_v), p)
    return (y_p, y_s, c_p, n_p, m_p, k_p, v_p, c_s, n_s, m_s, k_s, v_s)
```

```python
import functools
import math

import jax
import jax.numpy as jnp
from jax import lax
from jax.experimental import pallas as pl
from jax.experimental.pallas import tpu as pltpu

F32 = jnp.float32
BF16 = jnp.bfloat16

M_HEADS = 4
A_HEADS_Q = 16
A_HEADS_KV = 4
A_GROUP = A_HEADS_Q // A_HEADS_KV
WINDOW = 128
NUM_BUCKETS = 32
MAX_DISTANCE = 128
FFN_RESIDUAL = 0.5
RMS_EPS = 1e-6
LANES = 128
VMEM_LIMIT = 56 * 1024 * 1024


def _cparams(n_axes):
    return pltpu.CompilerParams(
        dimension_semantics=("arbitrary",) * n_axes, vmem_limit_bytes=VMEM_LIMIT)


def _resident(shape):
    nd = len(shape)
    return pl.BlockSpec(shape, lambda *_: (0,) * nd, pipeline_mode=pl.Buffered(1))


def _rms(x, g):
    return x * lax.rsqrt(jnp.mean(x * x, axis=-1, keepdims=True) + RMS_EPS) * g


def _dot(a, b):
    return jnp.dot(a, b, preferred_element_type=F32)


def _dot_nt(a, b):
    return lax.dot_general(a, b, (((1,), (1,)), ((), ())), preferred_element_type=F32)


def _dot_tn(a, b):
    return lax.dot_general(a, b, (((0,), (0,)), ((), ())), preferred_element_type=F32)


def _row_tile(t, pref):
    tm = min(t, pref)
    assert t % tm == 0
    return tm


def _interleave(main, side):
    done = 0
    for j, thunk in enumerate(main):
        thunk()
        upto = (j + 1) * len(side) // len(main)
        for other in side[done:upto]:
            other()
        done = upto


def _ffn_items(get_x, g_ref, wgu_ref, wd_ref, o_ref, act_ref, d_ff, tf):
    st = {}

    def head():
        y = get_x()
        o_ref[...] = y
        st["xn"] = _rms(y, g_ref[...]).astype(BF16)

    def chunk(lo):
        gate = _dot(st["xn"], wgu_ref[:, lo:lo + tf])
        up = _dot(st["xn"], wgu_ref[:, d_ff + lo:d_ff + lo + tf])
        act_ref[:, lo:lo + tf] = (gate * jax.nn.sigmoid(gate) * up).astype(BF16)

    def tail():
        o_ref[...] = o_ref[...] + FFN_RESIDUAL * _dot(act_ref[...], wd_ref[...])

    return [head] + [functools.partial(chunk, lo) for lo in range(0, d_ff, tf)] + [tail]


def _ffn_body(*refs, d_ff, tf, has_pre):
    if has_pre:
        a_ref, wa_ref, x_ref, g_ref, wgu_ref, wd_ref, o_ref, act_ref = refs
        get_x = lambda: x_ref[...] + _dot(a_ref[...], wa_ref[...])
    else:
        x_ref, g_ref, wgu_ref, wd_ref, o_ref, act_ref = refs
        get_x = lambda: x_ref[...]
    for thunk in _ffn_items(get_x, g_ref, wgu_ref, wd_ref, o_ref, act_ref, d_ff, tf):
        thunk()


FFN_TF = 256


def _ffn(x, g, wgu, wd, pre=None):
    t, d = x.shape
    d_ff = wd.shape[0]
    tm = _row_tile(t, 512)
    assert d_ff % FFN_TF == 0
    row = lambda n: pl.BlockSpec((tm, n), lambda i: (i, 0))
    pre_specs, pre_args = [], []
    if pre is not None:
        a, wa = pre
        pre_specs, pre_args = [row(a.shape[1]), _resident(wa.shape)], [a, wa]
    return pl.pallas_call(
        functools.partial(_ffn_body, d_ff=d_ff, tf=FFN_TF, has_pre=pre is not None),
        grid=(t // tm,),
        in_specs=pre_specs + [row(d), _resident((1, d)), _resident((d, 2 * d_ff)), _resident((d_ff, d))],
        out_specs=row(d),
        out_shape=jax.ShapeDtypeStruct((t, d), F32),
        scratch_shapes=[pltpu.VMEM((tm, d_ff), BF16)],
        compiler_params=_cparams(1),
        name="ffn",
    )(*pre_args, x, g.reshape(1, d), wgu, wd)


def _mlstm_proj_body(x_ref, g_ref, w_ref, wg_ref, bg_ref,
                     q_ref, k_ref, v_ref, og_ref, gcol_ref, grow_ref, tri_ref, *, qk, vd, dk, chunk):
    h_n = M_HEADS
    xn = _rms(x_ref[...], g_ref[...]).astype(BF16)
    tm = xn.shape[0]

    @pl.when(pl.program_id(0) == 0)
    def _():
        shift = chunk.bit_length() - 1
        r_i = lax.broadcasted_iota(jnp.int32, (tm, tm), 0)
        c_i = lax.broadcasted_iota(jnp.int32, (tm, tm), 1)
        same_chunk = lax.shift_right_logical(r_i, shift) == lax.shift_right_logical(c_i, shift)
        tri_ref[...] = jnp.where((r_i >= c_i) & same_chunk, 1.0, 0.0).astype(BF16)

    gates = _dot(xn, wg_ref[...]) + bg_ref[...]
    q_ref[...] = _dot(xn, w_ref[:, 0:qk]).astype(BF16)
    lane = lax.broadcasted_iota(jnp.int32, (tm, LANES), 1)
    log_f = jnp.where((lane >= h_n) & (lane < 2 * h_n), jax.nn.log_sigmoid(gates), 0.0)
    hi = log_f.astype(BF16)
    lo = (log_f - hi.astype(F32)).astype(BF16)
    k_ref[...] = (_dot(xn, w_ref[:, qk:2 * qk]) * (dk ** -0.5)).astype(BF16)
    csum = _dot(tri_ref[...], jnp.concatenate([hi, lo], axis=1))
    gcol = jnp.where(lane < h_n, gates, csum[:, :LANES] + csum[:, LANES:])
    gcol_ref[...] = gcol
    v_ref[...] = _dot(xn, w_ref[:, 2 * qk:2 * qk + vd]).astype(BF16)
    g_t = gcol.T
    grow_ref[0] = jnp.concatenate([g_t[0:h_n] - g_t[h_n:2 * h_n], g_t[h_n:2 * h_n]], axis=0)
    og_ref[...] = jax.nn.sigmoid(_dot(xn, w_ref[:, 2 * qk + vd:2 * qk + 2 * vd])).astype(BF16)


def _mlstm_proj(x3, g, w_main, w_gates, b_gates, qk, vd, chunk):
    b, s, d = x3.shape
    t = b * s
    tm = _row_tile(s, 512)
    nt = s // tm
    assert chunk & (chunk - 1) == 0 and tm % chunk == 0
    row = lambda n: pl.BlockSpec((tm, n), lambda i: (i, 0))
    return pl.pallas_call(
        functools.partial(_mlstm_proj_body, qk=qk, vd=vd, dk=qk // M_HEADS, chunk=chunk),
        grid=(t // tm,),
        in_specs=[row(d), _resident((1, d)), _resident(w_main.shape),
                  _resident(w_gates.shape), _resident((1, LANES))],
        out_specs=[row(qk), row(qk), row(vd), row(vd), row(LANES),
                   pl.BlockSpec((1, 2 * M_HEADS, tm), lambda i: (i // nt, 0, i % nt))],
        out_shape=[jax.ShapeDtypeStruct((t, qk), BF16), jax.ShapeDtypeStruct((t, qk), BF16),
                   jax.ShapeDtypeStruct((t, vd), BF16), jax.ShapeDtypeStruct((t, vd), BF16),
                   jax.ShapeDtypeStruct((t, LANES), F32),
                   jax.ShapeDtypeStruct((b, 2 * M_HEADS, s), F32)],
        scratch_shapes=[pltpu.VMEM((tm, tm), BF16)],
        compiler_params=_cparams(1),
        name="mlstm_proj",
    )(x3.reshape(t, d), g.reshape(1, d), w_main, w_gates, b_gates)


def _mlstm_chunk_items(units, load, state, emit, rows):
    r_i = lax.broadcasted_iota(jnp.int32, (rows, rows), 0)
    c_i = lax.broadcasted_iota(jnp.int32, (rows, rows), 1)
    causal = r_i >= c_i
    us = range(len(units))
    st = {}

    def products():
        st["x"] = x = [load(u) for u in units]
        st["old"] = [state[u] for u in units]
        st["qk"] = [_dot_nt(x[i]["q"], x[i]["k"]) for i in us]
        st["qc"] = [_dot(x[i]["q"], st["old"][i][0].astype(BF16)) for i in us]

    def weights():
        x, old = st["x"], st["old"]
        log_d = [jnp.where(causal, x[i]["b_c"] + x[i]["imb"], -jnp.inf) for i in us]
        log_inter = [x[i]["b_c"] + old[i][2] for i in us]
        st["m_t"] = m_t = [jnp.maximum(log_inter[i], jnp.max(log_d[i], axis=-1, keepdims=True)) for i in us]
        st["inter"] = [jnp.exp(log_inter[i] - m_t[i]) for i in us]
        st["s"] = [st["qk"][i] * jnp.exp(log_d[i] - m_t[i]) for i in us]

    def hidden():
        x, old, s, inter, m_t = st["x"], st["old"], st["s"], st["inter"], st["m_t"]
        s_v = [_dot(s[i].astype(BF16), x[i]["v"]) for i in us]
        den = [inter[i] * jnp.sum(x[i]["q"].astype(F32) * old[i][1], axis=-1, keepdims=True)
               + jnp.sum(s[i], axis=-1, keepdims=True) for i in us]
        st["hid"] = [(inter[i] * st["qc"][i] + s_v[i]) / jnp.maximum(jnp.abs(den[i]), jnp.exp(-m_t[i]))
                     for i in us]

    def update():
        x, old = st["x"], st["old"]
        b_last = [x[i]["b_c"][rows - 1:rows, :] for i in us]
        log_w = [b_last[i] - x[i]["b_c"] + x[i]["i_c"] for i in us]
        m_new = [jnp.maximum(b_last[i] + old[i][2], jnp.max(log_w[i], axis=0, keepdims=True)) for i in us]
        decay = [jnp.exp(b_last[i] + old[i][2] - m_new[i]) for i in us]
        kw = [jnp.exp(log_w[i] - m_new[i]) * x[i]["k"].astype(F32) for i in us]
        kw_v = [_dot_tn(kw[i].astype(BF16), x[i]["v"]) for i in us]
        for i, u in enumerate(units):
            state[u] = (decay[i] * old[i][0] + kw_v[i],
                        decay[i] * old[i][1] + jnp.sum(kw[i], axis=0, keepdims=True), m_new[i])

    def output():
        hid = [st["hid"][i] * lax.rsqrt(jnp.mean(st["hid"][i] * st["hid"][i], axis=-1, keepdims=True) + RMS_EPS)
               for i in us]
        for i, u in enumerate(units):
            emit(u, (hid[i] * st["x"][i]["ng"] * st["x"][i]["og"].astype(F32)).astype(BF16))

    return [products, weights, hidden, update, output]


def _mlstm_cell_body(q_ref, k_ref, v_ref, og_ref, gcol_ref, grow_ref, ng_ref, c0_ref, n0_ref, m0_ref,
                     hg_ref, c_ref, n_ref, m_ref, *, dk, dv, bt):
    h_n = M_HEADS
    rows = q_ref.shape[1]
    units = [(b, h) for b in range(bt) for h in range(h_n)]

    def load(u):
        b, h = u
        kc, vc = slice(h * dk, (h + 1) * dk), slice(h * dv, (h + 1) * dv)
        return dict(q=q_ref[b, :, kc], k=k_ref[b, :, kc], v=v_ref[b, :, vc], og=og_ref[b, :, vc],
                    ng=ng_ref[:, vc], b_c=gcol_ref[b, :, h_n + h:h_n + h + 1], i_c=gcol_ref[b, :, h:h + 1],
                    imb=grow_ref[b, h:h + 1, :])

    def emit(u, value):
        b, h = u
        hg_ref[b, :, h * dv:(h + 1) * dv] = value

    state = {(b, h): (c0_ref[b, h], n0_ref[b, h:h + 1, :], m0_ref[b, :, h:h + 1]) for b, h in units}
    for thunk in _mlstm_chunk_items(units, load, state, emit, rows):
        thunk()
    for b, h in units:
        c_ref[b, h], n_ref[b, h:h + 1, :], m_ref[b, :, h:h + 1] = state[(b, h)]


def _mlstm_cell(q, k, v, og, gcol, grow, norm_g, c0, n0, m0, bt):
    b, s, qk = q.shape
    vd = v.shape[-1]
    dk, dv = qk // M_HEADS, vd // M_HEADS
    assert b % bt == 0
    tok = lambda n: pl.BlockSpec((bt, s, n), lambda i: (i, 0, 0))
    st_c = pl.BlockSpec((bt, M_HEADS, dk, dv), lambda i: (i, 0, 0, 0))
    st_n = pl.BlockSpec((bt, M_HEADS, dk), lambda i: (i, 0, 0))
    st_m = pl.BlockSpec((bt, 1, M_HEADS), lambda i: (i, 0, 0))
    return pl.pallas_call(
        functools.partial(_mlstm_cell_body, dk=dk, dv=dv, bt=bt),
        grid=(b // bt,),
        in_specs=[tok(qk), tok(qk), tok(vd), tok(vd), tok(LANES),
                  pl.BlockSpec((bt, 2 * M_HEADS, s), lambda i: (i, 0, 0)),
                  pl.BlockSpec((1, vd), lambda i: (0, 0)), st_c, st_n, st_m],
        out_specs=[tok(vd), st_c, st_n, st_m],
        out_shape=[jax.ShapeDtypeStruct((b, s, vd), BF16),
                   jax.ShapeDtypeStruct((b, M_HEADS, dk, dv), F32),
                   jax.ShapeDtypeStruct((b, M_HEADS, dk), F32),
                   jax.ShapeDtypeStruct((b, 1, M_HEADS), F32)],
        compiler_params=_cparams(1),
        name="mlstm_cell",
    )(q, k, v, og, gcol, grow, norm_g.reshape(1, vd), c0, n0, m0)


def _mlstm_ffn_body(q_ref, k_ref, v_ref, og_ref, gcol_ref, grow_ref, ng_ref, wa_ref,
                    x_ref, g_ref, wgu_ref, wd_ref, o_ref, c_ref, n_ref, m_ref, act_ref, hg_ref,
                    *, d_ff, tf, dk, dv, chunk, tiles_per_seq):
    h_n = M_HEADS
    i = pl.program_id(0)
    last = pl.num_programs(0) - 1
    live = i < last
    first_of_seq = lax.rem(jnp.minimum(i, last - 1), tiles_per_seq) == 0

    @pl.when(i == 0)
    def _():
        hg_ref[...] = jnp.zeros_like(hg_ref)

    @pl.when(first_of_seq & live)
    def _():
        c_ref[...] = jnp.zeros_like(c_ref)
        n_ref[...] = jnp.zeros_like(n_ref)
        m_ref[...] = jnp.zeros_like(m_ref)

    cur = lax.rem(i, 2)
    heads = list(range(h_n))
    state = {h: (c_ref[0, h], n_ref[0, h:h + 1, :], m_ref[0, :, h:h + 1]) for h in heads}
    side = []
    for j in range(q_ref.shape[0] // chunk):
        rows = slice(j * chunk, (j + 1) * chunk)

        def load(h, rows=rows):
            kc, vc = slice(h * dk, (h + 1) * dk), slice(h * dv, (h + 1) * dv)
            return dict(q=q_ref[rows, kc], k=k_ref[rows, kc], v=v_ref[rows, vc], og=og_ref[rows, vc],
                        ng=ng_ref[:, vc], b_c=gcol_ref[rows, h_n + h:h_n + h + 1], i_c=gcol_ref[rows, h:h + 1],
                        imb=grow_ref[0, h:h + 1, rows])

        def emit(h, value, rows=rows):
            hg_ref[cur, rows, h * dv:(h + 1) * dv] = value

        side += _mlstm_chunk_items(heads, load, state, emit, chunk)
    get_x = lambda: x_ref[...] + _dot(hg_ref[1 - cur], wa_ref[...])
    _interleave(_ffn_items(get_x, g_ref, wgu_ref, wd_ref, o_ref, act_ref, d_ff, tf), side)

    @pl.when(live)
    def _():
        for h in heads:
            c_ref[0, h], n_ref[0, h:h + 1, :], m_ref[0, :, h:h + 1] = state[h]


def _mlstm_ffn(x, q, k, v, og, gcol, grow, norm_g, wa, g, wgu, wd, chunk):
    t, d = x.shape
    b, _, s = grow.shape
    d_ff = wd.shape[0]
    qk, vd = q.shape[1], v.shape[1]
    dk, dv = qk // M_HEADS, vd // M_HEADS
    tm = _row_tile(s, 512)
    nt, tps = t // tm, s // tm
    assert tm % chunk == 0
    mix_tile = lambda i: jnp.minimum(i, nt - 1)
    ffn_row = lambda n: pl.BlockSpec((tm, n), lambda i: (jnp.maximum(i - 1, 0), 0))
    mix_row = lambda n: pl.BlockSpec((tm, n), lambda i: (mix_tile(i), 0))
    seq = lambda *tail: pl.BlockSpec((1,) + tail, lambda i: (mix_tile(i) // tps,) + (0,) * len(tail))
    return pl.pallas_call(
        functools.partial(_mlstm_ffn_body, d_ff=d_ff, tf=FFN_TF, dk=dk, dv=dv, chunk=chunk, tiles_per_seq=tps),
        grid=(nt + 1,),
        in_specs=[mix_row(qk), mix_row(qk), mix_row(vd), mix_row(vd), mix_row(LANES),
                  pl.BlockSpec((1, 2 * M_HEADS, tm), lambda i: (mix_tile(i) // tps, 0, mix_tile(i) % tps)),
                  _resident((1, vd)), _resident(wa.shape),
                  ffn_row(d), _resident((1, d)), _resident((d, 2 * d_ff)), _resident((d_ff, d))],
        out_specs=[ffn_row(d), seq(M_HEADS, dk, dv), seq(M_HEADS, dk), seq(1, M_HEADS)],
        out_shape=[jax.ShapeDtypeStruct((t, d), F32),
                   jax.ShapeDtypeStruct((b, M_HEADS, dk, dv), F32),
                   jax.ShapeDtypeStruct((b, M_HEADS, dk), F32),
                   jax.ShapeDtypeStruct((b, 1, M_HEADS), F32)],
        scratch_shapes=[pltpu.VMEM((tm, d_ff), BF16), pltpu.VMEM((2, tm, vd), BF16)],
        compiler_params=_cparams(1),
        name="mlstm_ffn",
    )(q, k, v, og, gcol, grow, norm_g.reshape(1, vd), wa, x, g.reshape(1, d), wgu, wd)


def _mlstm_layer(x3, state, p):
    b, s, d = x3.shape
    qk, vd = p["mlstm_qk"], p["mlstm_vd"]
    proj = functools.partial(_mlstm_proj, g=p["mix_norm0"], w_main=p["mlstm_w_main"],
                             w_gates=p["mlstm_w_gates"], b_gates=p["mlstm_b_gates"], qk=qk, vd=vd)
    if state is None:
        chunk = min(s, 128)
        q, k, v, og, gcol, grow = proj(x3, chunk=chunk)
        return dict(kind="mlstm", args=(q, k, v, og, gcol, grow, p["mlstm_out_norm"], p["mlstm_w_out"]),
                    chunk=chunk), None
    c0, n0, m0 = state
    q, k, v, og, gcol, grow = proj(x3.reshape(1, b * s, d), chunk=s)
    grow = jnp.swapaxes(grow.reshape(2 * M_HEADS, b, s), 0, 1)
    sh = lambda a: a.reshape(b, s, a.shape[-1])
    hg, c1, n1, m1 = _mlstm_cell(sh(q), sh(k), sh(v), sh(og), sh(gcol), grow, p["mlstm_out_norm"],
                                 c0, n0, m0.reshape(b, 1, M_HEADS), math.gcd(b, 8))
    return (hg.reshape(b * s, vd), p["mlstm_w_out"]), (c1, n1, m1.reshape(b, M_HEADS))


def _attn_proj_body(x_ref, g_ref, w_ref, wvt_ref, qg_ref, kg_ref, q_ref, k_ref, vt_ref, *, dh, nq, nk):
    xn = _rms(x_ref[...], g_ref[...]).astype(BF16)
    r_i = lax.broadcasted_iota(jnp.int32, (LANES, LANES), 0)
    c_i = lax.broadcasted_iota(jnp.int32, (LANES, LANES), 1)
    same_half = jnp.where((r_i < dh) == (c_i < dh), 1.0, 0.0).astype(BF16)

    def half_norm(y, gain):
        sq = y * y
        hi = sq.astype(BF16)
        lo = (sq - hi.astype(F32)).astype(BF16)
        ms = (_dot(hi, same_half) + _dot(lo, same_half)) * (1.0 / dh)
        return y * lax.rsqrt(ms + RMS_EPS) * gain

    q = _dot(xn, w_ref[:, 0:nq])
    for s in range(nq // LANES):
        sl = slice(s * LANES, (s + 1) * LANES)
        q_ref[:, sl] = half_norm(q[:, sl], qg_ref[...]).astype(BF16)
    k = _dot(xn, w_ref[:, nq:nq + nk])
    for s in range(nk // LANES):
        sl = slice(s * LANES, (s + 1) * LANES)
        k_ref[:, sl] = half_norm(k[:, sl], kg_ref[...])
    vt_ref[0] = _dot_nt(wvt_ref[...], xn)


def _attn_proj(x3, g, w, wvt, qg, kg, dh, nq, nk):
    b, s, d = x3.shape
    t = b * s
    nv = wvt.shape[0]
    tm = _row_tile(s, 512)
    nt = s // tm
    row = lambda n: pl.BlockSpec((tm, n), lambda i: (i, 0))
    return pl.pallas_call(
        functools.partial(_attn_proj_body, dh=dh, nq=nq, nk=nk),
        grid=(t // tm,),
        in_specs=[row(d), _resident((1, d)), _resident(w.shape), _resident(wvt.shape),
                  _resident((1, LANES)), _resident((1, LANES))],
        out_specs=[row(nq), row(nk), pl.BlockSpec((1, nv, tm), lambda i: (i // nt, 0, i % nt))],
        out_shape=[jax.ShapeDtypeStruct((t, nq), BF16), jax.ShapeDtypeStruct((t, nk), F32),
                   jax.ShapeDtypeStruct((b, nv, s), F32)],
        compiler_params=_cparams(1),
        name="attn_proj",
    )(x3.reshape(t, d), g.reshape(1, d), w, wvt, qg, kg)


def _bias_body(rel_ref, o_ref, *, tq, key_major):
    variant = pl.program_id(0)
    h = pl.program_id(1)
    nk = 2 * WINDOW
    shape, q_ax = ((nk, tq), 1) if key_major else ((tq, nk), 0)
    qi = lax.broadcasted_iota(jnp.int32, shape, q_ax)
    ki = lax.broadcasted_iota(jnp.int32, shape, 1 - q_ax)
    dist = qi + WINDOW - ki
    max_exact = NUM_BUCKETS // 2
    d = jnp.maximum(dist, 0)
    log_ratio = (jnp.log(jnp.maximum(d, 1).astype(F32) / max_exact)
                 / math.log(MAX_DISTANCE / max_exact))
    large = jnp.minimum(max_exact + (log_ratio * (NUM_BUCKETS - max_exact)).astype(jnp.int32),
                        NUM_BUCKETS - 1)
    bucket = jnp.where(d < max_exact, d, large)
    bias = jnp.zeros(shape, F32)
    for bkt in range(NUM_BUCKETS):
        bias = jnp.where(bucket == bkt, rel_ref[bkt, h], bias)
    visible = (dist >= 0) & (dist < WINDOW) & ((ki >= WINDOW) | (variant > 0))
    o_ref[0, 0] = jnp.where(visible, bias, -jnp.inf)


def _bias_table(rel_bias, tq, key_major):
    nk = 2 * WINDOW
    if key_major:
        spec = pl.BlockSpec((1, 1, nk, tq), lambda v, h: (v, h // 2, 0, h % 2))
        shape = (2, A_HEADS_Q // 2, nk, 2 * tq)
    else:
        spec = pl.BlockSpec((1, 1, tq, nk), lambda v, h: (v, h, 0, 0))
        shape = (2, A_HEADS_Q, tq, nk)
    return pl.pallas_call(
        functools.partial(_bias_body, tq=tq, key_major=key_major),
        grid=(2, A_HEADS_Q),
        in_specs=[pl.BlockSpec(memory_space=pltpu.SMEM)],
        out_specs=spec,
        out_shape=jax.ShapeDtypeStruct(shape, F32),
        compiler_params=_cparams(2),
        name="rel_bias_table",
    )(rel_bias)


def _attn_tile_items(sink_ref, q_ref, kp_ref, ko_ref, vp_ref, vo_ref, bias_ref, first_of_seq, write, dh):
    blk = WINDOW
    n_blk = ko_ref.shape[0] // blk
    slots_per_kv = A_GROUP // 2
    n_slots = A_HEADS_Q // 2
    lane = lax.broadcasted_iota(jnp.int32, (blk, LANES), 1)
    lo_half = lane < dh
    left = lax.broadcasted_iota(jnp.int32, (1, 2 * blk), 1) < blk
    ones = jnp.ones((dh, 2 * blk), BF16)
    sinks = [jnp.where(left, sink_ref[2 * i], sink_ref[2 * i + 1]) for i in range(n_slots)]
    items = []
    for j in range(n_blk):
        st = {}
        own = slice(j * blk, (j + 1) * blk)
        before = slice((j - 1) * blk, j * blk)
        variant = jnp.where(first_of_seq, 0, 1) if j == 0 else 1

        def scores(st=st, j=j, own=own, before=before, variant=variant):
            st["vts"], st["sts"] = [], []
            for g in range(A_HEADS_KV):
                sl = slice(g * LANES, (g + 1) * LANES)
                rows = slice(g * dh, (g + 1) * dh)
                k_prev = kp_ref[:, sl] if j == 0 else ko_ref[before, sl]
                v_prev = vp_ref[0, rows, :] if j == 0 else vo_ref[0, rows, before]
                kk = jnp.concatenate([k_prev, ko_ref[own, sl]], axis=0).astype(BF16)
                vt = jnp.concatenate([v_prev, vo_ref[0, rows, own]], axis=1).astype(BF16)
                st["vts"].append(jnp.concatenate([vt, ones], axis=0))
                for pr in range(slots_per_kv):
                    slot_i = g * slots_per_kv + pr
                    slot = q_ref[own, slot_i * LANES:(slot_i + 1) * LANES]
                    zero = jnp.zeros_like(slot)
                    q2 = jnp.concatenate([jnp.where(lo_half, slot, zero), jnp.where(lo_half, zero, slot)],
                                         axis=0)
                    st["sts"].append(_dot_nt(kk, q2) + bias_ref[variant, slot_i])

        def softmax(st=st):
            st["ms"] = [jnp.maximum(jnp.max(st["sts"][i], axis=0, keepdims=True), sinks[i])
                        for i in range(n_slots)]
            st["pts"] = [jnp.exp(st["sts"][i] - st["ms"][i]).astype(BF16) for i in range(n_slots)]

        def values(st=st):
            st["oes"] = [_dot(st["vts"][i // slots_per_kv], st["pts"][i]) for i in range(n_slots)]

        def finish(st=st, j=j):
            heads_t = []
            for i in range(n_slots):
                oe = st["oes"][i]
                ot = oe[0:dh, :] / (oe[dh:dh + 1, :] + jnp.exp(sinks[i] - st["ms"][i]))
                heads_t += [ot[:, 0:blk], ot[:, blk:2 * blk]]
            write(j, jnp.concatenate(heads_t, axis=0).T.astype(BF16))

        items += [scores, softmax, values, finish]
    return items


def _attn_ffn_body(sink_ref, q_ref, kp_ref, ko_ref, vp_ref, vo_ref, bias_ref, wa_ref,
                   x_ref, g_ref, wgu_ref, wd_ref, o_ref, act_ref, att_ref, *, d_ff, tf, dh, tiles_per_seq):
    i = pl.program_id(0)

    @pl.when(i == 0)
    def _():
        att_ref[...] = jnp.zeros_like(att_ref)

    cur = lax.rem(i, 2)
    first_of_seq = lax.rem(jnp.minimum(i, pl.num_programs(0) - 2), tiles_per_seq) == 0

    def write(j, tile):
        att_ref[cur, j * WINDOW:(j + 1) * WINDOW, :] = tile

    side = _attn_tile_items(sink_ref, q_ref, kp_ref, ko_ref, vp_ref, vo_ref, bias_ref,
                            first_of_seq, write, dh)
    get_x = lambda: x_ref[...] + _dot(att_ref[1 - cur], wa_ref[...])
    _interleave(_ffn_items(get_x, g_ref, wgu_ref, wd_ref, o_ref, act_ref, d_ff, tf), side)


def _attn_ffn(x, q, k, vt, bias, sinks, wa, g, wgu, wd, dh):
    t, d = x.shape
    b, vd, s = vt.shape
    d_ff = wd.shape[0]
    qd, kd = q.shape[1], k.shape[1]
    blk = WINDOW
    tm = _row_tile(s, 512)
    nt, tps, bpt = t // tm, s // tm, tm // blk
    att_tile = lambda i: jnp.minimum(i, nt - 1)
    ffn_tile = lambda i: jnp.maximum(i - 1, 0)
    ffn_row = lambda n: pl.BlockSpec((tm, n), lambda i: (ffn_tile(i), 0))
    att_row = lambda n: pl.BlockSpec((tm, n), lambda i: (att_tile(i), 0))
    return pl.pallas_call(
        functools.partial(_attn_ffn_body, d_ff=d_ff, tf=FFN_TF, dh=dh, tiles_per_seq=tps),
        grid=(nt + 1,),
        in_specs=[pl.BlockSpec(memory_space=pltpu.SMEM),
                  att_row(qd),
                  pl.BlockSpec((blk, kd), lambda i: (jnp.maximum(att_tile(i) * bpt - 1, 0), 0)),
                  att_row(kd),
                  pl.BlockSpec((1, vd, blk), lambda i: (att_tile(i) // tps,
                                                        0, jnp.maximum(att_tile(i) % tps * bpt - 1, 0))),
                  pl.BlockSpec((1, vd, tm), lambda i: (att_tile(i) // tps, 0, att_tile(i) % tps)),
                  _resident(bias.shape), _resident(wa.shape),
                  ffn_row(d), _resident((1, d)), _resident((d, 2 * d_ff)), _resident((d_ff, d))],
        out_specs=ffn_row(d),
        out_shape=jax.ShapeDtypeStruct((t, d), F32),
        scratch_shapes=[pltpu.VMEM((tm, d_ff), BF16), pltpu.VMEM((2, tm, qd), BF16)],
        compiler_params=_cparams(1),
        name="attn_ffn",
    )(sinks, q, k, k, vt, vt, bias, wa, x, g.reshape(1, d), wgu, wd)


def _attn_sample_body(sink_ref, q_ref, kc_ref, kn_ref, vc_ref, vn_ref, bias_ref,
                      o_ref, ko_ref, vo_ref, *, dh, bt):
    tq = kn_ref.shape[1]
    heads_per_slab = 2 * A_GROUP
    n_slabs = A_HEADS_Q // heads_per_slab
    sinks = [jnp.concatenate([jnp.full((tq, 1), sink_ref[sl * heads_per_slab + hh], F32)
                              for hh in range(heads_per_slab)], axis=0) for sl in range(n_slabs)]

    units = [(b, sl) for b in range(bt) for sl in range(n_slabs)]
    lanes = lambda sl: slice(sl * LANES, (sl + 1) * LANES)
    qs = [jnp.concatenate([q_ref[b, :, (sl * heads_per_slab + hh) * LANES:(sl * heads_per_slab + hh + 1) * LANES]
                           for hh in range(heads_per_slab)], axis=0) for b, sl in units]
    sp = [_dot_nt(qs[u], kc_ref[b, :, lanes(sl)].astype(BF16)) + bias_ref[sl][:, :WINDOW]
          for u, (b, sl) in enumerate(units)]
    so = [_dot_nt(qs[u], kn_ref[b, :, lanes(sl)].astype(BF16)) + bias_ref[sl][:, WINDOW:WINDOW + tq]
          for u, (b, sl) in enumerate(units)]
    m = [jnp.maximum(jnp.maximum(jnp.max(sp[u], axis=-1, keepdims=True),
                                 jnp.max(so[u], axis=-1, keepdims=True)), sinks[sl])
         for u, (b, sl) in enumerate(units)]
    pp = [jnp.exp(sp[u] - m[u]) for u in range(len(units))]
    po = [jnp.exp(so[u] - m[u]) for u in range(len(units))]
    den = [jnp.sum(pp[u], axis=-1, keepdims=True) + jnp.sum(po[u], axis=-1, keepdims=True)
           + jnp.exp(sinks[sl] - m[u]) for u, (b, sl) in enumerate(units)]
    out = [_dot((pp[u] / den[u]).astype(BF16), vc_ref[b, :, lanes(sl)].astype(BF16))
           + _dot((po[u] / den[u]).astype(BF16), vn_ref[b, :, lanes(sl)].astype(BF16))
           for u, (b, sl) in enumerate(units)]
    for u, (b, sl) in enumerate(units):
        for hh in range(heads_per_slab):
            h = sl * heads_per_slab + hh
            o_ref[b, :, h * LANES:(h + 1) * LANES] = out[u][hh * tq:(hh + 1) * tq].astype(BF16)
    ko_ref[:, 0:WINDOW - tq, :] = kc_ref[:, tq:WINDOW, :]
    ko_ref[:, WINDOW - tq:WINDOW, :] = kn_ref[...]
    vo_ref[:, 0:WINDOW - tq, :] = vc_ref[:, tq:WINDOW, :]
    vo_ref[:, WINDOW - tq:WINDOW, :] = vn_ref[...]


def _attn_sample(q, k, v, k_cache, v_cache, bias, sinks, dh):
    b, s, qd = q.shape
    kd = k.shape[-1]
    bt = math.gcd(b, 8)
    new = lambda n: pl.BlockSpec((bt, s, n), lambda i: (i, 0, 0))
    win = pl.BlockSpec((bt, WINDOW, kd), lambda i: (i, 0, 0))
    return pl.pallas_call(
        functools.partial(_attn_sample_body, dh=dh, bt=bt),
        grid=(b // bt,),
        in_specs=[pl.BlockSpec(memory_space=pltpu.SMEM),
                  new(qd), win, new(kd), win, new(kd),
                  pl.BlockSpec(bias.shape, lambda i: (0, 0, 0))],
        out_specs=[new(qd), win, win],
        out_shape=[jax.ShapeDtypeStruct((b, s, qd), BF16),
                   jax.ShapeDtypeStruct((b, WINDOW, kd), F32),
                   jax.ShapeDtypeStruct((b, WINDOW, kd), F32)],
        compiler_params=_cparams(1),
        name="attn_sample",
    )(sinks, q, k_cache, k, v_cache, v, bias)


def _swa_layer(x3, buffers, p):
    b, s, d = x3.shape
    dh = p["attn_dh"]
    kd = A_HEADS_KV * dh
    x = x3.reshape(b * s, d)
    kv_shape = (b, WINDOW, A_HEADS_KV, dh)
    if buffers is None:
        nq, nk = A_HEADS_Q * dh, A_HEADS_KV * LANES
        q, k, vt = _attn_proj(x3, p["mix_norm1"], p["attn_w_prompt"], p["attn_wvt"],
                              p["attn_qg"], p["attn_kg"], dh, nq, nk)
        bias = _bias_table(p["rel_bias"], WINDOW, key_major=True)
        new_k = k.reshape(b, s, nk)[:, s - WINDOW:].reshape(b, WINDOW, A_HEADS_KV, LANES)[..., :dh]
        new_v = jnp.swapaxes(vt[:, :, s - WINDOW:], 1, 2)
        mixed = dict(kind="attn", args=(q, k, vt, bias, p["attn_sinks"], p["attn_w_out"]), dh=dh)
    else:
        k_buf, v_buf = buffers
        nq, nk = A_HEADS_Q * LANES, kd
        q, k, vt = _attn_proj(x.reshape(1, b * s, d), p["mix_norm1"], p["attn_w_sample"], p["attn_wvt"],
                              p["attn_qg"], p["attn_kg"], dh, nq, nk)
        v = vt[0].T
        sh = lambda a: a.reshape(b, s, a.shape[-1])
        bias = _bias_table(p["rel_bias"], s, key_major=False)[1].reshape(2, 2 * A_GROUP * s, 2 * WINDOW)
        att, new_k, new_v = _attn_sample(sh(q), sh(k), sh(v), k_buf.reshape(b, WINDOW, kd),
                                         v_buf.reshape(b, WINDOW, kd), bias, p["attn_sinks"], dh)
        mixed = (att.reshape(b * s, nq), p["attn_w_out_slots"])
    return mixed, (new_k.reshape(kv_shape), new_v.reshape(kv_shape))


def _attn_weights(w_in, w_out, q_norm, k_norm, dh):
    d = w_in.shape[0]
    qd, kd = A_HEADS_Q * dh, A_HEADS_KV * dh
    wq = w_in[:, :qd].reshape(d, A_HEADS_Q, dh)
    wk = w_in[:, qd:qd + kd].reshape(d, A_HEADS_KV, dh)
    k_dup = jnp.concatenate([wk, wk], axis=-1).reshape(d, A_HEADS_KV * LANES)
    w_prompt = jnp.concatenate([w_in[:, :qd], k_dup], axis=1).astype(BF16)
    zero = jnp.zeros_like(wq)
    odd = ((jnp.arange(A_HEADS_Q) // A_GROUP) % 2 == 1)[None, :, None]
    wq_slots = jnp.concatenate([jnp.where(odd, zero, wq), jnp.where(odd, wq, zero)], axis=-1)
    w_sample = jnp.concatenate([wq_slots.reshape(d, A_HEADS_Q * LANES), w_in[:, qd:qd + kd]],
                               axis=1).astype(BF16)
    wo = w_out.reshape(A_HEADS_Q, dh, d)
    zo = jnp.zeros_like(wo)
    wo_slots = jnp.concatenate([jnp.where(odd.reshape(-1, 1, 1), zo, wo),
                                jnp.where(odd.reshape(-1, 1, 1), wo, zo)], axis=1)
    qg = jnp.concatenate([q_norm, q_norm]).reshape(1, LANES) * (dh ** -0.5)
    kg = jnp.concatenate([k_norm, k_norm]).reshape(1, LANES)
    return {"attn_w_prompt": w_prompt, "attn_w_sample": w_sample,
            "attn_wvt": w_in[:, qd + kd:].T.astype(BF16),
            "attn_w_out": w_out.astype(BF16),
            "attn_w_out_slots": wo_slots.reshape(A_HEADS_Q * LANES, d).astype(BF16),
            "attn_qg": qg, "attn_kg": kg}


def _trunk(x3, mlstm_state, swa_buffers, p):
    b, s, d = x3.shape

    def ffn(x, name, layer, mixed=None):
        w = (p[name + "_norm"][layer], p[name + "_wgu"][layer], p[name + "_wd"][layer])
        if not isinstance(mixed, dict):
            return _ffn(x, *w, pre=mixed)
        if mixed["kind"] == "attn":
            return _attn_ffn(x, *mixed["args"], *w, mixed["dh"])
        return _mlstm_ffn(x, *mixed["args"], *w, mixed["chunk"])

    x = ffn(x3.reshape(b * s, d), "ffn1", 0)
    mixed, new_mlstm = _mlstm_layer(x.reshape(b, s, d), mlstm_state, p)
    x = ffn(x, "ffn2", 0, mixed)
    if new_mlstm is None:
        x, c1, n1, m1 = x
        new_mlstm = (c1, n1, m1.reshape(b, M_HEADS))
    x = ffn(x, "ffn1", 1)
    mixed, new_swa = _swa_layer(x.reshape(b, s, d), swa_buffers, p)
    x = ffn(x, "ffn2", 1, mixed)
    return x.reshape(b, s, d), new_mlstm, new_swa


def kernel(x_prompt, x_sample, state_mlstm_C, state_mlstm_n, state_mlstm_m, cache_swa_k, cache_swa_v,
           ffn1_norm, ffn1_w_gate_up, ffn1_w_down, mix_norm, ffn2_norm, ffn2_w_gate_up, ffn2_w_down,
           mlstm_w_in, mlstm_b_gates, mlstm_out_norm, mlstm_w_out,
           attn_w_in, attn_q_norm, attn_k_norm, attn_sinks, rel_bias, attn_w_out):
    d = x_prompt.shape[-1]
    vd = mlstm_w_out.shape[0]
    n_gates = 2 * M_HEADS
    qk = (mlstm_w_in.shape[1] - 2 * vd - n_gates) // 2
    dh = attn_q_norm.shape[0]
    assert 2 * dh == LANES
    w_gates = jnp.zeros((d, LANES), BF16).at[:, :n_gates].set(mlstm_w_in[:, 2 * qk + 2 * vd:].astype(BF16))
    b_gates = jnp.zeros((1, LANES), F32).at[0, :n_gates].set(mlstm_b_gates)
    p = {
        "ffn1_norm": ffn1_norm, "ffn2_norm": ffn2_norm,
        "ffn1_wgu": ffn1_w_gate_up.astype(BF16), "ffn1_wd": ffn1_w_down.astype(BF16),
        "ffn2_wgu": ffn2_w_gate_up.astype(BF16), "ffn2_wd": ffn2_w_down.astype(BF16),
        "mix_norm0": mix_norm[0], "mix_norm1": mix_norm[1],
        "mlstm_qk": qk, "mlstm_vd": vd,
        "mlstm_w_main": mlstm_w_in[:, :2 * qk + 2 * vd].astype(BF16),
        "mlstm_w_gates": w_gates, "mlstm_b_gates": b_gates,
        "mlstm_out_norm": mlstm_out_norm, "mlstm_w_out": mlstm_w_out.astype(BF16),
        "attn_dh": dh, "attn_sinks": attn_sinks, "rel_bias": rel_bias,
    }
    p.update(_attn_weights(attn_w_in, attn_w_out, attn_q_norm, attn_k_norm, dh))
    y_p, (c_p, n_p, m_p), (k_p, v_p) = _trunk(x_prompt, None, None, p)
    y_s, (c_s, n_s, m_s), (k_s, v_s) = _trunk(
        x_sample, (state_mlstm_C, state_mlstm_n, state_mlstm_m), (cache_swa_k, cache_swa_v), p)
    return (y_p, y_s, c_p, n_p, m_p, k_p, v_p, c_s, n_s, m_s, k_s, v_s)
```

```python
import functools
import math

import jax
import jax.numpy as jnp
from jax import lax
from jax.experimental import pallas as pl
from jax.experimental.pallas import tpu as pltpu

F32 = jnp.float32
BF16 = jnp.bfloat16

M_HEADS = 4
A_HEADS_Q = 16
A_HEADS_KV = 4
A_GROUP = A_HEADS_Q // A_HEADS_KV
WINDOW = 128
NUM_BUCKETS = 32
MAX_DISTANCE = 128
FFN_RESIDUAL = 0.5
RMS_EPS = 1e-6
LANES = 128
VMEM_LIMIT = 56 * 1024 * 1024


def _cparams(n_axes):
    return pltpu.CompilerParams(
        dimension_semantics=("arbitrary",) * n_axes, vmem_limit_bytes=VMEM_LIMIT)


def _resident(shape):
    nd = len(shape)
    return pl.BlockSpec(shape, lambda *_: (0,) * nd, pipeline_mode=pl.Buffered(1))


def _rms(x, g):
    return x * lax.rsqrt(jnp.mean(x * x, axis=-1, keepdims=True) + RMS_EPS) * g


def _dot(a, b):
    return jnp.dot(a, b, preferred_element_type=F32)


def _dot_nt(a, b):
    return lax.dot_general(a, b, (((1,), (1,)), ((), ())), preferred_element_type=F32)


def _dot_tn(a, b):
    return lax.dot_general(a, b, (((0,), (0,)), ((), ())), preferred_element_type=F32)


def _row_tile(t, pref):
    tm = min(t, pref)
    assert t % tm == 0
    return tm


def _interleave(main, side):
    done = 0
    for j, thunk in enumerate(main):
        thunk()
        upto = (j + 1) * len(side) // len(main)
        for other in side[done:upto]:
            other()
        done = upto


def _ffn_items(get_x, g_ref, wgu_ref, wd_ref, o_ref, act_ref, d_ff, tf):
    st = {}

    def head():
        y = get_x()
        o_ref[...] = y
        st["xn"] = _rms(y, g_ref[...]).astype(BF16)

    def chunk(lo):
        gate = _dot(st["xn"], wgu_ref[:, lo:lo + tf])
        up = _dot(st["xn"], wgu_ref[:, d_ff + lo:d_ff + lo + tf])
        act_ref[:, lo:lo + tf] = (gate * jax.nn.sigmoid(gate) * up).astype(BF16)

    def tail():
        o_ref[...] = o_ref[...] + FFN_RESIDUAL * _dot(act_ref[...], wd_ref[...])

    return [head] + [functools.partial(chunk, lo) for lo in range(0, d_ff, tf)] + [tail]


def _ffn_body(*refs, d_ff, tf, has_pre):
    if has_pre:
        a_ref, wa_ref, x_ref, g_ref, wgu_ref, wd_ref, o_ref, act_ref = refs
        get_x = lambda: x_ref[...] + _dot(a_ref[...], wa_ref[...])
    else:
        x_ref, g_ref, wgu_ref, wd_ref, o_ref, act_ref = refs
        get_x = lambda: x_ref[...]
    for thunk in _ffn_items(get_x, g_ref, wgu_ref, wd_ref, o_ref, act_ref, d_ff, tf):
        thunk()


FFN_TF = 256


def _ffn(x, g, wgu, wd, pre=None):
    t, d = x.shape
    d_ff = wd.shape[0]
    tm = _row_tile(t, 1024)
    assert d_ff % FFN_TF == 0
    row = lambda n: pl.BlockSpec((tm, n), lambda i: (i, 0))
    pre_specs, pre_args = [], []
    if pre is not None:
        a, wa = pre
        pre_specs, pre_args = [row(a.shape[1]), _resident(wa.shape)], [a, wa]
    return pl.pallas_call(
        functools.partial(_ffn_body, d_ff=d_ff, tf=FFN_TF, has_pre=pre is not None),
        grid=(t // tm,),
        in_specs=pre_specs + [row(d), _resident((1, d)), _resident((d, 2 * d_ff)), _resident((d_ff, d))],
        out_specs=row(d),
        out_shape=jax.ShapeDtypeStruct((t, d), F32),
        scratch_shapes=[pltpu.VMEM((tm, d_ff), BF16)],
        compiler_params=_cparams(1),
        name="ffn",
    )(*pre_args, x, g.reshape(1, d), wgu, wd)


def _mlstm_proj_body(x_ref, g_ref, w_ref, wg_ref, bg_ref,
                     q_ref, k_ref, v_ref, og_ref, gcol_ref, grow_ref, tri_ref, *, qk, vd, dk, chunk):
    h_n = M_HEADS
    xn = _rms(x_ref[...], g_ref[...]).astype(BF16)
    tm = xn.shape[0]

    @pl.when(pl.program_id(0) == 0)
    def _():
        shift = chunk.bit_length() - 1
        r_i = lax.broadcasted_iota(jnp.int32, (tm, tm), 0)
        c_i = lax.broadcasted_iota(jnp.int32, (tm, tm), 1)
        same_chunk = lax.shift_right_logical(r_i, shift) == lax.shift_right_logical(c_i, shift)
        tri_ref[...] = jnp.where((r_i >= c_i) & same_chunk, 1.0, 0.0).astype(BF16)

    gates = _dot(xn, wg_ref[...]) + bg_ref[...]
    q_ref[...] = _dot(xn, w_ref[:, 0:qk]).astype(BF16)
    lane = lax.broadcasted_iota(jnp.int32, (tm, LANES), 1)
    log_f = jnp.where((lane >= h_n) & (lane < 2 * h_n), jax.nn.log_sigmoid(gates), 0.0)
    hi = log_f.astype(BF16)
    lo = (log_f - hi.astype(F32)).astype(BF16)
    k_ref[...] = (_dot(xn, w_ref[:, qk:2 * qk]) * (dk ** -0.5)).astype(BF16)
    csum = _dot(tri_ref[...], jnp.concatenate([hi, lo], axis=1))
    gcol = jnp.where(lane < h_n, gates, csum[:, :LANES] + csum[:, LANES:])
    gcol_ref[...] = gcol
    v_ref[...] = _dot(xn, w_ref[:, 2 * qk:2 * qk + vd]).astype(BF16)
    g_t = gcol.T
    grow_ref[0] = jnp.concatenate([g_t[0:h_n] - g_t[h_n:2 * h_n], g_t[h_n:2 * h_n]], axis=0)
    og_ref[...] = jax.nn.sigmoid(_dot(xn, w_ref[:, 2 * qk + vd:2 * qk + 2 * vd])).astype(BF16)


def _mlstm_proj(x3, g, w_main, w_gates, b_gates, qk, vd, chunk):
    b, s, d = x3.shape
    t = b * s
    tm = _row_tile(s, 512)
    nt = s // tm
    assert chunk & (chunk - 1) == 0 and tm % chunk == 0
    row = lambda n: pl.BlockSpec((tm, n), lambda i: (i, 0))
    return pl.pallas_call(
        functools.partial(_mlstm_proj_body, qk=qk, vd=vd, dk=qk // M_HEADS, chunk=chunk),
        grid=(t // tm,),
        in_specs=[row(d), _resident((1, d)), _resident(w_main.shape),
                  _resident(w_gates.shape), _resident((1, LANES))],
        out_specs=[row(qk), row(qk), row(vd), row(vd), row(LANES),
                   pl.BlockSpec((1, 2 * M_HEADS, tm), lambda i: (i // nt, 0, i % nt))],
        out_shape=[jax.ShapeDtypeStruct((t, qk), BF16), jax.ShapeDtypeStruct((t, qk), BF16),
                   jax.ShapeDtypeStruct((t, vd), BF16), jax.ShapeDtypeStruct((t, vd), BF16),
                   jax.ShapeDtypeStruct((t, LANES), F32),
                   jax.ShapeDtypeStruct((b, 2 * M_HEADS, s), F32)],
        scratch_shapes=[pltpu.VMEM((tm, tm), BF16)],
        compiler_params=_cparams(1),
        name="mlstm_proj",
    )(x3.reshape(t, d), g.reshape(1, d), w_main, w_gates, b_gates)


def _mlstm_chunk_items(units, load, state, emit, rows):
    r_i = lax.broadcasted_iota(jnp.int32, (rows, rows), 0)
    c_i = lax.broadcasted_iota(jnp.int32, (rows, rows), 1)
    causal = r_i >= c_i
    us = range(len(units))
    st = {}

    def products():
        st["x"] = x = [load(u) for u in units]
        st["old"] = [state[u] for u in units]
        st["qk"] = [_dot_nt(x[i]["q"], x[i]["k"]) for i in us]
        st["qc"] = [_dot(x[i]["q"], st["old"][i][0].astype(BF16)) for i in us]

    def weights():
        x, old = st["x"], st["old"]
        log_d = [jnp.where(causal, x[i]["b_c"] + x[i]["imb"], -jnp.inf) for i in us]
        log_inter = [x[i]["b_c"] + old[i][2] for i in us]
        st["m_t"] = m_t = [jnp.maximum(log_inter[i], jnp.max(log_d[i], axis=-1, keepdims=True)) for i in us]
        st["inter"] = [jnp.exp(log_inter[i] - m_t[i]) for i in us]
        st["s"] = [st["qk"][i] * jnp.exp(log_d[i] - m_t[i]) for i in us]

    def hidden():
        x, old, s, inter, m_t = st["x"], st["old"], st["s"], st["inter"], st["m_t"]
        s_v = [_dot(s[i].astype(BF16), x[i]["v"]) for i in us]
        den = [inter[i] * jnp.sum(x[i]["q"].astype(F32) * old[i][1], axis=-1, keepdims=True)
               + jnp.sum(s[i], axis=-1, keepdims=True) for i in us]
        st["hid"] = [(inter[i] * st["qc"][i] + s_v[i]) / jnp.maximum(jnp.abs(den[i]), jnp.exp(-m_t[i]))
                     for i in us]

    def update():
        x, old = st["x"], st["old"]
        b_last = [x[i]["b_c"][rows - 1:rows, :] for i in us]
        log_w = [b_last[i] - x[i]["b_c"] + x[i]["i_c"] for i in us]
        m_new = [jnp.maximum(b_last[i] + old[i][2], jnp.max(log_w[i], axis=0, keepdims=True)) for i in us]
        decay = [jnp.exp(b_last[i] + old[i][2] - m_new[i]) for i in us]
        kw = [jnp.exp(log_w[i] - m_new[i]) * x[i]["k"].astype(F32) for i in us]
        kw_v = [_dot_tn(kw[i].astype(BF16), x[i]["v"]) for i in us]
        for i, u in enumerate(units):
            state[u] = (decay[i] * old[i][0] + kw_v[i],
                        decay[i] * old[i][1] + jnp.sum(kw[i], axis=0, keepdims=True), m_new[i])

    def output():
        hid = [st["hid"][i] * lax.rsqrt(jnp.mean(st["hid"][i] * st["hid"][i], axis=-1, keepdims=True) + RMS_EPS)
               for i in us]
        for i, u in enumerate(units):
            emit(u, (hid[i] * st["x"][i]["ng"] * st["x"][i]["og"].astype(F32)).astype(BF16))

    return [products, weights, hidden, update, output]


def _mlstm_cell_body(q_ref, k_ref, v_ref, og_ref, gcol_ref, grow_ref, ng_ref, c0_ref, n0_ref, m0_ref,
                     hg_ref, c_ref, n_ref, m_ref, *, dk, dv, bt):
    h_n = M_HEADS
    rows = q_ref.shape[1]
    units = [(b, h) for b in range(bt) for h in range(h_n)]

    def load(u):
        b, h = u
        kc, vc = slice(h * dk, (h + 1) * dk), slice(h * dv, (h + 1) * dv)
        return dict(q=q_ref[b, :, kc], k=k_ref[b, :, kc], v=v_ref[b, :, vc], og=og_ref[b, :, vc],
                    ng=ng_ref[:, vc], b_c=gcol_ref[b, :, h_n + h:h_n + h + 1], i_c=gcol_ref[b, :, h:h + 1],
                    imb=grow_ref[b, h:h + 1, :])

    def emit(u, value):
        b, h = u
        hg_ref[b, :, h * dv:(h + 1) * dv] = value

    state = {(b, h): (c0_ref[b, h], n0_ref[b, h:h + 1, :], m0_ref[b, :, h:h + 1]) for b, h in units}
    for thunk in _mlstm_chunk_items(units, load, state, emit, rows):
        thunk()
    for b, h in units:
        c_ref[b, h], n_ref[b, h:h + 1, :], m_ref[b, :, h:h + 1] = state[(b, h)]


def _mlstm_cell(q, k, v, og, gcol, grow, norm_g, c0, n0, m0, bt):
    b, s, qk = q.shape
    vd = v.shape[-1]
    dk, dv = qk // M_HEADS, vd // M_HEADS
    assert b % bt == 0
    tok = lambda n: pl.BlockSpec((bt, s, n), lambda i: (i, 0, 0))
    st_c = pl.BlockSpec((bt, M_HEADS, dk, dv), lambda i: (i, 0, 0, 0))
    st_n = pl.BlockSpec((bt, M_HEADS, dk), lambda i: (i, 0, 0))
    st_m = pl.BlockSpec((bt, 1, M_HEADS), lambda i: (i, 0, 0))
    return pl.pallas_call(
        functools.partial(_mlstm_cell_body, dk=dk, dv=dv, bt=bt),
        grid=(b // bt,),
        in_specs=[tok(qk), tok(qk), tok(vd), tok(vd), tok(LANES),
                  pl.BlockSpec((bt, 2 * M_HEADS, s), lambda i: (i, 0, 0)),
                  pl.BlockSpec((1, vd), lambda i: (0, 0)), st_c, st_n, st_m],
        out_specs=[tok(vd), st_c, st_n, st_m],
        out_shape=[jax.ShapeDtypeStruct((b, s, vd), BF16),
                   jax.ShapeDtypeStruct((b, M_HEADS, dk, dv), F32),
                   jax.ShapeDtypeStruct((b, M_HEADS, dk), F32),
                   jax.ShapeDtypeStruct((b, 1, M_HEADS), F32)],
        compiler_params=_cparams(1),
        name="mlstm_cell",
    )(q, k, v, og, gcol, grow, norm_g.reshape(1, vd), c0, n0, m0)


def _mlstm_ffn_body(q_ref, k_ref, v_ref, og_ref, gcol_ref, grow_ref, ng_ref, wa_ref,
                    x_ref, g_ref, wgu_ref, wd_ref, o_ref, c_ref, n_ref, m_ref, act_ref, hg_ref,
                    *, d_ff, tf, dk, dv, chunk, tiles_per_seq):
    h_n = M_HEADS
    i = pl.program_id(0)
    last = pl.num_programs(0) - 1
    live = i < last
    first_of_seq = lax.rem(jnp.minimum(i, last - 1), tiles_per_seq) == 0

    @pl.when(i == 0)
    def _():
        hg_ref[...] = jnp.zeros_like(hg_ref)

    @pl.when(first_of_seq & live)
    def _():
        c_ref[...] = jnp.zeros_like(c_ref)
        n_ref[...] = jnp.zeros_like(n_ref)
        m_ref[...] = jnp.zeros_like(m_ref)

    cur = lax.rem(i, 2)
    heads = list(range(h_n))
    state = {h: (c_ref[0, h], n_ref[0, h:h + 1, :], m_ref[0, :, h:h + 1]) for h in heads}
    side = []
    for j in range(q_ref.shape[0] // chunk):
        rows = slice(j * chunk, (j + 1) * chunk)

        def load(h, rows=rows):
            kc, vc = slice(h * dk, (h + 1) * dk), slice(h * dv, (h + 1) * dv)
            return dict(q=q_ref[rows, kc], k=k_ref[rows, kc], v=v_ref[rows, vc], og=og_ref[rows, vc],
                        ng=ng_ref[:, vc], b_c=gcol_ref[rows, h_n + h:h_n + h + 1], i_c=gcol_ref[rows, h:h + 1],
                        imb=grow_ref[0, h:h + 1, rows])

        def emit(h, value, rows=rows):
            hg_ref[cur, rows, h * dv:(h + 1) * dv] = value

        side += _mlstm_chunk_items(heads, load, state, emit, chunk)
    get_x = lambda: x_ref[...] + _dot(hg_ref[1 - cur], wa_ref[...])
    _interleave(_ffn_items(get_x, g_ref, wgu_ref, wd_ref, o_ref, act_ref, d_ff, tf), side)

    @pl.when(live)
    def _():
        for h in heads:
            c_ref[0, h], n_ref[0, h:h + 1, :], m_ref[0, :, h:h + 1] = state[h]


def _mlstm_ffn(x, q, k, v, og, gcol, grow, norm_g, wa, g, wgu, wd, chunk):
    t, d = x.shape
    b, _, s = grow.shape
    d_ff = wd.shape[0]
    qk, vd = q.shape[1], v.shape[1]
    dk, dv = qk // M_HEADS, vd // M_HEADS
    tm = _row_tile(s, 512)
    nt, tps = t // tm, s // tm
    assert tm % chunk == 0
    mix_tile = lambda i: jnp.minimum(i, nt - 1)
    ffn_row = lambda n: pl.BlockSpec((tm, n), lambda i: (jnp.maximum(i - 1, 0), 0))
    mix_row = lambda n: pl.BlockSpec((tm, n), lambda i: (mix_tile(i), 0))
    seq = lambda *tail: pl.BlockSpec((1,) + tail, lambda i: (mix_tile(i) // tps,) + (0,) * len(tail))
    return pl.pallas_call(
        functools.partial(_mlstm_ffn_body, d_ff=d_ff, tf=FFN_TF, dk=dk, dv=dv, chunk=chunk, tiles_per_seq=tps),
        grid=(nt + 1,),
        in_specs=[mix_row(qk), mix_row(qk), mix_row(vd), mix_row(vd), mix_row(LANES),
                  pl.BlockSpec((1, 2 * M_HEADS, tm), lambda i: (mix_tile(i) // tps, 0, mix_tile(i) % tps)),
                  _resident((1, vd)), _resident(wa.shape),
                  ffn_row(d), _resident((1, d)), _resident((d, 2 * d_ff)), _resident((d_ff, d))],
        out_specs=[ffn_row(d), seq(M_HEADS, dk, dv), seq(M_HEADS, dk), seq(1, M_HEADS)],
        out_shape=[jax.ShapeDtypeStruct((t, d), F32),
                   jax.ShapeDtypeStruct((b, M_HEADS, dk, dv), F32),
                   jax.ShapeDtypeStruct((b, M_HEADS, dk), F32),
                   jax.ShapeDtypeStruct((b, 1, M_HEADS), F32)],
        scratch_shapes=[pltpu.VMEM((tm, d_ff), BF16), pltpu.VMEM((2, tm, vd), BF16)],
        compiler_params=_cparams(1),
        name="mlstm_ffn",
    )(q, k, v, og, gcol, grow, norm_g.reshape(1, vd), wa, x, g.reshape(1, d), wgu, wd)


def _mlstm_layer(x3, state, p):
    b, s, d = x3.shape
    qk, vd = p["mlstm_qk"], p["mlstm_vd"]
    proj = functools.partial(_mlstm_proj, g=p["mix_norm0"], w_main=p["mlstm_w_main"],
                             w_gates=p["mlstm_w_gates"], b_gates=p["mlstm_b_gates"], qk=qk, vd=vd)
    if state is None:
        chunk = min(s, 128)
        q, k, v, og, gcol, grow = proj(x3, chunk=chunk)
        return dict(kind="mlstm", args=(q, k, v, og, gcol, grow, p["mlstm_out_norm"], p["mlstm_w_out"]),
                    chunk=chunk), None
    c0, n0, m0 = state
    q, k, v, og, gcol, grow = proj(x3.reshape(1, b * s, d), chunk=s)
    grow = jnp.swapaxes(grow.reshape(2 * M_HEADS, b, s), 0, 1)
    sh = lambda a: a.reshape(b, s, a.shape[-1])
    hg, c1, n1, m1 = _mlstm_cell(sh(q), sh(k), sh(v), sh(og), sh(gcol), grow, p["mlstm_out_norm"],
                                 c0, n0, m0.reshape(b, 1, M_HEADS), math.gcd(b, 8))
    return (hg.reshape(b * s, vd), p["mlstm_w_out"]), (c1, n1, m1.reshape(b, M_HEADS))


def _attn_proj_body(x_ref, g_ref, w_ref, wvt_ref, qg_ref, kg_ref, q_ref, k_ref, vt_ref, *, dh, nq, nk):
    xn = _rms(x_ref[...], g_ref[...]).astype(BF16)
    r_i = lax.broadcasted_iota(jnp.int32, (LANES, LANES), 0)
    c_i = lax.broadcasted_iota(jnp.int32, (LANES, LANES), 1)
    same_half = jnp.where((r_i < dh) == (c_i < dh), 1.0, 0.0).astype(BF16)

    def half_norm(y, gain):
        sq = y * y
        hi = sq.astype(BF16)
        lo = (sq - hi.astype(F32)).astype(BF16)
        ms = (_dot(hi, same_half) + _dot(lo, same_half)) * (1.0 / dh)
        return y * lax.rsqrt(ms + RMS_EPS) * gain

    q = _dot(xn, w_ref[:, 0:nq])
    for s in range(nq // LANES):
        sl = slice(s * LANES, (s + 1) * LANES)
        q_ref[:, sl] = half_norm(q[:, sl], qg_ref[...]).astype(BF16)
    k = _dot(xn, w_ref[:, nq:nq + nk])
    for s in range(nk // LANES):
        sl = slice(s * LANES, (s + 1) * LANES)
        k_ref[:, sl] = half_norm(k[:, sl], kg_ref[...])
    vt_ref[0] = _dot_nt(wvt_ref[...], xn)


def _attn_proj(x3, g, w, wvt, qg, kg, dh, nq, nk):
    b, s, d = x3.shape
    t = b * s
    nv = wvt.shape[0]
    tm = _row_tile(s, 512)
    nt = s // tm
    row = lambda n: pl.BlockSpec((tm, n), lambda i: (i, 0))
    return pl.pallas_call(
        functools.partial(_attn_proj_body, dh=dh, nq=nq, nk=nk),
        grid=(t // tm,),
        in_specs=[row(d), _resident((1, d)), _resident(w.shape), _resident(wvt.shape),
                  _resident((1, LANES)), _resident((1, LANES))],
        out_specs=[row(nq), row(nk), pl.BlockSpec((1, nv, tm), lambda i: (i // nt, 0, i % nt))],
        out_shape=[jax.ShapeDtypeStruct((t, nq), BF16), jax.ShapeDtypeStruct((t, nk), F32),
                   jax.ShapeDtypeStruct((b, nv, s), F32)],
        compiler_params=_cparams(1),
        name="attn_proj",
    )(x3.reshape(t, d), g.reshape(1, d), w, wvt, qg, kg)


def _bias_body(rel_ref, o_ref, *, tq, key_major):
    variant = pl.program_id(0)
    h = pl.program_id(1)
    nk = 2 * WINDOW
    shape, q_ax = ((nk, tq), 1) if key_major else ((tq, nk), 0)
    qi = lax.broadcasted_iota(jnp.int32, shape, q_ax)
    ki = lax.broadcasted_iota(jnp.int32, shape, 1 - q_ax)
    dist = qi + WINDOW - ki
    max_exact = NUM_BUCKETS // 2
    d = jnp.maximum(dist, 0)
    log_ratio = (jnp.log(jnp.maximum(d, 1).astype(F32) / max_exact)
                 / math.log(MAX_DISTANCE / max_exact))
    large = jnp.minimum(max_exact + (log_ratio * (NUM_BUCKETS - max_exact)).astype(jnp.int32),
                        NUM_BUCKETS - 1)
    bucket = jnp.where(d < max_exact, d, large)
    bias = jnp.zeros(shape, F32)
    for bkt in range(NUM_BUCKETS):
        bias = jnp.where(bucket == bkt, rel_ref[bkt, h], bias)
    visible = (dist >= 0) & (dist < WINDOW) & ((ki >= WINDOW) | (variant > 0))
    o_ref[0, 0] = jnp.where(visible, bias, -jnp.inf)


def _bias_table(rel_bias, tq, key_major):
    nk = 2 * WINDOW
    if key_major:
        spec = pl.BlockSpec((1, 1, nk, tq), lambda v, h: (v, h // 2, 0, h % 2))
        shape = (2, A_HEADS_Q // 2, nk, 2 * tq)
    else:
        spec = pl.BlockSpec((1, 1, tq, nk), lambda v, h: (v, h, 0, 0))
        shape = (2, A_HEADS_Q, tq, nk)
    return pl.pallas_call(
        functools.partial(_bias_body, tq=tq, key_major=key_major),
        grid=(2, A_HEADS_Q),
        in_specs=[pl.BlockSpec(memory_space=pltpu.SMEM)],
        out_specs=spec,
        out_shape=jax.ShapeDtypeStruct(shape, F32),
        compiler_params=_cparams(2),
        name="rel_bias_table",
    )(rel_bias)


def _attn_tile_items(sink_ref, q_ref, kp_ref, ko_ref, vp_ref, vo_ref, bias_ref, first_of_seq, write, dh):
    blk = WINDOW
    n_blk = ko_ref.shape[0] // blk
    slots_per_kv = A_GROUP // 2
    n_slots = A_HEADS_Q // 2
    lane = lax.broadcasted_iota(jnp.int32, (blk, LANES), 1)
    lo_half = lane < dh
    left = lax.broadcasted_iota(jnp.int32, (1, 2 * blk), 1) < blk
    ones = jnp.ones((dh, 2 * blk), BF16)
    sinks = [jnp.where(left, sink_ref[2 * i], sink_ref[2 * i + 1]) for i in range(n_slots)]
    items = []
    for j in range(n_blk):
        st = {}
        own = slice(j * blk, (j + 1) * blk)
        before = slice((j - 1) * blk, j * blk)
        variant = jnp.where(first_of_seq, 0, 1) if j == 0 else 1

        def scores(st=st, j=j, own=own, before=before, variant=variant):
            st["vts"], st["sts"] = [], []
            for g in range(A_HEADS_KV):
                sl = slice(g * LANES, (g + 1) * LANES)
                rows = slice(g * dh, (g + 1) * dh)
                k_prev = kp_ref[:, sl] if j == 0 else ko_ref[before, sl]
                v_prev = vp_ref[0, rows, :] if j == 0 else vo_ref[0, rows, before]
                kk = jnp.concatenate([k_prev, ko_ref[own, sl]], axis=0).astype(BF16)
                vt = jnp.concatenate([v_prev, vo_ref[0, rows, own]], axis=1).astype(BF16)
                st["vts"].append(jnp.concatenate([vt, ones], axis=0))
                for pr in range(slots_per_kv):
                    slot_i = g * slots_per_kv + pr
                    slot = q_ref[own, slot_i * LANES:(slot_i + 1) * LANES]
                    zero = jnp.zeros_like(slot)
                    q2 = jnp.concatenate([jnp.where(lo_half, slot, zero), jnp.where(lo_half, zero, slot)],
                                         axis=0)
                    st["sts"].append(_dot_nt(kk, q2) + bias_ref[variant, slot_i])

        def softmax(st=st):
            st["ms"] = [jnp.maximum(jnp.max(st["sts"][i], axis=0, keepdims=True), sinks[i])
                        for i in range(n_slots)]
            st["pts"] = [jnp.exp(st["sts"][i] - st["ms"][i]).astype(BF16) for i in range(n_slots)]

        def values(st=st):
            st["oes"] = [_dot(st["vts"][i // slots_per_kv], st["pts"][i]) for i in range(n_slots)]

        def finish(st=st, j=j):
            heads_t = []
            for i in range(n_slots):
                oe = st["oes"][i]
                ot = oe[0:dh, :] / (oe[dh:dh + 1, :] + jnp.exp(sinks[i] - st["ms"][i]))
                heads_t += [ot[:, 0:blk], ot[:, blk:2 * blk]]
            write(j, jnp.concatenate(heads_t, axis=0).T.astype(BF16))

        items += [scores, softmax, values, finish]
    return items


def _attn_ffn_body(sink_ref, q_ref, kp_ref, ko_ref, vp_ref, vo_ref, bias_ref, wa_ref,
                   x_ref, g_ref, wgu_ref, wd_ref, o_ref, act_ref, att_ref, *, d_ff, tf, dh, tiles_per_seq):
    i = pl.program_id(0)

    @pl.when(i == 0)
    def _():
        att_ref[...] = jnp.zeros_like(att_ref)

    cur = lax.rem(i, 2)
    first_of_seq = lax.rem(jnp.minimum(i, pl.num_programs(0) - 2), tiles_per_seq) == 0

    def write(j, tile):
        att_ref[cur, j * WINDOW:(j + 1) * WINDOW, :] = tile

    side = _attn_tile_items(sink_ref, q_ref, kp_ref, ko_ref, vp_ref, vo_ref, bias_ref,
                            first_of_seq, write, dh)
    get_x = lambda: x_ref[...] + _dot(att_ref[1 - cur], wa_ref[...])
    _interleave(_ffn_items(get_x, g_ref, wgu_ref, wd_ref, o_ref, act_ref, d_ff, tf), side)


def _attn_ffn(x, q, k, vt, bias, sinks, wa, g, wgu, wd, dh):
    t, d = x.shape
    b, vd, s = vt.shape
    d_ff = wd.shape[0]
    qd, kd = q.shape[1], k.shape[1]
    blk = WINDOW
    tm = _row_tile(s, 512)
    nt, tps, bpt = t // tm, s // tm, tm // blk
    att_tile = lambda i: jnp.minimum(i, nt - 1)
    ffn_tile = lambda i: jnp.maximum(i - 1, 0)
    ffn_row = lambda n: pl.BlockSpec((tm, n), lambda i: (ffn_tile(i), 0))
    att_row = lambda n: pl.BlockSpec((tm, n), lambda i: (att_tile(i), 0))
    return pl.pallas_call(
        functools.partial(_attn_ffn_body, d_ff=d_ff, tf=FFN_TF, dh=dh, tiles_per_seq=tps),
        grid=(nt + 1,),
        in_specs=[pl.BlockSpec(memory_space=pltpu.SMEM),
                  att_row(qd),
                  pl.BlockSpec((blk, kd), lambda i: (jnp.maximum(att_tile(i) * bpt - 1, 0), 0)),
                  att_row(kd),
                  pl.BlockSpec((1, vd, blk), lambda i: (att_tile(i) // tps,
                                                        0, jnp.maximum(att_tile(i) % tps * bpt - 1, 0))),
                  pl.BlockSpec((1, vd, tm), lambda i: (att_tile(i) // tps, 0, att_tile(i) % tps)),
                  _resident(bias.shape), _resident(wa.shape),
                  ffn_row(d), _resident((1, d)), _resident((d, 2 * d_ff)), _resident((d_ff, d))],
        out_specs=ffn_row(d),
        out_shape=jax.ShapeDtypeStruct((t, d), F32),
        scratch_shapes=[pltpu.VMEM((tm, d_ff), BF16), pltpu.VMEM((2, tm, qd), BF16)],
        compiler_params=_cparams(1),
        name="attn_ffn",
    )(sinks, q, k, k, vt, vt, bias, wa, x, g.reshape(1, d), wgu, wd)


def _attn_sample_body(sink_ref, q_ref, kc_ref, kn_ref, vc_ref, vn_ref, bias_ref,
                      o_ref, ko_ref, vo_ref, *, bt):
    tq = kn_ref.shape[1]
    sinks = [jnp.concatenate([jnp.full((tq, 1), sink_ref[g * A_GROUP + j], F32) for j in range(A_GROUP)],
                             axis=0) for g in range(A_HEADS_KV)]

    units = [(b, g) for b in range(bt) for g in range(A_HEADS_KV)]
    us = range(len(units))
    q = [q_ref[b, g] for b, g in units]
    sp = [_dot_nt(q[u], kc_ref[b, :, g, :].astype(BF16)) + bias_ref[g][:, :WINDOW]
          for u, (b, g) in enumerate(units)]
    so = [_dot_nt(q[u], kn_ref[b, :, g, :].astype(BF16)) + bias_ref[g][:, WINDOW:WINDOW + tq]
          for u, (b, g) in enumerate(units)]
    m = [jnp.maximum(jnp.maximum(jnp.max(sp[u], axis=-1, keepdims=True),
                                 jnp.max(so[u], axis=-1, keepdims=True)), sinks[g])
         for u, (b, g) in enumerate(units)]
    pp = [jnp.exp(sp[u] - m[u]) for u in us]
    po = [jnp.exp(so[u] - m[u]) for u in us]
    den = [jnp.sum(pp[u], axis=-1, keepdims=True) + jnp.sum(po[u], axis=-1, keepdims=True)
           + jnp.exp(sinks[g] - m[u]) for u, (b, g) in enumerate(units)]
    out = [_dot((pp[u] / den[u]).astype(BF16), vc_ref[b, :, g, :].astype(BF16))
           + _dot((po[u] / den[u]).astype(BF16), vn_ref[b, :, g, :].astype(BF16))
           for u, (b, g) in enumerate(units)]
    for u, (b, g) in enumerate(units):
        o_ref[b, g] = out[u].astype(BF16)
    ko_ref[:, 0:WINDOW - tq] = kc_ref[:, tq:WINDOW]
    ko_ref[:, WINDOW - tq:WINDOW] = kn_ref[...]
    vo_ref[:, 0:WINDOW - tq] = vc_ref[:, tq:WINDOW]
    vo_ref[:, WINDOW - tq:WINDOW] = vn_ref[...]


def _attn_sample(q, k, v, k_cache, v_cache, bias, sinks):
    b, n_kv, rows, dh = q.shape
    tq = k.shape[1]
    bt = math.gcd(b, 8)
    lead = lambda *tail: pl.BlockSpec((bt,) + tail, lambda i: (i, 0, 0, 0))
    return pl.pallas_call(
        functools.partial(_attn_sample_body, bt=bt),
        grid=(b // bt,),
        in_specs=[pl.BlockSpec(memory_space=pltpu.SMEM),
                  lead(n_kv, rows, dh), lead(WINDOW, n_kv, dh), lead(tq, n_kv, dh),
                  lead(WINDOW, n_kv, dh), lead(tq, n_kv, dh),
                  pl.BlockSpec(bias.shape, lambda i: (0, 0, 0))],
        out_specs=[lead(n_kv, rows, dh), lead(WINDOW, n_kv, dh), lead(WINDOW, n_kv, dh)],
        out_shape=[jax.ShapeDtypeStruct(q.shape, BF16),
                   jax.ShapeDtypeStruct(k_cache.shape, F32),
                   jax.ShapeDtypeStruct(v_cache.shape, F32)],
        compiler_params=_cparams(1),
        name="attn_sample",
    )(sinks, q, k_cache, k, v_cache, v, bias)


def _swa_layer(x3, buffers, p):
    b, s, d = x3.shape
    dh = p["attn_dh"]
    kd = A_HEADS_KV * dh
    x = x3.reshape(b * s, d)
    kv_shape = (b, WINDOW, A_HEADS_KV, dh)
    if buffers is None:
        nq, nk = A_HEADS_Q * dh, A_HEADS_KV * LANES
        q, k, vt = _attn_proj(x3, p["mix_norm1"], p["attn_w_prompt"], p["attn_wvt"],
                              p["attn_qg"], p["attn_kg"], dh, nq, nk)
        bias = _bias_table(p["rel_bias"], WINDOW, key_major=True)
        new_k = k.reshape(b, s, nk)[:, s - WINDOW:].reshape(b, WINDOW, A_HEADS_KV, LANES)[..., :dh]
        new_v = jnp.swapaxes(vt[:, :, s - WINDOW:], 1, 2)
        mixed = dict(kind="attn", args=(q, k, vt, bias, p["attn_sinks"], p["attn_w_out"]), dh=dh)
    else:
        k_buf, v_buf = buffers
        nq, nk = A_HEADS_Q * dh, kd
        q, k, vt = _attn_proj(x.reshape(1, b * s, d), p["mix_norm1"], p["attn_w_sample"], p["attn_wvt"],
                              p["attn_qg"], p["attn_kg"], dh, nq, nk)
        rows = A_GROUP * s
        q = jnp.transpose(q.reshape(b, s, A_HEADS_KV, A_GROUP, dh), (0, 2, 3, 1, 4))
        bias = _bias_table(p["rel_bias"], s, key_major=False)[1].reshape(A_HEADS_KV, rows, 2 * WINDOW)
        att, new_k, new_v = _attn_sample(q.reshape(b, A_HEADS_KV, rows, dh), k.reshape(b, s, A_HEADS_KV, dh),
                                         vt[0].T.reshape(b, s, A_HEADS_KV, dh), k_buf, v_buf,
                                         bias, p["attn_sinks"])
        att = jnp.transpose(att.reshape(b, A_HEADS_KV, A_GROUP, s, dh), (0, 3, 1, 2, 4))
        mixed = (att.reshape(b * s, nq), p["attn_w_out"])
    return mixed, (new_k.reshape(kv_shape), new_v.reshape(kv_shape))


def _attn_weights(w_in, w_out, q_norm, k_norm, dh):
    d = w_in.shape[0]
    qd, kd = A_HEADS_Q * dh, A_HEADS_KV * dh
    wk = w_in[:, qd:qd + kd].reshape(d, A_HEADS_KV, dh)
    k_dup = jnp.concatenate([wk, wk], axis=-1).reshape(d, A_HEADS_KV * LANES)
    w_prompt = jnp.concatenate([w_in[:, :qd], k_dup], axis=1).astype(BF16)
    qg = jnp.concatenate([q_norm, q_norm]).reshape(1, LANES) * (dh ** -0.5)
    kg = jnp.concatenate([k_norm, k_norm]).reshape(1, LANES)
    return {"attn_w_prompt": w_prompt, "attn_w_sample": w_in[:, :qd + kd].astype(BF16),
            "attn_wvt": w_in[:, qd + kd:].T.astype(BF16),
            "attn_w_out": w_out.astype(BF16),
            "attn_qg": qg, "attn_kg": kg}


def _trunk(x3, mlstm_state, swa_buffers, p):
    b, s, d = x3.shape

    def ffn(x, name, layer, mixed=None):
        w = (p[name + "_norm"][layer], p[name + "_wgu"][layer], p[name + "_wd"][layer])
        if not isinstance(mixed, dict):
            return _ffn(x, *w, pre=mixed)
        if mixed["kind"] == "attn":
            return _attn_ffn(x, *mixed["args"], *w, mixed["dh"])
        return _mlstm_ffn(x, *mixed["args"], *w, mixed["chunk"])

    x = ffn(x3.reshape(b * s, d), "ffn1", 0)
    mixed, new_mlstm = _mlstm_layer(x.reshape(b, s, d), mlstm_state, p)
    x = ffn(x, "ffn2", 0, mixed)
    if new_mlstm is None:
        x, c1, n1, m1 = x
        new_mlstm = (c1, n1, m1.reshape(b, M_HEADS))
    x = ffn(x, "ffn1", 1)
    mixed, new_swa = _swa_layer(x.reshape(b, s, d), swa_buffers, p)
    x = ffn(x, "ffn2", 1, mixed)
    return x.reshape(b, s, d), new_mlstm, new_swa


def kernel(x_prompt, x_sample, state_mlstm_C, state_mlstm_n, state_mlstm_m, cache_swa_k, cache_swa_v,
           ffn1_norm, ffn1_w_gate_up, ffn1_w_down, mix_norm, ffn2_norm, ffn2_w_gate_up, ffn2_w_down,
           mlstm_w_in, mlstm_b_gates, mlstm_out_norm, mlstm_w_out,
           attn_w_in, attn_q_norm, attn_k_norm, attn_sinks, rel_bias, attn_w_out):
    d = x_prompt.shape[-1]
    vd = mlstm_w_out.shape[0]
    n_gates = 2 * M_HEADS
    qk = (mlstm_w_in.shape[1] - 2 * vd - n_gates) // 2
    dh = attn_q_norm.shape[0]
    assert 2 * dh == LANES
    w_gates = jnp.zeros((d, LANES), BF16).at[:, :n_gates].set(mlstm_w_in[:, 2 * qk + 2 * vd:].astype(BF16))
    b_gates = jnp.zeros((1, LANES), F32).at[0, :n_gates].set(mlstm_b_gates)
    p = {
        "ffn1_norm": ffn1_norm, "ffn2_norm": ffn2_norm,
        "ffn1_wgu": [w.astype(BF16) for w in ffn1_w_gate_up], "ffn1_wd": [w.astype(BF16) for w in ffn1_w_down],
        "ffn2_wgu": [w.astype(BF16) for w in ffn2_w_gate_up], "ffn2_wd": [w.astype(BF16) for w in ffn2_w_down],
        "mix_norm0": mix_norm[0], "mix_norm1": mix_norm[1],
        "mlstm_qk": qk, "mlstm_vd": vd,
        "mlstm_w_main": mlstm_w_in[:, :2 * qk + 2 * vd].astype(BF16),
        "mlstm_w_gates": w_gates, "mlstm_b_gates": b_gates,
        "mlstm_out_norm": mlstm_out_norm, "mlstm_w_out": mlstm_w_out.astype(BF16),
        "attn_dh": dh, "attn_sinks": attn_sinks, "rel_bias": rel_bias,
    }
    p.update(_attn_weights(attn_w_in, attn_w_out, attn_q_norm, attn_k_norm, dh))
    y_p, (c_p, n_p, m_p), (k_p, v_p) = _trunk(x_prompt, None, None, p)
    y_s, (c_s, n_s, m_s), (k_s, v_s) = _trunk(
        x_sample, (state_mlstm_C, state_mlstm_n, state_mlstm_m), (cache_swa_k, cache_swa_v), p)
    return (y_p, y_s, c_p, n_p, m_p, k_p, v_p, c_s, n_s, m_s, k_s, v_s)
```

```python
import functools
import math

import jax
import jax.numpy as jnp
from jax import lax
from jax.experimental import pallas as pl
from jax.experimental.pallas import tpu as pltpu

F32 = jnp.float32
BF16 = jnp.bfloat16

M_HEADS = 4
A_HEADS_Q = 16
A_HEADS_KV = 4
A_GROUP = A_HEADS_Q // A_HEADS_KV
WINDOW = 128
NUM_BUCKETS = 32
MAX_DISTANCE = 128
FFN_RESIDUAL = 0.5
RMS_EPS = 1e-6
LANES = 128
VMEM_LIMIT = 56 * 1024 * 1024


def _cparams(n_axes):
    return pltpu.CompilerParams(
        dimension_semantics=("arbitrary",) * n_axes, vmem_limit_bytes=VMEM_LIMIT)


def _resident(shape):
    nd = len(shape)
    return pl.BlockSpec(shape, lambda *_: (0,) * nd, pipeline_mode=pl.Buffered(1))


def _rms(x, g):
    return x * lax.rsqrt(jnp.mean(x * x, axis=-1, keepdims=True) + RMS_EPS) * g


def _dot(a, b):
    return jnp.dot(a, b, preferred_element_type=F32)


def _dot_nt(a, b):
    return lax.dot_general(a, b, (((1,), (1,)), ((), ())), preferred_element_type=F32)


def _dot_tn(a, b):
    return lax.dot_general(a, b, (((0,), (0,)), ((), ())), preferred_element_type=F32)


def _row_tile(t, pref):
    tm = min(t, pref)
    assert t % tm == 0
    return tm


def _cast_body(w_ref, o_ref):
    o_ref[...] = w_ref[...].astype(o_ref.dtype)


def _layer_bf16(w, layer):
    _, r, c = w.shape
    tr = _row_tile(r, 256)
    return pl.pallas_call(
        _cast_body,
        grid=(r // tr,),
        in_specs=[pl.BlockSpec((None, tr, c), lambda i: (layer, i, 0))],
        out_specs=pl.BlockSpec((tr, c), lambda i: (i, 0)),
        out_shape=jax.ShapeDtypeStruct((r, c), BF16),
        compiler_params=_cparams(1),
        name="layer_bf16",
    )(w)


def _interleave(main, side):
    done = 0
    for j, thunk in enumerate(main):
        thunk()
        upto = (j + 1) * len(side) // len(main)
        for other in side[done:upto]:
            other()
        done = upto


def _ffn_items(get_x, g_ref, wgu_ref, wd_ref, o_ref, act_ref, d_ff, tf):
    st = {}

    def head():
        y = get_x()
        o_ref[...] = y
        st["xn"] = _rms(y, g_ref[...]).astype(BF16)

    def chunk(lo):
        gate = _dot(st["xn"], wgu_ref[:, lo:lo + tf])
        up = _dot(st["xn"], wgu_ref[:, d_ff + lo:d_ff + lo + tf])
        act_ref[:, lo:lo + tf] = (gate * jax.nn.sigmoid(gate) * up).astype(BF16)

    def tail():
        o_ref[...] = o_ref[...] + FFN_RESIDUAL * _dot(act_ref[...], wd_ref[...])

    return [head] + [functools.partial(chunk, lo) for lo in range(0, d_ff, tf)] + [tail]


def _ffn_body(*refs, d_ff, tf, has_pre):
    if has_pre:
        a_ref, wa_ref, x_ref, g_ref, wgu_ref, wd_ref, o_ref, act_ref = refs
        get_x = lambda: x_ref[...] + _dot(a_ref[...], wa_ref[...])
    else:
        x_ref, g_ref, wgu_ref, wd_ref, o_ref, act_ref = refs
        get_x = lambda: x_ref[...]
    for thunk in _ffn_items(get_x, g_ref, wgu_ref, wd_ref, o_ref, act_ref, d_ff, tf):
        thunk()


FFN_TF = 256


def _ffn(x, g, wgu, wd, pre=None):
    t, d = x.shape
    d_ff = wd.shape[0]
    tm = _row_tile(t, 1024)
    assert d_ff % FFN_TF == 0
    row = lambda n: pl.BlockSpec((tm, n), lambda i: (i, 0))
    pre_specs, pre_args = [], []
    if pre is not None:
        a, wa = pre
        pre_specs, pre_args = [row(a.shape[1]), _resident(wa.shape)], [a, wa]
    return pl.pallas_call(
        functools.partial(_ffn_body, d_ff=d_ff, tf=FFN_TF, has_pre=pre is not None),
        grid=(t // tm,),
        in_specs=pre_specs + [row(d), _resident((1, d)), _resident((d, 2 * d_ff)), _resident((d_ff, d))],
        out_specs=row(d),
        out_shape=jax.ShapeDtypeStruct((t, d), F32),
        scratch_shapes=[pltpu.VMEM((tm, d_ff), BF16)],
        compiler_params=_cparams(1),
        name="ffn",
    )(*pre_args, x, g.reshape(1, d), wgu, wd)


def _mlstm_proj_body(x_ref, g_ref, w_ref, wg_ref, bg_ref,
                     q_ref, k_ref, v_ref, og_ref, gcol_ref, grow_ref, tri_ref, *, qk, vd, dk, chunk):
    h_n = M_HEADS
    xn = _rms(x_ref[...], g_ref[...]).astype(BF16)
    tm = xn.shape[0]

    @pl.when(pl.program_id(0) == 0)
    def _():
        shift = chunk.bit_length() - 1
        r_i = lax.broadcasted_iota(jnp.int32, (tm, tm), 0)
        c_i = lax.broadcasted_iota(jnp.int32, (tm, tm), 1)
        same_chunk = lax.shift_right_logical(r_i, shift) == lax.shift_right_logical(c_i, shift)
        tri_ref[...] = jnp.where((r_i >= c_i) & same_chunk, 1.0, 0.0).astype(BF16)

    gates = _dot(xn, wg_ref[...]) + bg_ref[...]
    q_ref[...] = _dot(xn, w_ref[:, 0:qk]).astype(BF16)
    lane = lax.broadcasted_iota(jnp.int32, (tm, LANES), 1)
    log_f = jnp.where((lane >= h_n) & (lane < 2 * h_n), jax.nn.log_sigmoid(gates), 0.0)
    hi = log_f.astype(BF16)
    lo = (log_f - hi.astype(F32)).astype(BF16)
    k_ref[...] = (_dot(xn, w_ref[:, qk:2 * qk]) * (dk ** -0.5)).astype(BF16)
    csum = _dot(tri_ref[...], jnp.concatenate([hi, lo], axis=1))
    gcol = jnp.where(lane < h_n, gates, csum[:, :LANES] + csum[:, LANES:])
    gcol_ref[...] = gcol
    v_ref[...] = _dot(xn, w_ref[:, 2 * qk:2 * qk + vd]).astype(BF16)
    g_t = gcol.T
    grow_ref[0] = jnp.concatenate([g_t[0:h_n] - g_t[h_n:2 * h_n], g_t[h_n:2 * h_n]], axis=0)
    og_ref[...] = jax.nn.sigmoid(_dot(xn, w_ref[:, 2 * qk + vd:2 * qk + 2 * vd])).astype(BF16)


def _mlstm_proj(x3, g, w_main, w_gates, b_gates, qk, vd, chunk):
    b, s, d = x3.shape
    t = b * s
    tm = _row_tile(s, 512)
    nt = s // tm
    assert chunk & (chunk - 1) == 0 and tm % chunk == 0
    row = lambda n: pl.BlockSpec((tm, n), lambda i: (i, 0))
    return pl.pallas_call(
        functools.partial(_mlstm_proj_body, qk=qk, vd=vd, dk=qk // M_HEADS, chunk=chunk),
        grid=(t // tm,),
        in_specs=[row(d), _resident((1, d)), _resident(w_main.shape),
                  _resident(w_gates.shape), _resident((1, LANES))],
        out_specs=[row(qk), row(qk), row(vd), row(vd), row(LANES),
                   pl.BlockSpec((1, 2 * M_HEADS, tm), lambda i: (i // nt, 0, i % nt))],
        out_shape=[jax.ShapeDtypeStruct((t, qk), BF16), jax.ShapeDtypeStruct((t, qk), BF16),
                   jax.ShapeDtypeStruct((t, vd), BF16), jax.ShapeDtypeStruct((t, vd), BF16),
                   jax.ShapeDtypeStruct((t, LANES), F32),
                   jax.ShapeDtypeStruct((b, 2 * M_HEADS, s), F32)],
        scratch_shapes=[pltpu.VMEM((tm, tm), BF16)],
        compiler_params=_cparams(1),
        name="mlstm_proj",
    )(x3.reshape(t, d), g.reshape(1, d), w_main, w_gates, b_gates)


def _mlstm_chunk_items(units, load, state, emit, rows):
    r_i = lax.broadcasted_iota(jnp.int32, (rows, rows), 0)
    c_i = lax.broadcasted_iota(jnp.int32, (rows, rows), 1)
    causal = r_i >= c_i
    us = range(len(units))
    st = {}

    def products():
        st["x"] = x = [load(u) for u in units]
        st["old"] = [state[u] for u in units]
        st["qk"] = [_dot_nt(x[i]["q"], x[i]["k"]) for i in us]
        st["qc"] = [_dot(x[i]["q"], st["old"][i][0].astype(BF16)) for i in us]

    def weights():
        x, old = st["x"], st["old"]
        log_d = [jnp.where(causal, x[i]["b_c"] + x[i]["imb"], -jnp.inf) for i in us]
        log_inter = [x[i]["b_c"] + old[i][2] for i in us]
        st["m_t"] = m_t = [jnp.maximum(log_inter[i], jnp.max(log_d[i], axis=-1, keepdims=True)) for i in us]
        st["inter"] = [jnp.exp(log_inter[i] - m_t[i]) for i in us]
        st["s"] = [st["qk"][i] * jnp.exp(log_d[i] - m_t[i]) for i in us]

    def hidden():
        x, old, s, inter, m_t = st["x"], st["old"], st["s"], st["inter"], st["m_t"]
        s_v = [_dot(s[i].astype(BF16), x[i]["v"]) for i in us]
        den = [inter[i] * jnp.sum(x[i]["q"].astype(F32) * old[i][1], axis=-1, keepdims=True)
               + jnp.sum(s[i], axis=-1, keepdims=True) for i in us]
        st["hid"] = [(inter[i] * st["qc"][i] + s_v[i]) / jnp.maximum(jnp.abs(den[i]), jnp.exp(-m_t[i]))
                     for i in us]

    def update():
        x, old = st["x"], st["old"]
        b_last = [x[i]["b_c"][rows - 1:rows, :] for i in us]
        log_w = [b_last[i] - x[i]["b_c"] + x[i]["i_c"] for i in us]
        m_new = [jnp.maximum(b_last[i] + old[i][2], jnp.max(log_w[i], axis=0, keepdims=True)) for i in us]
        decay = [jnp.exp(b_last[i] + old[i][2] - m_new[i]) for i in us]
        kw = [jnp.exp(log_w[i] - m_new[i]) * x[i]["k"].astype(F32) for i in us]
        kw_v = [_dot_tn(kw[i].astype(BF16), x[i]["v"]) for i in us]
        for i, u in enumerate(units):
            state[u] = (decay[i] * old[i][0] + kw_v[i],
                        decay[i] * old[i][1] + jnp.sum(kw[i], axis=0, keepdims=True), m_new[i])

    def output():
        hid = [st["hid"][i] * lax.rsqrt(jnp.mean(st["hid"][i] * st["hid"][i], axis=-1, keepdims=True) + RMS_EPS)
               for i in us]
        for i, u in enumerate(units):
            emit(u, (hid[i] * st["x"][i]["ng"] * st["x"][i]["og"].astype(F32)).astype(BF16))

    return [products, weights, hidden, update, output]


def _mlstm_cell_body(q_ref, k_ref, v_ref, og_ref, gcol_ref, grow_ref, ng_ref, c0_ref, n0_ref, m0_ref,
                     hg_ref, c_ref, n_ref, m_ref, *, dk, dv, bt):
    h_n = M_HEADS
    rows = q_ref.shape[1]
    units = [(b, h) for b in range(bt) for h in range(h_n)]

    def load(u):
        b, h = u
        kc, vc = slice(h * dk, (h + 1) * dk), slice(h * dv, (h + 1) * dv)
        return dict(q=q_ref[b, :, kc], k=k_ref[b, :, kc], v=v_ref[b, :, vc], og=og_ref[b, :, vc],
                    ng=ng_ref[:, vc], b_c=gcol_ref[b, :, h_n + h:h_n + h + 1], i_c=gcol_ref[b, :, h:h + 1],
                    imb=grow_ref[b, h:h + 1, :])

    def emit(u, value):
        b, h = u
        hg_ref[b, :, h * dv:(h + 1) * dv] = value

    state = {(b, h): (c0_ref[b, h], n0_ref[b, h:h + 1, :], m0_ref[b, :, h:h + 1]) for b, h in units}
    for thunk in _mlstm_chunk_items(units, load, state, emit, rows):
        thunk()
    for b, h in units:
        c_ref[b, h], n_ref[b, h:h + 1, :], m_ref[b, :, h:h + 1] = state[(b, h)]


def _mlstm_cell(q, k, v, og, gcol, grow, norm_g, c0, n0, m0, bt):
    b, s, qk = q.shape
    vd = v.shape[-1]
    dk, dv = qk // M_HEADS, vd // M_HEADS
    assert b % bt == 0
    tok = lambda n: pl.BlockSpec((bt, s, n), lambda i: (i, 0, 0))
    st_c = pl.BlockSpec((bt, M_HEADS, dk, dv), lambda i: (i, 0, 0, 0))
    st_n = pl.BlockSpec((bt, M_HEADS, dk), lambda i: (i, 0, 0))
    st_m = pl.BlockSpec((bt, 1, M_HEADS), lambda i: (i, 0, 0))
    return pl.pallas_call(
        functools.partial(_mlstm_cell_body, dk=dk, dv=dv, bt=bt),
        grid=(b // bt,),
        in_specs=[tok(qk), tok(qk), tok(vd), tok(vd), tok(LANES),
                  pl.BlockSpec((bt, 2 * M_HEADS, s), lambda i: (i, 0, 0)),
                  pl.BlockSpec((1, vd), lambda i: (0, 0)), st_c, st_n, st_m],
        out_specs=[tok(vd), st_c, st_n, st_m],
        out_shape=[jax.ShapeDtypeStruct((b, s, vd), BF16),
                   jax.ShapeDtypeStruct((b, M_HEADS, dk, dv), F32),
                   jax.ShapeDtypeStruct((b, M_HEADS, dk), F32),
                   jax.ShapeDtypeStruct((b, 1, M_HEADS), F32)],
        compiler_params=_cparams(1),
        name="mlstm_cell",
    )(q, k, v, og, gcol, grow, norm_g.reshape(1, vd), c0, n0, m0)


def _mlstm_ffn_body(q_ref, k_ref, v_ref, og_ref, gcol_ref, grow_ref, ng_ref, wa_ref,
                    x_ref, g_ref, wgu_ref, wd_ref, o_ref, c_ref, n_ref, m_ref, act_ref, hg_ref,
                    *, d_ff, tf, dk, dv, chunk, tiles_per_seq):
    h_n = M_HEADS
    i = pl.program_id(0)
    last = pl.num_programs(0) - 1
    live = i < last
    first_of_seq = lax.rem(jnp.minimum(i, last - 1), tiles_per_seq) == 0

    @pl.when(i == 0)
    def _():
        hg_ref[...] = jnp.zeros_like(hg_ref)

    @pl.when(first_of_seq & live)
    def _():
        c_ref[...] = jnp.zeros_like(c_ref)
        n_ref[...] = jnp.zeros_like(n_ref)
        m_ref[...] = jnp.zeros_like(m_ref)

    cur = lax.rem(i, 2)
    heads = list(range(h_n))
    state = {h: (c_ref[0, h], n_ref[0, h:h + 1, :], m_ref[0, :, h:h + 1]) for h in heads}
    side = []
    for j in range(q_ref.shape[0] // chunk):
        rows = slice(j * chunk, (j + 1) * chunk)

        def load(h, rows=rows):
            kc, vc = slice(h * dk, (h + 1) * dk), slice(h * dv, (h + 1) * dv)
            return dict(q=q_ref[rows, kc], k=k_ref[rows, kc], v=v_ref[rows, vc], og=og_ref[rows, vc],
                        ng=ng_ref[:, vc], b_c=gcol_ref[rows, h_n + h:h_n + h + 1], i_c=gcol_ref[rows, h:h + 1],
                        imb=grow_ref[0, h:h + 1, rows])

        def emit(h, value, rows=rows):
            hg_ref[cur, rows, h * dv:(h + 1) * dv] = value

        side += _mlstm_chunk_items(heads, load, state, emit, chunk)
    get_x = lambda: x_ref[...] + _dot(hg_ref[1 - cur], wa_ref[...])
    _interleave(_ffn_items(get_x, g_ref, wgu_ref, wd_ref, o_ref, act_ref, d_ff, tf), side)

    @pl.when(live)
    def _():
        for h in heads:
            c_ref[0, h], n_ref[0, h:h + 1, :], m_ref[0, :, h:h + 1] = state[h]


def _mlstm_ffn(x, q, k, v, og, gcol, grow, norm_g, wa, g, wgu, wd, chunk):
    t, d = x.shape
    b, _, s = grow.shape
    d_ff = wd.shape[0]
    qk, vd = q.shape[1], v.shape[1]
    dk, dv = qk // M_HEADS, vd // M_HEADS
    tm = _row_tile(s, 512)
    nt, tps = t // tm, s // tm
    assert tm % chunk == 0
    mix_tile = lambda i: jnp.minimum(i, nt - 1)
    ffn_row = lambda n: pl.BlockSpec((tm, n), lambda i: (jnp.maximum(i - 1, 0), 0))
    mix_row = lambda n: pl.BlockSpec((tm, n), lambda i: (mix_tile(i), 0))
    seq = lambda *tail: pl.BlockSpec((1,) + tail, lambda i: (mix_tile(i) // tps,) + (0,) * len(tail))
    return pl.pallas_call(
        functools.partial(_mlstm_ffn_body, d_ff=d_ff, tf=FFN_TF, dk=dk, dv=dv, chunk=chunk, tiles_per_seq=tps),
        grid=(nt + 1,),
        in_specs=[mix_row(qk), mix_row(qk), mix_row(vd), mix_row(vd), mix_row(LANES),
                  pl.BlockSpec((1, 2 * M_HEADS, tm), lambda i: (mix_tile(i) // tps, 0, mix_tile(i) % tps)),
                  _resident((1, vd)), _resident(wa.shape),
                  ffn_row(d), _resident((1, d)), _resident((d, 2 * d_ff)), _resident((d_ff, d))],
        out_specs=[ffn_row(d), seq(M_HEADS, dk, dv), seq(M_HEADS, dk), seq(1, M_HEADS)],
        out_shape=[jax.ShapeDtypeStruct((t, d), F32),
                   jax.ShapeDtypeStruct((b, M_HEADS, dk, dv), F32),
                   jax.ShapeDtypeStruct((b, M_HEADS, dk), F32),
                   jax.ShapeDtypeStruct((b, 1, M_HEADS), F32)],
        scratch_shapes=[pltpu.VMEM((tm, d_ff), BF16), pltpu.VMEM((2, tm, vd), BF16)],
        compiler_params=_cparams(1),
        name="mlstm_ffn",
    )(q, k, v, og, gcol, grow, norm_g.reshape(1, vd), wa, x, g.reshape(1, d), wgu, wd)


def _mlstm_layer(x3, state, p):
    b, s, d = x3.shape
    qk, vd = p["mlstm_qk"], p["mlstm_vd"]
    proj = functools.partial(_mlstm_proj, g=p["mix_norm0"], w_main=p["mlstm_w_main"],
                             w_gates=p["mlstm_w_gates"], b_gates=p["mlstm_b_gates"], qk=qk, vd=vd)
    if state is None:
        chunk = min(s, 128)
        q, k, v, og, gcol, grow = proj(x3, chunk=chunk)
        return dict(kind="mlstm", args=(q, k, v, og, gcol, grow, p["mlstm_out_norm"], p["mlstm_w_out"]),
                    chunk=chunk), None
    c0, n0, m0 = state
    q, k, v, og, gcol, grow = proj(x3.reshape(1, b * s, d), chunk=s)
    grow = jnp.swapaxes(grow.reshape(2 * M_HEADS, b, s), 0, 1)
    sh = lambda a: a.reshape(b, s, a.shape[-1])
    hg, c1, n1, m1 = _mlstm_cell(sh(q), sh(k), sh(v), sh(og), sh(gcol), grow, p["mlstm_out_norm"],
                                 c0, n0, m0.reshape(b, 1, M_HEADS), math.gcd(b, 8))
    return (hg.reshape(b * s, vd), p["mlstm_w_out"]), (c1, n1, m1.reshape(b, M_HEADS))


def _attn_proj_body(x_ref, g_ref, w_ref, wvt_ref, qg_ref, kg_ref, q_ref, k_ref, vt_ref, *, dh, nq, nk):
    xn = _rms(x_ref[...], g_ref[...]).astype(BF16)
    r_i = lax.broadcasted_iota(jnp.int32, (LANES, LANES), 0)
    c_i = lax.broadcasted_iota(jnp.int32, (LANES, LANES), 1)
    same_half = jnp.where((r_i < dh) == (c_i < dh), 1.0, 0.0).astype(BF16)

    def half_norm(y, gain):
        sq = y * y
        hi = sq.astype(BF16)
        lo = (sq - hi.astype(F32)).astype(BF16)
        ms = (_dot(hi, same_half) + _dot(lo, same_half)) * (1.0 / dh)
        return y * lax.rsqrt(ms + RMS_EPS) * gain

    q = _dot(xn, w_ref[:, 0:nq])
    for s in range(nq // LANES):
        sl = slice(s * LANES, (s + 1) * LANES)
        q_ref[:, sl] = half_norm(q[:, sl], qg_ref[...]).astype(BF16)
    k = _dot(xn, w_ref[:, nq:nq + nk])
    for s in range(nk // LANES):
        sl = slice(s * LANES, (s + 1) * LANES)
        k_ref[:, sl] = half_norm(k[:, sl], kg_ref[...])
    vt_ref[0] = _dot_nt(wvt_ref[...], xn)


def _attn_proj(x3, g, w, wvt, qg, kg, dh, nq, nk):
    b, s, d = x3.shape
    t = b * s
    nv = wvt.shape[0]
    tm = _row_tile(s, 512)
    nt = s // tm
    row = lambda n: pl.BlockSpec((tm, n), lambda i: (i, 0))
    return pl.pallas_call(
        functools.partial(_attn_proj_body, dh=dh, nq=nq, nk=nk),
        grid=(t // tm,),
        in_specs=[row(d), _resident((1, d)), _resident(w.shape), _resident(wvt.shape),
                  _resident((1, LANES)), _resident((1, LANES))],
        out_specs=[row(nq), row(nk), pl.BlockSpec((1, nv, tm), lambda i: (i // nt, 0, i % nt))],
        out_shape=[jax.ShapeDtypeStruct((t, nq), BF16), jax.ShapeDtypeStruct((t, nk), F32),
                   jax.ShapeDtypeStruct((b, nv, s), F32)],
        compiler_params=_cparams(1),
        name="attn_proj",
    )(x3.reshape(t, d), g.reshape(1, d), w, wvt, qg, kg)


def _bias_body(rel_ref, o_ref, *, tq, key_major):
    nk = 2 * WINDOW
    shape, q_ax = ((nk, tq), 1) if key_major else ((tq, nk), 0)
    qi = lax.broadcasted_iota(jnp.int32, shape, q_ax)
    ki = lax.broadcasted_iota(jnp.int32, shape, 1 - q_ax)
    dist = qi + WINDOW - ki
    max_exact = NUM_BUCKETS // 2
    d = jnp.maximum(dist, 0)
    log_ratio = (jnp.log(jnp.maximum(d, 1).astype(F32) / max_exact)
                 / math.log(MAX_DISTANCE / max_exact))
    large = jnp.minimum(max_exact + (log_ratio * (NUM_BUCKETS - max_exact)).astype(jnp.int32),
                        NUM_BUCKETS - 1)
    bucket = jnp.where(d < max_exact, d, large)
    in_window = (dist >= 0) & (dist < WINDOW)
    visible = (in_window & (ki >= WINDOW), in_window)
    for h in range(A_HEADS_Q):
        bias = jnp.zeros(shape, F32)
        for bkt in range(NUM_BUCKETS):
            bias = jnp.where(bucket == bkt, rel_ref[bkt, h], bias)
        for variant in range(2):
            tile = jnp.where(visible[variant], bias, -jnp.inf)
            if key_major:
                o_ref[variant, h // 2, :, (h % 2) * tq:(h % 2 + 1) * tq] = tile
            else:
                o_ref[variant, h] = tile


def _bias_table(rel_bias, tq, key_major):
    nk = 2 * WINDOW
    shape = (2, A_HEADS_Q // 2, nk, 2 * tq) if key_major else (2, A_HEADS_Q, tq, nk)
    return pl.pallas_call(
        functools.partial(_bias_body, tq=tq, key_major=key_major),
        grid=(1,),
        in_specs=[pl.BlockSpec(memory_space=pltpu.SMEM)],
        out_specs=pl.BlockSpec(shape, lambda i: (0, 0, 0, 0)),
        out_shape=jax.ShapeDtypeStruct(shape, F32),
        compiler_params=_cparams(1),
        name="rel_bias_table",
    )(rel_bias)


def _attn_tile_items(sink_ref, q_ref, kp_ref, ko_ref, vp_ref, vo_ref, bias_ref, first_of_seq, write, dh):
    blk = WINDOW
    n_blk = ko_ref.shape[0] // blk
    slots_per_kv = A_GROUP // 2
    n_slots = A_HEADS_Q // 2
    lane = lax.broadcasted_iota(jnp.int32, (blk, LANES), 1)
    lo_half = lane < dh
    left = lax.broadcasted_iota(jnp.int32, (1, 2 * blk), 1) < blk
    ones = jnp.ones((dh, 2 * blk), BF16)
    sinks = [jnp.where(left, sink_ref[2 * i], sink_ref[2 * i + 1]) for i in range(n_slots)]
    items = []
    for j in range(n_blk):
        st = {}
        own = slice(j * blk, (j + 1) * blk)
        before = slice((j - 1) * blk, j * blk)
        variant = jnp.where(first_of_seq, 0, 1) if j == 0 else 1

        def scores(st=st, j=j, own=own, before=before, variant=variant):
            st["vts"], st["sts"] = [], []
            for g in range(A_HEADS_KV):
                sl = slice(g * LANES, (g + 1) * LANES)
                rows = slice(g * dh, (g + 1) * dh)
                k_prev = kp_ref[:, sl] if j == 0 else ko_ref[before, sl]
                v_prev = vp_ref[0, rows, :] if j == 0 else vo_ref[0, rows, before]
                kk = jnp.concatenate([k_prev, ko_ref[own, sl]], axis=0).astype(BF16)
                vt = jnp.concatenate([v_prev, vo_ref[0, rows, own]], axis=1).astype(BF16)
                st["vts"].append(jnp.concatenate([vt, ones], axis=0))
                for pr in range(slots_per_kv):
                    slot_i = g * slots_per_kv + pr
                    slot = q_ref[own, slot_i * LANES:(slot_i + 1) * LANES]
                    zero = jnp.zeros_like(slot)
                    q2 = jnp.concatenate([jnp.where(lo_half, slot, zero), jnp.where(lo_half, zero, slot)],
                                         axis=0)
                    st["sts"].append(_dot_nt(kk, q2) + bias_ref[variant, slot_i])

        def softmax(st=st):
            st["ms"] = [jnp.maximum(jnp.max(st["sts"][i], axis=0, keepdims=True), sinks[i])
                        for i in range(n_slots)]
            st["pts"] = [jnp.exp(st["sts"][i] - st["ms"][i]).astype(BF16) for i in range(n_slots)]

        def values(st=st):
            st["oes"] = [_dot(st["vts"][i // slots_per_kv], st["pts"][i]) for i in range(n_slots)]

        def finish(st=st, j=j):
            heads_t = []
            for i in range(n_slots):
                oe = st["oes"][i]
                ot = oe[0:dh, :] / (oe[dh:dh + 1, :] + jnp.exp(sinks[i] - st["ms"][i]))
                heads_t += [ot[:, 0:blk], ot[:, blk:2 * blk]]
            write(j, jnp.concatenate(heads_t, axis=0).T.astype(BF16))

        items += [scores, softmax, values, finish]
    return items


def _attn_ffn_body(sink_ref, q_ref, kp_ref, ko_ref, vp_ref, vo_ref, bias_ref, wa_ref,
                   x_ref, g_ref, wgu_ref, wd_ref, o_ref, act_ref, att_ref, *, d_ff, tf, dh, tiles_per_seq):
    i = pl.program_id(0)

    @pl.when(i == 0)
    def _():
        att_ref[...] = jnp.zeros_like(att_ref)

    cur = lax.rem(i, 2)
    first_of_seq = lax.rem(jnp.minimum(i, pl.num_programs(0) - 2), tiles_per_seq) == 0

    def write(j, tile):
        att_ref[cur, j * WINDOW:(j + 1) * WINDOW, :] = tile

    side = _attn_tile_items(sink_ref, q_ref, kp_ref, ko_ref, vp_ref, vo_ref, bias_ref,
                            first_of_seq, write, dh)
    get_x = lambda: x_ref[...] + _dot(att_ref[1 - cur], wa_ref[...])
    _interleave(_ffn_items(get_x, g_ref, wgu_ref, wd_ref, o_ref, act_ref, d_ff, tf), side)


def _attn_ffn(x, q, k, vt, bias, sinks, wa, g, wgu, wd, dh):
    t, d = x.shape
    b, vd, s = vt.shape
    d_ff = wd.shape[0]
    qd, kd = q.shape[1], k.shape[1]
    blk = WINDOW
    tm = _row_tile(s, 512)
    nt, tps, bpt = t // tm, s // tm, tm // blk
    att_tile = lambda i: jnp.minimum(i, nt - 1)
    ffn_tile = lambda i: jnp.maximum(i - 1, 0)
    ffn_row = lambda n: pl.BlockSpec((tm, n), lambda i: (ffn_tile(i), 0))
    att_row = lambda n: pl.BlockSpec((tm, n), lambda i: (att_tile(i), 0))
    return pl.pallas_call(
        functools.partial(_attn_ffn_body, d_ff=d_ff, tf=FFN_TF, dh=dh, tiles_per_seq=tps),
        grid=(nt + 1,),
        in_specs=[pl.BlockSpec(memory_space=pltpu.SMEM),
                  att_row(qd),
                  pl.BlockSpec((blk, kd), lambda i: (jnp.maximum(att_tile(i) * bpt - 1, 0), 0)),
                  att_row(kd),
                  pl.BlockSpec((1, vd, blk), lambda i: (att_tile(i) // tps,
                                                        0, jnp.maximum(att_tile(i) % tps * bpt - 1, 0))),
                  pl.BlockSpec((1, vd, tm), lambda i: (att_tile(i) // tps, 0, att_tile(i) % tps)),
                  _resident(bias.shape), _resident(wa.shape),
                  ffn_row(d), _resident((1, d)), _resident((d, 2 * d_ff)), _resident((d_ff, d))],
        out_specs=ffn_row(d),
        out_shape=jax.ShapeDtypeStruct((t, d), F32),
        scratch_shapes=[pltpu.VMEM((tm, d_ff), BF16), pltpu.VMEM((2, tm, qd), BF16)],
        compiler_params=_cparams(1),
        name="attn_ffn",
    )(sinks, q, k, k, vt, vt, bias, wa, x, g.reshape(1, d), wgu, wd)


def _attn_sample_body(sink_ref, q_ref, kc_ref, kn_ref, vc_ref, vn_ref, bias_ref,
                      o_ref, ko_ref, vo_ref, *, bt, tq):
    dh = q_ref.shape[-1]
    fresh = lax.broadcasted_iota(jnp.int32, (dh, WINDOW), 1) >= WINDOW - tq
    sinks = [jnp.concatenate([jnp.full((tq, 1), sink_ref[g * A_GROUP + j], F32) for j in range(A_GROUP)],
                             axis=0) for g in range(A_HEADS_KV)]

    def slide(win_ref, new_ref, b, g):
        new = pltpu.roll(new_ref[g], (WINDOW - tq - b * tq) % WINDOW, axis=1)
        return jnp.where(fresh, new, pltpu.roll(win_ref[b, g], WINDOW - tq, axis=1))

    units = [(b, g) for b in range(bt) for g in range(A_HEADS_KV)]
    us = range(len(units))
    q = [q_ref[b, g] for b, g in units]
    k_win = [slide(kc_ref, kn_ref, b, g) for b, g in units]
    v_win = [slide(vc_ref, vn_ref, b, g) for b, g in units]
    for u, (b, g) in enumerate(units):
        ko_ref[b, g] = k_win[u]
        vo_ref[b, g] = v_win[u]
    s_old = [_dot(q[u], kc_ref[b, g].astype(BF16)) + bias_ref[0, g] for u, (b, g) in enumerate(units)]
    s_new = [_dot(q[u], k_win[u].astype(BF16)) + bias_ref[1, g] for u, (b, g) in enumerate(units)]
    m = [jnp.maximum(jnp.maximum(jnp.max(s_old[u], axis=-1, keepdims=True),
                                 jnp.max(s_new[u], axis=-1, keepdims=True)), sinks[g])
         for u, (b, g) in enumerate(units)]
    p_old = [jnp.exp(s_old[u] - m[u]) for u in us]
    p_new = [jnp.exp(s_new[u] - m[u]) for u in us]
    den = [jnp.sum(p_old[u], axis=-1, keepdims=True) + jnp.sum(p_new[u], axis=-1, keepdims=True)
           + jnp.exp(sinks[g] - m[u]) for u, (b, g) in enumerate(units)]
    out = [_dot_nt((p_old[u] / den[u]).astype(BF16), vc_ref[b, g].astype(BF16))
           + _dot_nt((p_new[u] / den[u]).astype(BF16), v_win[u].astype(BF16))
           for u, (b, g) in enumerate(units)]
    for u, (b, g) in enumerate(units):
        o_ref[b, g] = out[u].astype(BF16)


def _attn_sample(q, kn, vn, k_win, v_win, bias, sinks, tq):
    b, n_kv, rows, dh = q.shape
    assert LANES % tq == 0
    bt = LANES // tq
    assert b % bt == 0
    lead = lambda *tail: pl.BlockSpec((bt,) + tail, lambda i: (i, 0, 0, 0))
    new = pl.BlockSpec((n_kv, dh, LANES), lambda i: (0, 0, i))
    return pl.pallas_call(
        functools.partial(_attn_sample_body, bt=bt, tq=tq),
        grid=(b // bt,),
        in_specs=[pl.BlockSpec(memory_space=pltpu.SMEM),
                  lead(n_kv, rows, dh), lead(n_kv, dh, WINDOW), new, lead(n_kv, dh, WINDOW), new,
                  pl.BlockSpec(bias.shape, lambda i: (0, 0, 0, 0))],
        out_specs=[lead(n_kv, rows, dh), lead(n_kv, dh, WINDOW), lead(n_kv, dh, WINDOW)],
        out_shape=[jax.ShapeDtypeStruct(q.shape, BF16),
                   jax.ShapeDtypeStruct(k_win.shape, F32),
                   jax.ShapeDtypeStruct(v_win.shape, F32)],
        compiler_params=_cparams(1),
        name="attn_sample",
    )(sinks, q, k_win, kn, v_win, vn, bias)


def _swa_layer(x3, buffers, p):
    b, s, d = x3.shape
    dh = p["attn_dh"]
    kd = A_HEADS_KV * dh
    x = x3.reshape(b * s, d)
    kv_shape = (b, WINDOW, A_HEADS_KV, dh)
    if buffers is None:
        nq, nk = A_HEADS_Q * dh, A_HEADS_KV * LANES
        q, k, vt = _attn_proj(x3, p["mix_norm1"], p["attn_w_prompt"], p["attn_wvt"],
                              p["attn_qg"], p["attn_kg"], dh, nq, nk)
        bias = _bias_table(p["rel_bias"], WINDOW, key_major=True)
        new_k = k.reshape(b, s, nk)[:, s - WINDOW:].reshape(b, WINDOW, A_HEADS_KV, LANES)[..., :dh]
        new_v = jnp.swapaxes(vt[:, :, s - WINDOW:], 1, 2)
        mixed = dict(kind="attn", args=(q, k, vt, bias, p["attn_sinks"], p["attn_w_out"]), dh=dh)
    else:
        k_buf, v_buf = buffers
        nq, nk = A_HEADS_Q * dh, kd
        q, k, vt = _attn_proj(x.reshape(1, b * s, d), p["mix_norm1"], p["attn_w_sample"], p["attn_wvt"],
                              p["attn_qg"], p["attn_kg"], dh, nq, nk)
        rows = A_GROUP * s
        q = jnp.transpose(q.reshape(b, s, A_HEADS_KV, A_GROUP, dh), (0, 2, 3, 1, 4))
        table = _bias_table(p["rel_bias"], s, key_major=False)[1].reshape(A_HEADS_KV, rows, 2 * WINDOW)
        own = jnp.pad(table[..., WINDOW:WINDOW + s], ((0, 0), (0, 0), (WINDOW - s, 0)),
                      constant_values=-jnp.inf)
        bias = jnp.stack([table[..., :WINDOW], own])
        to_lanes = lambda a: jnp.transpose(a, (0, 2, 3, 1))
        att, new_k, new_v = _attn_sample(q.reshape(b, A_HEADS_KV, rows, dh),
                                         k.T.reshape(A_HEADS_KV, dh, b * s), vt[0].reshape(A_HEADS_KV, dh, b * s),
                                         to_lanes(k_buf), to_lanes(v_buf), bias, p["attn_sinks"], s)
        new_k, new_v = (jnp.transpose(a, (0, 3, 1, 2)) for a in (new_k, new_v))
        att = jnp.transpose(att.reshape(b, A_HEADS_KV, A_GROUP, s, dh), (0, 3, 1, 2, 4))
        mixed = (att.reshape(b * s, nq), p["attn_w_out"])
    return mixed, (new_k.reshape(kv_shape), new_v.reshape(kv_shape))


def _attn_weights(w_in, w_out, q_norm, k_norm, dh):
    d = w_in.shape[0]
    qd, kd = A_HEADS_Q * dh, A_HEADS_KV * dh
    wk = w_in[:, qd:qd + kd].reshape(d, A_HEADS_KV, dh)
    k_dup = jnp.concatenate([wk, wk], axis=-1).reshape(d, A_HEADS_KV * LANES)
    w_prompt = jnp.concatenate([w_in[:, :qd], k_dup], axis=1).astype(BF16)
    qg = jnp.concatenate([q_norm, q_norm]).reshape(1, LANES) * (dh ** -0.5)
    kg = jnp.concatenate([k_norm, k_norm]).reshape(1, LANES)
    return {"attn_w_prompt": w_prompt, "attn_w_sample": w_in[:, :qd + kd].astype(BF16),
            "attn_wvt": w_in[:, qd + kd:].T.astype(BF16),
            "attn_w_out": w_out.astype(BF16),
            "attn_qg": qg, "attn_kg": kg}


def _trunk(x3, mlstm_state, swa_buffers, p):
    b, s, d = x3.shape

    def ffn(x, name, layer, mixed=None):
        w = (p[name + "_norm"][layer], p[name + "_wgu"][layer], p[name + "_wd"][layer])
        if not isinstance(mixed, dict):
            return _ffn(x, *w, pre=mixed)
        if mixed["kind"] == "attn":
            return _attn_ffn(x, *mixed["args"], *w, mixed["dh"])
        return _mlstm_ffn(x, *mixed["args"], *w, mixed["chunk"])

    x = ffn(x3.reshape(b * s, d), "ffn1", 0)
    mixed, new_mlstm = _mlstm_layer(x.reshape(b, s, d), mlstm_state, p)
    x = ffn(x, "ffn2", 0, mixed)
    if new_mlstm is None:
        x, c1, n1, m1 = x
        new_mlstm = (c1, n1, m1.reshape(b, M_HEADS))
    x = ffn(x, "ffn1", 1)
    mixed, new_swa = _swa_layer(x.reshape(b, s, d), swa_buffers, p)
    x = ffn(x, "ffn2", 1, mixed)
    return x.reshape(b, s, d), new_mlstm, new_swa


def kernel(x_prompt, x_sample, state_mlstm_C, state_mlstm_n, state_mlstm_m, cache_swa_k, cache_swa_v,
           ffn1_norm, ffn1_w_gate_up, ffn1_w_down, mix_norm, ffn2_norm, ffn2_w_gate_up, ffn2_w_down,
           mlstm_w_in, mlstm_b_gates, mlstm_out_norm, mlstm_w_out,
           attn_w_in, attn_q_norm, attn_k_norm, attn_sinks, rel_bias, attn_w_out):
    d = x_prompt.shape[-1]
    vd = mlstm_w_out.shape[0]
    n_gates = 2 * M_HEADS
    qk = (mlstm_w_in.shape[1] - 2 * vd - n_gates) // 2
    dh = attn_q_norm.shape[0]
    assert 2 * dh == LANES
    w_gates = jnp.zeros((d, LANES), BF16).at[:, :n_gates].set(mlstm_w_in[:, 2 * qk + 2 * vd:].astype(BF16))
    b_gates = jnp.zeros((1, LANES), F32).at[0, :n_gates].set(mlstm_b_gates)
    layers = lambda w: [_layer_bf16(w, layer) for layer in range(w.shape[0])]
    p = {
        "ffn1_norm": ffn1_norm, "ffn2_norm": ffn2_norm,
        "ffn1_wgu": layers(ffn1_w_gate_up), "ffn1_wd": layers(ffn1_w_down),
        "ffn2_wgu": layers(ffn2_w_gate_up), "ffn2_wd": layers(ffn2_w_down),
        "mix_norm0": mix_norm[0], "mix_norm1": mix_norm[1],
        "mlstm_qk": qk, "mlstm_vd": vd,
        "mlstm_w_main": mlstm_w_in[:, :2 * qk + 2 * vd].astype(BF16),
        "mlstm_w_gates": w_gates, "mlstm_b_gates": b_gates,
        "mlstm_out_norm": mlstm_out_norm, "mlstm_w_out": mlstm_w_out.astype(BF16),
        "attn_dh": dh, "attn_sinks": attn_sinks, "rel_bias": rel_bias,
    }
    p.update(_attn_weights(attn_w_in, attn_w_out, attn_q_norm, attn_k_norm, dh))
    y_p, (c_p, n_p, m_p), (k_p, v_p) = _trunk(x_prompt, None, None, p)
    y_s, (c_s, n_s, m_s), (k_s, v_s) = _trunk(
        x_sample, (state_mlstm_C, state_mlstm_n, state_mlstm_m), (cache_swa_k, cache_swa_v), p)
    return (y_p, y_s, c_p, n_p, m_p, k_p, v_p, c_s, n_s, m_s, k_s, v_s)
```

```python
import functools
import math

import jax
import jax.numpy as jnp
from jax import lax
from jax.experimental import pallas as pl
from jax.experimental.pallas import tpu as pltpu

F32 = jnp.float32
BF16 = jnp.bfloat16

M_HEADS = 4
A_HEADS_Q = 16
A_HEADS_KV = 4
A_GROUP = A_HEADS_Q // A_HEADS_KV
WINDOW = 128
NUM_BUCKETS = 32
MAX_DISTANCE = 128
FFN_RESIDUAL = 0.5
RMS_EPS = 1e-6
LANES = 128
VMEM_LIMIT = 56 * 1024 * 1024
CAST_BLOCK_BYTES = 6 * 1024 * 1024


def _cparams(n_axes):
    return pltpu.CompilerParams(
        dimension_semantics=("arbitrary",) * n_axes, vmem_limit_bytes=VMEM_LIMIT)


def _resident(shape):
    nd = len(shape)
    return pl.BlockSpec(shape, lambda *_: (0,) * nd, pipeline_mode=pl.Buffered(1))


def _rms(x, g):
    return x * lax.rsqrt(jnp.mean(x * x, axis=-1, keepdims=True) + RMS_EPS) * g


def _dot(a, b):
    return jnp.dot(a, b, preferred_element_type=F32)


def _dot_nt(a, b):
    return lax.dot_general(a, b, (((1,), (1,)), ((), ())), preferred_element_type=F32)


def _dot_tn(a, b):
    return lax.dot_general(a, b, (((0,), (0,)), ((), ())), preferred_element_type=F32)


def _row_tile(t, pref):
    tm = min(t, pref)
    assert t % tm == 0
    return tm


def _cast_body(w_ref, o_ref):
    o_ref[...] = w_ref[...].astype(o_ref.dtype)


def _layer_bf16(w, layer):
    _, r, c = w.shape
    tr = max(t for t in range(8, r + 1, 8) if r % t == 0 and t * c * 4 <= CAST_BLOCK_BYTES)
    return pl.pallas_call(
        _cast_body,
        grid=(r // tr,),
        in_specs=[pl.BlockSpec((None, tr, c), lambda i: (layer, i, 0))],
        out_specs=pl.BlockSpec((tr, c), lambda i: (i, 0)),
        out_shape=jax.ShapeDtypeStruct((r, c), BF16),
        compiler_params=_cparams(1),
        name="layer_bf16",
    )(w)


def _interleave(main, side):
    done = 0
    for j, thunk in enumerate(main):
        thunk()
        upto = (j + 1) * len(side) // len(main)
        for other in side[done:upto]:
            other()
        done = upto


def _ffn_items(get_x, g_ref, wgu_ref, wd_ref, o_ref, act_ref, d_ff, tf):
    st = {}

    def head():
        y = get_x()
        o_ref[...] = y
        st["xn"] = _rms(y, g_ref[...]).astype(BF16)

    def chunk(lo):
        gate = _dot(st["xn"], wgu_ref[:, lo:lo + tf])
        up = _dot(st["xn"], wgu_ref[:, d_ff + lo:d_ff + lo + tf])
        act_ref[:, lo:lo + tf] = (gate * jax.nn.sigmoid(gate) * up).astype(BF16)

    def tail():
        o_ref[...] = o_ref[...] + FFN_RESIDUAL * _dot(act_ref[...], wd_ref[...])

    return [head] + [functools.partial(chunk, lo) for lo in range(0, d_ff, tf)] + [tail]


def _ffn_body(*refs, d_ff, tf, has_pre):
    if has_pre:
        a_ref, wa_ref, x_ref, g_ref, wgu_ref, wd_ref, o_ref, act_ref = refs
        get_x = lambda: x_ref[...] + _dot(a_ref[...], wa_ref[...])
    else:
        x_ref, g_ref, wgu_ref, wd_ref, o_ref, act_ref = refs
        get_x = lambda: x_ref[...]
    for thunk in _ffn_items(get_x, g_ref, wgu_ref, wd_ref, o_ref, act_ref, d_ff, tf):
        thunk()


FFN_TF = 256


def _ffn(x, g, wgu, wd, pre=None):
    t, d = x.shape
    d_ff = wd.shape[0]
    tm = _row_tile(t, 1024)
    assert d_ff % FFN_TF == 0
    row = lambda n: pl.BlockSpec((tm, n), lambda i: (i, 0))
    pre_specs, pre_args = [], []
    if pre is not None:
        a, wa = pre
        pre_specs, pre_args = [row(a.shape[1]), _resident(wa.shape)], [a, wa]
    return pl.pallas_call(
        functools.partial(_ffn_body, d_ff=d_ff, tf=FFN_TF, has_pre=pre is not None),
        grid=(t // tm,),
        in_specs=pre_specs + [row(d), _resident((1, d)), _resident((d, 2 * d_ff)), _resident((d_ff, d))],
        out_specs=row(d),
        out_shape=jax.ShapeDtypeStruct((t, d), F32),
        scratch_shapes=[pltpu.VMEM((tm, d_ff), BF16)],
        compiler_params=_cparams(1),
        name="ffn",
    )(*pre_args, x, g.reshape(1, d), wgu, wd)


def _mlstm_proj_body(x_ref, g_ref, w_ref, wg_ref, bg_ref,
                     q_ref, k_ref, v_ref, og_ref, gcol_ref, grow_ref, *, qk, vd, dk, chunk):
    h_n = M_HEADS
    xn = _rms(x_ref[...], g_ref[...]).astype(BF16)
    tm = xn.shape[0]

    shift = chunk.bit_length() - 1
    r_i = lax.broadcasted_iota(jnp.int32, (LANES, LANES), 0)
    c_i = lax.broadcasted_iota(jnp.int32, (LANES, LANES), 1)
    same_chunk = lax.shift_right_logical(r_i, shift) == lax.shift_right_logical(c_i, shift)
    tri = jnp.where((r_i >= c_i) & same_chunk, 1.0, 0.0).astype(BF16)

    gates = _dot(xn, wg_ref[...]) + bg_ref[...]
    q_ref[...] = _dot(xn, w_ref[:, 0:qk]).astype(BF16)
    lane = lax.broadcasted_iota(jnp.int32, (tm, LANES), 1)
    log_f = jnp.where((lane >= h_n) & (lane < 2 * h_n), jax.nn.log_sigmoid(gates), 0.0)
    hi = log_f.astype(BF16)
    lo = (log_f - hi.astype(F32)).astype(BF16)
    k_ref[...] = (_dot(xn, w_ref[:, qk:2 * qk]) * (dk ** -0.5)).astype(BF16)
    hi_lo = jnp.concatenate([hi, lo], axis=1)
    csum = jnp.concatenate([_dot(tri, hi_lo[r:r + LANES]) for r in range(0, tm, LANES)], axis=0)
    gcol = jnp.where(lane < h_n, gates, csum[:, :LANES] + csum[:, LANES:])
    gcol_ref[...] = gcol
    v_ref[...] = _dot(xn, w_ref[:, 2 * qk:2 * qk + vd]).astype(BF16)
    g_t = gcol.T
    grow_ref[0] = jnp.concatenate([g_t[0:h_n] - g_t[h_n:2 * h_n], g_t[h_n:2 * h_n]], axis=0)
    og_ref[...] = jax.nn.sigmoid(_dot(xn, w_ref[:, 2 * qk + vd:2 * qk + 2 * vd])).astype(BF16)


def _mlstm_proj(x3, g, w_main, w_gates, b_gates, qk, vd, chunk):
    b, s, d = x3.shape
    t = b * s
    tm = _row_tile(s, 1024)
    nt = s // tm
    assert chunk & (chunk - 1) == 0 and LANES % chunk == 0 and tm % LANES == 0
    row = lambda n: pl.BlockSpec((tm, n), lambda i: (i, 0))
    return pl.pallas_call(
        functools.partial(_mlstm_proj_body, qk=qk, vd=vd, dk=qk // M_HEADS, chunk=chunk),
        grid=(t // tm,),
        in_specs=[row(d), _resident((1, d)), _resident(w_main.shape),
                  _resident(w_gates.shape), _resident((1, LANES))],
        out_specs=[row(qk), row(qk), row(vd), row(vd), row(LANES),
                   pl.BlockSpec((1, 2 * M_HEADS, tm), lambda i: (i // nt, 0, i % nt))],
        out_shape=[jax.ShapeDtypeStruct((t, qk), BF16), jax.ShapeDtypeStruct((t, qk), BF16),
                   jax.ShapeDtypeStruct((t, vd), BF16), jax.ShapeDtypeStruct((t, vd), BF16),
                   jax.ShapeDtypeStruct((t, LANES), F32),
                   jax.ShapeDtypeStruct((b, 2 * M_HEADS, s), F32)],
        compiler_params=_cparams(1),
        name="mlstm_proj",
    )(x3.reshape(t, d), g.reshape(1, d), w_main, w_gates, b_gates)


def _mlstm_chunk_items(units, load, state, emit, rows):
    r_i = lax.broadcasted_iota(jnp.int32, (rows, rows), 0)
    c_i = lax.broadcasted_iota(jnp.int32, (rows, rows), 1)
    causal = r_i >= c_i
    us = range(len(units))
    st = {}

    def products():
        st["x"] = x = [load(u) for u in units]
        st["old"] = [state[u] for u in units]
        st["qk"] = [_dot_nt(x[i]["q"], x[i]["k"]) for i in us]
        st["qc"] = [_dot(x[i]["q"], st["old"][i][0].astype(BF16)) for i in us]

    def weights():
        x, old = st["x"], st["old"]
        log_d = [jnp.where(causal, x[i]["b_c"] + x[i]["imb"], -jnp.inf) for i in us]
        log_inter = [x[i]["b_c"] + old[i][2] for i in us]
        st["m_t"] = m_t = [jnp.maximum(log_inter[i], jnp.max(log_d[i], axis=-1, keepdims=True)) for i in us]
        st["inter"] = [jnp.exp(log_inter[i] - m_t[i]) for i in us]
        st["s"] = [st["qk"][i] * jnp.exp(log_d[i] - m_t[i]) for i in us]

    def hidden():
        x, old, s, inter, m_t = st["x"], st["old"], st["s"], st["inter"], st["m_t"]
        s_v = [_dot(s[i].astype(BF16), x[i]["v"]) for i in us]
        den = [inter[i] * jnp.sum(x[i]["q"].astype(F32) * old[i][1], axis=-1, keepdims=True)
               + jnp.sum(s[i], axis=-1, keepdims=True) for i in us]
        st["hid"] = [(inter[i] * st["qc"][i] + s_v[i]) / jnp.maximum(jnp.abs(den[i]), jnp.exp(-m_t[i]))
                     for i in us]

    def update():
        x, old = st["x"], st["old"]
        b_last = [x[i]["b_c"][rows - 1:rows, :] for i in us]
        log_w = [b_last[i] - x[i]["b_c"] + x[i]["i_c"] for i in us]
        m_new = [jnp.maximum(b_last[i] + old[i][2], jnp.max(log_w[i], axis=0, keepdims=True)) for i in us]
        decay = [jnp.exp(b_last[i] + old[i][2] - m_new[i]) for i in us]
        kw = [jnp.exp(log_w[i] - m_new[i]) * x[i]["k"].astype(F32) for i in us]
        kw_v = [_dot_tn(kw[i].astype(BF16), x[i]["v"]) for i in us]
        for i, u in enumerate(units):
            state[u] = (decay[i] * old[i][0] + kw_v[i],
                        decay[i] * old[i][1] + jnp.sum(kw[i], axis=0, keepdims=True), m_new[i])

    def output():
        hid = [st["hid"][i] * lax.rsqrt(jnp.mean(st["hid"][i] * st["hid"][i], axis=-1, keepdims=True) + RMS_EPS)
               for i in us]
        for i, u in enumerate(units):
            emit(u, (hid[i] * st["x"][i]["ng"] * st["x"][i]["og"].astype(F32)).astype(BF16))

    return [products, weights, hidden, update, output]


def _mlstm_cell_body(q_ref, k_ref, v_ref, og_ref, gcol_ref, grow_ref, ng_ref, c0_ref, n0_ref, m0_ref,
                     hg_ref, c_ref, n_ref, m_ref, *, dk, dv, bt):
    h_n = M_HEADS
    rows = q_ref.shape[1]
    units = [(b, h) for b in range(bt) for h in range(h_n)]

    def load(u):
        b, h = u
        kc, vc = slice(h * dk, (h + 1) * dk), slice(h * dv, (h + 1) * dv)
        return dict(q=q_ref[b, :, kc], k=k_ref[b, :, kc], v=v_ref[b, :, vc], og=og_ref[b, :, vc],
                    ng=ng_ref[:, vc], b_c=gcol_ref[b, :, h_n + h:h_n + h + 1], i_c=gcol_ref[b, :, h:h + 1],
                    imb=grow_ref[b, h:h + 1, :])

    def emit(u, value):
        b, h = u
        hg_ref[b, :, h * dv:(h + 1) * dv] = value

    state = {(b, h): (c0_ref[b, h], n0_ref[b, h:h + 1, :], m0_ref[b, :, h:h + 1]) for b, h in units}
    for thunk in _mlstm_chunk_items(units, load, state, emit, rows):
        thunk()
    for b, h in units:
        c_ref[b, h], n_ref[b, h:h + 1, :], m_ref[b, :, h:h + 1] = state[(b, h)]


def _mlstm_cell(q, k, v, og, gcol, grow, norm_g, c0, n0, m0, bt):
    b, s, qk = q.shape
    vd = v.shape[-1]
    dk, dv = qk // M_HEADS, vd // M_HEADS
    assert b % bt == 0
    tok = lambda n: pl.BlockSpec((bt, s, n), lambda i: (i, 0, 0))
    st_c = pl.BlockSpec((bt, M_HEADS, dk, dv), lambda i: (i, 0, 0, 0))
    st_n = pl.BlockSpec((bt, M_HEADS, dk), lambda i: (i, 0, 0))
    st_m = pl.BlockSpec((bt, 1, M_HEADS), lambda i: (i, 0, 0))
    return pl.pallas_call(
        functools.partial(_mlstm_cell_body, dk=dk, dv=dv, bt=bt),
        grid=(b // bt,),
        in_specs=[tok(qk), tok(qk), tok(vd), tok(vd), tok(LANES),
                  pl.BlockSpec((bt, 2 * M_HEADS, s), lambda i: (i, 0, 0)),
                  pl.BlockSpec((1, vd), lambda i: (0, 0)), st_c, st_n, st_m],
        out_specs=[tok(vd), st_c, st_n, st_m],
        out_shape=[jax.ShapeDtypeStruct((b, s, vd), BF16),
                   jax.ShapeDtypeStruct((b, M_HEADS, dk, dv), F32),
                   jax.ShapeDtypeStruct((b, M_HEADS, dk), F32),
                   jax.ShapeDtypeStruct((b, 1, M_HEADS), F32)],
        compiler_params=_cparams(1),
        name="mlstm_cell",
    )(q, k, v, og, gcol, grow, norm_g.reshape(1, vd), c0, n0, m0)


def _mlstm_ffn_body(q_ref, k_ref, v_ref, og_ref, gcol_ref, grow_ref, ng_ref, wa_ref,
                    x_ref, g_ref, wgu_ref, wd_ref, o_ref, c_ref, n_ref, m_ref, act_ref, hg_ref,
                    *, d_ff, tf, dk, dv, chunk, tiles_per_seq):
    h_n = M_HEADS
    i = pl.program_id(0)
    last = pl.num_programs(0) - 1
    live = i < last
    first_of_seq = lax.rem(jnp.minimum(i, last - 1), tiles_per_seq) == 0

    @pl.when(i == 0)
    def _():
        hg_ref[...] = jnp.zeros_like(hg_ref)

    @pl.when(first_of_seq & live)
    def _():
        c_ref[...] = jnp.zeros_like(c_ref)
        n_ref[...] = jnp.zeros_like(n_ref)
        m_ref[...] = jnp.zeros_like(m_ref)

    cur = lax.rem(i, 2)
    heads = list(range(h_n))
    state = {h: (c_ref[0, h], n_ref[0, h:h + 1, :], m_ref[0, :, h:h + 1]) for h in heads}
    side = []
    for j in range(q_ref.shape[0] // chunk):
        rows = slice(j * chunk, (j + 1) * chunk)

        def load(h, rows=rows):
            kc, vc = slice(h * dk, (h + 1) * dk), slice(h * dv, (h + 1) * dv)
            return dict(q=q_ref[rows, kc], k=k_ref[rows, kc], v=v_ref[rows, vc], og=og_ref[rows, vc],
                        ng=ng_ref[:, vc], b_c=gcol_ref[rows, h_n + h:h_n + h + 1], i_c=gcol_ref[rows, h:h + 1],
                        imb=grow_ref[0, h:h + 1, rows])

        def emit(h, value, rows=rows):
            hg_ref[cur, rows, h * dv:(h + 1) * dv] = value

        side += _mlstm_chunk_items(heads, load, state, emit, chunk)
    get_x = lambda: x_ref[...] + _dot(hg_ref[1 - cur], wa_ref[...])
    _interleave(_ffn_items(get_x, g_ref, wgu_ref, wd_ref, o_ref, act_ref, d_ff, tf), side)

    @pl.when(live)
    def _():
        for h in heads:
            c_ref[0, h], n_ref[0, h:h + 1, :], m_ref[0, :, h:h + 1] = state[h]


def _mlstm_ffn(x, q, k, v, og, gcol, grow, norm_g, wa, g, wgu, wd, chunk):
    t, d = x.shape
    b, _, s = grow.shape
    d_ff = wd.shape[0]
    qk, vd = q.shape[1], v.shape[1]
    dk, dv = qk // M_HEADS, vd // M_HEADS
    tm = _row_tile(s, 512)
    nt, tps = t // tm, s // tm
    assert tm % chunk == 0
    mix_tile = lambda i: jnp.minimum(i, nt - 1)
    ffn_row = lambda n: pl.BlockSpec((tm, n), lambda i: (jnp.maximum(i - 1, 0), 0))
    mix_row = lambda n: pl.BlockSpec((tm, n), lambda i: (mix_tile(i), 0))
    seq = lambda *tail: pl.BlockSpec((1,) + tail, lambda i: (mix_tile(i) // tps,) + (0,) * len(tail))
    return pl.pallas_call(
        functools.partial(_mlstm_ffn_body, d_ff=d_ff, tf=FFN_TF, dk=dk, dv=dv, chunk=chunk, tiles_per_seq=tps),
        grid=(nt + 1,),
        in_specs=[mix_row(qk), mix_row(qk), mix_row(vd), mix_row(vd), mix_row(LANES),
                  pl.BlockSpec((1, 2 * M_HEADS, tm), lambda i: (mix_tile(i) // tps, 0, mix_tile(i) % tps)),
                  _resident((1, vd)), _resident(wa.shape),
                  ffn_row(d), _resident((1, d)), _resident((d, 2 * d_ff)), _resident((d_ff, d))],
        out_specs=[ffn_row(d), seq(M_HEADS, dk, dv), seq(M_HEADS, dk), seq(1, M_HEADS)],
        out_shape=[jax.ShapeDtypeStruct((t, d), F32),
                   jax.ShapeDtypeStruct((b, M_HEADS, dk, dv), F32),
                   jax.ShapeDtypeStruct((b, M_HEADS, dk), F32),
                   jax.ShapeDtypeStruct((b, 1, M_HEADS), F32)],
        scratch_shapes=[pltpu.VMEM((tm, d_ff), BF16), pltpu.VMEM((2, tm, vd), BF16)],
        compiler_params=_cparams(1),
        name="mlstm_ffn",
    )(q, k, v, og, gcol, grow, norm_g.reshape(1, vd), wa, x, g.reshape(1, d), wgu, wd)


def _mlstm_layer(x3, state, p):
    b, s, d = x3.shape
    qk, vd = p["mlstm_qk"], p["mlstm_vd"]
    proj = functools.partial(_mlstm_proj, g=p["mix_norm0"], w_main=p["mlstm_w_main"],
                             w_gates=p["mlstm_w_gates"], b_gates=p["mlstm_b_gates"], qk=qk, vd=vd)
    if state is None:
        chunk = min(s, 128)
        q, k, v, og, gcol, grow = proj(x3, chunk=chunk)
        return dict(kind="mlstm", args=(q, k, v, og, gcol, grow, p["mlstm_out_norm"], p["mlstm_w_out"]),
                    chunk=chunk), None
    c0, n0, m0 = state
    q, k, v, og, gcol, grow = proj(x3.reshape(1, b * s, d), chunk=s)
    grow = jnp.swapaxes(grow.reshape(2 * M_HEADS, b, s), 0, 1)
    sh = lambda a: a.reshape(b, s, a.shape[-1])
    hg, c1, n1, m1 = _mlstm_cell(sh(q), sh(k), sh(v), sh(og), sh(gcol), grow, p["mlstm_out_norm"],
                                 c0, n0, m0.reshape(b, 1, M_HEADS), math.gcd(b, 8))
    return (hg.reshape(b * s, vd), p["mlstm_w_out"]), (c1, n1, m1.reshape(b, M_HEADS))


def _attn_proj_body(x_ref, g_ref, w_ref, wvt_ref, qg_ref, kg_ref, q_ref, k_ref, vt_ref, *, dh, nq, nk):
    xn = _rms(x_ref[...], g_ref[...]).astype(BF16)
    pair = 2 * LANES
    r_i = lax.broadcasted_iota(jnp.int32, (pair, pair), 0)
    c_i = lax.broadcasted_iota(jnp.int32, (pair, pair), 1)
    shift = dh.bit_length() - 1
    same_head = jnp.where(lax.shift_right_logical(r_i, shift) == lax.shift_right_logical(c_i, shift),
                          1.0, 0.0).astype(BF16)

    def head_norm(y, gain):
        sq = y * y
        hi = sq.astype(BF16)
        lo = (sq - hi.astype(F32)).astype(BF16)
        ms = (_dot(hi, same_head) + _dot(lo, same_head)) * (1.0 / dh)
        return y * lax.rsqrt(ms + RMS_EPS) * gain

    qg = jnp.concatenate([qg_ref[...], qg_ref[...]], axis=1)
    kg = jnp.concatenate([kg_ref[...], kg_ref[...]], axis=1)
    q = _dot(xn, w_ref[:, 0:nq])
    for s in range(nq // pair):
        sl = slice(s * pair, (s + 1) * pair)
        q_ref[:, sl] = head_norm(q[:, sl], qg).astype(BF16)
    k = _dot(xn, w_ref[:, nq:nq + nk])
    for s in range(nk // pair):
        sl = slice(s * pair, (s + 1) * pair)
        k_ref[:, sl] = head_norm(k[:, sl], kg)
    vt_ref[0] = _dot_nt(wvt_ref[...], xn)


def _attn_proj(x3, g, w, wvt, qg, kg, dh, nq, nk):
    b, s, d = x3.shape
    t = b * s
    nv = wvt.shape[0]
    tm = _row_tile(s, 1024)
    nt = s // tm
    row = lambda n: pl.BlockSpec((tm, n), lambda i: (i, 0))
    return pl.pallas_call(
        functools.partial(_attn_proj_body, dh=dh, nq=nq, nk=nk),
        grid=(t // tm,),
        in_specs=[row(d), _resident((1, d)), _resident(w.shape), _resident(wvt.shape),
                  _resident((1, LANES)), _resident((1, LANES))],
        out_specs=[row(nq), row(nk), pl.BlockSpec((1, nv, tm), lambda i: (i // nt, 0, i % nt))],
        out_shape=[jax.ShapeDtypeStruct((t, nq), BF16), jax.ShapeDtypeStruct((t, nk), F32),
                   jax.ShapeDtypeStruct((b, nv, s), F32)],
        compiler_params=_cparams(1),
        name="attn_proj",
    )(x3.reshape(t, d), g.reshape(1, d), w, wvt, qg, kg)


def _bias_body(rel_ref, o_ref, *, tq, key_major):
    nk = 2 * WINDOW
    shape, q_ax = ((nk, tq), 1) if key_major else ((tq, nk), 0)
    qi = lax.broadcasted_iota(jnp.int32, shape, q_ax)
    ki = lax.broadcasted_iota(jnp.int32, shape, 1 - q_ax)
    dist = qi + WINDOW - ki
    max_exact = NUM_BUCKETS // 2
    d = jnp.maximum(dist, 0)
    log_ratio = (jnp.log(jnp.maximum(d, 1).astype(F32) / max_exact)
                 / math.log(MAX_DISTANCE / max_exact))
    large = jnp.minimum(max_exact + (log_ratio * (NUM_BUCKETS - max_exact)).astype(jnp.int32),
                        NUM_BUCKETS - 1)
    bucket = jnp.where(d < max_exact, d, large)
    in_window = (dist >= 0) & (dist < WINDOW)
    visible = (in_window & (ki >= WINDOW), in_window)
    for h in range(A_HEADS_Q):
        bias = jnp.zeros(shape, F32)
        for bkt in range(NUM_BUCKETS):
            bias = jnp.where(bucket == bkt, rel_ref[bkt, h], bias)
        for variant in range(2):
            tile = jnp.where(visible[variant], bias, -jnp.inf)
            if key_major:
                o_ref[variant, h // 2, :, (h % 2) * tq:(h % 2 + 1) * tq] = tile
            else:
                o_ref[variant, h] = tile


def _bias_table(rel_bias, tq, key_major):
    nk = 2 * WINDOW
    shape = (2, A_HEADS_Q // 2, nk, 2 * tq) if key_major else (2, A_HEADS_Q, tq, nk)
    return pl.pallas_call(
        functools.partial(_bias_body, tq=tq, key_major=key_major),
        grid=(1,),
        in_specs=[pl.BlockSpec(memory_space=pltpu.SMEM)],
        out_specs=pl.BlockSpec(shape, lambda i: (0, 0, 0, 0)),
        out_shape=jax.ShapeDtypeStruct(shape, F32),
        compiler_params=_cparams(1),
        name="rel_bias_table",
    )(rel_bias)


def _attn_tile_items(sink_ref, q_ref, kp_ref, ko_ref, vp_ref, vo_ref, bias_ref, first_of_seq, write, dh):
    blk = WINDOW
    n_blk = ko_ref.shape[0] // blk
    slots_per_kv = A_GROUP // 2
    n_slots = A_HEADS_Q // 2
    lane = lax.broadcasted_iota(jnp.int32, (blk, LANES), 1)
    lo_half = lane < dh
    left = lax.broadcasted_iota(jnp.int32, (1, 2 * blk), 1) < blk
    ones = jnp.ones((dh, 2 * blk), BF16)
    sinks = [jnp.where(left, sink_ref[2 * i], sink_ref[2 * i + 1]) for i in range(n_slots)]
    items = []
    for j in range(n_blk):
        st = {}
        own = slice(j * blk, (j + 1) * blk)
        before = slice((j - 1) * blk, j * blk)
        variant = jnp.where(first_of_seq, 0, 1) if j == 0 else 1

        def scores(st=st, j=j, own=own, before=before, variant=variant):
            st["vts"], st["sts"] = [], []
            for g in range(A_HEADS_KV):
                sl = slice(g * LANES, (g + 1) * LANES)
                rows = slice(g * dh, (g + 1) * dh)
                k_prev = kp_ref[:, sl] if j == 0 else ko_ref[before, sl]
                v_prev = vp_ref[0, rows, :] if j == 0 else vo_ref[0, rows, before]
                kk = jnp.concatenate([k_prev, ko_ref[own, sl]], axis=0).astype(BF16)
                vt = jnp.concatenate([v_prev, vo_ref[0, rows, own]], axis=1).astype(BF16)
                st["vts"].append(jnp.concatenate([vt, ones], axis=0))
                for pr in range(slots_per_kv):
                    slot_i = g * slots_per_kv + pr
                    slot = q_ref[own, slot_i * LANES:(slot_i + 1) * LANES]
                    zero = jnp.zeros_like(slot)
                    q2 = jnp.concatenate([jnp.where(lo_half, slot, zero), jnp.where(lo_half, zero, slot)],
                                         axis=0)
                    st["sts"].append(_dot_nt(kk, q2) + bias_ref[variant, slot_i])

        def softmax(st=st):
            st["ms"] = [jnp.maximum(jnp.max(st["sts"][i], axis=0, keepdims=True), sinks[i])
                        for i in range(n_slots)]
            st["pts"] = [jnp.exp(st["sts"][i] - st["ms"][i]).astype(BF16) for i in range(n_slots)]

        def values(st=st):
            st["oes"] = [_dot(st["vts"][i // slots_per_kv], st["pts"][i]) for i in range(n_slots)]

        def finish(st=st, j=j):
            heads_t = []
            for i in range(n_slots):
                oe = st["oes"][i]
                ot = oe[0:dh, :] / (oe[dh:dh + 1, :] + jnp.exp(sinks[i] - st["ms"][i]))
                heads_t += [ot[:, 0:blk], ot[:, blk:2 * blk]]
            write(j, jnp.concatenate(heads_t, axis=0).T.astype(BF16))

        items += [scores, softmax, values, finish]
    return items


def _attn_ffn_body(sink_ref, q_ref, kp_ref, ko_ref, vp_ref, vo_ref, bias_ref, wa_ref,
                   x_ref, g_ref, wgu_ref, wd_ref, o_ref, act_ref, att_ref, *, d_ff, tf, dh, tiles_per_seq):
    i = pl.program_id(0)

    @pl.when(i == 0)
    def _():
        att_ref[...] = jnp.zeros_like(att_ref)

    cur = lax.rem(i, 2)
    first_of_seq = lax.rem(jnp.minimum(i, pl.num_programs(0) - 2), tiles_per_seq) == 0

    def write(j, tile):
        att_ref[cur, j * WINDOW:(j + 1) * WINDOW, :] = tile

    side = _attn_tile_items(sink_ref, q_ref, kp_ref, ko_ref, vp_ref, vo_ref, bias_ref,
                            first_of_seq, write, dh)
    get_x = lambda: x_ref[...] + _dot(att_ref[1 - cur], wa_ref[...])
    _interleave(_ffn_items(get_x, g_ref, wgu_ref, wd_ref, o_ref, act_ref, d_ff, tf), side)


def _attn_ffn(x, q, k, vt, bias, sinks, wa, g, wgu, wd, dh):
    t, d = x.shape
    b, vd, s = vt.shape
    d_ff = wd.shape[0]
    qd, kd = q.shape[1], k.shape[1]
    blk = WINDOW
    tm = _row_tile(s, 512)
    nt, tps, bpt = t // tm, s // tm, tm // blk
    att_tile = lambda i: jnp.minimum(i, nt - 1)
    ffn_tile = lambda i: jnp.maximum(i - 1, 0)
    ffn_row = lambda n: pl.BlockSpec((tm, n), lambda i: (ffn_tile(i), 0))
    att_row = lambda n: pl.BlockSpec((tm, n), lambda i: (att_tile(i), 0))
    return pl.pallas_call(
        functools.partial(_attn_ffn_body, d_ff=d_ff, tf=FFN_TF, dh=dh, tiles_per_seq=tps),
        grid=(nt + 1,),
        in_specs=[pl.BlockSpec(memory_space=pltpu.SMEM),
                  att_row(qd),
                  pl.BlockSpec((blk, kd), lambda i: (jnp.maximum(att_tile(i) * bpt - 1, 0), 0)),
                  att_row(kd),
                  pl.BlockSpec((1, vd, blk), lambda i: (att_tile(i) // tps,
                                                        0, jnp.maximum(att_tile(i) % tps * bpt - 1, 0))),
                  pl.BlockSpec((1, vd, tm), lambda i: (att_tile(i) // tps, 0, att_tile(i) % tps)),
                  _resident(bias.shape), _resident(wa.shape),
                  ffn_row(d), _resident((1, d)), _resident((d, 2 * d_ff)), _resident((d_ff, d))],
        out_specs=ffn_row(d),
        out_shape=jax.ShapeDtypeStruct((t, d), F32),
        scratch_shapes=[pltpu.VMEM((tm, d_ff), BF16), pltpu.VMEM((2, tm, qd), BF16)],
        compiler_params=_cparams(1),
        name="attn_ffn",
    )(sinks, q, k, k, vt, vt, bias, wa, x, g.reshape(1, d), wgu, wd)


def _attn_sample_body(sink_ref, q_ref, kc_ref, kn_ref, vc_ref, vn_ref, bias_ref,
                      o_ref, ko_ref, vo_ref, *, bt, tq):
    dh = q_ref.shape[-1]
    fresh = lax.broadcasted_iota(jnp.int32, (dh, WINDOW), 1) >= WINDOW - tq
    sinks = [jnp.concatenate([jnp.full((tq, 1), sink_ref[g * A_GROUP + j], F32) for j in range(A_GROUP)],
                             axis=0) for g in range(A_HEADS_KV)]

    def slide(win_ref, new_ref, b, g):
        new = pltpu.roll(new_ref[g], (WINDOW - tq - b * tq) % WINDOW, axis=1)
        return jnp.where(fresh, new, pltpu.roll(win_ref[b, g], WINDOW - tq, axis=1))

    units = [(b, g) for b in range(bt) for g in range(A_HEADS_KV)]
    us = range(len(units))
    q = [q_ref[b, g] for b, g in units]
    k_win = [slide(kc_ref, kn_ref, b, g) for b, g in units]
    v_win = [slide(vc_ref, vn_ref, b, g) for b, g in units]
    for u, (b, g) in enumerate(units):
        ko_ref[b, g] = k_win[u]
        vo_ref[b, g] = v_win[u]
    s_old = [_dot(q[u], kc_ref[b, g].astype(BF16)) + bias_ref[0, g] for u, (b, g) in enumerate(units)]
    s_new = [_dot(q[u], k_win[u].astype(BF16)) + bias_ref[1, g] for u, (b, g) in enumerate(units)]
    m = [jnp.maximum(jnp.maximum(jnp.max(s_old[u], axis=-1, keepdims=True),
                                 jnp.max(s_new[u], axis=-1, keepdims=True)), sinks[g])
         for u, (b, g) in enumerate(units)]
    p_old = [jnp.exp(s_old[u] - m[u]) for u in us]
    p_new = [jnp.exp(s_new[u] - m[u]) for u in us]
    den = [jnp.sum(p_old[u], axis=-1, keepdims=True) + jnp.sum(p_new[u], axis=-1, keepdims=True)
           + jnp.exp(sinks[g] - m[u]) for u, (b, g) in enumerate(units)]
    out = [_dot_nt((p_old[u] / den[u]).astype(BF16), vc_ref[b, g].astype(BF16))
           + _dot_nt((p_new[u] / den[u]).astype(BF16), v_win[u].astype(BF16))
           for u, (b, g) in enumerate(units)]
    for u, (b, g) in enumerate(units):
        o_ref[b, g] = out[u].astype(BF16)


def _attn_sample(q, kn, vn, k_win, v_win, bias, sinks, tq):
    b, n_kv, rows, dh = q.shape
    assert LANES % tq == 0
    bt = LANES // tq
    assert b % bt == 0
    lead = lambda *tail: pl.BlockSpec((bt,) + tail, lambda i: (i, 0, 0, 0))
    new = pl.BlockSpec((n_kv, dh, LANES), lambda i: (0, 0, i))
    return pl.pallas_call(
        functools.partial(_attn_sample_body, bt=bt, tq=tq),
        grid=(b // bt,),
        in_specs=[pl.BlockSpec(memory_space=pltpu.SMEM),
                  lead(n_kv, rows, dh), lead(n_kv, dh, WINDOW), new, lead(n_kv, dh, WINDOW), new,
                  pl.BlockSpec(bias.shape, lambda i: (0, 0, 0, 0))],
        out_specs=[lead(n_kv, rows, dh), lead(n_kv, dh, WINDOW), lead(n_kv, dh, WINDOW)],
        out_shape=[jax.ShapeDtypeStruct(q.shape, BF16),
                   jax.ShapeDtypeStruct(k_win.shape, F32),
                   jax.ShapeDtypeStruct(v_win.shape, F32)],
        compiler_params=_cparams(1),
        name="attn_sample",
    )(sinks, q, k_win, kn, v_win, vn, bias)


def _swa_layer(x3, buffers, p):
    b, s, d = x3.shape
    dh = p["attn_dh"]
    kd = A_HEADS_KV * dh
    x = x3.reshape(b * s, d)
    kv_shape = (b, WINDOW, A_HEADS_KV, dh)
    if buffers is None:
        nq, nk = A_HEADS_Q * dh, A_HEADS_KV * LANES
        q, k, vt = _attn_proj(x3, p["mix_norm1"], p["attn_w_prompt"], p["attn_wvt"],
                              p["attn_qg"], p["attn_kg"], dh, nq, nk)
        bias = _bias_table(p["rel_bias"], WINDOW, key_major=True)
        new_k = k.reshape(b, s, nk)[:, s - WINDOW:].reshape(b, WINDOW, A_HEADS_KV, LANES)[..., :dh]
        new_v = jnp.swapaxes(vt[:, :, s - WINDOW:], 1, 2)
        mixed = dict(kind="attn", args=(q, k, vt, bias, p["attn_sinks"], p["attn_w_out"]), dh=dh)
    else:
        k_buf, v_buf = buffers
        nq, nk = A_HEADS_Q * dh, kd
        q, k, vt = _attn_proj(x.reshape(1, b * s, d), p["mix_norm1"], p["attn_w_sample"], p["attn_wvt"],
                              p["attn_qg"], p["attn_kg"], dh, nq, nk)
        rows = A_GROUP * s
        q = jnp.transpose(q.reshape(b, s, A_HEADS_KV, A_GROUP, dh), (0, 2, 3, 1, 4))
        table = _bias_table(p["rel_bias"], s, key_major=False)[1].reshape(A_HEADS_KV, rows, 2 * WINDOW)
        own = jnp.pad(table[..., WINDOW:WINDOW + s], ((0, 0), (0, 0), (WINDOW - s, 0)),
                      constant_values=-jnp.inf)
        bias = jnp.stack([table[..., :WINDOW], own])
        to_lanes = lambda a: jnp.transpose(a, (0, 2, 3, 1))
        att, new_k, new_v = _attn_sample(q.reshape(b, A_HEADS_KV, rows, dh),
                                         k.T.reshape(A_HEADS_KV, dh, b * s), vt[0].reshape(A_HEADS_KV, dh, b * s),
                                         to_lanes(k_buf), to_lanes(v_buf), bias, p["attn_sinks"], s)
        new_k, new_v = (jnp.transpose(a, (0, 3, 1, 2)) for a in (new_k, new_v))
        att = jnp.transpose(att.reshape(b, A_HEADS_KV, A_GROUP, s, dh), (0, 3, 1, 2, 4))
        mixed = (att.reshape(b * s, nq), p["attn_w_out"])
    return mixed, (new_k.reshape(kv_shape), new_v.reshape(kv_shape))


def _attn_weights(w_in, w_out, q_norm, k_norm, dh):
    d = w_in.shape[0]
    qd, kd = A_HEADS_Q * dh, A_HEADS_KV * dh
    wk = w_in[:, qd:qd + kd].reshape(d, A_HEADS_KV, dh)
    k_dup = jnp.concatenate([wk, wk], axis=-1).reshape(d, A_HEADS_KV * LANES)
    w_prompt = jnp.concatenate([w_in[:, :qd], k_dup], axis=1).astype(BF16)
    qg = jnp.concatenate([q_norm, q_norm]).reshape(1, LANES) * (dh ** -0.5)
    kg = jnp.concatenate([k_norm, k_norm]).reshape(1, LANES)
    return {"attn_w_prompt": w_prompt, "attn_w_sample": w_in[:, :qd + kd].astype(BF16),
            "attn_wvt": w_in[:, qd + kd:].T.astype(BF16),
            "attn_w_out": w_out.astype(BF16),
            "attn_qg": qg, "attn_kg": kg}


def _trunk(x3, mlstm_state, swa_buffers, p):
    b, s, d = x3.shape

    def ffn(x, name, layer, mixed=None):
        w = (p[name + "_norm"][layer], p[name + "_wgu"][layer], p[name + "_wd"][layer])
        if not isinstance(mixed, dict):
            return _ffn(x, *w, pre=mixed)
        if mixed["kind"] == "attn":
            return _attn_ffn(x, *mixed["args"], *w, mixed["dh"])
        return _mlstm_ffn(x, *mixed["args"], *w, mixed["chunk"])

    x = ffn(x3.reshape(b * s, d), "ffn1", 0)
    mixed, new_mlstm = _mlstm_layer(x.reshape(b, s, d), mlstm_state, p)
    x = ffn(x, "ffn2", 0, mixed)
    if new_mlstm is None:
        x, c1, n1, m1 = x
        new_mlstm = (c1, n1, m1.reshape(b, M_HEADS))
    x = ffn(x, "ffn1", 1)
    mixed, new_swa = _swa_layer(x.reshape(b, s, d), swa_buffers, p)
    x = ffn(x, "ffn2", 1, mixed)
    return x.reshape(b, s, d), new_mlstm, new_swa


def kernel(x_prompt, x_sample, state_mlstm_C, state_mlstm_n, state_mlstm_m, cache_swa_k, cache_swa_v,
           ffn1_norm, ffn1_w_gate_up, ffn1_w_down, mix_norm, ffn2_norm, ffn2_w_gate_up, ffn2_w_down,
           mlstm_w_in, mlstm_b_gates, mlstm_out_norm, mlstm_w_out,
           attn_w_in, attn_q_norm, attn_k_norm, attn_sinks, rel_bias, attn_w_out):
    d = x_prompt.shape[-1]
    vd = mlstm_w_out.shape[0]
    n_gates = 2 * M_HEADS
    qk = (mlstm_w_in.shape[1] - 2 * vd - n_gates) // 2
    dh = attn_q_norm.shape[0]
    assert 2 * dh == LANES
    w_gates = jnp.zeros((d, LANES), BF16).at[:, :n_gates].set(mlstm_w_in[:, 2 * qk + 2 * vd:].astype(BF16))
    b_gates = jnp.zeros((1, LANES), F32).at[0, :n_gates].set(mlstm_b_gates)
    layers = lambda w: [_layer_bf16(w, layer) for layer in range(w.shape[0])]
    p = {
        "ffn1_norm": ffn1_norm, "ffn2_norm": ffn2_norm,
        "ffn1_wgu": layers(ffn1_w_gate_up), "ffn1_wd": layers(ffn1_w_down),
        "ffn2_wgu": layers(ffn2_w_gate_up), "ffn2_wd": layers(ffn2_w_down),
        "mix_norm0": mix_norm[0], "mix_norm1": mix_norm[1],
        "mlstm_qk": qk, "mlstm_vd": vd,
        "mlstm_w_main": mlstm_w_in[:, :2 * qk + 2 * vd].astype(BF16),
        "mlstm_w_gates": w_gates, "mlstm_b_gates": b_gates,
        "mlstm_out_norm": mlstm_out_norm, "mlstm_w_out": mlstm_w_out.astype(BF16),
        "attn_dh": dh, "attn_sinks": attn_sinks, "rel_bias": rel_bias,
    }
    p.update(_attn_weights(attn_w_in, attn_w_out, attn_q_norm, attn_k_norm, dh))
    y_p, (c_p, n_p, m_p), (k_p, v_p) = _trunk(x_prompt, None, None, p)
    y_s, (c_s, n_s, m_s), (k_s, v_s) = _trunk(
        x_sample, (state_mlstm_C, state_mlstm_n, state_mlstm_m), (cache_swa_k, cache_swa_v), p)
    return (y_p, y_s, c_p, n_p, m_p, k_p, v_p, c_s, n_s, m_s, k_s, v_s)
```

```python
import functools
import math

import jax
import jax.numpy as jnp
from jax import lax
from jax.experimental import pallas as pl
from jax.experimental.pallas import tpu as pltpu

F32 = jnp.float32
BF16 = jnp.bfloat16

M_HEADS = 4
A_HEADS_Q = 16
A_HEADS_KV = 4
A_GROUP = A_HEADS_Q // A_HEADS_KV
WINDOW = 128
NUM_BUCKETS = 32
MAX_DISTANCE = 128
FFN_RESIDUAL = 0.5
RMS_EPS = 1e-6
LANES = 128
VMEM_LIMIT = 56 * 1024 * 1024


def _cparams(n_axes):
    return pltpu.CompilerParams(
        dimension_semantics=("arbitrary",) * n_axes, vmem_limit_bytes=VMEM_LIMIT)


def _resident(shape):
    nd = len(shape)
    return pl.BlockSpec(shape, lambda *_: (0,) * nd, pipeline_mode=pl.Buffered(1))


def _rms(x, g):
    return x * lax.rsqrt(jnp.mean(x * x, axis=-1, keepdims=True) + RMS_EPS) * g


def _dot(a, b):
    return jnp.dot(a, b, preferred_element_type=F32)


def _dot_nt(a, b):
    return lax.dot_general(a, b, (((1,), (1,)), ((), ())), preferred_element_type=F32)


def _dot_tn(a, b):
    return lax.dot_general(a, b, (((0,), (0,)), ((), ())), preferred_element_type=F32)


def _row_tile(t, pref):
    tm = min(t, pref)
    assert t % tm == 0
    return tm


FFN_TF = 256


def _ffn_stream_body(*refs, has_pre):
    if has_pre:
        a_ref, wa_ref, x_ref, g_ref, wg_ref, wu_ref, wd_ref, o_ref, wgu_out, wd_out, xn_ref = refs
    else:
        x_ref, g_ref, wg_ref, wu_ref, wd_ref, o_ref, wgu_out, wd_out, xn_ref = refs

    @pl.when(pl.program_id(0) == 0)
    def _():
        y = x_ref[...] + _dot(a_ref[...], wa_ref[...]) if has_pre else x_ref[...]
        o_ref[...] = y
        xn_ref[...] = _rms(y, g_ref[...]).astype(BF16)

    wg, wu, wd = (r[...].astype(BF16) for r in (wg_ref, wu_ref, wd_ref))
    wgu_out[...] = jnp.concatenate([wg, wu], axis=1)
    wd_out[...] = wd
    xn = xn_ref[...]
    gate = _dot(xn, wg)
    up = _dot(xn, wu)
    act = (gate * jax.nn.sigmoid(gate) * up).astype(BF16)
    o_ref[...] = o_ref[...] + FFN_RESIDUAL * _dot(act, wd)


def _ffn_stream(x, g, wgu_f32, wd_f32, layer, pre=None):
    t, d = x.shape
    d_ff = wd_f32.shape[1]
    tf = FFN_TF
    n_chunks = d_ff // tf
    assert d_ff % tf == 0
    whole = lambda r, c: pl.BlockSpec((r, c), lambda i: (0, 0))
    pre_specs, pre_args = [], []
    if pre is not None:
        a, wa = pre
        pre_specs, pre_args = [whole(t, a.shape[1]), whole(*wa.shape)], [a, wa]
    return pl.pallas_call(
        functools.partial(_ffn_stream_body, has_pre=pre is not None),
        grid=(n_chunks,),
        in_specs=pre_specs + [
            whole(t, d), whole(1, d),
            pl.BlockSpec((None, d, tf), lambda c: (layer, 0, c)),
            pl.BlockSpec((None, d, tf), lambda c: (layer, 0, n_chunks + c)),
            pl.BlockSpec((None, tf, d), lambda c: (layer, c, 0))],
        out_specs=[whole(t, d), pl.BlockSpec((d, 2 * tf), lambda c: (0, c)),
                   pl.BlockSpec((tf, d), lambda c: (c, 0))],
        out_shape=[jax.ShapeDtypeStruct((t, d), F32), jax.ShapeDtypeStruct((d, 2 * d_ff), BF16),
                   jax.ShapeDtypeStruct((d_ff, d), BF16)],
        scratch_shapes=[pltpu.VMEM((t, d), BF16)],
        compiler_params=_cparams(1),
        name="ffn_stream",
    )(*pre_args, x, g.reshape(1, d), wgu_f32, wgu_f32, wd_f32)

def _interleave(main, side):
    done = 0
    for j, thunk in enumerate(main):
        thunk()
        upto = (j + 1) * len(side) // len(main)
        for other in side[done:upto]:
            other()
        done = upto


def _ffn_items(get_x, g_ref, wgu_ref, wd_ref, o_ref, act_ref, d_ff, tf):
    st = {}

    def head():
        y = get_x()
        o_ref[...] = y
        st["xn"] = _rms(y, g_ref[...]).astype(BF16)

    def chunk(lo):
        gate = _dot(st["xn"], wgu_ref[:, 2 * lo:2 * lo + tf])
        up = _dot(st["xn"], wgu_ref[:, 2 * lo + tf:2 * lo + 2 * tf])
        act_ref[:, lo:lo + tf] = (gate * jax.nn.sigmoid(gate) * up).astype(BF16)

    def tail():
        o_ref[...] = o_ref[...] + FFN_RESIDUAL * _dot(act_ref[...], wd_ref[...])

    return [head] + [functools.partial(chunk, lo) for lo in range(0, d_ff, tf)] + [tail]


def _ffn_body(*refs, d_ff, tf, has_pre):
    if has_pre:
        a_ref, wa_ref, x_ref, g_ref, wgu_ref, wd_ref, o_ref, act_ref = refs
        get_x = lambda: x_ref[...] + _dot(a_ref[...], wa_ref[...])
    else:
        x_ref, g_ref, wgu_ref, wd_ref, o_ref, act_ref = refs
        get_x = lambda: x_ref[...]
    for thunk in _ffn_items(get_x, g_ref, wgu_ref, wd_ref, o_ref, act_ref, d_ff, tf):
        thunk()


def _ffn(x, g, wgu, wd, pre=None):
    t, d = x.shape
    d_ff = wd.shape[0]
    tm = _row_tile(t, 1024)
    assert d_ff % FFN_TF == 0
    row = lambda n: pl.BlockSpec((tm, n), lambda i: (i, 0))
    pre_specs, pre_args = [], []
    if pre is not None:
        a, wa = pre
        pre_specs, pre_args = [row(a.shape[1]), _resident(wa.shape)], [a, wa]
    return pl.pallas_call(
        functools.partial(_ffn_body, d_ff=d_ff, tf=FFN_TF, has_pre=pre is not None),
        grid=(t // tm,),
        in_specs=pre_specs + [row(d), _resident((1, d)), _resident((d, 2 * d_ff)), _resident((d_ff, d))],
        out_specs=row(d),
        out_shape=jax.ShapeDtypeStruct((t, d), F32),
        scratch_shapes=[pltpu.VMEM((tm, d_ff), BF16)],
        compiler_params=_cparams(1),
        name="ffn",
    )(*pre_args, x, g.reshape(1, d), wgu, wd)


def _mlstm_proj_body(x_ref, g_ref, w_ref, wg_ref, bg_ref,
                     q_ref, k_ref, v_ref, og_ref, gcol_ref, grow_ref, *, qk, vd, dk, chunk):
    h_n = M_HEADS
    xn = _rms(x_ref[...], g_ref[...]).astype(BF16)
    tm = xn.shape[0]

    shift = chunk.bit_length() - 1
    r_i = lax.broadcasted_iota(jnp.int32, (LANES, LANES), 0)
    c_i = lax.broadcasted_iota(jnp.int32, (LANES, LANES), 1)
    same_chunk = lax.shift_right_logical(r_i, shift) == lax.shift_right_logical(c_i, shift)
    tri = jnp.where((r_i >= c_i) & same_chunk, 1.0, 0.0).astype(BF16)

    gates = _dot(xn, wg_ref[...]) + bg_ref[...]
    q_ref[...] = _dot(xn, w_ref[:, 0:qk]).astype(BF16)
    lane = lax.broadcasted_iota(jnp.int32, (tm, LANES), 1)
    log_f = jnp.where((lane >= h_n) & (lane < 2 * h_n), jax.nn.log_sigmoid(gates), 0.0)
    hi = log_f.astype(BF16)
    lo = (log_f - hi.astype(F32)).astype(BF16)
    k_ref[...] = (_dot(xn, w_ref[:, qk:2 * qk]) * (dk ** -0.5)).astype(BF16)
    hi_lo = jnp.concatenate([hi, lo], axis=1)
    csum = jnp.concatenate([_dot(tri, hi_lo[r:r + LANES]) for r in range(0, tm, LANES)], axis=0)
    gcol = jnp.where(lane < h_n, gates, csum[:, :LANES] + csum[:, LANES:])
    gcol_ref[...] = gcol
    v_ref[...] = _dot(xn, w_ref[:, 2 * qk:2 * qk + vd]).astype(BF16)
    g_t = gcol.T
    grow_ref[0] = jnp.concatenate([g_t[0:h_n] - g_t[h_n:2 * h_n], g_t[h_n:2 * h_n]], axis=0)
    og_ref[...] = jax.nn.sigmoid(_dot(xn, w_ref[:, 2 * qk + vd:2 * qk + 2 * vd])).astype(BF16)


def _mlstm_proj(x3, g, w_main, w_gates, b_gates, qk, vd, chunk):
    b, s, d = x3.shape
    t = b * s
    tm = _row_tile(s, 1024)
    nt = s // tm
    assert chunk & (chunk - 1) == 0 and LANES % chunk == 0 and tm % LANES == 0
    row = lambda n: pl.BlockSpec((tm, n), lambda i: (i, 0))
    return pl.pallas_call(
        functools.partial(_mlstm_proj_body, qk=qk, vd=vd, dk=qk // M_HEADS, chunk=chunk),
        grid=(t // tm,),
        in_specs=[row(d), _resident((1, d)), _resident(w_main.shape),
                  _resident(w_gates.shape), _resident((1, LANES))],
        out_specs=[row(qk), row(qk), row(vd), row(vd), row(LANES),
                   pl.BlockSpec((1, 2 * M_HEADS, tm), lambda i: (i // nt, 0, i % nt))],
        out_shape=[jax.ShapeDtypeStruct((t, qk), BF16), jax.ShapeDtypeStruct((t, qk), BF16),
                   jax.ShapeDtypeStruct((t, vd), BF16), jax.ShapeDtypeStruct((t, vd), BF16),
                   jax.ShapeDtypeStruct((t, LANES), F32),
                   jax.ShapeDtypeStruct((b, 2 * M_HEADS, s), F32)],
        compiler_params=_cparams(1),
        name="mlstm_proj",
    )(x3.reshape(t, d), g.reshape(1, d), w_main, w_gates, b_gates)


def _mlstm_chunk_items(units, load, state, emit, rows):
    r_i = lax.broadcasted_iota(jnp.int32, (rows, rows), 0)
    c_i = lax.broadcasted_iota(jnp.int32, (rows, rows), 1)
    causal = r_i >= c_i
    us = range(len(units))
    st = {}

    def products():
        st["x"] = x = [load(u) for u in units]
        st["old"] = [state[u] for u in units]
        st["qk"] = [_dot_nt(x[i]["q"], x[i]["k"]) for i in us]
        st["qc"] = [_dot(x[i]["q"], st["old"][i][0].astype(BF16)) for i in us]

    def weights():
        x, old = st["x"], st["old"]
        log_d = [jnp.where(causal, x[i]["b_c"] + x[i]["imb"], -jnp.inf) for i in us]
        log_inter = [x[i]["b_c"] + old[i][2] for i in us]
        st["m_t"] = m_t = [jnp.maximum(log_inter[i], jnp.max(log_d[i], axis=-1, keepdims=True)) for i in us]
        st["inter"] = [jnp.exp(log_inter[i] - m_t[i]) for i in us]
        st["s"] = [st["qk"][i] * jnp.exp(log_d[i] - m_t[i]) for i in us]

    def hidden():
        x, old, s, inter, m_t = st["x"], st["old"], st["s"], st["inter"], st["m_t"]
        s_v = [_dot(s[i].astype(BF16), x[i]["v"]) for i in us]
        den = [inter[i] * jnp.sum(x[i]["q"].astype(F32) * old[i][1], axis=-1, keepdims=True)
               + jnp.sum(s[i], axis=-1, keepdims=True) for i in us]
        st["hid"] = [(inter[i] * st["qc"][i] + s_v[i]) / jnp.maximum(jnp.abs(den[i]), jnp.exp(-m_t[i]))
                     for i in us]

    def update():
        x, old = st["x"], st["old"]
        b_last = [x[i]["b_c"][rows - 1:rows, :] for i in us]
        log_w = [b_last[i] - x[i]["b_c"] + x[i]["i_c"] for i in us]
        m_new = [jnp.maximum(b_last[i] + old[i][2], jnp.max(log_w[i], axis=0, keepdims=True)) for i in us]
        decay = [jnp.exp(b_last[i] + old[i][2] - m_new[i]) for i in us]
        kw = [jnp.exp(log_w[i] - m_new[i]) * x[i]["k"].astype(F32) for i in us]
        kw_v = [_dot_tn(kw[i].astype(BF16), x[i]["v"]) for i in us]
        for i, u in enumerate(units):
            state[u] = (decay[i] * old[i][0] + kw_v[i],
                        decay[i] * old[i][1] + jnp.sum(kw[i], axis=0, keepdims=True), m_new[i])

    def output():
        hid = [st["hid"][i] * lax.rsqrt(jnp.mean(st["hid"][i] * st["hid"][i], axis=-1, keepdims=True) + RMS_EPS)
               for i in us]
        for i, u in enumerate(units):
            emit(u, (hid[i] * st["x"][i]["ng"] * st["x"][i]["og"].astype(F32)).astype(BF16))

    return [products, weights, hidden, update, output]


def _mlstm_cell_body(q_ref, k_ref, v_ref, og_ref, gcol_ref, grow_ref, ng_ref, c0_ref, n0_ref, m0_ref,
                     hg_ref, c_ref, n_ref, m_ref, *, dk, dv, bt):
    h_n = M_HEADS
    rows = q_ref.shape[1]
    units = [(b, h) for b in range(bt) for h in range(h_n)]

    def load(u):
        b, h = u
        kc, vc = slice(h * dk, (h + 1) * dk), slice(h * dv, (h + 1) * dv)
        return dict(q=q_ref[b, :, kc], k=k_ref[b, :, kc], v=v_ref[b, :, vc], og=og_ref[b, :, vc],
                    ng=ng_ref[:, vc], b_c=gcol_ref[b, :, h_n + h:h_n + h + 1], i_c=gcol_ref[b, :, h:h + 1],
                    imb=grow_ref[b, h:h + 1, :])

    def emit(u, value):
        b, h = u
        hg_ref[b, :, h * dv:(h + 1) * dv] = value

    state = {(b, h): (c0_ref[b, h], n0_ref[b, h:h + 1, :], m0_ref[b, :, h:h + 1]) for b, h in units}
    for thunk in _mlstm_chunk_items(units, load, state, emit, rows):
        thunk()
    for b, h in units:
        c_ref[b, h], n_ref[b, h:h + 1, :], m_ref[b, :, h:h + 1] = state[(b, h)]


def _mlstm_cell(q, k, v, og, gcol, grow, norm_g, c0, n0, m0, bt):
    b, s, qk = q.shape
    vd = v.shape[-1]
    dk, dv = qk // M_HEADS, vd // M_HEADS
    assert b % bt == 0
    tok = lambda n: pl.BlockSpec((bt, s, n), lambda i: (i, 0, 0))
    st_c = pl.BlockSpec((bt, M_HEADS, dk, dv), lambda i: (i, 0, 0, 0))
    st_n = pl.BlockSpec((bt, M_HEADS, dk), lambda i: (i, 0, 0))
    st_m = pl.BlockSpec((bt, 1, M_HEADS), lambda i: (i, 0, 0))
    return pl.pallas_call(
        functools.partial(_mlstm_cell_body, dk=dk, dv=dv, bt=bt),
        grid=(b // bt,),
        in_specs=[tok(qk), tok(qk), tok(vd), tok(vd), tok(LANES),
                  pl.BlockSpec((bt, 2 * M_HEADS, s), lambda i: (i, 0, 0)),
                  pl.BlockSpec((1, vd), lambda i: (0, 0)), st_c, st_n, st_m],
        out_specs=[tok(vd), st_c, st_n, st_m],
        out_shape=[jax.ShapeDtypeStruct((b, s, vd), BF16),
                   jax.ShapeDtypeStruct((b, M_HEADS, dk, dv), F32),
                   jax.ShapeDtypeStruct((b, M_HEADS, dk), F32),
                   jax.ShapeDtypeStruct((b, 1, M_HEADS), F32)],
        compiler_params=_cparams(1),
        name="mlstm_cell",
    )(q, k, v, og, gcol, grow, norm_g.reshape(1, vd), c0, n0, m0)


def _mlstm_ffn_body(q_ref, k_ref, v_ref, og_ref, gcol_ref, grow_ref, ng_ref, wa_ref,
                    x_ref, g_ref, wgu_ref, wd_ref, o_ref, c_ref, n_ref, m_ref, act_ref, hg_ref,
                    *, d_ff, tf, dk, dv, chunk, tiles_per_seq):
    h_n = M_HEADS
    i = pl.program_id(0)
    last = pl.num_programs(0) - 1
    live = i < last
    first_of_seq = lax.rem(jnp.minimum(i, last - 1), tiles_per_seq) == 0

    @pl.when(i == 0)
    def _():
        hg_ref[...] = jnp.zeros_like(hg_ref)

    @pl.when(first_of_seq & live)
    def _():
        c_ref[...] = jnp.zeros_like(c_ref)
        n_ref[...] = jnp.zeros_like(n_ref)
        m_ref[...] = jnp.zeros_like(m_ref)

    cur = lax.rem(i, 2)
    heads = list(range(h_n))
    state = {h: (c_ref[0, h], n_ref[0, h:h + 1, :], m_ref[0, :, h:h + 1]) for h in heads}
    side = []
    for j in range(q_ref.shape[0] // chunk):
        rows = slice(j * chunk, (j + 1) * chunk)

        def load(h, rows=rows):
            kc, vc = slice(h * dk, (h + 1) * dk), slice(h * dv, (h + 1) * dv)
            return dict(q=q_ref[rows, kc], k=k_ref[rows, kc], v=v_ref[rows, vc], og=og_ref[rows, vc],
                        ng=ng_ref[:, vc], b_c=gcol_ref[rows, h_n + h:h_n + h + 1], i_c=gcol_ref[rows, h:h + 1],
                        imb=grow_ref[0, h:h + 1, rows])

        def emit(h, value, rows=rows):
            hg_ref[cur, rows, h * dv:(h + 1) * dv] = value

        side += _mlstm_chunk_items(heads, load, state, emit, chunk)
    get_x = lambda: x_ref[...] + _dot(hg_ref[1 - cur], wa_ref[...])
    _interleave(_ffn_items(get_x, g_ref, wgu_ref, wd_ref, o_ref, act_ref, d_ff, tf), side)

    @pl.when(live)
    def _():
        for h in heads:
            c_ref[0, h], n_ref[0, h:h + 1, :], m_ref[0, :, h:h + 1] = state[h]


def _mlstm_ffn(x, q, k, v, og, gcol, grow, norm_g, wa, g, wgu, wd, chunk):
    t, d = x.shape
    b, _, s = grow.shape
    d_ff = wd.shape[0]
    qk, vd = q.shape[1], v.shape[1]
    dk, dv = qk // M_HEADS, vd // M_HEADS
    tm = _row_tile(s, 512)
    nt, tps = t // tm, s // tm
    assert tm % chunk == 0
    mix_tile = lambda i: jnp.minimum(i, nt - 1)
    ffn_row = lambda n: pl.BlockSpec((tm, n), lambda i: (jnp.maximum(i - 1, 0), 0))
    mix_row = lambda n: pl.BlockSpec((tm, n), lambda i: (mix_tile(i), 0))
    seq = lambda *tail: pl.BlockSpec((1,) + tail, lambda i: (mix_tile(i) // tps,) + (0,) * len(tail))
    return pl.pallas_call(
        functools.partial(_mlstm_ffn_body, d_ff=d_ff, tf=FFN_TF, dk=dk, dv=dv, chunk=chunk, tiles_per_seq=tps),
        grid=(nt + 1,),
        in_specs=[mix_row(qk), mix_row(qk), mix_row(vd), mix_row(vd), mix_row(LANES),
                  pl.BlockSpec((1, 2 * M_HEADS, tm), lambda i: (mix_tile(i) // tps, 0, mix_tile(i) % tps)),
                  _resident((1, vd)), _resident(wa.shape),
                  ffn_row(d), _resident((1, d)), _resident((d, 2 * d_ff)), _resident((d_ff, d))],
        out_specs=[ffn_row(d), seq(M_HEADS, dk, dv), seq(M_HEADS, dk), seq(1, M_HEADS)],
        out_shape=[jax.ShapeDtypeStruct((t, d), F32),
                   jax.ShapeDtypeStruct((b, M_HEADS, dk, dv), F32),
                   jax.ShapeDtypeStruct((b, M_HEADS, dk), F32),
                   jax.ShapeDtypeStruct((b, 1, M_HEADS), F32)],
        scratch_shapes=[pltpu.VMEM((tm, d_ff), BF16), pltpu.VMEM((2, tm, vd), BF16)],
        compiler_params=_cparams(1),
        name="mlstm_ffn",
    )(q, k, v, og, gcol, grow, norm_g.reshape(1, vd), wa, x, g.reshape(1, d), wgu, wd)


def _mlstm_layer(x3, state, p):
    b, s, d = x3.shape
    qk, vd = p["mlstm_qk"], p["mlstm_vd"]
    proj = functools.partial(_mlstm_proj, g=p["mix_norm0"], w_main=p["mlstm_w_main"],
                             w_gates=p["mlstm_w_gates"], b_gates=p["mlstm_b_gates"], qk=qk, vd=vd)
    if state is None:
        chunk = min(s, 128)
        q, k, v, og, gcol, grow = proj(x3, chunk=chunk)
        return dict(kind="mlstm", args=(q, k, v, og, gcol, grow, p["mlstm_out_norm"], p["mlstm_w_out"]),
                    chunk=chunk), None
    c0, n0, m0 = state
    q, k, v, og, gcol, grow = proj(x3.reshape(1, b * s, d), chunk=s)
    grow = jnp.swapaxes(grow.reshape(2 * M_HEADS, b, s), 0, 1)
    sh = lambda a: a.reshape(b, s, a.shape[-1])
    hg, c1, n1, m1 = _mlstm_cell(sh(q), sh(k), sh(v), sh(og), sh(gcol), grow, p["mlstm_out_norm"],
                                 c0, n0, m0.reshape(b, 1, M_HEADS), math.gcd(b, 8))
    return (hg.reshape(b * s, vd), p["mlstm_w_out"]), (c1, n1, m1.reshape(b, M_HEADS))


def _attn_proj_body(x_ref, g_ref, w_ref, wvt_ref, qg_ref, kg_ref, q_ref, k_ref, vt_ref, *, dh, nq, nk):
    xn = _rms(x_ref[...], g_ref[...]).astype(BF16)
    pair = 2 * LANES
    r_i = lax.broadcasted_iota(jnp.int32, (pair, pair), 0)
    c_i = lax.broadcasted_iota(jnp.int32, (pair, pair), 1)
    shift = dh.bit_length() - 1
    same_head = jnp.where(lax.shift_right_logical(r_i, shift) == lax.shift_right_logical(c_i, shift),
                          1.0, 0.0).astype(BF16)

    def head_norm(y, gain):
        sq = y * y
        hi = sq.astype(BF16)
        lo = (sq - hi.astype(F32)).astype(BF16)
        ms = (_dot(hi, same_head) + _dot(lo, same_head)) * (1.0 / dh)
        return y * lax.rsqrt(ms + RMS_EPS) * gain

    qg = jnp.concatenate([qg_ref[...], qg_ref[...]], axis=1)
    kg = jnp.concatenate([kg_ref[...], kg_ref[...]], axis=1)
    q = _dot(xn, w_ref[:, 0:nq])
    for s in range(nq // pair):
        sl = slice(s * pair, (s + 1) * pair)
        q_ref[:, sl] = head_norm(q[:, sl], qg).astype(BF16)
    k = _dot(xn, w_ref[:, nq:nq + nk])
    for s in range(nk // pair):
        sl = slice(s * pair, (s + 1) * pair)
        k_ref[:, sl] = head_norm(k[:, sl], kg)
    vt_ref[0] = _dot_nt(wvt_ref[...], xn)


def _attn_proj(x3, g, w, wvt, qg, kg, dh, nq, nk):
    b, s, d = x3.shape
    t = b * s
    nv = wvt.shape[0]
    tm = _row_tile(s, 1024)
    nt = s // tm
    row = lambda n: pl.BlockSpec((tm, n), lambda i: (i, 0))
    return pl.pallas_call(
        functools.partial(_attn_proj_body, dh=dh, nq=nq, nk=nk),
        grid=(t // tm,),
        in_specs=[row(d), _resident((1, d)), _resident(w.shape), _resident(wvt.shape),
                  _resident((1, LANES)), _resident((1, LANES))],
        out_specs=[row(nq), row(nk), pl.BlockSpec((1, nv, tm), lambda i: (i // nt, 0, i % nt))],
        out_shape=[jax.ShapeDtypeStruct((t, nq), BF16), jax.ShapeDtypeStruct((t, nk), F32),
                   jax.ShapeDtypeStruct((b, nv, s), F32)],
        compiler_params=_cparams(1),
        name="attn_proj",
    )(x3.reshape(t, d), g.reshape(1, d), w, wvt, qg, kg)


def _bias_body(rel_ref, o_ref, *, tq, key_major):
    nk = 2 * WINDOW
    shape, q_ax = ((nk, tq), 1) if key_major else ((tq, nk), 0)
    qi = lax.broadcasted_iota(jnp.int32, shape, q_ax)
    ki = lax.broadcasted_iota(jnp.int32, shape, 1 - q_ax)
    dist = qi + WINDOW - ki
    max_exact = NUM_BUCKETS // 2
    d = jnp.maximum(dist, 0)
    log_ratio = (jnp.log(jnp.maximum(d, 1).astype(F32) / max_exact)
                 / math.log(MAX_DISTANCE / max_exact))
    large = jnp.minimum(max_exact + (log_ratio * (NUM_BUCKETS - max_exact)).astype(jnp.int32),
                        NUM_BUCKETS - 1)
    bucket = jnp.where(d < max_exact, d, large)
    in_window = (dist >= 0) & (dist < WINDOW)
    visible = (in_window & (ki >= WINDOW), in_window)
    for h in range(A_HEADS_Q):
        bias = jnp.zeros(shape, F32)
        for bkt in range(NUM_BUCKETS):
            bias = jnp.where(bucket == bkt, rel_ref[bkt, h], bias)
        for variant in range(2):
            tile = jnp.where(visible[variant], bias, -jnp.inf)
            if key_major:
                o_ref[variant, h // 2, :, (h % 2) * tq:(h % 2 + 1) * tq] = tile
            else:
                o_ref[variant, h] = tile


def _bias_table(rel_bias, tq, key_major):
    nk = 2 * WINDOW
    shape = (2, A_HEADS_Q // 2, nk, 2 * tq) if key_major else (2, A_HEADS_Q, tq, nk)
    return pl.pallas_call(
        functools.partial(_bias_body, tq=tq, key_major=key_major),
        grid=(1,),
        in_specs=[pl.BlockSpec(memory_space=pltpu.SMEM)],
        out_specs=pl.BlockSpec(shape, lambda i: (0, 0, 0, 0)),
        out_shape=jax.ShapeDtypeStruct(shape, F32),
        compiler_params=_cparams(1),
        name="rel_bias_table",
    )(rel_bias)


def _attn_tile_items(sink_ref, q_ref, kp_ref, ko_ref, vp_ref, vo_ref, bias_ref, first_of_seq, write, dh):
    blk = WINDOW
    n_blk = ko_ref.shape[0] // blk
    slots_per_kv = A_GROUP // 2
    n_slots = A_HEADS_Q // 2
    lane = lax.broadcasted_iota(jnp.int32, (blk, LANES), 1)
    lo_half = lane < dh
    left = lax.broadcasted_iota(jnp.int32, (1, 2 * blk), 1) < blk
    ones = jnp.ones((dh, 2 * blk), BF16)
    sinks = [jnp.where(left, sink_ref[2 * i], sink_ref[2 * i + 1]) for i in range(n_slots)]
    items = []
    for j in range(n_blk):
        st = {}
        own = slice(j * blk, (j + 1) * blk)
        before = slice((j - 1) * blk, j * blk)
        variant = jnp.where(first_of_seq, 0, 1) if j == 0 else 1

        def scores(st=st, j=j, own=own, before=before, variant=variant):
            st["vts"], st["sts"] = [], []
            for g in range(A_HEADS_KV):
                sl = slice(g * LANES, (g + 1) * LANES)
                rows = slice(g * dh, (g + 1) * dh)
                k_prev = kp_ref[:, sl] if j == 0 else ko_ref[before, sl]
                v_prev = vp_ref[0, rows, :] if j == 0 else vo_ref[0, rows, before]
                kk = jnp.concatenate([k_prev, ko_ref[own, sl]], axis=0).astype(BF16)
                vt = jnp.concatenate([v_prev, vo_ref[0, rows, own]], axis=1).astype(BF16)
                st["vts"].append(jnp.concatenate([vt, ones], axis=0))
                for pr in range(slots_per_kv):
                    slot_i = g * slots_per_kv + pr
                    slot = q_ref[own, slot_i * LANES:(slot_i + 1) * LANES]
                    zero = jnp.zeros_like(slot)
                    q2 = jnp.concatenate([jnp.where(lo_half, slot, zero), jnp.where(lo_half, zero, slot)],
                                         axis=0)
                    st["sts"].append(_dot_nt(kk, q2) + bias_ref[variant, slot_i])

        def softmax(st=st):
            st["ms"] = [jnp.maximum(jnp.max(st["sts"][i], axis=0, keepdims=True), sinks[i])
                        for i in range(n_slots)]
            st["pts"] = [jnp.exp(st["sts"][i] - st["ms"][i]).astype(BF16) for i in range(n_slots)]

        def values(st=st):
            st["oes"] = [_dot(st["vts"][i // slots_per_kv], st["pts"][i]) for i in range(n_slots)]

        def finish(st=st, j=j):
            heads_t = []
            for i in range(n_slots):
                oe = st["oes"][i]
                ot = oe[0:dh, :] / (oe[dh:dh + 1, :] + jnp.exp(sinks[i] - st["ms"][i]))
                heads_t += [ot[:, 0:blk], ot[:, blk:2 * blk]]
            write(j, jnp.concatenate(heads_t, axis=0).T.astype(BF16))

        items += [scores, softmax, values, finish]
    return items


def _attn_ffn_body(sink_ref, q_ref, kp_ref, ko_ref, vp_ref, vo_ref, bias_ref, wa_ref,
                   x_ref, g_ref, wgu_ref, wd_ref, o_ref, act_ref, att_ref, *, d_ff, tf, dh, tiles_per_seq):
    i = pl.program_id(0)

    @pl.when(i == 0)
    def _():
        att_ref[...] = jnp.zeros_like(att_ref)

    cur = lax.rem(i, 2)
    first_of_seq = lax.rem(jnp.minimum(i, pl.num_programs(0) - 2), tiles_per_seq) == 0

    def write(j, tile):
        att_ref[cur, j * WINDOW:(j + 1) * WINDOW, :] = tile

    side = _attn_tile_items(sink_ref, q_ref, kp_ref, ko_ref, vp_ref, vo_ref, bias_ref,
                            first_of_seq, write, dh)
    get_x = lambda: x_ref[...] + _dot(att_ref[1 - cur], wa_ref[...])
    _interleave(_ffn_items(get_x, g_ref, wgu_ref, wd_ref, o_ref, act_ref, d_ff, tf), side)


def _attn_ffn(x, q, k, vt, bias, sinks, wa, g, wgu, wd, dh):
    t, d = x.shape
    b, vd, s = vt.shape
    d_ff = wd.shape[0]
    qd, kd = q.shape[1], k.shape[1]
    blk = WINDOW
    tm = _row_tile(s, 512)
    nt, tps, bpt = t // tm, s // tm, tm // blk
    att_tile = lambda i: jnp.minimum(i, nt - 1)
    ffn_tile = lambda i: jnp.maximum(i - 1, 0)
    ffn_row = lambda n: pl.BlockSpec((tm, n), lambda i: (ffn_tile(i), 0))
    att_row = lambda n: pl.BlockSpec((tm, n), lambda i: (att_tile(i), 0))
    return pl.pallas_call(
        functools.partial(_attn_ffn_body, d_ff=d_ff, tf=FFN_TF, dh=dh, tiles_per_seq=tps),
        grid=(nt + 1,),
        in_specs=[pl.BlockSpec(memory_space=pltpu.SMEM),
                  att_row(qd),
                  pl.BlockSpec((blk, kd), lambda i: (jnp.maximum(att_tile(i) * bpt - 1, 0), 0)),
                  att_row(kd),
                  pl.BlockSpec((1, vd, blk), lambda i: (att_tile(i) // tps,
                                                        0, jnp.maximum(att_tile(i) % tps * bpt - 1, 0))),
                  pl.BlockSpec((1, vd, tm), lambda i: (att_tile(i) // tps, 0, att_tile(i) % tps)),
                  _resident(bias.shape), _resident(wa.shape),
                  ffn_row(d), _resident((1, d)), _resident((d, 2 * d_ff)), _resident((d_ff, d))],
        out_specs=ffn_row(d),
        out_shape=jax.ShapeDtypeStruct((t, d), F32),
        scratch_shapes=[pltpu.VMEM((tm, d_ff), BF16), pltpu.VMEM((2, tm, qd), BF16)],
        compiler_params=_cparams(1),
        name="attn_ffn",
    )(sinks, q, k, k, vt, vt, bias, wa, x, g.reshape(1, d), wgu, wd)


def _attn_sample_body(sink_ref, q_ref, kc_ref, kn_ref, vc_ref, vn_ref, bias_ref,
                      o_ref, ko_ref, vo_ref, *, bt, tq):
    dh = q_ref.shape[-1]
    fresh = lax.broadcasted_iota(jnp.int32, (dh, WINDOW), 1) >= WINDOW - tq
    sinks = [jnp.concatenate([jnp.full((tq, 1), sink_ref[g * A_GROUP + j], F32) for j in range(A_GROUP)],
                             axis=0) for g in range(A_HEADS_KV)]

    def slide(win_ref, new_ref, b, g):
        new = pltpu.roll(new_ref[g], (WINDOW - tq - b * tq) % WINDOW, axis=1)
        return jnp.where(fresh, new, pltpu.roll(win_ref[b, g], WINDOW - tq, axis=1))

    units = [(b, g) for b in range(bt) for g in range(A_HEADS_KV)]
    us = range(len(units))
    q = [q_ref[b, g] for b, g in units]
    k_win = [slide(kc_ref, kn_ref, b, g) for b, g in units]
    v_win = [slide(vc_ref, vn_ref, b, g) for b, g in units]
    for u, (b, g) in enumerate(units):
        ko_ref[b, g] = k_win[u]
        vo_ref[b, g] = v_win[u]
    s_old = [_dot(q[u], kc_ref[b, g].astype(BF16)) + bias_ref[0, g] for u, (b, g) in enumerate(units)]
    s_new = [_dot(q[u], k_win[u].astype(BF16)) + bias_ref[1, g] for u, (b, g) in enumerate(units)]
    m = [jnp.maximum(jnp.maximum(jnp.max(s_old[u], axis=-1, keepdims=True),
                                 jnp.max(s_new[u], axis=-1, keepdims=True)), sinks[g])
         for u, (b, g) in enumerate(units)]
    p_old = [jnp.exp(s_old[u] - m[u]) for u in us]
    p_new = [jnp.exp(s_new[u] - m[u]) for u in us]
    den = [jnp.sum(p_old[u], axis=-1, keepdims=True) + jnp.sum(p_new[u], axis=-1, keepdims=True)
           + jnp.exp(sinks[g] - m[u]) for u, (b, g) in enumerate(units)]
    out = [_dot_nt((p_old[u] / den[u]).astype(BF16), vc_ref[b, g].astype(BF16))
           + _dot_nt((p_new[u] / den[u]).astype(BF16), v_win[u].astype(BF16))
           for u, (b, g) in enumerate(units)]
    for u, (b, g) in enumerate(units):
        o_ref[b, g] = out[u].astype(BF16)


def _attn_sample(q, kn, vn, k_win, v_win, bias, sinks, tq):
    b, n_kv, rows, dh = q.shape
    assert LANES % tq == 0
    bt = LANES // tq
    assert b % bt == 0
    lead = lambda *tail: pl.BlockSpec((bt,) + tail, lambda i: (i, 0, 0, 0))
    new = pl.BlockSpec((n_kv, dh, LANES), lambda i: (0, 0, i))
    return pl.pallas_call(
        functools.partial(_attn_sample_body, bt=bt, tq=tq),
        grid=(b // bt,),
        in_specs=[pl.BlockSpec(memory_space=pltpu.SMEM),
                  lead(n_kv, rows, dh), lead(n_kv, dh, WINDOW), new, lead(n_kv, dh, WINDOW), new,
                  pl.BlockSpec(bias.shape, lambda i: (0, 0, 0, 0))],
        out_specs=[lead(n_kv, rows, dh), lead(n_kv, dh, WINDOW), lead(n_kv, dh, WINDOW)],
        out_shape=[jax.ShapeDtypeStruct(q.shape, BF16),
                   jax.ShapeDtypeStruct(k_win.shape, F32),
                   jax.ShapeDtypeStruct(v_win.shape, F32)],
        compiler_params=_cparams(1),
        name="attn_sample",
    )(sinks, q, k_win, kn, v_win, vn, bias)


def _swa_layer(x3, buffers, p):
    b, s, d = x3.shape
    dh = p["attn_dh"]
    kd = A_HEADS_KV * dh
    x = x3.reshape(b * s, d)
    kv_shape = (b, WINDOW, A_HEADS_KV, dh)
    if buffers is None:
        nq, nk = A_HEADS_Q * dh, A_HEADS_KV * LANES
        q, k, vt = _attn_proj(x3, p["mix_norm1"], p["attn_w_prompt"], p["attn_wvt"],
                              p["attn_qg"], p["attn_kg"], dh, nq, nk)
        bias = _bias_table(p["rel_bias"], WINDOW, key_major=True)
        new_k = k.reshape(b, s, nk)[:, s - WINDOW:].reshape(b, WINDOW, A_HEADS_KV, LANES)[..., :dh]
        new_v = jnp.swapaxes(vt[:, :, s - WINDOW:], 1, 2)
        mixed = dict(kind="attn", args=(q, k, vt, bias, p["attn_sinks"], p["attn_w_out"]), dh=dh)
    else:
        k_buf, v_buf = buffers
        nq, nk = A_HEADS_Q * dh, kd
        q, k, vt = _attn_proj(x.reshape(1, b * s, d), p["mix_norm1"], p["attn_w_sample"], p["attn_wvt"],
                              p["attn_qg"], p["attn_kg"], dh, nq, nk)
        rows = A_GROUP * s
        q = jnp.transpose(q.reshape(b, s, A_HEADS_KV, A_GROUP, dh), (0, 2, 3, 1, 4))
        table = _bias_table(p["rel_bias"], s, key_major=False)[1].reshape(A_HEADS_KV, rows, 2 * WINDOW)
        own = jnp.pad(table[..., WINDOW:WINDOW + s], ((0, 0), (0, 0), (WINDOW - s, 0)),
                      constant_values=-jnp.inf)
        bias = jnp.stack([table[..., :WINDOW], own])
        to_lanes = lambda a: jnp.transpose(a, (0, 2, 3, 1))
        att, new_k, new_v = _attn_sample(q.reshape(b, A_HEADS_KV, rows, dh),
                                         k.T.reshape(A_HEADS_KV, dh, b * s), vt[0].reshape(A_HEADS_KV, dh, b * s),
                                         to_lanes(k_buf), to_lanes(v_buf), bias, p["attn_sinks"], s)
        new_k, new_v = (jnp.transpose(a, (0, 3, 1, 2)) for a in (new_k, new_v))
        att = jnp.transpose(att.reshape(b, A_HEADS_KV, A_GROUP, s, dh), (0, 3, 1, 2, 4))
        mixed = (att.reshape(b * s, nq), p["attn_w_out"])
    return mixed, (new_k.reshape(kv_shape), new_v.reshape(kv_shape))


def _attn_weights(w_in, w_out, q_norm, k_norm, dh):
    d = w_in.shape[0]
    qd, kd = A_HEADS_Q * dh, A_HEADS_KV * dh
    wk = w_in[:, qd:qd + kd].reshape(d, A_HEADS_KV, dh)
    k_dup = jnp.concatenate([wk, wk], axis=-1).reshape(d, A_HEADS_KV * LANES)
    w_prompt = jnp.concatenate([w_in[:, :qd], k_dup], axis=1).astype(BF16)
    qg = jnp.concatenate([q_norm, q_norm]).reshape(1, LANES) * (dh ** -0.5)
    kg = jnp.concatenate([k_norm, k_norm]).reshape(1, LANES)
    return {"attn_w_prompt": w_prompt, "attn_w_sample": w_in[:, :qd + kd].astype(BF16),
            "attn_wvt": w_in[:, qd + kd:].T.astype(BF16),
            "attn_w_out": w_out.astype(BF16),
            "attn_qg": qg, "attn_kg": kg}


def _trunk(x3, mlstm_state, swa_buffers, p, ffn_bf16):
    b, s, d = x3.shape
    stream = mlstm_state is not None

    def ffn(x, name, layer, mixed=None):
        norm = p[name + "_norm"][layer]
        if stream:
            y, *ffn_bf16[name, layer] = _ffn_stream(x, norm, *p[name + "_f32"], layer, pre=mixed)
            return y
        w = (norm, *ffn_bf16[name, layer])
        if not isinstance(mixed, dict):
            return _ffn(x, *w, pre=mixed)
        if mixed["kind"] == "attn":
            return _attn_ffn(x, *mixed["args"], *w, mixed["dh"])
        return _mlstm_ffn(x, *mixed["args"], *w, mixed["chunk"])

    x = ffn(x3.reshape(b * s, d), "ffn1", 0)
    mixed, new_mlstm = _mlstm_layer(x.reshape(b, s, d), mlstm_state, p)
    x = ffn(x, "ffn2", 0, mixed)
    if new_mlstm is None:
        x, c1, n1, m1 = x
        new_mlstm = (c1, n1, m1.reshape(b, M_HEADS))
    x = ffn(x, "ffn1", 1)
    mixed, new_swa = _swa_layer(x.reshape(b, s, d), swa_buffers, p)
    x = ffn(x, "ffn2", 1, mixed)
    return x.reshape(b, s, d), new_mlstm, new_swa


def kernel(x_prompt, x_sample, state_mlstm_C, state_mlstm_n, state_mlstm_m, cache_swa_k, cache_swa_v,
           ffn1_norm, ffn1_w_gate_up, ffn1_w_down, mix_norm, ffn2_norm, ffn2_w_gate_up, ffn2_w_down,
           mlstm_w_in, mlstm_b_gates, mlstm_out_norm, mlstm_w_out,
           attn_w_in, attn_q_norm, attn_k_norm, attn_sinks, rel_bias, attn_w_out):
    d = x_prompt.shape[-1]
    vd = mlstm_w_out.shape[0]
    n_gates = 2 * M_HEADS
    qk = (mlstm_w_in.shape[1] - 2 * vd - n_gates) // 2
    dh = attn_q_norm.shape[0]
    assert 2 * dh == LANES
    w_gates = jnp.zeros((d, LANES), BF16).at[:, :n_gates].set(mlstm_w_in[:, 2 * qk + 2 * vd:].astype(BF16))
    b_gates = jnp.zeros((1, LANES), F32).at[0, :n_gates].set(mlstm_b_gates)
    p = {
        "ffn1_norm": ffn1_norm, "ffn2_norm": ffn2_norm,
        "ffn1_f32": (ffn1_w_gate_up, ffn1_w_down), "ffn2_f32": (ffn2_w_gate_up, ffn2_w_down),
        "mix_norm0": mix_norm[0], "mix_norm1": mix_norm[1],
        "mlstm_qk": qk, "mlstm_vd": vd,
        "mlstm_w_main": mlstm_w_in[:, :2 * qk + 2 * vd].astype(BF16),
        "mlstm_w_gates": w_gates, "mlstm_b_gates": b_gates,
        "mlstm_out_norm": mlstm_out_norm, "mlstm_w_out": mlstm_w_out.astype(BF16),
        "attn_dh": dh, "attn_sinks": attn_sinks, "rel_bias": rel_bias,
    }
    p.update(_attn_weights(attn_w_in, attn_w_out, attn_q_norm, attn_k_norm, dh))
    ffn_bf16 = {}
    y_s, (c_s, n_s, m_s), (k_s, v_s) = _trunk(
        x_sample, (state_mlstm_C, state_mlstm_n, state_mlstm_m), (cache_swa_k, cache_swa_v), p, ffn_bf16)
    y_p, (c_p, n_p, m_p), (k_p, v_p) = _trunk(x_prompt, None, None, p, ffn_bf16)
    return (y_p, y_s, c_p, n_p, m_p, k_p, v_p, c_s, n_s, m_s, k_s, v_s)
```

```python
import functools
import math

import jax
import jax.numpy as jnp
from jax import lax
from jax.experimental import pallas as pl
from jax.experimental.pallas import tpu as pltpu

F32 = jnp.float32
BF16 = jnp.bfloat16

M_HEADS = 4
A_HEADS_Q = 16
A_HEADS_KV = 4
A_GROUP = A_HEADS_Q // A_HEADS_KV
WINDOW = 128
NUM_BUCKETS = 32
MAX_DISTANCE = 128
FFN_RESIDUAL = 0.5
RMS_EPS = 1e-6
LANES = 128
VMEM_LIMIT = 56 * 1024 * 1024


def _cparams(n_axes):
    return pltpu.CompilerParams(
        dimension_semantics=("arbitrary",) * n_axes, vmem_limit_bytes=VMEM_LIMIT)


def _resident(shape):
    nd = len(shape)
    return pl.BlockSpec(shape, lambda *_: (0,) * nd, pipeline_mode=pl.Buffered(1))


def _rms(x, g):
    return x * lax.rsqrt(jnp.mean(x * x, axis=-1, keepdims=True) + RMS_EPS) * g


def _dot(a, b):
    return jnp.dot(a, b, preferred_element_type=F32)


def _dot_nt(a, b):
    return lax.dot_general(a, b, (((1,), (1,)), ((), ())), preferred_element_type=F32)


def _dot_tn(a, b):
    return lax.dot_general(a, b, (((0,), (0,)), ((), ())), preferred_element_type=F32)


def _row_tile(t, pref):
    tm = min(t, pref)
    assert t % tm == 0
    return tm


FFN_TF = 256


def _ffn_stream_body(*refs, has_pre):
    if has_pre:
        a_ref, wa_ref, x_ref, g_ref, wg_ref, wu_ref, wd_ref, o_ref, wgu_out, wd_out, xn_ref = refs
    else:
        x_ref, g_ref, wg_ref, wu_ref, wd_ref, o_ref, wgu_out, wd_out, xn_ref = refs

    @pl.when(pl.program_id(0) == 0)
    def _():
        y = x_ref[...] + _dot(a_ref[...], wa_ref[...]) if has_pre else x_ref[...]
        o_ref[...] = y
        xn_ref[...] = _rms(y, g_ref[...]).astype(BF16)

    wg, wu, wd = (r[...].astype(BF16) for r in (wg_ref, wu_ref, wd_ref))
    wgu_out[...] = jnp.concatenate([wg, wu], axis=1)
    wd_out[...] = wd
    xn = xn_ref[...]
    gate = _dot(xn, wg)
    up = _dot(xn, wu)
    act = (gate * jax.nn.sigmoid(gate) * up).astype(BF16)
    o_ref[...] = o_ref[...] + FFN_RESIDUAL * _dot(act, wd)


def _ffn_stream(x, g, wgu_f32, wd_f32, layer, pre=None):
    t, d = x.shape
    d_ff = wd_f32.shape[1]
    tf = FFN_TF
    n_chunks = d_ff // tf
    assert d_ff % tf == 0
    whole = lambda r, c: pl.BlockSpec((r, c), lambda i: (0, 0))
    pre_specs, pre_args = [], []
    if pre is not None:
        a, wa = pre
        pre_specs, pre_args = [whole(t, a.shape[1]), whole(*wa.shape)], [a, wa]
    return pl.pallas_call(
        functools.partial(_ffn_stream_body, has_pre=pre is not None),
        grid=(n_chunks,),
        in_specs=pre_specs + [
            whole(t, d), whole(1, d),
            pl.BlockSpec((None, d, tf), lambda c: (layer, 0, c)),
            pl.BlockSpec((None, d, tf), lambda c: (layer, 0, n_chunks + c)),
            pl.BlockSpec((None, tf, d), lambda c: (layer, c, 0))],
        out_specs=[whole(t, d), pl.BlockSpec((d, 2 * tf), lambda c: (0, c)),
                   pl.BlockSpec((tf, d), lambda c: (c, 0))],
        out_shape=[jax.ShapeDtypeStruct((t, d), F32), jax.ShapeDtypeStruct((d, 2 * d_ff), BF16),
                   jax.ShapeDtypeStruct((d_ff, d), BF16)],
        scratch_shapes=[pltpu.VMEM((t, d), BF16)],
        compiler_params=_cparams(1),
        name="ffn_stream",
    )(*pre_args, x, g.reshape(1, d), wgu_f32, wgu_f32, wd_f32)

def _interleave(main, side):
    done = 0
    for j, thunk in enumerate(main):
        thunk()
        upto = (j + 1) * len(side) // len(main)
        for other in side[done:upto]:
            other()
        done = upto


def _ffn_items(get_x, g_ref, wgu_ref, wd_ref, o_ref, act_ref, d_ff, tf):
    st = {}

    def head():
        y = get_x()
        o_ref[...] = y
        st["xn"] = _rms(y, g_ref[...]).astype(BF16)

    def chunk(lo):
        gate = _dot(st["xn"], wgu_ref[:, 2 * lo:2 * lo + tf])
        up = _dot(st["xn"], wgu_ref[:, 2 * lo + tf:2 * lo + 2 * tf])
        act_ref[:, lo:lo + tf] = (gate * jax.nn.sigmoid(gate) * up).astype(BF16)

    def tail():
        o_ref[...] = o_ref[...] + FFN_RESIDUAL * _dot(act_ref[...], wd_ref[...])

    return [head] + [functools.partial(chunk, lo) for lo in range(0, d_ff, tf)] + [tail]


def _ffn_body(*refs, d_ff, tf, has_pre):
    if has_pre:
        a_ref, wa_ref, x_ref, g_ref, wgu_ref, wd_ref, o_ref, act_ref = refs
        get_x = lambda: x_ref[...] + _dot(a_ref[...], wa_ref[...])
    else:
        x_ref, g_ref, wgu_ref, wd_ref, o_ref, act_ref = refs
        get_x = lambda: x_ref[...]
    for thunk in _ffn_items(get_x, g_ref, wgu_ref, wd_ref, o_ref, act_ref, d_ff, tf):
        thunk()


def _ffn(x, g, wgu, wd, pre=None):
    t, d = x.shape
    d_ff = wd.shape[0]
    tm = _row_tile(t, 1024)
    assert d_ff % FFN_TF == 0
    row = lambda n: pl.BlockSpec((tm, n), lambda i: (i, 0))
    pre_specs, pre_args = [], []
    if pre is not None:
        a, wa = pre
        pre_specs, pre_args = [row(a.shape[1]), _resident(wa.shape)], [a, wa]
    return pl.pallas_call(
        functools.partial(_ffn_body, d_ff=d_ff, tf=FFN_TF, has_pre=pre is not None),
        grid=(t // tm,),
        in_specs=pre_specs + [row(d), _resident((1, d)), _resident((d, 2 * d_ff)), _resident((d_ff, d))],
        out_specs=row(d),
        out_shape=jax.ShapeDtypeStruct((t, d), F32),
        scratch_shapes=[pltpu.VMEM((tm, d_ff), BF16)],
        compiler_params=_cparams(1),
        name="ffn",
    )(*pre_args, x, g.reshape(1, d), wgu, wd)


def _mlstm_proj_body(x_ref, g_ref, w_ref, wg_ref, bg_ref,
                     q_ref, k_ref, v_ref, og_ref, gcol_ref, grow_ref, *, qk, vd, dk, chunk):
    h_n = M_HEADS
    xn = _rms(x_ref[...], g_ref[...]).astype(BF16)
    tm = xn.shape[0]

    shift = chunk.bit_length() - 1
    r_i = lax.broadcasted_iota(jnp.int32, (LANES, LANES), 0)
    c_i = lax.broadcasted_iota(jnp.int32, (LANES, LANES), 1)
    same_chunk = lax.shift_right_logical(r_i, shift) == lax.shift_right_logical(c_i, shift)
    tri = jnp.where((r_i >= c_i) & same_chunk, 1.0, 0.0).astype(BF16)

    gates = _dot(xn, wg_ref[...]) + bg_ref[...]
    q_ref[...] = _dot(xn, w_ref[:, 0:qk]).astype(BF16)
    lane = lax.broadcasted_iota(jnp.int32, (tm, LANES), 1)
    log_f = jnp.where((lane >= h_n) & (lane < 2 * h_n), jax.nn.log_sigmoid(gates), 0.0)
    hi = log_f.astype(BF16)
    lo = (log_f - hi.astype(F32)).astype(BF16)
    k_ref[...] = (_dot(xn, w_ref[:, qk:2 * qk]) * (dk ** -0.5)).astype(BF16)
    hi_lo = jnp.concatenate([hi, lo], axis=1)
    csum = jnp.concatenate([_dot(tri, hi_lo[r:r + LANES]) for r in range(0, tm, LANES)], axis=0)
    gcol = jnp.where(lane < h_n, gates, csum[:, :LANES] + csum[:, LANES:])
    gcol_ref[...] = gcol
    v_ref[...] = _dot(xn, w_ref[:, 2 * qk:2 * qk + vd]).astype(BF16)
    g_t = gcol.T
    grow_ref[0] = jnp.concatenate([g_t[0:h_n] - g_t[h_n:2 * h_n], g_t[h_n:2 * h_n]], axis=0)
    og_ref[...] = jax.nn.sigmoid(_dot(xn, w_ref[:, 2 * qk + vd:2 * qk + 2 * vd])).astype(BF16)


def _mlstm_proj(x3, g, w_main, w_gates, b_gates, qk, vd, chunk):
    b, s, d = x3.shape
    t = b * s
    tm = _row_tile(s, 1024)
    nt = s // tm
    assert chunk & (chunk - 1) == 0 and LANES % chunk == 0 and tm % LANES == 0
    row = lambda n: pl.BlockSpec((tm, n), lambda i: (i, 0))
    return pl.pallas_call(
        functools.partial(_mlstm_proj_body, qk=qk, vd=vd, dk=qk // M_HEADS, chunk=chunk),
        grid=(t // tm,),
        in_specs=[row(d), _resident((1, d)), _resident(w_main.shape),
                  _resident(w_gates.shape), _resident((1, LANES))],
        out_specs=[row(qk), row(qk), row(vd), row(vd), row(LANES),
                   pl.BlockSpec((1, 2 * M_HEADS, tm), lambda i: (i // nt, 0, i % nt))],
        out_shape=[jax.ShapeDtypeStruct((t, qk), BF16), jax.ShapeDtypeStruct((t, qk), BF16),
                   jax.ShapeDtypeStruct((t, vd), BF16), jax.ShapeDtypeStruct((t, vd), BF16),
                   jax.ShapeDtypeStruct((t, LANES), F32),
                   jax.ShapeDtypeStruct((b, 2 * M_HEADS, s), F32)],
        compiler_params=_cparams(1),
        name="mlstm_proj",
    )(x3.reshape(t, d), g.reshape(1, d), w_main, w_gates, b_gates)


def _mlstm_chunk_items(units, load, state, emit, rows):
    r_i = lax.broadcasted_iota(jnp.int32, (rows, rows), 0)
    c_i = lax.broadcasted_iota(jnp.int32, (rows, rows), 1)
    causal = r_i >= c_i
    us = range(len(units))
    st = {}

    def products():
        st["x"] = x = [load(u) for u in units]
        st["old"] = [state[u] for u in units]
        st["qk"] = [_dot_nt(x[i]["q"], x[i]["k"]) for i in us]
        st["qc"] = [_dot(x[i]["q"], st["old"][i][0].astype(BF16)) for i in us]

    def weights():
        x, old = st["x"], st["old"]
        log_d = [jnp.where(causal, x[i]["b_c"] + x[i]["imb"], -jnp.inf) for i in us]
        log_inter = [x[i]["b_c"] + old[i][2] for i in us]
        st["m_t"] = m_t = [jnp.maximum(log_inter[i], jnp.max(log_d[i], axis=-1, keepdims=True)) for i in us]
        st["inter"] = [jnp.exp(log_inter[i] - m_t[i]) for i in us]
        st["s"] = [st["qk"][i] * jnp.exp(log_d[i] - m_t[i]) for i in us]

    def hidden():
        x, old, s, inter, m_t = st["x"], st["old"], st["s"], st["inter"], st["m_t"]
        s_v = [_dot(s[i].astype(BF16), x[i]["v"]) for i in us]
        den = [inter[i] * jnp.sum(x[i]["q"].astype(F32) * old[i][1], axis=-1, keepdims=True)
               + jnp.sum(s[i], axis=-1, keepdims=True) for i in us]
        st["hid"] = [(inter[i] * st["qc"][i] + s_v[i]) / jnp.maximum(jnp.abs(den[i]), jnp.exp(-m_t[i]))
                     for i in us]

    def update():
        x, old = st["x"], st["old"]
        b_last = [x[i]["b_c"][rows - 1:rows, :] for i in us]
        log_w = [b_last[i] - x[i]["b_c"] + x[i]["i_c"] for i in us]
        m_new = [jnp.maximum(b_last[i] + old[i][2], jnp.max(log_w[i], axis=0, keepdims=True)) for i in us]
        decay = [jnp.exp(b_last[i] + old[i][2] - m_new[i]) for i in us]
        kw = [jnp.exp(log_w[i] - m_new[i]) * x[i]["k"].astype(F32) for i in us]
        kw_v = [_dot_tn(kw[i].astype(BF16), x[i]["v"]) for i in us]
        for i, u in enumerate(units):
            state[u] = (decay[i] * old[i][0] + kw_v[i],
                        decay[i] * old[i][1] + jnp.sum(kw[i], axis=0, keepdims=True), m_new[i])

    def output():
        hid = [st["hid"][i] * lax.rsqrt(jnp.mean(st["hid"][i] * st["hid"][i], axis=-1, keepdims=True) + RMS_EPS)
               for i in us]
        for i, u in enumerate(units):
            emit(u, (hid[i] * st["x"][i]["ng"] * st["x"][i]["og"].astype(F32)).astype(BF16))

    return [products, weights, hidden, update, output]


def _mlstm_cell_body(q_ref, k_ref, v_ref, og_ref, gcol_ref, grow_ref, ng_ref, c0_ref, n0_ref, m0_ref,
                     hg_ref, c_ref, n_ref, m_ref, *, dk, dv, bt):
    h_n = M_HEADS
    rows = q_ref.shape[1]
    units = [(b, h) for b in range(bt) for h in range(h_n)]

    def load(u):
        b, h = u
        kc, vc = slice(h * dk, (h + 1) * dk), slice(h * dv, (h + 1) * dv)
        return dict(q=q_ref[b, :, kc], k=k_ref[b, :, kc], v=v_ref[b, :, vc], og=og_ref[b, :, vc],
                    ng=ng_ref[:, vc], b_c=gcol_ref[b, :, h_n + h:h_n + h + 1], i_c=gcol_ref[b, :, h:h + 1],
                    imb=grow_ref[b, h:h + 1, :])

    def emit(u, value):
        b, h = u
        hg_ref[b, :, h * dv:(h + 1) * dv] = value

    state = {(b, h): (c0_ref[b, h], n0_ref[b, h:h + 1, :], m0_ref[b, :, h:h + 1]) for b, h in units}
    for thunk in _mlstm_chunk_items(units, load, state, emit, rows):
        thunk()
    for b, h in units:
        c_ref[b, h], n_ref[b, h:h + 1, :], m_ref[b, :, h:h + 1] = state[(b, h)]


def _mlstm_cell(q, k, v, og, gcol, grow, norm_g, c0, n0, m0, bt):
    b, s, qk = q.shape
    vd = v.shape[-1]
    dk, dv = qk // M_HEADS, vd // M_HEADS
    assert b % bt == 0
    tok = lambda n: pl.BlockSpec((bt, s, n), lambda i: (i, 0, 0))
    st_c = pl.BlockSpec((bt, M_HEADS, dk, dv), lambda i: (i, 0, 0, 0))
    st_n = pl.BlockSpec((bt, M_HEADS, dk), lambda i: (i, 0, 0))
    st_m = pl.BlockSpec((bt, 1, M_HEADS), lambda i: (i, 0, 0))
    return pl.pallas_call(
        functools.partial(_mlstm_cell_body, dk=dk, dv=dv, bt=bt),
        grid=(b // bt,),
        in_specs=[tok(qk), tok(qk), tok(vd), tok(vd), tok(LANES),
                  pl.BlockSpec((bt, 2 * M_HEADS, s), lambda i: (i, 0, 0)),
                  pl.BlockSpec((1, vd), lambda i: (0, 0)), st_c, st_n, st_m],
        out_specs=[tok(vd), st_c, st_n, st_m],
        out_shape=[jax.ShapeDtypeStruct((b, s, vd), BF16),
                   jax.ShapeDtypeStruct((b, M_HEADS, dk, dv), F32),
                   jax.ShapeDtypeStruct((b, M_HEADS, dk), F32),
                   jax.ShapeDtypeStruct((b, 1, M_HEADS), F32)],
        compiler_params=_cparams(1),
        name="mlstm_cell",
    )(q, k, v, og, gcol, grow, norm_g.reshape(1, vd), c0, n0, m0)


def _mlstm_ffn_body(q_ref, k_ref, v_ref, og_ref, gcol_ref, grow_ref, ng_ref, wa_ref,
                    x_ref, g_ref, wgu_ref, wd_ref, o_ref, c_ref, n_ref, m_ref, act_ref, hg_ref,
                    *, d_ff, tf, dk, dv, chunk, tiles_per_seq):
    h_n = M_HEADS
    i = pl.program_id(0)
    last = pl.num_programs(0) - 1
    live = i < last
    first_of_seq = lax.rem(jnp.minimum(i, last - 1), tiles_per_seq) == 0

    @pl.when(first_of_seq & live)
    def _():
        c_ref[...] = jnp.zeros_like(c_ref)
        n_ref[...] = jnp.zeros_like(n_ref)
        m_ref[...] = jnp.zeros_like(m_ref)

    cur = lax.rem(i, 2)
    heads = list(range(h_n))

    def step(with_ffn):
        state = {h: (c_ref[0, h], n_ref[0, h:h + 1, :], m_ref[0, :, h:h + 1]) for h in heads}
        side = []
        for j in range(q_ref.shape[0] // chunk):
            rows = slice(j * chunk, (j + 1) * chunk)

            def load(h, rows=rows):
                kc, vc = slice(h * dk, (h + 1) * dk), slice(h * dv, (h + 1) * dv)
                return dict(q=q_ref[rows, kc], k=k_ref[rows, kc], v=v_ref[rows, vc], og=og_ref[rows, vc],
                            ng=ng_ref[:, vc], b_c=gcol_ref[rows, h_n + h:h_n + h + 1],
                            i_c=gcol_ref[rows, h:h + 1], imb=grow_ref[0, h:h + 1, rows])

            def emit(h, value, rows=rows):
                hg_ref[cur, rows, h * dv:(h + 1) * dv] = value

            side += _mlstm_chunk_items(heads, load, state, emit, chunk)
        if with_ffn:
            get_x = lambda: x_ref[...] + _dot(hg_ref[1 - cur], wa_ref[...])
            _interleave(_ffn_items(get_x, g_ref, wgu_ref, wd_ref, o_ref, act_ref, d_ff, tf), side)
        else:
            for thunk in side:
                thunk()

        @pl.when(live)
        def _():
            for h in heads:
                c_ref[0, h], n_ref[0, h:h + 1, :], m_ref[0, :, h:h + 1] = state[h]

    pl.when(i == 0)(lambda: step(False))
    pl.when(i > 0)(lambda: step(True))


def _mlstm_ffn(x, q, k, v, og, gcol, grow, norm_g, wa, g, wgu, wd, chunk):
    t, d = x.shape
    b, _, s = grow.shape
    d_ff = wd.shape[0]
    qk, vd = q.shape[1], v.shape[1]
    dk, dv = qk // M_HEADS, vd // M_HEADS
    tm = _row_tile(s, 512)
    nt, tps = t // tm, s // tm
    assert tm % chunk == 0
    mix_tile = lambda i: jnp.minimum(i, nt - 1)
    ffn_row = lambda n: pl.BlockSpec((tm, n), lambda i: (jnp.maximum(i - 1, 0), 0))
    mix_row = lambda n: pl.BlockSpec((tm, n), lambda i: (mix_tile(i), 0))
    seq = lambda *tail: pl.BlockSpec((1,) + tail, lambda i: (mix_tile(i) // tps,) + (0,) * len(tail))
    return pl.pallas_call(
        functools.partial(_mlstm_ffn_body, d_ff=d_ff, tf=FFN_TF, dk=dk, dv=dv, chunk=chunk, tiles_per_seq=tps),
        grid=(nt + 1,),
        in_specs=[mix_row(qk), mix_row(qk), mix_row(vd), mix_row(vd), mix_row(LANES),
                  pl.BlockSpec((1, 2 * M_HEADS, tm), lambda i: (mix_tile(i) // tps, 0, mix_tile(i) % tps)),
                  _resident((1, vd)), _resident(wa.shape),
                  ffn_row(d), _resident((1, d)), _resident((d, 2 * d_ff)), _resident((d_ff, d))],
        out_specs=[ffn_row(d), seq(M_HEADS, dk, dv), seq(M_HEADS, dk), seq(1, M_HEADS)],
        out_shape=[jax.ShapeDtypeStruct((t, d), F32),
                   jax.ShapeDtypeStruct((b, M_HEADS, dk, dv), F32),
                   jax.ShapeDtypeStruct((b, M_HEADS, dk), F32),
                   jax.ShapeDtypeStruct((b, 1, M_HEADS), F32)],
        scratch_shapes=[pltpu.VMEM((tm, d_ff), BF16), pltpu.VMEM((2, tm, vd), BF16)],
        compiler_params=_cparams(1),
        name="mlstm_ffn",
    )(q, k, v, og, gcol, grow, norm_g.reshape(1, vd), wa, x, g.reshape(1, d), wgu, wd)


def _mlstm_layer(x3, state, p):
    b, s, d = x3.shape
    qk, vd = p["mlstm_qk"], p["mlstm_vd"]
    proj = functools.partial(_mlstm_proj, g=p["mix_norm0"], w_main=p["mlstm_w_main"],
                             w_gates=p["mlstm_w_gates"], b_gates=p["mlstm_b_gates"], qk=qk, vd=vd)
    if state is None:
        chunk = min(s, 128)
        q, k, v, og, gcol, grow = proj(x3, chunk=chunk)
        return dict(kind="mlstm", args=(q, k, v, og, gcol, grow, p["mlstm_out_norm"], p["mlstm_w_out"]),
                    chunk=chunk), None
    c0, n0, m0 = state
    q, k, v, og, gcol, grow = proj(x3.reshape(1, b * s, d), chunk=s)
    grow = jnp.swapaxes(grow.reshape(2 * M_HEADS, b, s), 0, 1)
    sh = lambda a: a.reshape(b, s, a.shape[-1])
    hg, c1, n1, m1 = _mlstm_cell(sh(q), sh(k), sh(v), sh(og), sh(gcol), grow, p["mlstm_out_norm"],
                                 c0, n0, m0.reshape(b, 1, M_HEADS), math.gcd(b, 8))
    return (hg.reshape(b * s, vd), p["mlstm_w_out"]), (c1, n1, m1.reshape(b, M_HEADS))


def _attn_proj_body(x_ref, g_ref, w_ref, wvt_ref, qg_ref, kg_ref, q_ref, k_ref, vt_ref, *, dh, nq, nk):
    xn = _rms(x_ref[...], g_ref[...]).astype(BF16)
    pair = 2 * LANES
    r_i = lax.broadcasted_iota(jnp.int32, (pair, pair), 0)
    c_i = lax.broadcasted_iota(jnp.int32, (pair, pair), 1)
    shift = dh.bit_length() - 1
    same_head = jnp.where(lax.shift_right_logical(r_i, shift) == lax.shift_right_logical(c_i, shift),
                          1.0, 0.0).astype(BF16)

    def head_norm(y, gain):
        sq = y * y
        hi = sq.astype(BF16)
        lo = (sq - hi.astype(F32)).astype(BF16)
        ms = (_dot(hi, same_head) + _dot(lo, same_head)) * (1.0 / dh)
        return y * lax.rsqrt(ms + RMS_EPS) * gain

    qg = jnp.concatenate([qg_ref[...], qg_ref[...]], axis=1)
    kg = jnp.concatenate([kg_ref[...], kg_ref[...]], axis=1)
    q = _dot(xn, w_ref[:, 0:nq])
    for s in range(nq // pair):
        sl = slice(s * pair, (s + 1) * pair)
        q_ref[:, sl] = head_norm(q[:, sl], qg).astype(BF16)
    k = _dot(xn, w_ref[:, nq:nq + nk])
    for s in range(nk // pair):
        sl = slice(s * pair, (s + 1) * pair)
        k_ref[:, sl] = head_norm(k[:, sl], kg)
    vt_ref[0] = _dot_nt(wvt_ref[...], xn)


def _attn_proj(x3, g, w, wvt, qg, kg, dh, nq, nk):
    b, s, d = x3.shape
    t = b * s
    nv = wvt.shape[0]
    tm = _row_tile(s, 1024)
    nt = s // tm
    row = lambda n: pl.BlockSpec((tm, n), lambda i: (i, 0))
    return pl.pallas_call(
        functools.partial(_attn_proj_body, dh=dh, nq=nq, nk=nk),
        grid=(t // tm,),
        in_specs=[row(d), _resident((1, d)), _resident(w.shape), _resident(wvt.shape),
                  _resident((1, LANES)), _resident((1, LANES))],
        out_specs=[row(nq), row(nk), pl.BlockSpec((1, nv, tm), lambda i: (i // nt, 0, i % nt))],
        out_shape=[jax.ShapeDtypeStruct((t, nq), BF16), jax.ShapeDtypeStruct((t, nk), F32),
                   jax.ShapeDtypeStruct((b, nv, s), F32)],
        compiler_params=_cparams(1),
        name="attn_proj",
    )(x3.reshape(t, d), g.reshape(1, d), w, wvt, qg, kg)


def _bias_body(rel_ref, o_ref, *, tq, key_major):
    nk = 2 * WINDOW
    shape, q_ax = ((nk, tq), 1) if key_major else ((tq, nk), 0)
    qi = lax.broadcasted_iota(jnp.int32, shape, q_ax)
    ki = lax.broadcasted_iota(jnp.int32, shape, 1 - q_ax)
    dist = qi + WINDOW - ki
    max_exact = NUM_BUCKETS // 2
    d = jnp.maximum(dist, 0)
    log_ratio = (jnp.log(jnp.maximum(d, 1).astype(F32) / max_exact)
                 / math.log(MAX_DISTANCE / max_exact))
    large = jnp.minimum(max_exact + (log_ratio * (NUM_BUCKETS - max_exact)).astype(jnp.int32),
                        NUM_BUCKETS - 1)
    bucket = jnp.where(d < max_exact, d, large)
    in_window = (dist >= 0) & (dist < WINDOW)
    visible = (in_window & (ki >= WINDOW), in_window)
    for h in range(A_HEADS_Q):
        bias = jnp.zeros(shape, F32)
        for bkt in range(NUM_BUCKETS):
            bias = jnp.where(bucket == bkt, rel_ref[bkt, h], bias)
        for variant in range(2):
            tile = jnp.where(visible[variant], bias, -jnp.inf)
            if key_major:
                o_ref[variant, h // 2, :, (h % 2) * tq:(h % 2 + 1) * tq] = tile
            else:
                o_ref[variant, h] = tile


def _bias_table(rel_bias, tq, key_major):
    nk = 2 * WINDOW
    shape = (2, A_HEADS_Q // 2, nk, 2 * tq) if key_major else (2, A_HEADS_Q, tq, nk)
    return pl.pallas_call(
        functools.partial(_bias_body, tq=tq, key_major=key_major),
        grid=(1,),
        in_specs=[pl.BlockSpec(memory_space=pltpu.SMEM)],
        out_specs=pl.BlockSpec(shape, lambda i: (0, 0, 0, 0)),
        out_shape=jax.ShapeDtypeStruct(shape, F32),
        compiler_params=_cparams(1),
        name="rel_bias_table",
    )(rel_bias)


def _attn_tile_items(sink_ref, q_ref, kp_ref, ko_ref, vp_ref, vo_ref, bias_ref, first_of_seq, write, dh):
    blk = WINDOW
    n_blk = ko_ref.shape[0] // blk
    slots_per_kv = A_GROUP // 2
    n_slots = A_HEADS_Q // 2
    lane = lax.broadcasted_iota(jnp.int32, (blk, LANES), 1)
    lo_half = lane < dh
    left = lax.broadcasted_iota(jnp.int32, (1, 2 * blk), 1) < blk
    ones = jnp.ones((dh, 2 * blk), BF16)
    sinks = [jnp.where(left, sink_ref[2 * i], sink_ref[2 * i + 1]) for i in range(n_slots)]
    items = []
    for j in range(n_blk):
        st = {}
        own = slice(j * blk, (j + 1) * blk)
        before = slice((j - 1) * blk, j * blk)
        variant = jnp.where(first_of_seq, 0, 1) if j == 0 else 1

        def scores(st=st, j=j, own=own, before=before, variant=variant):
            st["vts"], st["sts"] = [], []
            for g in range(A_HEADS_KV):
                sl = slice(g * LANES, (g + 1) * LANES)
                rows = slice(g * dh, (g + 1) * dh)
                k_prev = kp_ref[:, sl] if j == 0 else ko_ref[before, sl]
                v_prev = vp_ref[0, rows, :] if j == 0 else vo_ref[0, rows, before]
                kk = jnp.concatenate([k_prev, ko_ref[own, sl]], axis=0).astype(BF16)
                vt = jnp.concatenate([v_prev, vo_ref[0, rows, own]], axis=1).astype(BF16)
                st["vts"].append(jnp.concatenate([vt, ones], axis=0))
                for pr in range(slots_per_kv):
                    slot_i = g * slots_per_kv + pr
                    slot = q_ref[own, slot_i * LANES:(slot_i + 1) * LANES]
                    zero = jnp.zeros_like(slot)
                    q2 = jnp.concatenate([jnp.where(lo_half, slot, zero), jnp.where(lo_half, zero, slot)],
                                         axis=0)
                    st["sts"].append(_dot_nt(kk, q2) + bias_ref[variant, slot_i])

        def softmax(st=st):
            st["ms"] = [jnp.maximum(jnp.max(st["sts"][i], axis=0, keepdims=True), sinks[i])
                        for i in range(n_slots)]
            st["pts"] = [jnp.exp(st["sts"][i] - st["ms"][i]).astype(BF16) for i in range(n_slots)]

        def values(st=st):
            st["oes"] = [_dot(st["vts"][i // slots_per_kv], st["pts"][i]) for i in range(n_slots)]

        def finish(st=st, j=j):
            heads_t = []
            for i in range(n_slots):
                oe = st["oes"][i]
                ot = oe[0:dh, :] / (oe[dh:dh + 1, :] + jnp.exp(sinks[i] - st["ms"][i]))
                heads_t += [ot[:, 0:blk], ot[:, blk:2 * blk]]
            write(j, jnp.concatenate(heads_t, axis=0).T.astype(BF16))

        items += [scores, softmax, values, finish]
    return items


def _attn_ffn_body(sink_ref, q_ref, kp_ref, ko_ref, vp_ref, vo_ref, bias_ref, wa_ref,
                   x_ref, g_ref, wgu_ref, wd_ref, o_ref, act_ref, att_ref, *, d_ff, tf, dh, tiles_per_seq):
    i = pl.program_id(0)
    cur = lax.rem(i, 2)
    first_of_seq = lax.rem(jnp.minimum(i, pl.num_programs(0) - 2), tiles_per_seq) == 0

    def write(j, tile):
        att_ref[cur, j * WINDOW:(j + 1) * WINDOW, :] = tile

    def side():
        return _attn_tile_items(sink_ref, q_ref, kp_ref, ko_ref, vp_ref, vo_ref, bias_ref,
                                first_of_seq, write, dh)

    @pl.when(i == 0)
    def _():
        for thunk in side():
            thunk()

    @pl.when(i > 0)
    def _():
        get_x = lambda: x_ref[...] + _dot(att_ref[1 - cur], wa_ref[...])
        _interleave(_ffn_items(get_x, g_ref, wgu_ref, wd_ref, o_ref, act_ref, d_ff, tf), side())


def _attn_ffn(x, q, k, vt, bias, sinks, wa, g, wgu, wd, dh):
    t, d = x.shape
    b, vd, s = vt.shape
    d_ff = wd.shape[0]
    qd, kd = q.shape[1], k.shape[1]
    blk = WINDOW
    tm = _row_tile(s, 512)
    nt, tps, bpt = t // tm, s // tm, tm // blk
    att_tile = lambda i: jnp.minimum(i, nt - 1)
    ffn_tile = lambda i: jnp.maximum(i - 1, 0)
    ffn_row = lambda n: pl.BlockSpec((tm, n), lambda i: (ffn_tile(i), 0))
    att_row = lambda n: pl.BlockSpec((tm, n), lambda i: (att_tile(i), 0))
    return pl.pallas_call(
        functools.partial(_attn_ffn_body, d_ff=d_ff, tf=FFN_TF, dh=dh, tiles_per_seq=tps),
        grid=(nt + 1,),
        in_specs=[pl.BlockSpec(memory_space=pltpu.SMEM),
                  att_row(qd),
                  pl.BlockSpec((blk, kd), lambda i: (jnp.maximum(att_tile(i) * bpt - 1, 0), 0)),
                  att_row(kd),
                  pl.BlockSpec((1, vd, blk), lambda i: (att_tile(i) // tps,
                                                        0, jnp.maximum(att_tile(i) % tps * bpt - 1, 0))),
                  pl.BlockSpec((1, vd, tm), lambda i: (att_tile(i) // tps, 0, att_tile(i) % tps)),
                  _resident(bias.shape), _resident(wa.shape),
                  ffn_row(d), _resident((1, d)), _resident((d, 2 * d_ff)), _resident((d_ff, d))],
        out_specs=ffn_row(d),
        out_shape=jax.ShapeDtypeStruct((t, d), F32),
        scratch_shapes=[pltpu.VMEM((tm, d_ff), BF16), pltpu.VMEM((2, tm, qd), BF16)],
        compiler_params=_cparams(1),
        name="attn_ffn",
    )(sinks, q, k, k, vt, vt, bias, wa, x, g.reshape(1, d), wgu, wd)


def _attn_sample_body(sink_ref, q_ref, kc_ref, kn_ref, vc_ref, vn_ref, bias_ref,
                      o_ref, ko_ref, vo_ref, *, bt, tq):
    dh = q_ref.shape[-1]
    fresh = lax.broadcasted_iota(jnp.int32, (dh, WINDOW), 1) >= WINDOW - tq
    sinks = [jnp.concatenate([jnp.full((tq, 1), sink_ref[g * A_GROUP + j], F32) for j in range(A_GROUP)],
                             axis=0) for g in range(A_HEADS_KV)]

    def slide(win_ref, new_ref, b, g):
        new = pltpu.roll(new_ref[g], (WINDOW - tq - b * tq) % WINDOW, axis=1)
        return jnp.where(fresh, new, pltpu.roll(win_ref[b, g], WINDOW - tq, axis=1))

    units = [(b, g) for b in range(bt) for g in range(A_HEADS_KV)]
    us = range(len(units))
    q = [q_ref[b, g] for b, g in units]
    k_win = [slide(kc_ref, kn_ref, b, g) for b, g in units]
    v_win = [slide(vc_ref, vn_ref, b, g) for b, g in units]
    for u, (b, g) in enumerate(units):
        ko_ref[b, g] = k_win[u]
        vo_ref[b, g] = v_win[u]
    s_old = [_dot(q[u], kc_ref[b, g].astype(BF16)) + bias_ref[0, g] for u, (b, g) in enumerate(units)]
    s_new = [_dot(q[u], k_win[u].astype(BF16)) + bias_ref[1, g] for u, (b, g) in enumerate(units)]
    m = [jnp.maximum(jnp.maximum(jnp.max(s_old[u], axis=-1, keepdims=True),
                                 jnp.max(s_new[u], axis=-1, keepdims=True)), sinks[g])
         for u, (b, g) in enumerate(units)]
    p_old = [jnp.exp(s_old[u] - m[u]) for u in us]
    p_new = [jnp.exp(s_new[u] - m[u]) for u in us]
    den = [jnp.sum(p_old[u], axis=-1, keepdims=True) + jnp.sum(p_new[u], axis=-1, keepdims=True)
           + jnp.exp(sinks[g] - m[u]) for u, (b, g) in enumerate(units)]
    out = [_dot_nt((p_old[u] / den[u]).astype(BF16), vc_ref[b, g].astype(BF16))
           + _dot_nt((p_new[u] / den[u]).astype(BF16), v_win[u].astype(BF16))
           for u, (b, g) in enumerate(units)]
    for u, (b, g) in enumerate(units):
        o_ref[b, g] = out[u].astype(BF16)


def _attn_sample(q, kn, vn, k_win, v_win, bias, sinks, tq):
    b, n_kv, rows, dh = q.shape
    assert LANES % tq == 0
    bt = LANES // tq
    assert b % bt == 0
    lead = lambda *tail: pl.BlockSpec((bt,) + tail, lambda i: (i, 0, 0, 0))
    new = pl.BlockSpec((n_kv, dh, LANES), lambda i: (0, 0, i))
    return pl.pallas_call(
        functools.partial(_attn_sample_body, bt=bt, tq=tq),
        grid=(b // bt,),
        in_specs=[pl.BlockSpec(memory_space=pltpu.SMEM),
                  lead(n_kv, rows, dh), lead(n_kv, dh, WINDOW), new, lead(n_kv, dh, WINDOW), new,
                  pl.BlockSpec(bias.shape, lambda i: (0, 0, 0, 0))],
        out_specs=[lead(n_kv, rows, dh), lead(n_kv, dh, WINDOW), lead(n_kv, dh, WINDOW)],
        out_shape=[jax.ShapeDtypeStruct(q.shape, BF16),
                   jax.ShapeDtypeStruct(k_win.shape, F32),
                   jax.ShapeDtypeStruct(v_win.shape, F32)],
        compiler_params=_cparams(1),
        name="attn_sample",
    )(sinks, q, k_win, kn, v_win, vn, bias)


def _swa_layer(x3, buffers, p):
    b, s, d = x3.shape
    dh = p["attn_dh"]
    kd = A_HEADS_KV * dh
    x = x3.reshape(b * s, d)
    kv_shape = (b, WINDOW, A_HEADS_KV, dh)
    if buffers is None:
        nq, nk = A_HEADS_Q * dh, A_HEADS_KV * LANES
        q, k, vt = _attn_proj(x3, p["mix_norm1"], p["attn_w_prompt"], p["attn_wvt"],
                              p["attn_qg"], p["attn_kg"], dh, nq, nk)
        bias = _bias_table(p["rel_bias"], WINDOW, key_major=True)
        new_k = k.reshape(b, s, nk)[:, s - WINDOW:].reshape(b, WINDOW, A_HEADS_KV, LANES)[..., :dh]
        new_v = jnp.swapaxes(vt[:, :, s - WINDOW:], 1, 2)
        mixed = dict(kind="attn", args=(q, k, vt, bias, p["attn_sinks"], p["attn_w_out"]), dh=dh)
    else:
        k_buf, v_buf = buffers
        nq, nk = A_HEADS_Q * dh, kd
        q, k, vt = _attn_proj(x.reshape(1, b * s, d), p["mix_norm1"], p["attn_w_sample"], p["attn_wvt"],
                              p["attn_qg"], p["attn_kg"], dh, nq, nk)
        rows = A_GROUP * s
        q = jnp.transpose(q.reshape(b, s, A_HEADS_KV, A_GROUP, dh), (0, 2, 3, 1, 4))
        table = _bias_table(p["rel_bias"], s, key_major=False)[1].reshape(A_HEADS_KV, rows, 2 * WINDOW)
        own = jnp.pad(table[..., WINDOW:WINDOW + s], ((0, 0), (0, 0), (WINDOW - s, 0)),
                      constant_values=-jnp.inf)
        bias = jnp.stack([table[..., :WINDOW], own])
        to_lanes = lambda a: jnp.transpose(a, (0, 2, 3, 1))
        att, new_k, new_v = _attn_sample(q.reshape(b, A_HEADS_KV, rows, dh),
                                         k.T.reshape(A_HEADS_KV, dh, b * s), vt[0].reshape(A_HEADS_KV, dh, b * s),
                                         to_lanes(k_buf), to_lanes(v_buf), bias, p["attn_sinks"], s)
        new_k, new_v = (jnp.transpose(a, (0, 3, 1, 2)) for a in (new_k, new_v))
        att = jnp.transpose(att.reshape(b, A_HEADS_KV, A_GROUP, s, dh), (0, 3, 1, 2, 4))
        mixed = (att.reshape(b * s, nq), p["attn_w_out"])
    return mixed, (new_k.reshape(kv_shape), new_v.reshape(kv_shape))


def _attn_weights(w_in, w_out, q_norm, k_norm, dh):
    d = w_in.shape[0]
    qd, kd = A_HEADS_Q * dh, A_HEADS_KV * dh
    wk = w_in[:, qd:qd + kd].reshape(d, A_HEADS_KV, dh)
    k_dup = jnp.concatenate([wk, wk], axis=-1).reshape(d, A_HEADS_KV * LANES)
    w_prompt = jnp.concatenate([w_in[:, :qd], k_dup], axis=1).astype(BF16)
    qg = jnp.concatenate([q_norm, q_norm]).reshape(1, LANES) * (dh ** -0.5)
    kg = jnp.concatenate([k_norm, k_norm]).reshape(1, LANES)
    return {"attn_w_prompt": w_prompt, "attn_w_sample": w_in[:, :qd + kd].astype(BF16),
            "attn_wvt": w_in[:, qd + kd:].T.astype(BF16),
            "attn_w_out": w_out.astype(BF16),
            "attn_qg": qg, "attn_kg": kg}


def _trunk(x3, mlstm_state, swa_buffers, p, ffn_bf16):
    b, s, d = x3.shape
    stream = mlstm_state is not None

    def ffn(x, name, layer, mixed=None):
        norm = p[name + "_norm"][layer]
        if stream:
            y, *ffn_bf16[name, layer] = _ffn_stream(x, norm, *p[name + "_f32"], layer, pre=mixed)
            return y
        w = (norm, *ffn_bf16[name, layer])
        if not isinstance(mixed, dict):
            return _ffn(x, *w, pre=mixed)
        if mixed["kind"] == "attn":
            return _attn_ffn(x, *mixed["args"], *w, mixed["dh"])
        return _mlstm_ffn(x, *mixed["args"], *w, mixed["chunk"])

    x = ffn(x3.reshape(b * s, d), "ffn1", 0)
    mixed, new_mlstm = _mlstm_layer(x.reshape(b, s, d), mlstm_state, p)
    x = ffn(x, "ffn2", 0, mixed)
    if new_mlstm is None:
        x, c1, n1, m1 = x
        new_mlstm = (c1, n1, m1.reshape(b, M_HEADS))
    x = ffn(x, "ffn1", 1)
    mixed, new_swa = _swa_layer(x.reshape(b, s, d), swa_buffers, p)
    x = ffn(x, "ffn2", 1, mixed)
    return x.reshape(b, s, d), new_mlstm, new_swa


def kernel(x_prompt, x_sample, state_mlstm_C, state_mlstm_n, state_mlstm_m, cache_swa_k, cache_swa_v,
           ffn1_norm, ffn1_w_gate_up, ffn1_w_down, mix_norm, ffn2_norm, ffn2_w_gate_up, ffn2_w_down,
           mlstm_w_in, mlstm_b_gates, mlstm_out_norm, mlstm_w_out,
           attn_w_in, attn_q_norm, attn_k_norm, attn_sinks, rel_bias, attn_w_out):
    d = x_prompt.shape[-1]
    vd = mlstm_w_out.shape[0]
    n_gates = 2 * M_HEADS
    qk = (mlstm_w_in.shape[1] - 2 * vd - n_gates) // 2
    dh = attn_q_norm.shape[0]
    assert 2 * dh == LANES
    w_gates = jnp.zeros((d, LANES), BF16).at[:, :n_gates].set(mlstm_w_in[:, 2 * qk + 2 * vd:].astype(BF16))
    b_gates = jnp.zeros((1, LANES), F32).at[0, :n_gates].set(mlstm_b_gates)
    p = {
        "ffn1_norm": ffn1_norm, "ffn2_norm": ffn2_norm,
        "ffn1_f32": (ffn1_w_gate_up, ffn1_w_down), "ffn2_f32": (ffn2_w_gate_up, ffn2_w_down),
        "mix_norm0": mix_norm[0], "mix_norm1": mix_norm[1],
        "mlstm_qk": qk, "mlstm_vd": vd,
        "mlstm_w_main": mlstm_w_in[:, :2 * qk + 2 * vd].astype(BF16),
        "mlstm_w_gates": w_gates, "mlstm_b_gates": b_gates,
        "mlstm_out_norm": mlstm_out_norm, "mlstm_w_out": mlstm_w_out.astype(BF16),
        "attn_dh": dh, "attn_sinks": attn_sinks, "rel_bias": rel_bias,
    }
    p.update(_attn_weights(attn_w_in, attn_w_out, attn_q_norm, attn_k_norm, dh))
    ffn_bf16 = {}
    y_s, (c_s, n_s, m_s), (k_s, v_s) = _trunk(
        x_sample, (state_mlstm_C, state_mlstm_n, state_mlstm_m), (cache_swa_k, cache_swa_v), p, ffn_bf16)
    y_p, (c_p, n_p, m_p), (k_p, v_p) = _trunk(x_prompt, None, None, p, ffn_bf16)
    return (y_p, y_s, c_p, n_p, m_p, k_p, v_p, c_s, n_s, m_s, k_s, v_s)
```

```python
import functools
import math

import jax
import jax.numpy as jnp
from jax import lax
from jax.experimental import pallas as pl
from jax.experimental.pallas import tpu as pltpu

F32 = jnp.float32
BF16 = jnp.bfloat16

M_HEADS = 4
A_HEADS_Q = 16
A_HEADS_KV = 4
A_GROUP = A_HEADS_Q // A_HEADS_KV
WINDOW = 128
NUM_BUCKETS = 32
MAX_DISTANCE = 128
FFN_RESIDUAL = 0.5
RMS_EPS = 1e-6
LANES = 128
VMEM_LIMIT = 56 * 1024 * 1024
ROWS_SINGLE = 1024
ROWS_FUSED = 512
CELL_BATCH = 16


def _cparams(n_axes):
    return pltpu.CompilerParams(
        dimension_semantics=("arbitrary",) * n_axes, vmem_limit_bytes=VMEM_LIMIT)


def _resident(shape):
    nd = len(shape)
    return pl.BlockSpec(shape, lambda *_: (0,) * nd, pipeline_mode=pl.Buffered(1))


def _rms(x, g):
    return x * lax.rsqrt(jnp.mean(x * x, axis=-1, keepdims=True) + RMS_EPS) * g


def _dot(a, b):
    return jnp.dot(a, b, preferred_element_type=F32)


def _dot_nt(a, b):
    return lax.dot_general(a, b, (((1,), (1,)), ((), ())), preferred_element_type=F32)


def _dot_tn(a, b):
    return lax.dot_general(a, b, (((0,), (0,)), ((), ())), preferred_element_type=F32)


def _row_tile(t, pref):
    tm = min(t, pref)
    assert t % tm == 0
    return tm


FFN_TF = 256


def _ffn_stream_body(*refs, has_pre):
    if has_pre:
        a_ref, wa_ref, x_ref, g_ref, wg_ref, wu_ref, wd_ref, o_ref, wgu_out, wd_out, xn_ref = refs
    else:
        x_ref, g_ref, wg_ref, wu_ref, wd_ref, o_ref, wgu_out, wd_out, xn_ref = refs

    @pl.when(pl.program_id(0) == 0)
    def _():
        y = x_ref[...] + _dot(a_ref[...], wa_ref[...]) if has_pre else x_ref[...]
        o_ref[...] = y
        xn_ref[...] = _rms(y, g_ref[...]).astype(BF16)

    wg, wu, wd = (r[...].astype(BF16) for r in (wg_ref, wu_ref, wd_ref))
    wgu_out[...] = jnp.concatenate([wg, wu], axis=1)
    wd_out[...] = wd
    xn = xn_ref[...]
    gate = _dot(xn, wg)
    up = _dot(xn, wu)
    act = (gate * jax.nn.sigmoid(gate) * up).astype(BF16)
    o_ref[...] = o_ref[...] + FFN_RESIDUAL * _dot(act, wd)


def _ffn_stream(x, g, wgu_f32, wd_f32, layer, pre=None):
    t, d = x.shape
    d_ff = wd_f32.shape[1]
    tf = FFN_TF
    n_chunks = d_ff // tf
    assert d_ff % tf == 0
    whole = lambda r, c: pl.BlockSpec((r, c), lambda i: (0, 0))
    pre_specs, pre_args = [], []
    if pre is not None:
        a, wa = pre
        pre_specs, pre_args = [whole(t, a.shape[1]), whole(*wa.shape)], [a, wa]
    return pl.pallas_call(
        functools.partial(_ffn_stream_body, has_pre=pre is not None),
        grid=(n_chunks,),
        in_specs=pre_specs + [
            whole(t, d), whole(1, d),
            pl.BlockSpec((None, d, tf), lambda c: (layer, 0, c)),
            pl.BlockSpec((None, d, tf), lambda c: (layer, 0, n_chunks + c)),
            pl.BlockSpec((None, tf, d), lambda c: (layer, c, 0))],
        out_specs=[whole(t, d), pl.BlockSpec((d, 2 * tf), lambda c: (0, c)),
                   pl.BlockSpec((tf, d), lambda c: (c, 0))],
        out_shape=[jax.ShapeDtypeStruct((t, d), F32), jax.ShapeDtypeStruct((d, 2 * d_ff), BF16),
                   jax.ShapeDtypeStruct((d_ff, d), BF16)],
        scratch_shapes=[pltpu.VMEM((t, d), BF16)],
        compiler_params=_cparams(1),
        name="ffn_stream",
    )(*pre_args, x, g.reshape(1, d), wgu_f32, wgu_f32, wd_f32)

def _interleave(main, side):
    done = 0
    for j, thunk in enumerate(main):
        thunk()
        upto = (j + 1) * len(side) // len(main)
        for other in side[done:upto]:
            other()
        done = upto


def _ffn_items(get_x, g_ref, wgu_ref, wd_ref, o_ref, act_ref, d_ff, tf):
    st = {}

    def head():
        y = get_x()
        o_ref[...] = y
        st["xn"] = _rms(y, g_ref[...]).astype(BF16)

    def chunk(lo):
        gate = _dot(st["xn"], wgu_ref[:, 2 * lo:2 * lo + tf])
        up = _dot(st["xn"], wgu_ref[:, 2 * lo + tf:2 * lo + 2 * tf])
        act_ref[:, lo:lo + tf] = (gate * jax.nn.sigmoid(gate) * up).astype(BF16)

    def tail():
        o_ref[...] = o_ref[...] + FFN_RESIDUAL * _dot(act_ref[...], wd_ref[...])

    return [head] + [functools.partial(chunk, lo) for lo in range(0, d_ff, tf)] + [tail]


def _ffn_body(x_ref, g_ref, wgu_ref, wd_ref, o_ref, act_ref, *, d_ff, tf):
    for thunk in _ffn_items(lambda: x_ref[...], g_ref, wgu_ref, wd_ref, o_ref, act_ref, d_ff, tf):
        thunk()


def _ffn(x, g, wgu, wd):
    t, d = x.shape
    d_ff = wd.shape[0]
    tm = _row_tile(t, ROWS_SINGLE)
    assert d_ff % FFN_TF == 0
    row = lambda n: pl.BlockSpec((tm, n), lambda i: (i, 0))
    return pl.pallas_call(
        functools.partial(_ffn_body, d_ff=d_ff, tf=FFN_TF),
        grid=(t // tm,),
        in_specs=[row(d), _resident((1, d)), _resident((d, 2 * d_ff)), _resident((d_ff, d))],
        out_specs=row(d),
        out_shape=jax.ShapeDtypeStruct((t, d), F32),
        scratch_shapes=[pltpu.VMEM((tm, d_ff), BF16)],
        compiler_params=_cparams(1),
        name="ffn",
    )(x, g.reshape(1, d), wgu, wd)


def _mlstm_proj_body(x_ref, g_ref, w_ref, wg_ref, bg_ref,
                     q_ref, k_ref, v_ref, og_ref, gcol_ref, grow_ref, *, qk, vd, dk, chunk):
    h_n = M_HEADS
    xn = _rms(x_ref[...], g_ref[...]).astype(BF16)
    tm = xn.shape[0]

    shift = chunk.bit_length() - 1
    r_i = lax.broadcasted_iota(jnp.int32, (LANES, LANES), 0)
    c_i = lax.broadcasted_iota(jnp.int32, (LANES, LANES), 1)
    same_chunk = lax.shift_right_logical(r_i, shift) == lax.shift_right_logical(c_i, shift)
    tri = jnp.where((r_i >= c_i) & same_chunk, 1.0, 0.0).astype(BF16)

    gates = _dot(xn, wg_ref[...]) + bg_ref[...]
    q_ref[...] = _dot(xn, w_ref[:, 0:qk]).astype(BF16)
    lane = lax.broadcasted_iota(jnp.int32, (tm, LANES), 1)
    log_f = jnp.where((lane >= h_n) & (lane < 2 * h_n), jax.nn.log_sigmoid(gates), 0.0)
    hi = log_f.astype(BF16)
    lo = (log_f - hi.astype(F32)).astype(BF16)
    k_ref[...] = (_dot(xn, w_ref[:, qk:2 * qk]) * (dk ** -0.5)).astype(BF16)
    hi_lo = jnp.concatenate([hi, lo], axis=1)
    csum = jnp.concatenate([_dot(tri, hi_lo[r:r + LANES]) for r in range(0, tm, LANES)], axis=0)
    gcol = jnp.where(lane < h_n, gates, csum[:, :LANES] + csum[:, LANES:])
    gcol_ref[...] = gcol
    v_ref[...] = _dot(xn, w_ref[:, 2 * qk:2 * qk + vd]).astype(BF16)
    g_t = gcol.T
    grow_ref[0] = jnp.concatenate([g_t[0:h_n] - g_t[h_n:2 * h_n], g_t[h_n:2 * h_n]], axis=0)
    og_ref[...] = jax.nn.sigmoid(_dot(xn, w_ref[:, 2 * qk + vd:2 * qk + 2 * vd])).astype(BF16)


def _mlstm_proj(x3, g, w_main, w_gates, b_gates, qk, vd, chunk):
    b, s, d = x3.shape
    t = b * s
    tm = _row_tile(s, ROWS_SINGLE)
    nt = s // tm
    assert chunk & (chunk - 1) == 0 and LANES % chunk == 0 and tm % LANES == 0
    row = lambda n: pl.BlockSpec((tm, n), lambda i: (i, 0))
    return pl.pallas_call(
        functools.partial(_mlstm_proj_body, qk=qk, vd=vd, dk=qk // M_HEADS, chunk=chunk),
        grid=(t // tm,),
        in_specs=[row(d), _resident((1, d)), _resident(w_main.shape),
                  _resident(w_gates.shape), _resident((1, LANES))],
        out_specs=[row(qk), row(qk), row(vd), row(vd), row(LANES),
                   pl.BlockSpec((1, 2 * M_HEADS, tm), lambda i: (i // nt, 0, i % nt))],
        out_shape=[jax.ShapeDtypeStruct((t, qk), BF16), jax.ShapeDtypeStruct((t, qk), BF16),
                   jax.ShapeDtypeStruct((t, vd), BF16), jax.ShapeDtypeStruct((t, vd), BF16),
                   jax.ShapeDtypeStruct((t, LANES), F32),
                   jax.ShapeDtypeStruct((b, 2 * M_HEADS, s), F32)],
        compiler_params=_cparams(1),
        name="mlstm_proj",
    )(x3.reshape(t, d), g.reshape(1, d), w_main, w_gates, b_gates)


def _mlstm_chunk_items(units, load, state, emit, rows):
    r_i = lax.broadcasted_iota(jnp.int32, (rows, rows), 0)
    c_i = lax.broadcasted_iota(jnp.int32, (rows, rows), 1)
    causal = r_i >= c_i
    us = range(len(units))
    st = {}

    def products():
        st["x"] = x = [load(u) for u in units]
        st["old"] = [state[u] for u in units]
        st["qk"] = [_dot_nt(x[i]["q"], x[i]["k"]) for i in us]
        st["qc"] = [_dot(x[i]["q"], st["old"][i][0].astype(BF16)) for i in us]

    def weights():
        x, old = st["x"], st["old"]
        log_d = [jnp.where(causal, x[i]["b_c"] + x[i]["imb"], -jnp.inf) for i in us]
        log_inter = [x[i]["b_c"] + old[i][2] for i in us]
        st["m_t"] = m_t = [jnp.maximum(log_inter[i], jnp.max(log_d[i], axis=-1, keepdims=True)) for i in us]
        st["inter"] = [jnp.exp(log_inter[i] - m_t[i]) for i in us]
        st["s"] = [st["qk"][i] * jnp.exp(log_d[i] - m_t[i]) for i in us]

    def hidden():
        x, old, s, inter, m_t = st["x"], st["old"], st["s"], st["inter"], st["m_t"]
        s_v = [_dot(s[i].astype(BF16), x[i]["v"]) for i in us]
        den = [inter[i] * jnp.sum(x[i]["q"].astype(F32) * old[i][1], axis=-1, keepdims=True)
               + jnp.sum(s[i], axis=-1, keepdims=True) for i in us]
        st["hid"] = [(inter[i] * st["qc"][i] + s_v[i]) / jnp.maximum(jnp.abs(den[i]), jnp.exp(-m_t[i]))
                     for i in us]

    def update():
        x, old = st["x"], st["old"]
        b_last = [x[i]["b_c"][rows - 1:rows, :] for i in us]
        log_w = [b_last[i] - x[i]["b_c"] + x[i]["i_c"] for i in us]
        m_new = [jnp.maximum(b_last[i] + old[i][2], jnp.max(log_w[i], axis=0, keepdims=True)) for i in us]
        decay = [jnp.exp(b_last[i] + old[i][2] - m_new[i]) for i in us]
        kw = [jnp.exp(log_w[i] - m_new[i]) * x[i]["k"].astype(F32) for i in us]
        kw_v = [_dot_tn(kw[i].astype(BF16), x[i]["v"]) for i in us]
        for i, u in enumerate(units):
            state[u] = (decay[i] * old[i][0] + kw_v[i],
                        decay[i] * old[i][1] + jnp.sum(kw[i], axis=0, keepdims=True), m_new[i])

    def output():
        hid = [st["hid"][i] * lax.rsqrt(jnp.mean(st["hid"][i] * st["hid"][i], axis=-1, keepdims=True) + RMS_EPS)
               for i in us]
        for i, u in enumerate(units):
            emit(u, (hid[i] * st["x"][i]["ng"] * st["x"][i]["og"].astype(F32)).astype(BF16))

    return [products, weights, hidden, update, output]


def _mlstm_cell_body(q_ref, k_ref, v_ref, og_ref, gcol_ref, grow_ref, ng_ref, c0_ref, n0_ref, m0_ref,
                     hg_ref, c_ref, n_ref, m_ref, *, dk, dv, bt):
    h_n = M_HEADS
    rows = q_ref.shape[1]
    units = [(b, h) for b in range(bt) for h in range(h_n)]

    def load(u):
        b, h = u
        kc, vc = slice(h * dk, (h + 1) * dk), slice(h * dv, (h + 1) * dv)
        return dict(q=q_ref[b, :, kc], k=k_ref[b, :, kc], v=v_ref[b, :, vc], og=og_ref[b, :, vc],
                    ng=ng_ref[:, vc], b_c=gcol_ref[b, :, h_n + h:h_n + h + 1], i_c=gcol_ref[b, :, h:h + 1],
                    imb=grow_ref[b, h:h + 1, :])

    def emit(u, value):
        b, h = u
        hg_ref[b, :, h * dv:(h + 1) * dv] = value

    state = {(b, h): (c0_ref[b, h], n0_ref[b, h:h + 1, :], m0_ref[b, :, h:h + 1]) for b, h in units}
    for thunk in _mlstm_chunk_items(units, load, state, emit, rows):
        thunk()
    for b, h in units:
        c_ref[b, h], n_ref[b, h:h + 1, :], m_ref[b, :, h:h + 1] = state[(b, h)]


def _mlstm_cell(q, k, v, og, gcol, grow, norm_g, c0, n0, m0, bt):
    b, s, qk = q.shape
    vd = v.shape[-1]
    dk, dv = qk // M_HEADS, vd // M_HEADS
    assert b % bt == 0
    tok = lambda n: pl.BlockSpec((bt, s, n), lambda i: (i, 0, 0))
    st_c = pl.BlockSpec((bt, M_HEADS, dk, dv), lambda i: (i, 0, 0, 0))
    st_n = pl.BlockSpec((bt, M_HEADS, dk), lambda i: (i, 0, 0))
    st_m = pl.BlockSpec((bt, 1, M_HEADS), lambda i: (i, 0, 0))
    return pl.pallas_call(
        functools.partial(_mlstm_cell_body, dk=dk, dv=dv, bt=bt),
        grid=(b // bt,),
        in_specs=[tok(qk), tok(qk), tok(vd), tok(vd), tok(LANES),
                  pl.BlockSpec((bt, 2 * M_HEADS, s), lambda i: (i, 0, 0)),
                  pl.BlockSpec((1, vd), lambda i: (0, 0)), st_c, st_n, st_m],
        out_specs=[tok(vd), st_c, st_n, st_m],
        out_shape=[jax.ShapeDtypeStruct((b, s, vd), BF16),
                   jax.ShapeDtypeStruct((b, M_HEADS, dk, dv), F32),
                   jax.ShapeDtypeStruct((b, M_HEADS, dk), F32),
                   jax.ShapeDtypeStruct((b, 1, M_HEADS), F32)],
        compiler_params=_cparams(1),
        name="mlstm_cell",
    )(q, k, v, og, gcol, grow, norm_g.reshape(1, vd), c0, n0, m0)


def _mlstm_ffn_body(q_ref, k_ref, v_ref, og_ref, gcol_ref, grow_ref, ng_ref, wa_ref,
                    x_ref, g_ref, wgu_ref, wd_ref, o_ref, c_ref, n_ref, m_ref, act_ref, hg_ref,
                    *, d_ff, tf, dk, dv, chunk, tiles_per_seq):
    h_n = M_HEADS
    i = pl.program_id(0)
    last = pl.num_programs(0) - 1
    live = i < last
    first_of_seq = lax.rem(jnp.minimum(i, last - 1), tiles_per_seq) == 0

    @pl.when(i == 0)
    def _():
        hg_ref[...] = jnp.zeros_like(hg_ref)

    @pl.when(first_of_seq & live)
    def _():
        c_ref[...] = jnp.zeros_like(c_ref)
        n_ref[...] = jnp.zeros_like(n_ref)
        m_ref[...] = jnp.zeros_like(m_ref)

    cur = lax.rem(i, 2)
    heads = list(range(h_n))
    state = {h: (c_ref[0, h], n_ref[0, h:h + 1, :], m_ref[0, :, h:h + 1]) for h in heads}
    side = []
    for j in range(q_ref.shape[0] // chunk):
        rows = slice(j * chunk, (j + 1) * chunk)

        def load(h, rows=rows):
            kc, vc = slice(h * dk, (h + 1) * dk), slice(h * dv, (h + 1) * dv)
            return dict(q=q_ref[rows, kc], k=k_ref[rows, kc], v=v_ref[rows, vc], og=og_ref[rows, vc],
                        ng=ng_ref[:, vc], b_c=gcol_ref[rows, h_n + h:h_n + h + 1], i_c=gcol_ref[rows, h:h + 1],
                        imb=grow_ref[0, h:h + 1, rows])

        def emit(h, value, rows=rows):
            hg_ref[cur, rows, h * dv:(h + 1) * dv] = value

        side += _mlstm_chunk_items(heads, load, state, emit, chunk)
    get_x = lambda: x_ref[...] + _dot(hg_ref[1 - cur], wa_ref[...])
    _interleave(_ffn_items(get_x, g_ref, wgu_ref, wd_ref, o_ref, act_ref, d_ff, tf), side)

    @pl.when(live)
    def _():
        for h in heads:
            c_ref[0, h], n_ref[0, h:h + 1, :], m_ref[0, :, h:h + 1] = state[h]


def _mlstm_ffn(x, q, k, v, og, gcol, grow, norm_g, wa, g, wgu, wd, chunk):
    t, d = x.shape
    b, _, s = grow.shape
    d_ff = wd.shape[0]
    qk, vd = q.shape[1], v.shape[1]
    dk, dv = qk // M_HEADS, vd // M_HEADS
    tm = _row_tile(s, ROWS_FUSED)
    nt, tps = t // tm, s // tm
    assert tm % chunk == 0
    mix_tile = lambda i: jnp.minimum(i, nt - 1)
    ffn_row = lambda n: pl.BlockSpec((tm, n), lambda i: (jnp.maximum(i - 1, 0), 0))
    mix_row = lambda n: pl.BlockSpec((tm, n), lambda i: (mix_tile(i), 0))
    seq = lambda *tail: pl.BlockSpec((1,) + tail, lambda i: (mix_tile(i) // tps,) + (0,) * len(tail))
    return pl.pallas_call(
        functools.partial(_mlstm_ffn_body, d_ff=d_ff, tf=FFN_TF, dk=dk, dv=dv, chunk=chunk, tiles_per_seq=tps),
        grid=(nt + 1,),
        in_specs=[mix_row(qk), mix_row(qk), mix_row(vd), mix_row(vd), mix_row(LANES),
                  pl.BlockSpec((1, 2 * M_HEADS, tm), lambda i: (mix_tile(i) // tps, 0, mix_tile(i) % tps)),
                  _resident((1, vd)), _resident(wa.shape),
                  ffn_row(d), _resident((1, d)), _resident((d, 2 * d_ff)), _resident((d_ff, d))],
        out_specs=[ffn_row(d), seq(M_HEADS, dk, dv), seq(M_HEADS, dk), seq(1, M_HEADS)],
        out_shape=[jax.ShapeDtypeStruct((t, d), F32),
                   jax.ShapeDtypeStruct((b, M_HEADS, dk, dv), F32),
                   jax.ShapeDtypeStruct((b, M_HEADS, dk), F32),
                   jax.ShapeDtypeStruct((b, 1, M_HEADS), F32)],
        scratch_shapes=[pltpu.VMEM((tm, d_ff), BF16), pltpu.VMEM((2, tm, vd), BF16)],
        compiler_params=_cparams(1),
        name="mlstm_ffn",
    )(q, k, v, og, gcol, grow, norm_g.reshape(1, vd), wa, x, g.reshape(1, d), wgu, wd)


def _mlstm_layer(x3, state, p):
    b, s, d = x3.shape
    qk, vd = p["mlstm_qk"], p["mlstm_vd"]
    proj = functools.partial(_mlstm_proj, g=p["mix_norm0"], w_main=p["mlstm_w_main"],
                             w_gates=p["mlstm_w_gates"], b_gates=p["mlstm_b_gates"], qk=qk, vd=vd)
    if state is None:
        chunk = min(s, 128)
        q, k, v, og, gcol, grow = proj(x3, chunk=chunk)
        return dict(kind="mlstm", args=(q, k, v, og, gcol, grow, p["mlstm_out_norm"], p["mlstm_w_out"]),
                    chunk=chunk), None
    c0, n0, m0 = state
    q, k, v, og, gcol, grow = proj(x3.reshape(1, b * s, d), chunk=s)
    grow = jnp.swapaxes(grow.reshape(2 * M_HEADS, b, s), 0, 1)
    sh = lambda a: a.reshape(b, s, a.shape[-1])
    hg, c1, n1, m1 = _mlstm_cell(sh(q), sh(k), sh(v), sh(og), sh(gcol), grow, p["mlstm_out_norm"],
                                 c0, n0, m0.reshape(b, 1, M_HEADS), math.gcd(b, CELL_BATCH))
    return (hg.reshape(b * s, vd), p["mlstm_w_out"]), (c1, n1, m1.reshape(b, M_HEADS))


def _attn_proj_body(x_ref, g_ref, w_ref, wvt_ref, qg_ref, kg_ref, q_ref, k_ref, vt_ref, *, dh, nq, nk):
    xn = _rms(x_ref[...], g_ref[...]).astype(BF16)
    pair = 2 * LANES
    r_i = lax.broadcasted_iota(jnp.int32, (pair, pair), 0)
    c_i = lax.broadcasted_iota(jnp.int32, (pair, pair), 1)
    shift = dh.bit_length() - 1
    same_head = jnp.where(lax.shift_right_logical(r_i, shift) == lax.shift_right_logical(c_i, shift),
                          1.0, 0.0).astype(BF16)

    def head_norm(y, gain):
        sq = y * y
        hi = sq.astype(BF16)
        lo = (sq - hi.astype(F32)).astype(BF16)
        ms = (_dot(hi, same_head) + _dot(lo, same_head)) * (1.0 / dh)
        return y * lax.rsqrt(ms + RMS_EPS) * gain

    qg = jnp.concatenate([qg_ref[...], qg_ref[...]], axis=1)
    kg = jnp.concatenate([kg_ref[...], kg_ref[...]], axis=1)
    q = _dot(xn, w_ref[:, 0:nq])
    for s in range(nq // pair):
        sl = slice(s * pair, (s + 1) * pair)
        q_ref[:, sl] = head_norm(q[:, sl], qg).astype(BF16)
    k = _dot(xn, w_ref[:, nq:nq + nk])
    for s in range(nk // pair):
        sl = slice(s * pair, (s + 1) * pair)
        k_ref[:, sl] = head_norm(k[:, sl], kg)
    vt_ref[0] = _dot_nt(wvt_ref[...], xn)


def _attn_proj(x3, g, w, wvt, qg, kg, dh, nq, nk):
    b, s, d = x3.shape
    t = b * s
    nv = wvt.shape[0]
    tm = _row_tile(s, ROWS_SINGLE)
    nt = s // tm
    row = lambda n: pl.BlockSpec((tm, n), lambda i: (i, 0))
    return pl.pallas_call(
        functools.partial(_attn_proj_body, dh=dh, nq=nq, nk=nk),
        grid=(t // tm,),
        in_specs=[row(d), _resident((1, d)), _resident(w.shape), _resident(wvt.shape),
                  _resident((1, LANES)), _resident((1, LANES))],
        out_specs=[row(nq), row(nk), pl.BlockSpec((1, nv, tm), lambda i: (i // nt, 0, i % nt))],
        out_shape=[jax.ShapeDtypeStruct((t, nq), BF16), jax.ShapeDtypeStruct((t, nk), F32),
                   jax.ShapeDtypeStruct((b, nv, s), F32)],
        compiler_params=_cparams(1),
        name="attn_proj",
    )(x3.reshape(t, d), g.reshape(1, d), w, wvt, qg, kg)


def _bias_body(rel_ref, o_ref, *, tq, key_major):
    nk = 2 * WINDOW
    shape, q_ax = ((nk, tq), 1) if key_major else ((tq, nk), 0)
    qi = lax.broadcasted_iota(jnp.int32, shape, q_ax)
    ki = lax.broadcasted_iota(jnp.int32, shape, 1 - q_ax)
    dist = qi + WINDOW - ki
    max_exact = NUM_BUCKETS // 2
    d = jnp.maximum(dist, 0)
    log_ratio = (jnp.log(jnp.maximum(d, 1).astype(F32) / max_exact)
                 / math.log(MAX_DISTANCE / max_exact))
    large = jnp.minimum(max_exact + (log_ratio * (NUM_BUCKETS - max_exact)).astype(jnp.int32),
                        NUM_BUCKETS - 1)
    bucket = jnp.where(d < max_exact, d, large)
    in_window = (dist >= 0) & (dist < WINDOW)
    visible = (in_window & (ki >= WINDOW), in_window)
    for h in range(A_HEADS_Q):
        bias = jnp.zeros(shape, F32)
        for bkt in range(NUM_BUCKETS):
            bias = jnp.where(bucket == bkt, rel_ref[bkt, h], bias)
        for variant in range(2):
            tile = jnp.where(visible[variant], bias, -jnp.inf)
            if key_major:
                o_ref[variant, h // 2, :, (h % 2) * tq:(h % 2 + 1) * tq] = tile
            else:
                o_ref[variant, h] = tile


def _bias_table(rel_bias, tq, key_major):
    nk = 2 * WINDOW
    shape = (2, A_HEADS_Q // 2, nk, 2 * tq) if key_major else (2, A_HEADS_Q, tq, nk)
    return pl.pallas_call(
        functools.partial(_bias_body, tq=tq, key_major=key_major),
        grid=(1,),
        in_specs=[pl.BlockSpec(memory_space=pltpu.SMEM)],
        out_specs=pl.BlockSpec(shape, lambda i: (0, 0, 0, 0)),
        out_shape=jax.ShapeDtypeStruct(shape, F32),
        compiler_params=_cparams(1),
        name="rel_bias_table",
    )(rel_bias)


def _attn_tile_items(sink_ref, q_ref, kp_ref, ko_ref, vp_ref, vo_ref, bias_ref, first_of_seq, write, dh):
    blk = WINDOW
    n_blk = ko_ref.shape[0] // blk
    slots_per_kv = A_GROUP // 2
    n_slots = A_HEADS_Q // 2
    lane = lax.broadcasted_iota(jnp.int32, (blk, LANES), 1)
    lo_half = lane < dh
    left = lax.broadcasted_iota(jnp.int32, (1, 2 * blk), 1) < blk
    ones = jnp.ones((dh, 2 * blk), BF16)
    sinks = [jnp.where(left, sink_ref[2 * i], sink_ref[2 * i + 1]) for i in range(n_slots)]
    items = []
    for j in range(n_blk):
        st = {}
        own = slice(j * blk, (j + 1) * blk)
        before = slice((j - 1) * blk, j * blk)
        variant = jnp.where(first_of_seq, 0, 1) if j == 0 else 1

        def scores(st=st, j=j, own=own, before=before, variant=variant):
            st["vts"], st["sts"] = [], []
            for g in range(A_HEADS_KV):
                sl = slice(g * LANES, (g + 1) * LANES)
                rows = slice(g * dh, (g + 1) * dh)
                k_prev = kp_ref[:, sl] if j == 0 else ko_ref[before, sl]
                v_prev = vp_ref[0, rows, :] if j == 0 else vo_ref[0, rows, before]
                kk = jnp.concatenate([k_prev, ko_ref[own, sl]], axis=0).astype(BF16)
                vt = jnp.concatenate([v_prev, vo_ref[0, rows, own]], axis=1).astype(BF16)
                st["vts"].append(jnp.concatenate([vt, ones], axis=0))
                for pr in range(slots_per_kv):
                    slot_i = g * slots_per_kv + pr
                    slot = q_ref[own, slot_i * LANES:(slot_i + 1) * LANES]
                    zero = jnp.zeros_like(slot)
                    q2 = jnp.concatenate([jnp.where(lo_half, slot, zero), jnp.where(lo_half, zero, slot)],
                                         axis=0)
                    st["sts"].append(_dot_nt(kk, q2) + bias_ref[variant, slot_i])

        def softmax(st=st):
            st["ms"] = [jnp.maximum(jnp.max(st["sts"][i], axis=0, keepdims=True), sinks[i])
                        for i in range(n_slots)]
            st["pts"] = [jnp.exp(st["sts"][i] - st["ms"][i]).astype(BF16) for i in range(n_slots)]

        def values(st=st):
            st["oes"] = [_dot(st["vts"][i // slots_per_kv], st["pts"][i]) for i in range(n_slots)]

        def finish(st=st, j=j):
            heads_t = []
            for i in range(n_slots):
                oe = st["oes"][i]
                ot = oe[0:dh, :] / (oe[dh:dh + 1, :] + jnp.exp(sinks[i] - st["ms"][i]))
                heads_t += [ot[:, 0:blk], ot[:, blk:2 * blk]]
            write(j, jnp.concatenate(heads_t, axis=0).T.astype(BF16))

        items += [scores, softmax, values, finish]
    return items


def _attn_ffn_body(sink_ref, q_ref, kp_ref, ko_ref, vp_ref, vo_ref, bias_ref, wa_ref,
                   x_ref, g_ref, wgu_ref, wd_ref, o_ref, act_ref, att_ref, *, d_ff, tf, dh, tiles_per_seq):
    i = pl.program_id(0)

    @pl.when(i == 0)
    def _():
        att_ref[...] = jnp.zeros_like(att_ref)

    cur = lax.rem(i, 2)
    first_of_seq = lax.rem(jnp.minimum(i, pl.num_programs(0) - 2), tiles_per_seq) == 0

    def write(j, tile):
        att_ref[cur, j * WINDOW:(j + 1) * WINDOW, :] = tile

    side = _attn_tile_items(sink_ref, q_ref, kp_ref, ko_ref, vp_ref, vo_ref, bias_ref,
                            first_of_seq, write, dh)
    get_x = lambda: x_ref[...] + _dot(att_ref[1 - cur], wa_ref[...])
    _interleave(_ffn_items(get_x, g_ref, wgu_ref, wd_ref, o_ref, act_ref, d_ff, tf), side)


def _attn_ffn(x, q, k, vt, bias, sinks, wa, g, wgu, wd, dh):
    t, d = x.shape
    b, vd, s = vt.shape
    d_ff = wd.shape[0]
    qd, kd = q.shape[1], k.shape[1]
    blk = WINDOW
    tm = _row_tile(s, ROWS_FUSED)
    nt, tps, bpt = t // tm, s // tm, tm // blk
    att_tile = lambda i: jnp.minimum(i, nt - 1)
    ffn_tile = lambda i: jnp.maximum(i - 1, 0)
    ffn_row = lambda n: pl.BlockSpec((tm, n), lambda i: (ffn_tile(i), 0))
    att_row = lambda n: pl.BlockSpec((tm, n), lambda i: (att_tile(i), 0))
    return pl.pallas_call(
        functools.partial(_attn_ffn_body, d_ff=d_ff, tf=FFN_TF, dh=dh, tiles_per_seq=tps),
        grid=(nt + 1,),
        in_specs=[pl.BlockSpec(memory_space=pltpu.SMEM),
                  att_row(qd),
                  pl.BlockSpec((blk, kd), lambda i: (jnp.maximum(att_tile(i) * bpt - 1, 0), 0)),
                  att_row(kd),
                  pl.BlockSpec((1, vd, blk), lambda i: (att_tile(i) // tps,
                                                        0, jnp.maximum(att_tile(i) % tps * bpt - 1, 0))),
                  pl.BlockSpec((1, vd, tm), lambda i: (att_tile(i) // tps, 0, att_tile(i) % tps)),
                  _resident(bias.shape), _resident(wa.shape),
                  ffn_row(d), _resident((1, d)), _resident((d, 2 * d_ff)), _resident((d_ff, d))],
        out_specs=ffn_row(d),
        out_shape=jax.ShapeDtypeStruct((t, d), F32),
        scratch_shapes=[pltpu.VMEM((tm, d_ff), BF16), pltpu.VMEM((2, tm, qd), BF16)],
        compiler_params=_cparams(1),
        name="attn_ffn",
    )(sinks, q, k, k, vt, vt, bias, wa, x, g.reshape(1, d), wgu, wd)


def _attn_sample_body(sink_ref, q_ref, kc_ref, kn_ref, vc_ref, vn_ref, bias_ref,
                      o_ref, ko_ref, vo_ref, *, bt, tq):
    dh = q_ref.shape[-1]
    fresh = lax.broadcasted_iota(jnp.int32, (dh, WINDOW), 1) >= WINDOW - tq
    sinks = [jnp.concatenate([jnp.full((tq, 1), sink_ref[g * A_GROUP + j], F32) for j in range(A_GROUP)],
                             axis=0) for g in range(A_HEADS_KV)]

    def slide(win_ref, new_ref, b, g):
        new = pltpu.roll(new_ref[g], (WINDOW - tq - b * tq) % WINDOW, axis=1)
        return jnp.where(fresh, new, pltpu.roll(win_ref[b, g], WINDOW - tq, axis=1))

    units = [(b, g) for b in range(bt) for g in range(A_HEADS_KV)]
    us = range(len(units))
    q = [q_ref[b, g] for b, g in units]
    k_win = [slide(kc_ref, kn_ref, b, g) for b, g in units]
    v_win = [slide(vc_ref, vn_ref, b, g) for b, g in units]
    for u, (b, g) in enumerate(units):
        ko_ref[b, g] = k_win[u]
        vo_ref[b, g] = v_win[u]
    s_old = [_dot(q[u], kc_ref[b, g].astype(BF16)) + bias_ref[0, g] for u, (b, g) in enumerate(units)]
    s_new = [_dot(q[u], k_win[u].astype(BF16)) + bias_ref[1, g] for u, (b, g) in enumerate(units)]
    m = [jnp.maximum(jnp.maximum(jnp.max(s_old[u], axis=-1, keepdims=True),
                                 jnp.max(s_new[u], axis=-1, keepdims=True)), sinks[g])
         for u, (b, g) in enumerate(units)]
    p_old = [jnp.exp(s_old[u] - m[u]) for u in us]
    p_new = [jnp.exp(s_new[u] - m[u]) for u in us]
    den = [jnp.sum(p_old[u], axis=-1, keepdims=True) + jnp.sum(p_new[u], axis=-1, keepdims=True)
           + jnp.exp(sinks[g] - m[u]) for u, (b, g) in enumerate(units)]
    out = [_dot_nt((p_old[u] / den[u]).astype(BF16), vc_ref[b, g].astype(BF16))
           + _dot_nt((p_new[u] / den[u]).astype(BF16), v_win[u].astype(BF16))
           for u, (b, g) in enumerate(units)]
    for u, (b, g) in enumerate(units):
        o_ref[b, g] = out[u].astype(BF16)


def _attn_sample(q, kn, vn, k_win, v_win, bias, sinks, tq):
    b, n_kv, rows, dh = q.shape
    assert LANES % tq == 0
    bt = LANES // tq
    assert b % bt == 0
    lead = lambda *tail: pl.BlockSpec((bt,) + tail, lambda i: (i, 0, 0, 0))
    new = pl.BlockSpec((n_kv, dh, LANES), lambda i: (0, 0, i))
    return pl.pallas_call(
        functools.partial(_attn_sample_body, bt=bt, tq=tq),
        grid=(b // bt,),
        in_specs=[pl.BlockSpec(memory_space=pltpu.SMEM),
                  lead(n_kv, rows, dh), lead(n_kv, dh, WINDOW), new, lead(n_kv, dh, WINDOW), new,
                  pl.BlockSpec(bias.shape, lambda i: (0, 0, 0, 0))],
        out_specs=[lead(n_kv, rows, dh), lead(n_kv, dh, WINDOW), lead(n_kv, dh, WINDOW)],
        out_shape=[jax.ShapeDtypeStruct(q.shape, BF16),
                   jax.ShapeDtypeStruct(k_win.shape, F32),
                   jax.ShapeDtypeStruct(v_win.shape, F32)],
        compiler_params=_cparams(1),
        name="attn_sample",
    )(sinks, q, k_win, kn, v_win, vn, bias)


def _swa_layer(x3, buffers, p):
    b, s, d = x3.shape
    dh = p["attn_dh"]
    kd = A_HEADS_KV * dh
    x = x3.reshape(b * s, d)
    kv_shape = (b, WINDOW, A_HEADS_KV, dh)
    if buffers is None:
        nq, nk = A_HEADS_Q * dh, A_HEADS_KV * LANES
        q, k, vt = _attn_proj(x3, p["mix_norm1"], p["attn_w_prompt"], p["attn_wvt"],
                              p["attn_qg"], p["attn_kg"], dh, nq, nk)
        bias = _bias_table(p["rel_bias"], WINDOW, key_major=True)
        new_k = k.reshape(b, s, nk)[:, s - WINDOW:].reshape(b, WINDOW, A_HEADS_KV, LANES)[..., :dh]
        new_v = jnp.swapaxes(vt[:, :, s - WINDOW:], 1, 2)
        mixed = dict(kind="attn", args=(q, k, vt, bias, p["attn_sinks"], p["attn_w_out"]), dh=dh)
    else:
        k_buf, v_buf = buffers
        nq, nk = A_HEADS_Q * dh, kd
        q, k, vt = _attn_proj(x.reshape(1, b * s, d), p["mix_norm1"], p["attn_w_sample"], p["attn_wvt"],
                              p["attn_qg"], p["attn_kg"], dh, nq, nk)
        rows = A_GROUP * s
        q = jnp.transpose(q.reshape(b, s, A_HEADS_KV, A_GROUP, dh), (0, 2, 3, 1, 4))
        table = _bias_table(p["rel_bias"], s, key_major=False)[1].reshape(A_HEADS_KV, rows, 2 * WINDOW)
        own = jnp.pad(table[..., WINDOW:WINDOW + s], ((0, 0), (0, 0), (WINDOW - s, 0)),
                      constant_values=-jnp.inf)
        bias = jnp.stack([table[..., :WINDOW], own])
        to_lanes = lambda a: jnp.transpose(a, (0, 2, 3, 1))
        att, new_k, new_v = _attn_sample(q.reshape(b, A_HEADS_KV, rows, dh),
                                         k.T.reshape(A_HEADS_KV, dh, b * s), vt[0].reshape(A_HEADS_KV, dh, b * s),
                                         to_lanes(k_buf), to_lanes(v_buf), bias, p["attn_sinks"], s)
        new_k, new_v = (jnp.transpose(a, (0, 3, 1, 2)) for a in (new_k, new_v))
        att = jnp.transpose(att.reshape(b, A_HEADS_KV, A_GROUP, s, dh), (0, 3, 1, 2, 4))
        mixed = (att.reshape(b * s, nq), p["attn_w_out"])
    return mixed, (new_k.reshape(kv_shape), new_v.reshape(kv_shape))


def _attn_weights(w_in, w_out, q_norm, k_norm, dh):
    d = w_in.shape[0]
    qd, kd = A_HEADS_Q * dh, A_HEADS_KV * dh
    wk = w_in[:, qd:qd + kd].reshape(d, A_HEADS_KV, dh)
    k_dup = jnp.concatenate([wk, wk], axis=-1).reshape(d, A_HEADS_KV * LANES)
    w_prompt = jnp.concatenate([w_in[:, :qd], k_dup], axis=1).astype(BF16)
    qg = jnp.concatenate([q_norm, q_norm]).reshape(1, LANES) * (dh ** -0.5)
    kg = jnp.concatenate([k_norm, k_norm]).reshape(1, LANES)
    return {"attn_w_prompt": w_prompt, "attn_w_sample": w_in[:, :qd + kd].astype(BF16),
            "attn_wvt": w_in[:, qd + kd:].T.astype(BF16),
            "attn_w_out": w_out.astype(BF16),
            "attn_qg": qg, "attn_kg": kg}


def _trunk(x3, mlstm_state, swa_buffers, p, ffn_bf16):
    b, s, d = x3.shape
    stream = mlstm_state is not None

    def ffn(x, name, layer, mixed=None):
        norm = p[name + "_norm"][layer]
        if stream:
            y, *ffn_bf16[name, layer] = _ffn_stream(x, norm, *p[name + "_f32"], layer, pre=mixed)
            return y
        w = (norm, *ffn_bf16[name, layer])
        if mixed is None:
            return _ffn(x, *w)
        if mixed["kind"] == "attn":
            return _attn_ffn(x, *mixed["args"], *w, mixed["dh"])
        return _mlstm_ffn(x, *mixed["args"], *w, mixed["chunk"])

    x = ffn(x3.reshape(b * s, d), "ffn1", 0)
    mixed, new_mlstm = _mlstm_layer(x.reshape(b, s, d), mlstm_state, p)
    x = ffn(x, "ffn2", 0, mixed)
    if new_mlstm is None:
        x, c1, n1, m1 = x
        new_mlstm = (c1, n1, m1.reshape(b, M_HEADS))
    x = ffn(x, "ffn1", 1)
    mixed, new_swa = _swa_layer(x.reshape(b, s, d), swa_buffers, p)
    x = ffn(x, "ffn2", 1, mixed)
    return x.reshape(b, s, d), new_mlstm, new_swa


def kernel(x_prompt, x_sample, state_mlstm_C, state_mlstm_n, state_mlstm_m, cache_swa_k, cache_swa_v,
           ffn1_norm, ffn1_w_gate_up, ffn1_w_down, mix_norm, ffn2_norm, ffn2_w_gate_up, ffn2_w_down,
           mlstm_w_in, mlstm_b_gates, mlstm_out_norm, mlstm_w_out,
           attn_w_in, attn_q_norm, attn_k_norm, attn_sinks, rel_bias, attn_w_out):
    d = x_prompt.shape[-1]
    vd = mlstm_w_out.shape[0]
    n_gates = 2 * M_HEADS
    qk = (mlstm_w_in.shape[1] - 2 * vd - n_gates) // 2
    dh = attn_q_norm.shape[0]
    assert 2 * dh == LANES
    w_gates = jnp.zeros((d, LANES), BF16).at[:, :n_gates].set(mlstm_w_in[:, 2 * qk + 2 * vd:].astype(BF16))
    b_gates = jnp.zeros((1, LANES), F32).at[0, :n_gates].set(mlstm_b_gates)
    p = {
        "ffn1_norm": ffn1_norm, "ffn2_norm": ffn2_norm,
        "ffn1_f32": (ffn1_w_gate_up, ffn1_w_down), "ffn2_f32": (ffn2_w_gate_up, ffn2_w_down),
        "mix_norm0": mix_norm[0], "mix_norm1": mix_norm[1],
        "mlstm_qk": qk, "mlstm_vd": vd,
        "mlstm_w_main": mlstm_w_in[:, :2 * qk + 2 * vd].astype(BF16),
        "mlstm_w_gates": w_gates, "mlstm_b_gates": b_gates,
        "mlstm_out_norm": mlstm_out_norm, "mlstm_w_out": mlstm_w_out.astype(BF16),
        "attn_dh": dh, "attn_sinks": attn_sinks, "rel_bias": rel_bias,
    }
    p.update(_attn_weights(attn_w_in, attn_w_out, attn_q_norm, attn_k_norm, dh))
    ffn_bf16 = {}
    y_s, (c_s, n_s, m_s), (k_s, v_s) = _trunk(
        x_sample, (state_mlstm_C, state_mlstm_n, state_mlstm_m), (cache_swa_k, cache_swa_v), p, ffn_bf16)
    y_p, (c_p, n_p, m_p), (k_p, v_p) = _trunk(x_prompt, None, None, p, ffn_bf16)
    return (y_p, y_s, c_p, n_p, m_p, k_p, v_p, c_s, n_s, m_s, k_s, v_s)
```

```python
import functools
import math

import jax
import jax.numpy as jnp
from jax import lax
from jax.experimental import pallas as pl
from jax.experimental.pallas import tpu as pltpu

F32 = jnp.float32
BF16 = jnp.bfloat16

M_HEADS = 4
A_HEADS_Q = 16
A_HEADS_KV = 4
A_GROUP = A_HEADS_Q // A_HEADS_KV
WINDOW = 128
NUM_BUCKETS = 32
MAX_DISTANCE = 128
FFN_RESIDUAL = 0.5
RMS_EPS = 1e-6
LANES = 128
VMEM_LIMIT = 56 * 1024 * 1024
ROWS_SINGLE = 1024
ROWS_FUSED = 512
CELL_BATCH = 16


def _cparams(n_axes):
    return pltpu.CompilerParams(
        dimension_semantics=("arbitrary",) * n_axes, vmem_limit_bytes=VMEM_LIMIT)


def _resident(shape):
    nd = len(shape)
    return pl.BlockSpec(shape, lambda *_: (0,) * nd, pipeline_mode=pl.Buffered(1))


def _rms(x, g):
    return x * lax.rsqrt(jnp.mean(x * x, axis=-1, keepdims=True) + RMS_EPS) * g


def _dot(a, b):
    return jnp.dot(a, b, preferred_element_type=F32)


def _dot_nt(a, b):
    return lax.dot_general(a, b, (((1,), (1,)), ((), ())), preferred_element_type=F32)


def _dot_tn(a, b):
    return lax.dot_general(a, b, (((0,), (0,)), ((), ())), preferred_element_type=F32)


def _row_tile(t, pref):
    tm = min(t, pref)
    assert t % tm == 0
    return tm


FFN_TF = 256


def _ffn_stream_body(*refs, has_pre):
    if has_pre:
        a_ref, wa_ref, x_ref, g_ref, wg_ref, wu_ref, wd_ref, o_ref, wgu_out, wd_out, xn_ref = refs
    else:
        x_ref, g_ref, wg_ref, wu_ref, wd_ref, o_ref, wgu_out, wd_out, xn_ref = refs

    @pl.when(pl.program_id(0) == 0)
    def _():
        y = x_ref[...] + _dot(a_ref[...], wa_ref[...]) if has_pre else x_ref[...]
        o_ref[...] = y
        xn_ref[...] = _rms(y, g_ref[...]).astype(BF16)

    wg, wu, wd = (r[...].astype(BF16) for r in (wg_ref, wu_ref, wd_ref))
    wgu_out[...] = jnp.concatenate([wg, wu], axis=1)
    wd_out[...] = wd
    xn = xn_ref[...]
    gate = _dot(xn, wg)
    up = _dot(xn, wu)
    act = (gate * jax.nn.sigmoid(gate) * up).astype(BF16)
    o_ref[...] = o_ref[...] + FFN_RESIDUAL * _dot(act, wd)


def _ffn_stream(x, g, wgu_f32, wd_f32, layer, pre=None):
    t, d = x.shape
    d_ff = wd_f32.shape[1]
    tf = FFN_TF
    n_chunks = d_ff // tf
    assert d_ff % tf == 0
    whole = lambda r, c: pl.BlockSpec((r, c), lambda i: (0, 0))
    pre_specs, pre_args = [], []
    if pre is not None:
        a, wa = pre
        pre_specs, pre_args = [whole(t, a.shape[1]), whole(*wa.shape)], [a, wa]
    return pl.pallas_call(
        functools.partial(_ffn_stream_body, has_pre=pre is not None),
        grid=(n_chunks,),
        in_specs=pre_specs + [
            whole(t, d), whole(1, d),
            pl.BlockSpec((None, d, tf), lambda c: (layer, 0, c)),
            pl.BlockSpec((None, d, tf), lambda c: (layer, 0, n_chunks + c)),
            pl.BlockSpec((None, tf, d), lambda c: (layer, c, 0))],
        out_specs=[whole(t, d), pl.BlockSpec((d, 2 * tf), lambda c: (0, c)),
                   pl.BlockSpec((tf, d), lambda c: (c, 0))],
        out_shape=[jax.ShapeDtypeStruct((t, d), F32), jax.ShapeDtypeStruct((d, 2 * d_ff), BF16),
                   jax.ShapeDtypeStruct((d_ff, d), BF16)],
        scratch_shapes=[pltpu.VMEM((t, d), BF16)],
        compiler_params=_cparams(1),
        name="ffn_stream",
    )(*pre_args, x, g.reshape(1, d), wgu_f32, wgu_f32, wd_f32)

def _interleave(main, side, side_before_last=False):
    slots = len(main) - 1 if side_before_last else len(main)
    done = 0
    for j, thunk in enumerate(main):
        thunk()
        upto = min(len(side), -(-(j + 1) * len(side) // slots)) if side_before_last else (
            (j + 1) * len(side) // slots)
        for other in side[done:upto]:
            other()
        done = upto


def _ffn_items(get_x, g_ref, wgu_ref, wd_ref, o_ref, act_ref, d_ff, tf):
    st = {}

    def head():
        y = get_x()
        o_ref[...] = y
        st["xn"] = _rms(y, g_ref[...]).astype(BF16)

    def chunk(lo):
        gate = _dot(st["xn"], wgu_ref[:, 2 * lo:2 * lo + tf])
        up = _dot(st["xn"], wgu_ref[:, 2 * lo + tf:2 * lo + 2 * tf])
        act_ref[:, lo:lo + tf] = (gate * jax.nn.sigmoid(gate) * up).astype(BF16)

    def tail():
        o_ref[...] = o_ref[...] + FFN_RESIDUAL * _dot(act_ref[...], wd_ref[...])

    return [head] + [functools.partial(chunk, lo) for lo in range(0, d_ff, tf)] + [tail]


def _ffn_body(x_ref, g_ref, wgu_ref, wd_ref, o_ref, act_ref, *, d_ff, tf):
    for thunk in _ffn_items(lambda: x_ref[...], g_ref, wgu_ref, wd_ref, o_ref, act_ref, d_ff, tf):
        thunk()


def _ffn(x, g, wgu, wd):
    t, d = x.shape
    d_ff = wd.shape[0]
    tm = _row_tile(t, ROWS_SINGLE)
    assert d_ff % FFN_TF == 0
    row = lambda n: pl.BlockSpec((tm, n), lambda i: (i, 0))
    return pl.pallas_call(
        functools.partial(_ffn_body, d_ff=d_ff, tf=FFN_TF),
        grid=(t // tm,),
        in_specs=[row(d), _resident((1, d)), _resident((d, 2 * d_ff)), _resident((d_ff, d))],
        out_specs=row(d),
        out_shape=jax.ShapeDtypeStruct((t, d), F32),
        scratch_shapes=[pltpu.VMEM((tm, d_ff), BF16)],
        compiler_params=_cparams(1),
        name="ffn",
    )(x, g.reshape(1, d), wgu, wd)


def _mlstm_proj_body(x_ref, g_ref, w_ref, wg_ref, bg_ref,
                     q_ref, k_ref, v_ref, og_ref, gcol_ref, grow_ref, *, qk, vd, dk, chunk):
    h_n = M_HEADS
    xn = _rms(x_ref[...], g_ref[...]).astype(BF16)
    tm = xn.shape[0]

    shift = chunk.bit_length() - 1
    r_i = lax.broadcasted_iota(jnp.int32, (LANES, LANES), 0)
    c_i = lax.broadcasted_iota(jnp.int32, (LANES, LANES), 1)
    same_chunk = lax.shift_right_logical(r_i, shift) == lax.shift_right_logical(c_i, shift)
    tri = jnp.where((r_i >= c_i) & same_chunk, 1.0, 0.0).astype(BF16)

    gates = _dot(xn, wg_ref[...]) + bg_ref[...]
    q_ref[...] = _dot(xn, w_ref[:, 0:qk]).astype(BF16)
    lane = lax.broadcasted_iota(jnp.int32, (tm, LANES), 1)
    log_f = jnp.where((lane >= h_n) & (lane < 2 * h_n), jax.nn.log_sigmoid(gates), 0.0)
    hi = log_f.astype(BF16)
    lo = (log_f - hi.astype(F32)).astype(BF16)
    k_ref[...] = (_dot(xn, w_ref[:, qk:2 * qk]) * (dk ** -0.5)).astype(BF16)
    hi_lo = jnp.concatenate([hi, lo], axis=1)
    csum = jnp.concatenate([_dot(tri, hi_lo[r:r + LANES]) for r in range(0, tm, LANES)], axis=0)
    gcol = jnp.where(lane < h_n, gates, csum[:, :LANES] + csum[:, LANES:])
    gcol_ref[...] = gcol
    og_ref[...] = jax.nn.sigmoid(_dot(xn, w_ref[:, 2 * qk + vd:2 * qk + 2 * vd])).astype(BF16)
    g_t = gcol.T
    grow_ref[0] = jnp.concatenate([g_t[0:h_n] - g_t[h_n:2 * h_n], g_t[h_n:2 * h_n]], axis=0)
    v_ref[...] = _dot(xn, w_ref[:, 2 * qk:2 * qk + vd]).astype(BF16)


def _mlstm_proj(x3, g, w_main, w_gates, b_gates, qk, vd, chunk):
    b, s, d = x3.shape
    t = b * s
    tm = _row_tile(s, ROWS_SINGLE)
    nt = s // tm
    assert chunk & (chunk - 1) == 0 and LANES % chunk == 0 and tm % LANES == 0
    row = lambda n: pl.BlockSpec((tm, n), lambda i: (i, 0))
    return pl.pallas_call(
        functools.partial(_mlstm_proj_body, qk=qk, vd=vd, dk=qk // M_HEADS, chunk=chunk),
        grid=(t // tm,),
        in_specs=[row(d), _resident((1, d)), _resident(w_main.shape),
                  _resident(w_gates.shape), _resident((1, LANES))],
        out_specs=[row(qk), row(qk), row(vd), row(vd), row(LANES),
                   pl.BlockSpec((1, 2 * M_HEADS, tm), lambda i: (i // nt, 0, i % nt))],
        out_shape=[jax.ShapeDtypeStruct((t, qk), BF16), jax.ShapeDtypeStruct((t, qk), BF16),
                   jax.ShapeDtypeStruct((t, vd), BF16), jax.ShapeDtypeStruct((t, vd), BF16),
                   jax.ShapeDtypeStruct((t, LANES), F32),
                   jax.ShapeDtypeStruct((b, 2 * M_HEADS, s), F32)],
        compiler_params=_cparams(1),
        name="mlstm_proj",
    )(x3.reshape(t, d), g.reshape(1, d), w_main, w_gates, b_gates)


def _mlstm_chunk_items(units, load, state, emit, rows, update_early):
    r_i = lax.broadcasted_iota(jnp.int32, (rows, rows), 0)
    c_i = lax.broadcasted_iota(jnp.int32, (rows, rows), 1)
    causal = r_i >= c_i
    us = range(len(units))
    st = {}

    def products():
        st["x"] = x = [load(u) for u in units]
        st["old"] = [state[u] for u in units]
        st["qk"] = [_dot_nt(x[i]["q"], x[i]["k"]) for i in us]
        st["qc"] = [_dot(x[i]["q"], st["old"][i][0].astype(BF16)) for i in us]

    def weights():
        x, old = st["x"], st["old"]
        log_d = [jnp.where(causal, x[i]["b_c"] + x[i]["imb"], -jnp.inf) for i in us]
        log_inter = [x[i]["b_c"] + old[i][2] for i in us]
        st["m_t"] = m_t = [jnp.maximum(log_inter[i], jnp.max(log_d[i], axis=-1, keepdims=True)) for i in us]
        st["inter"] = [jnp.exp(log_inter[i] - m_t[i]) for i in us]
        st["s"] = [st["qk"][i] * jnp.exp(log_d[i] - m_t[i]) for i in us]

    def hidden():
        x, old, s, inter, m_t = st["x"], st["old"], st["s"], st["inter"], st["m_t"]
        s_v = [_dot(s[i].astype(BF16), x[i]["v"]) for i in us]
        den = [inter[i] * jnp.sum(x[i]["q"].astype(F32) * old[i][1], axis=-1, keepdims=True)
               + jnp.sum(s[i], axis=-1, keepdims=True) for i in us]
        st["hid"] = [(inter[i] * st["qc"][i] + s_v[i]) / jnp.maximum(jnp.abs(den[i]), jnp.exp(-m_t[i]))
                     for i in us]

    def update():
        x, old = st["x"], st["old"]
        b_last = [x[i]["b_c"][rows - 1:rows, :] for i in us]
        log_w = [b_last[i] - x[i]["b_c"] + x[i]["i_c"] for i in us]
        m_new = [jnp.maximum(b_last[i] + old[i][2], jnp.max(log_w[i], axis=0, keepdims=True)) for i in us]
        decay = [jnp.exp(b_last[i] + old[i][2] - m_new[i]) for i in us]
        kw = [jnp.exp(log_w[i] - m_new[i]) * x[i]["k"].astype(F32) for i in us]
        kw_v = [_dot_tn(kw[i].astype(BF16), x[i]["v"]) for i in us]
        for i, u in enumerate(units):
            state[u] = (decay[i] * old[i][0] + kw_v[i],
                        decay[i] * old[i][1] + jnp.sum(kw[i], axis=0, keepdims=True), m_new[i])

    def output():
        hid = [st["hid"][i] * lax.rsqrt(jnp.mean(st["hid"][i] * st["hid"][i], axis=-1, keepdims=True) + RMS_EPS)
               for i in us]
        for i, u in enumerate(units):
            emit(u, (hid[i] * st["x"][i]["ng"] * st["x"][i]["og"].astype(F32)).astype(BF16))

    return [products, update, weights, hidden, output] if update_early else [
        products, weights, hidden, update, output]


def _mlstm_cell_body(q_ref, k_ref, v_ref, og_ref, gcol_ref, grow_ref, ng_ref, c0_ref, n0_ref, m0_ref,
                     hg_ref, c_ref, n_ref, m_ref, *, dk, dv, bt):
    h_n = M_HEADS
    rows = q_ref.shape[1]
    units = [(b, h) for b in range(bt) for h in range(h_n)]

    def load(u):
        b, h = u
        kc, vc = slice(h * dk, (h + 1) * dk), slice(h * dv, (h + 1) * dv)
        return dict(q=q_ref[b, :, kc], k=k_ref[b, :, kc], v=v_ref[b, :, vc], og=og_ref[b, :, vc],
                    ng=ng_ref[:, vc], b_c=gcol_ref[b, :, h_n + h:h_n + h + 1], i_c=gcol_ref[b, :, h:h + 1],
                    imb=grow_ref[b, h:h + 1, :])

    def emit(u, value):
        b, h = u
        hg_ref[b, :, h * dv:(h + 1) * dv] = value

    state = {(b, h): (c0_ref[b, h], n0_ref[b, h:h + 1, :], m0_ref[b, :, h:h + 1]) for b, h in units}
    for thunk in _mlstm_chunk_items(units, load, state, emit, rows, update_early=True):
        thunk()
    for b, h in units:
        c_ref[b, h], n_ref[b, h:h + 1, :], m_ref[b, :, h:h + 1] = state[(b, h)]


def _mlstm_cell(q, k, v, og, gcol, grow, norm_g, c0, n0, m0, bt):
    b, s, qk = q.shape
    vd = v.shape[-1]
    dk, dv = qk // M_HEADS, vd // M_HEADS
    assert b % bt == 0
    tok = lambda n: pl.BlockSpec((bt, s, n), lambda i: (i, 0, 0))
    st_c = pl.BlockSpec((bt, M_HEADS, dk, dv), lambda i: (i, 0, 0, 0))
    st_n = pl.BlockSpec((bt, M_HEADS, dk), lambda i: (i, 0, 0))
    st_m = pl.BlockSpec((bt, 1, M_HEADS), lambda i: (i, 0, 0))
    return pl.pallas_call(
        functools.partial(_mlstm_cell_body, dk=dk, dv=dv, bt=bt),
        grid=(b // bt,),
        in_specs=[tok(qk), tok(qk), tok(vd), tok(vd), tok(LANES),
                  pl.BlockSpec((bt, 2 * M_HEADS, s), lambda i: (i, 0, 0)),
                  pl.BlockSpec((1, vd), lambda i: (0, 0)), st_c, st_n, st_m],
        out_specs=[tok(vd), st_c, st_n, st_m],
        out_shape=[jax.ShapeDtypeStruct((b, s, vd), BF16),
                   jax.ShapeDtypeStruct((b, M_HEADS, dk, dv), F32),
                   jax.ShapeDtypeStruct((b, M_HEADS, dk), F32),
                   jax.ShapeDtypeStruct((b, 1, M_HEADS), F32)],
        compiler_params=_cparams(1),
        name="mlstm_cell",
    )(q, k, v, og, gcol, grow, norm_g.reshape(1, vd), c0, n0, m0)


def _mlstm_ffn_body(q_ref, k_ref, v_ref, og_ref, gcol_ref, grow_ref, ng_ref, wa_ref,
                    x_ref, g_ref, wgu_ref, wd_ref, o_ref, c_ref, n_ref, m_ref, act_ref, hg_ref,
                    *, d_ff, tf, dk, dv, chunk, tiles_per_seq):
    h_n = M_HEADS
    i = pl.program_id(0)
    last = pl.num_programs(0) - 1
    live = i < last
    first_of_seq = lax.rem(jnp.minimum(i, last - 1), tiles_per_seq) == 0

    @pl.when(i == 0)
    def _():
        hg_ref[...] = jnp.zeros_like(hg_ref)

    @pl.when(first_of_seq & live)
    def _():
        c_ref[...] = jnp.zeros_like(c_ref)
        n_ref[...] = jnp.zeros_like(n_ref)
        m_ref[...] = jnp.zeros_like(m_ref)

    cur = lax.rem(i, 2)
    heads = list(range(h_n))
    state = {h: (c_ref[0, h], n_ref[0, h:h + 1, :], m_ref[0, :, h:h + 1]) for h in heads}
    side = []
    for j in range(q_ref.shape[0] // chunk):
        rows = slice(j * chunk, (j + 1) * chunk)

        def load(h, rows=rows):
            kc, vc = slice(h * dk, (h + 1) * dk), slice(h * dv, (h + 1) * dv)
            return dict(q=q_ref[rows, kc], k=k_ref[rows, kc], v=v_ref[rows, vc], og=og_ref[rows, vc],
                        ng=ng_ref[:, vc], b_c=gcol_ref[rows, h_n + h:h_n + h + 1], i_c=gcol_ref[rows, h:h + 1],
                        imb=grow_ref[0, h:h + 1, rows])

        def emit(h, value, rows=rows):
            hg_ref[cur, rows, h * dv:(h + 1) * dv] = value

        side += _mlstm_chunk_items(heads, load, state, emit, chunk, update_early=False)
    get_x = lambda: x_ref[...] + _dot(hg_ref[1 - cur], wa_ref[...])
    _interleave(_ffn_items(get_x, g_ref, wgu_ref, wd_ref, o_ref, act_ref, d_ff, tf), side)

    @pl.when(live)
    def _():
        for h in heads:
            c_ref[0, h], n_ref[0, h:h + 1, :], m_ref[0, :, h:h + 1] = state[h]


def _mlstm_ffn(x, q, k, v, og, gcol, grow, norm_g, wa, g, wgu, wd, chunk):
    t, d = x.shape
    b, _, s = grow.shape
    d_ff = wd.shape[0]
    qk, vd = q.shape[1], v.shape[1]
    dk, dv = qk // M_HEADS, vd // M_HEADS
    tm = _row_tile(s, ROWS_FUSED)
    nt, tps = t // tm, s // tm
    assert tm % chunk == 0
    mix_tile = lambda i: jnp.minimum(i, nt - 1)
    ffn_row = lambda n: pl.BlockSpec((tm, n), lambda i: (jnp.maximum(i - 1, 0), 0))
    mix_row = lambda n: pl.BlockSpec((tm, n), lambda i: (mix_tile(i), 0))
    seq = lambda *tail: pl.BlockSpec((1,) + tail, lambda i: (mix_tile(i) // tps,) + (0,) * len(tail))
    return pl.pallas_call(
        functools.partial(_mlstm_ffn_body, d_ff=d_ff, tf=FFN_TF, dk=dk, dv=dv, chunk=chunk, tiles_per_seq=tps),
        grid=(nt + 1,),
        in_specs=[mix_row(qk), mix_row(qk), mix_row(vd), mix_row(vd), mix_row(LANES),
                  pl.BlockSpec((1, 2 * M_HEADS, tm), lambda i: (mix_tile(i) // tps, 0, mix_tile(i) % tps)),
                  _resident((1, vd)), _resident(wa.shape),
                  ffn_row(d), _resident((1, d)), _resident((d, 2 * d_ff)), _resident((d_ff, d))],
        out_specs=[ffn_row(d), seq(M_HEADS, dk, dv), seq(M_HEADS, dk), seq(1, M_HEADS)],
        out_shape=[jax.ShapeDtypeStruct((t, d), F32),
                   jax.ShapeDtypeStruct((b, M_HEADS, dk, dv), F32),
                   jax.ShapeDtypeStruct((b, M_HEADS, dk), F32),
                   jax.ShapeDtypeStruct((b, 1, M_HEADS), F32)],
        scratch_shapes=[pltpu.VMEM((tm, d_ff), BF16), pltpu.VMEM((2, tm, vd), BF16)],
        compiler_params=_cparams(1),
        name="mlstm_ffn",
    )(q, k, v, og, gcol, grow, norm_g.reshape(1, vd), wa, x, g.reshape(1, d), wgu, wd)


def _mlstm_layer(x3, state, p):
    b, s, d = x3.shape
    qk, vd = p["mlstm_qk"], p["mlstm_vd"]
    proj = functools.partial(_mlstm_proj, g=p["mix_norm0"], w_main=p["mlstm_w_main"],
                             w_gates=p["mlstm_w_gates"], b_gates=p["mlstm_b_gates"], qk=qk, vd=vd)
    if state is None:
        chunk = min(s, 128)
        q, k, v, og, gcol, grow = proj(x3, chunk=chunk)
        return dict(kind="mlstm", args=(q, k, v, og, gcol, grow, p["mlstm_out_norm"], p["mlstm_w_out"]),
                    chunk=chunk), None
    c0, n0, m0 = state
    q, k, v, og, gcol, grow = proj(x3.reshape(1, b * s, d), chunk=s)
    grow = jnp.swapaxes(grow.reshape(2 * M_HEADS, b, s), 0, 1)
    sh = lambda a: a.reshape(b, s, a.shape[-1])
    hg, c1, n1, m1 = _mlstm_cell(sh(q), sh(k), sh(v), sh(og), sh(gcol), grow, p["mlstm_out_norm"],
                                 c0, n0, m0.reshape(b, 1, M_HEADS), math.gcd(b, CELL_BATCH))
    return (hg.reshape(b * s, vd), p["mlstm_w_out"]), (c1, n1, m1.reshape(b, M_HEADS))


def _attn_proj_body(x_ref, g_ref, w_ref, wvt_ref, qg_ref, kg_ref, q_ref, k_ref, vt_ref, *, dh, nq, nk):
    xn = _rms(x_ref[...], g_ref[...]).astype(BF16)
    pair = 2 * LANES
    r_i = lax.broadcasted_iota(jnp.int32, (pair, pair), 0)
    c_i = lax.broadcasted_iota(jnp.int32, (pair, pair), 1)
    shift = dh.bit_length() - 1
    same_head = jnp.where(lax.shift_right_logical(r_i, shift) == lax.shift_right_logical(c_i, shift),
                          1.0, 0.0).astype(BF16)

    def head_norm(y, gain):
        sq = y * y
        hi = sq.astype(BF16)
        lo = (sq - hi.astype(F32)).astype(BF16)
        ms = (_dot(hi, same_head) + _dot(lo, same_head)) * (1.0 / dh)
        return y * lax.rsqrt(ms + RMS_EPS) * gain

    qg = jnp.concatenate([qg_ref[...], qg_ref[...]], axis=1)
    kg = jnp.concatenate([kg_ref[...], kg_ref[...]], axis=1)
    q = _dot(xn, w_ref[:, 0:nq])
    for s in range(nq // pair):
        sl = slice(s * pair, (s + 1) * pair)
        q_ref[:, sl] = head_norm(q[:, sl], qg).astype(BF16)
    k = _dot(xn, w_ref[:, nq:nq + nk])
    for s in range(nk // pair):
        sl = slice(s * pair, (s + 1) * pair)
        k_ref[:, sl] = head_norm(k[:, sl], kg)
    vt_ref[0] = _dot_nt(wvt_ref[...], xn)


def _attn_proj(x3, g, w, wvt, qg, kg, dh, nq, nk):
    b, s, d = x3.shape
    t = b * s
    nv = wvt.shape[0]
    tm = _row_tile(s, ROWS_SINGLE)
    nt = s // tm
    row = lambda n: pl.BlockSpec((tm, n), lambda i: (i, 0))
    return pl.pallas_call(
        functools.partial(_attn_proj_body, dh=dh, nq=nq, nk=nk),
        grid=(t // tm,),
        in_specs=[row(d), _resident((1, d)), _resident(w.shape), _resident(wvt.shape),
                  _resident((1, LANES)), _resident((1, LANES))],
        out_specs=[row(nq), row(nk), pl.BlockSpec((1, nv, tm), lambda i: (i // nt, 0, i % nt))],
        out_shape=[jax.ShapeDtypeStruct((t, nq), BF16), jax.ShapeDtypeStruct((t, nk), F32),
                   jax.ShapeDtypeStruct((b, nv, s), F32)],
        compiler_params=_cparams(1),
        name="attn_proj",
    )(x3.reshape(t, d), g.reshape(1, d), w, wvt, qg, kg)


def _bias_body(rel_ref, o_ref, *, tq, key_major):
    nk = 2 * WINDOW
    shape, q_ax = ((nk, tq), 1) if key_major else ((tq, nk), 0)
    qi = lax.broadcasted_iota(jnp.int32, shape, q_ax)
    ki = lax.broadcasted_iota(jnp.int32, shape, 1 - q_ax)
    dist = qi + WINDOW - ki
    max_exact = NUM_BUCKETS // 2
    d = jnp.maximum(dist, 0)
    log_ratio = (jnp.log(jnp.maximum(d, 1).astype(F32) / max_exact)
                 / math.log(MAX_DISTANCE / max_exact))
    large = jnp.minimum(max_exact + (log_ratio * (NUM_BUCKETS - max_exact)).astype(jnp.int32),
                        NUM_BUCKETS - 1)
    bucket = jnp.where(d < max_exact, d, large)
    in_window = (dist >= 0) & (dist < WINDOW)
    visible = (in_window & (ki >= WINDOW), in_window)
    for h in range(A_HEADS_Q):
        bias = jnp.zeros(shape, F32)
        for bkt in range(NUM_BUCKETS):
            bias = jnp.where(bucket == bkt, rel_ref[bkt, h], bias)
        for variant in range(2):
            tile = jnp.where(visible[variant], bias, -jnp.inf)
            if key_major:
                o_ref[variant, h // 2, :, (h % 2) * tq:(h % 2 + 1) * tq] = tile
            else:
                o_ref[variant, h] = tile


def _bias_table(rel_bias, tq, key_major):
    nk = 2 * WINDOW
    shape = (2, A_HEADS_Q // 2, nk, 2 * tq) if key_major else (2, A_HEADS_Q, tq, nk)
    return pl.pallas_call(
        functools.partial(_bias_body, tq=tq, key_major=key_major),
        grid=(1,),
        in_specs=[pl.BlockSpec(memory_space=pltpu.SMEM)],
        out_specs=pl.BlockSpec(shape, lambda i: (0, 0, 0, 0)),
        out_shape=jax.ShapeDtypeStruct(shape, F32),
        compiler_params=_cparams(1),
        name="rel_bias_table",
    )(rel_bias)


def _attn_tile_items(sink_ref, q_ref, kp_ref, ko_ref, vp_ref, vo_ref, bias_ref, first_of_seq, write, dh):
    blk = WINDOW
    n_blk = ko_ref.shape[0] // blk
    slots_per_kv = A_GROUP // 2
    n_slots = A_HEADS_Q // 2
    lane = lax.broadcasted_iota(jnp.int32, (blk, LANES), 1)
    lo_half = lane < dh
    left = lax.broadcasted_iota(jnp.int32, (1, 2 * blk), 1) < blk
    ones = jnp.ones((dh, 2 * blk), BF16)
    sinks = [jnp.where(left, sink_ref[2 * i], sink_ref[2 * i + 1]) for i in range(n_slots)]
    items = []
    for j in range(n_blk):
        st = {}
        own = slice(j * blk, (j + 1) * blk)
        before = slice((j - 1) * blk, j * blk)
        variant = jnp.where(first_of_seq, 0, 1) if j == 0 else 1

        def scores(st=st, j=j, own=own, before=before, variant=variant):
            st["vts"], st["sts"] = [], []
            for g in range(A_HEADS_KV):
                sl = slice(g * LANES, (g + 1) * LANES)
                rows = slice(g * dh, (g + 1) * dh)
                k_prev = kp_ref[:, sl] if j == 0 else ko_ref[before, sl]
                v_prev = vp_ref[0, rows, :] if j == 0 else vo_ref[0, rows, before]
                kk = jnp.concatenate([k_prev, ko_ref[own, sl]], axis=0).astype(BF16)
                vt = jnp.concatenate([v_prev, vo_ref[0, rows, own]], axis=1).astype(BF16)
                st["vts"].append(jnp.concatenate([vt, ones], axis=0))
                for pr in range(slots_per_kv):
                    slot_i = g * slots_per_kv + pr
                    slot = q_ref[own, slot_i * LANES:(slot_i + 1) * LANES]
                    zero = jnp.zeros_like(slot)
                    q2 = jnp.concatenate([jnp.where(lo_half, slot, zero), jnp.where(lo_half, zero, slot)],
                                         axis=0)
                    st["sts"].append(_dot_nt(kk, q2) + bias_ref[variant, slot_i])

        def softmax(st=st):
            st["ms"] = [jnp.maximum(jnp.max(st["sts"][i], axis=0, keepdims=True), sinks[i])
                        for i in range(n_slots)]
            st["pts"] = [jnp.exp(st["sts"][i] - st["ms"][i]).astype(BF16) for i in range(n_slots)]

        def values(st=st):
            st["oes"] = [_dot(st["vts"][i // slots_per_kv], st["pts"][i]) for i in range(n_slots)]

        def finish(st=st, j=j):
            heads_t = []
            for i in range(n_slots):
                oe = st["oes"][i]
                ot = oe[0:dh, :] / (oe[dh:dh + 1, :] + jnp.exp(sinks[i] - st["ms"][i]))
                heads_t += [ot[:, 0:blk], ot[:, blk:2 * blk]]
            write(j, jnp.concatenate(heads_t, axis=0).T.astype(BF16))

        items += [scores, softmax, values, finish]
    return items


def _attn_ffn_body(sink_ref, q_ref, kp_ref, ko_ref, vp_ref, vo_ref, bias_ref, wa_ref,
                   x_ref, g_ref, wgu_ref, wd_ref, o_ref, act_ref, att_ref, *, d_ff, tf, dh, tiles_per_seq):
    i = pl.program_id(0)

    @pl.when(i == 0)
    def _():
        att_ref[...] = jnp.zeros_like(att_ref)

    cur = lax.rem(i, 2)
    first_of_seq = lax.rem(jnp.minimum(i, pl.num_programs(0) - 2), tiles_per_seq) == 0

    def write(j, tile):
        att_ref[cur, j * WINDOW:(j + 1) * WINDOW, :] = tile

    side = _attn_tile_items(sink_ref, q_ref, kp_ref, ko_ref, vp_ref, vo_ref, bias_ref,
                            first_of_seq, write, dh)
    get_x = lambda: x_ref[...] + _dot(att_ref[1 - cur], wa_ref[...])
    _interleave(_ffn_items(get_x, g_ref, wgu_ref, wd_ref, o_ref, act_ref, d_ff, tf), side,
                side_before_last=True)


def _attn_ffn(x, q, k, vt, bias, sinks, wa, g, wgu, wd, dh):
    t, d = x.shape
    b, vd, s = vt.shape
    d_ff = wd.shape[0]
    qd, kd = q.shape[1], k.shape[1]
    blk = WINDOW
    tm = _row_tile(s, ROWS_FUSED)
    nt, tps, bpt = t // tm, s // tm, tm // blk
    att_tile = lambda i: jnp.minimum(i, nt - 1)
    ffn_tile = lambda i: jnp.maximum(i - 1, 0)
    ffn_row = lambda n: pl.BlockSpec((tm, n), lambda i: (ffn_tile(i), 0))
    att_row = lambda n: pl.BlockSpec((tm, n), lambda i: (att_tile(i), 0))
    return pl.pallas_call(
        functools.partial(_attn_ffn_body, d_ff=d_ff, tf=FFN_TF, dh=dh, tiles_per_seq=tps),
        grid=(nt + 1,),
        in_specs=[pl.BlockSpec(memory_space=pltpu.SMEM),
                  att_row(qd),
                  pl.BlockSpec((blk, kd), lambda i: (jnp.maximum(att_tile(i) * bpt - 1, 0), 0)),
                  att_row(kd),
                  pl.BlockSpec((1, vd, blk), lambda i: (att_tile(i) // tps,
                                                        0, jnp.maximum(att_tile(i) % tps * bpt - 1, 0))),
                  pl.BlockSpec((1, vd, tm), lambda i: (att_tile(i) // tps, 0, att_tile(i) % tps)),
                  _resident(bias.shape), _resident(wa.shape),
                  ffn_row(d), _resident((1, d)), _resident((d, 2 * d_ff)), _resident((d_ff, d))],
        out_specs=ffn_row(d),
        out_shape=jax.ShapeDtypeStruct((t, d), F32),
        scratch_shapes=[pltpu.VMEM((tm, d_ff), BF16), pltpu.VMEM((2, tm, qd), BF16)],
        compiler_params=_cparams(1),
        name="attn_ffn",
    )(sinks, q, k, k, vt, vt, bias, wa, x, g.reshape(1, d), wgu, wd)


def _attn_sample_body(sink_ref, q_ref, kc_ref, kn_ref, vc_ref, vn_ref, bias_ref,
                      o_ref, ko_ref, vo_ref, *, bt, tq):
    dh = q_ref.shape[-1]
    fresh = lax.broadcasted_iota(jnp.int32, (dh, WINDOW), 1) >= WINDOW - tq
    sinks = [jnp.concatenate([jnp.full((tq, 1), sink_ref[g * A_GROUP + j], F32) for j in range(A_GROUP)],
                             axis=0) for g in range(A_HEADS_KV)]

    def slide(win_ref, new_ref, b, g):
        new = pltpu.roll(new_ref[g], (WINDOW - tq - b * tq) % WINDOW, axis=1)
        return jnp.where(fresh, new, pltpu.roll(win_ref[b, g], WINDOW - tq, axis=1))

    units = [(b, g) for b in range(bt) for g in range(A_HEADS_KV)]
    us = range(len(units))
    q = [q_ref[b, g] for b, g in units]
    k_win = [slide(kc_ref, kn_ref, b, g) for b, g in units]
    v_win = [slide(vc_ref, vn_ref, b, g) for b, g in units]
    for u, (b, g) in enumerate(units):
        ko_ref[b, g] = k_win[u]
        vo_ref[b, g] = v_win[u]
    s_old = [_dot(q[u], kc_ref[b, g].astype(BF16)) + bias_ref[0, g] for u, (b, g) in enumerate(units)]
    s_new = [_dot(q[u], k_win[u].astype(BF16)) + bias_ref[1, g] for u, (b, g) in enumerate(units)]
    m = [jnp.maximum(jnp.maximum(jnp.max(s_old[u], axis=-1, keepdims=True),
                                 jnp.max(s_new[u], axis=-1, keepdims=True)), sinks[g])
         for u, (b, g) in enumerate(units)]
    p_old = [jnp.exp(s_old[u] - m[u]) for u in us]
    p_new = [jnp.exp(s_new[u] - m[u]) for u in us]
    den = [jnp.sum(p_old[u], axis=-1, keepdims=True) + jnp.sum(p_new[u], axis=-1, keepdims=True)
           + jnp.exp(sinks[g] - m[u]) for u, (b, g) in enumerate(units)]
    out = [_dot_nt((p_old[u] / den[u]).astype(BF16), vc_ref[b, g].astype(BF16))
           + _dot_nt((p_new[u] / den[u]).astype(BF16), v_win[u].astype(BF16))
           for u, (b, g) in enumerate(units)]
    for u, (b, g) in enumerate(units):
        o_ref[b, g] = out[u].astype(BF16)


def _attn_sample(q, kn, vn, k_win, v_win, bias, sinks, tq):
    b, n_kv, rows, dh = q.shape
    assert LANES % tq == 0
    bt = LANES // tq
    assert b % bt == 0
    lead = lambda *tail: pl.BlockSpec((bt,) + tail, lambda i: (i, 0, 0, 0))
    new = pl.BlockSpec((n_kv, dh, LANES), lambda i: (0, 0, i))
    return pl.pallas_call(
        functools.partial(_attn_sample_body, bt=bt, tq=tq),
        grid=(b // bt,),
        in_specs=[pl.BlockSpec(memory_space=pltpu.SMEM),
                  lead(n_kv, rows, dh), lead(n_kv, dh, WINDOW), new, lead(n_kv, dh, WINDOW), new,
                  pl.BlockSpec(bias.shape, lambda i: (0, 0, 0, 0))],
        out_specs=[lead(n_kv, rows, dh), lead(n_kv, dh, WINDOW), lead(n_kv, dh, WINDOW)],
        out_shape=[jax.ShapeDtypeStruct(q.shape, BF16),
                   jax.ShapeDtypeStruct(k_win.shape, F32),
                   jax.ShapeDtypeStruct(v_win.shape, F32)],
        compiler_params=_cparams(1),
        name="attn_sample",
    )(sinks, q, k_win, kn, v_win, vn, bias)


def _swa_layer(x3, buffers, p):
    b, s, d = x3.shape
    dh = p["attn_dh"]
    kd = A_HEADS_KV * dh
    x = x3.reshape(b * s, d)
    kv_shape = (b, WINDOW, A_HEADS_KV, dh)
    if buffers is None:
        nq, nk = A_HEADS_Q * dh, A_HEADS_KV * LANES
        q, k, vt = _attn_proj(x3, p["mix_norm1"], p["attn_w_prompt"], p["attn_wvt"],
                              p["attn_qg"], p["attn_kg"], dh, nq, nk)
        bias = _bias_table(p["rel_bias"], WINDOW, key_major=True)
        new_k = k.reshape(b, s, nk)[:, s - WINDOW:].reshape(b, WINDOW, A_HEADS_KV, LANES)[..., :dh]
        new_v = jnp.swapaxes(vt[:, :, s - WINDOW:], 1, 2)
        mixed = dict(kind="attn", args=(q, k, vt, bias, p["attn_sinks"], p["attn_w_out"]), dh=dh)
    else:
        k_buf, v_buf = buffers
        nq, nk = A_HEADS_Q * dh, kd
        q, k, vt = _attn_proj(x.reshape(1, b * s, d), p["mix_norm1"], p["attn_w_sample"], p["attn_wvt"],
                              p["attn_qg"], p["attn_kg"], dh, nq, nk)
        rows = A_GROUP * s
        q = jnp.transpose(q.reshape(b, s, A_HEADS_KV, A_GROUP, dh), (0, 2, 3, 1, 4))
        table = _bias_table(p["rel_bias"], s, key_major=False)[1].reshape(A_HEADS_KV, rows, 2 * WINDOW)
        own = jnp.pad(table[..., WINDOW:WINDOW + s], ((0, 0), (0, 0), (WINDOW - s, 0)),
                      constant_values=-jnp.inf)
        bias = jnp.stack([table[..., :WINDOW], own])
        to_lanes = lambda a: jnp.transpose(a, (0, 2, 3, 1))
        att, new_k, new_v = _attn_sample(q.reshape(b, A_HEADS_KV, rows, dh),
                                         k.T.reshape(A_HEADS_KV, dh, b * s), vt[0].reshape(A_HEADS_KV, dh, b * s),
                                         to_lanes(k_buf), to_lanes(v_buf), bias, p["attn_sinks"], s)
        new_k, new_v = (jnp.transpose(a, (0, 3, 1, 2)) for a in (new_k, new_v))
        att = jnp.transpose(att.reshape(b, A_HEADS_KV, A_GROUP, s, dh), (0, 3, 1, 2, 4))
        mixed = (att.reshape(b * s, nq), p["attn_w_out"])
    return mixed, (new_k.reshape(kv_shape), new_v.reshape(kv_shape))


def _attn_weights(w_in, w_out, q_norm, k_norm, dh):
    d = w_in.shape[0]
    qd, kd = A_HEADS_Q * dh, A_HEADS_KV * dh
    wk = w_in[:, qd:qd + kd].reshape(d, A_HEADS_KV, dh)
    k_dup = jnp.concatenate([wk, wk], axis=-1).reshape(d, A_HEADS_KV * LANES)
    w_prompt = jnp.concatenate([w_in[:, :qd], k_dup], axis=1).astype(BF16)
    qg = jnp.concatenate([q_norm, q_norm]).reshape(1, LANES) * (dh ** -0.5)
    kg = jnp.concatenate([k_norm, k_norm]).reshape(1, LANES)
    return {"attn_w_prompt": w_prompt, "attn_w_sample": w_in[:, :qd + kd].astype(BF16),
            "attn_wvt": w_in[:, qd + kd:].T.astype(BF16),
            "attn_w_out": w_out.astype(BF16),
            "attn_qg": qg, "attn_kg": kg}


def _trunk(x3, mlstm_state, swa_buffers, p, ffn_bf16):
    b, s, d = x3.shape
    stream = mlstm_state is not None

    def ffn(x, name, layer, mixed=None):
        norm = p[name + "_norm"][layer]
        if stream:
            y, *ffn_bf16[name, layer] = _ffn_stream(x, norm, *p[name + "_f32"], layer, pre=mixed)
            return y
        w = (norm, *ffn_bf16[name, layer])
        if mixed is None:
            return _ffn(x, *w)
        if mixed["kind"] == "attn":
            return _attn_ffn(x, *mixed["args"], *w, mixed["dh"])
        return _mlstm_ffn(x, *mixed["args"], *w, mixed["chunk"])

    x = ffn(x3.reshape(b * s, d), "ffn1", 0)
    mixed, new_mlstm = _mlstm_layer(x.reshape(b, s, d), mlstm_state, p)
    x = ffn(x, "ffn2", 0, mixed)
    if new_mlstm is None:
        x, c1, n1, m1 = x
        new_mlstm = (c1, n1, m1.reshape(b, M_HEADS))
    x = ffn(x, "ffn1", 1)
    mixed, new_swa = _swa_layer(x.reshape(b, s, d), swa_buffers, p)
    x = ffn(x, "ffn2", 1, mixed)
    return x.reshape(b, s, d), new_mlstm, new_swa


def kernel(x_prompt, x_sample, state_mlstm_C, state_mlstm_n, state_mlstm_m, cache_swa_k, cache_swa_v,
           ffn1_norm, ffn1_w_gate_up, ffn1_w_down, mix_norm, ffn2_norm, ffn2_w_gate_up, ffn2_w_down,
           mlstm_w_in, mlstm_b_gates, mlstm_out_norm, mlstm_w_out,
           attn_w_in, attn_q_norm, attn_k_norm, attn_sinks, rel_bias, attn_w_out):
    d = x_prompt.shape[-1]
    vd = mlstm_w_out.shape[0]
    n_gates = 2 * M_HEADS
    qk = (mlstm_w_in.shape[1] - 2 * vd - n_gates) // 2
    dh = attn_q_norm.shape[0]
    assert 2 * dh == LANES
    w_gates = jnp.zeros((d, LANES), BF16).at[:, :n_gates].set(mlstm_w_in[:, 2 * qk + 2 * vd:].astype(BF16))
    b_gates = jnp.zeros((1, LANES), F32).at[0, :n_gates].set(mlstm_b_gates)
    p = {
        "ffn1_norm": ffn1_norm, "ffn2_norm": ffn2_norm,
        "ffn1_f32": (ffn1_w_gate_up, ffn1_w_down), "ffn2_f32": (ffn2_w_gate_up, ffn2_w_down),
        "mix_norm0": mix_norm[0], "mix_norm1": mix_norm[1],
        "mlstm_qk": qk, "mlstm_vd": vd,
        "mlstm_w_main": mlstm_w_in[:, :2 * qk + 2 * vd].astype(BF16),
        "mlstm_w_gates": w_gates, "mlstm_b_gates": b_gates,
        "mlstm_out_norm": mlstm_out_norm, "mlstm_w_out": mlstm_w_out.astype(BF16),
        "attn_dh": dh, "attn_sinks": attn_sinks, "rel_bias": rel_bias,
    }
    p.update(_attn_weights(attn_w_in, attn_w_out, attn_q_norm, attn_k_norm, dh))
    ffn_bf16 = {}
    y_s, (c_s, n_s, m_s), (k_s, v_s) = _trunk(
        x_sample, (state_mlstm_C, state_mlstm_n, state_mlstm_m), (cache_swa_k, cache_swa_v), p, ffn_bf16)
    y_p, (c_p, n_p, m_p), (k_p, v_p) = _trunk(x_prompt, None, None, p, ffn_bf16)
    return (y_p, y_s, c_p, n_p, m_p, k_p, v_p, c_s, n_s, m_s, k_s, v_s)
```

```python
import functools
import math

import jax
import jax.numpy as jnp
from jax import lax
from jax.experimental import pallas as pl
from jax.experimental.pallas import tpu as pltpu

F32 = jnp.float32
BF16 = jnp.bfloat16

M_HEADS = 4
A_HEADS_Q = 16
A_HEADS_KV = 4
A_GROUP = A_HEADS_Q // A_HEADS_KV
WINDOW = 128
NUM_BUCKETS = 32
MAX_DISTANCE = 128
FFN_RESIDUAL = 0.5
RMS_EPS = 1e-6
LANES = 128
VMEM_LIMIT = 56 * 1024 * 1024
ROWS_SINGLE = 1024
ROWS_FUSED = 512
CELL_BATCH = 16


def _cparams(n_axes):
    return pltpu.CompilerParams(
        dimension_semantics=("arbitrary",) * n_axes, vmem_limit_bytes=VMEM_LIMIT)


def _resident(shape):
    nd = len(shape)
    return pl.BlockSpec(shape, lambda *_: (0,) * nd, pipeline_mode=pl.Buffered(1))


def _rms(x, g):
    return x * lax.rsqrt(jnp.mean(x * x, axis=-1, keepdims=True) + RMS_EPS) * g


def _dot(a, b):
    return jnp.dot(a, b, preferred_element_type=F32)


def _dot_nt(a, b):
    return lax.dot_general(a, b, (((1,), (1,)), ((), ())), preferred_element_type=F32)


def _dot_tn(a, b):
    return lax.dot_general(a, b, (((0,), (0,)), ((), ())), preferred_element_type=F32)


def _row_tile(t, pref):
    tm = min(t, pref)
    assert t % tm == 0
    return tm


FFN_TF = 256


STREAM_IN_BUFS = 3
STREAM_OUT_BUFS = 2


def _ffn_stream_body(*refs, has_pre, layer, d_ff, tf):
    if has_pre:
        a_ref, wa_ref, *refs = refs
    (x_ref, g_ref, wgu_hbm, wd_hbm, o_ref, wgu_out, wd_out,
     wg_buf, wu_buf, wd_buf, wgu_obuf, wd_obuf, in_sem, out_sem) = refs
    n_chunks = d_ff // tf

    def copies_in(c):
        s = c % STREAM_IN_BUFS
        cols = lambda lo: pl.ds(lo, tf)
        return (pltpu.make_async_copy(wgu_hbm.at[layer, :, cols(c * tf)], wg_buf.at[s], in_sem.at[0, s]),
                pltpu.make_async_copy(wgu_hbm.at[layer, :, cols(d_ff + c * tf)], wu_buf.at[s], in_sem.at[1, s]),
                pltpu.make_async_copy(wd_hbm.at[layer, cols(c * tf), :], wd_buf.at[s], in_sem.at[2, s]))

    def copies_out(c):
        s = c % STREAM_OUT_BUFS
        return (pltpu.make_async_copy(wgu_obuf.at[s], wgu_out.at[:, pl.ds(c * 2 * tf, 2 * tf)], out_sem.at[0, s]),
                pltpu.make_async_copy(wd_obuf.at[s], wd_out.at[pl.ds(c * tf, tf), :], out_sem.at[1, s]))

    for c in range(min(STREAM_IN_BUFS, n_chunks)):
        for cp in copies_in(c):
            cp.start()
    y = x_ref[...] + _dot(a_ref[...], wa_ref[...]) if has_pre else x_ref[...]
    xn = _rms(y, g_ref[...]).astype(BF16)
    o_ref[...] = y
    for c in range(n_chunks):
        for cp in copies_in(c):
            cp.wait()
        s_in, s_out = c % STREAM_IN_BUFS, c % STREAM_OUT_BUFS
        wg, wu, wd = (buf[s_in].astype(BF16) for buf in (wg_buf, wu_buf, wd_buf))
        if c + STREAM_IN_BUFS < n_chunks:
            for cp in copies_in(c + STREAM_IN_BUFS):
                cp.start()
        if c >= STREAM_OUT_BUFS:
            for cp in copies_out(c - STREAM_OUT_BUFS):
                cp.wait()
        wgu_obuf[s_out] = jnp.concatenate([wg, wu], axis=1)
        wd_obuf[s_out] = wd
        for cp in copies_out(c):
            cp.start()
        gate = _dot(xn, wg)
        up = _dot(xn, wu)
        act = (gate * jax.nn.sigmoid(gate) * up).astype(BF16)
        o_ref[...] = o_ref[...] + FFN_RESIDUAL * _dot(act, wd)
    for c in range(max(0, n_chunks - STREAM_OUT_BUFS), n_chunks):
        for cp in copies_out(c):
            cp.wait()


def _ffn_stream(x, g, wgu_f32, wd_f32, layer, pre=None):
    t, d = x.shape
    d_ff = wd_f32.shape[1]
    tf = FFN_TF
    assert d_ff % tf == 0
    whole = lambda r, c: pl.BlockSpec((r, c), lambda i: (0, 0))
    hbm = pl.BlockSpec(memory_space=pl.ANY)
    pre_specs, pre_args = [], []
    if pre is not None:
        a, wa = pre
        pre_specs, pre_args = [whole(t, a.shape[1]), whole(*wa.shape)], [a, wa]
    return pl.pallas_call(
        functools.partial(_ffn_stream_body, has_pre=pre is not None, layer=layer, d_ff=d_ff, tf=tf),
        grid=(1,),
        in_specs=pre_specs + [whole(t, d), whole(1, d), hbm, hbm],
        out_specs=[whole(t, d), hbm, hbm],
        out_shape=[jax.ShapeDtypeStruct((t, d), F32), jax.ShapeDtypeStruct((d, 2 * d_ff), BF16),
                   jax.ShapeDtypeStruct((d_ff, d), BF16)],
        scratch_shapes=[pltpu.VMEM((STREAM_IN_BUFS, d, tf), F32), pltpu.VMEM((STREAM_IN_BUFS, d, tf), F32),
                        pltpu.VMEM((STREAM_IN_BUFS, tf, d), F32),
                        pltpu.VMEM((STREAM_OUT_BUFS, d, 2 * tf), BF16), pltpu.VMEM((STREAM_OUT_BUFS, tf, d), BF16),
                        pltpu.SemaphoreType.DMA((3, STREAM_IN_BUFS)),
                        pltpu.SemaphoreType.DMA((2, STREAM_OUT_BUFS))],
        compiler_params=_cparams(1),
        name="ffn_stream",
    )(*pre_args, x, g.reshape(1, d), wgu_f32, wd_f32)

def _interleave(main, side, side_before_last=False):
    slots = len(main) - 1 if side_before_last else len(main)
    done = 0
    for j, thunk in enumerate(main):
        thunk()
        upto = min(len(side), -(-(j + 1) * len(side) // slots)) if side_before_last else (
            (j + 1) * len(side) // slots)
        for other in side[done:upto]:
            other()
        done = upto


def _ffn_items(get_x, g_ref, wgu_ref, wd_ref, o_ref, act_ref, d_ff, tf):
    st = {}

    def head():
        y = get_x()
        o_ref[...] = y
        st["xn"] = _rms(y, g_ref[...]).astype(BF16)

    def chunk(lo):
        gate = _dot(st["xn"], wgu_ref[:, 2 * lo:2 * lo + tf])
        up = _dot(st["xn"], wgu_ref[:, 2 * lo + tf:2 * lo + 2 * tf])
        act_ref[:, lo:lo + tf] = (gate * jax.nn.sigmoid(gate) * up).astype(BF16)

    def tail():
        o_ref[...] = o_ref[...] + FFN_RESIDUAL * _dot(act_ref[...], wd_ref[...])

    return [head] + [functools.partial(chunk, lo) for lo in range(0, d_ff, tf)] + [tail]


def _ffn_body(x_ref, g_ref, wgu_ref, wd_ref, o_ref, act_ref, *, d_ff, tf):
    for thunk in _ffn_items(lambda: x_ref[...], g_ref, wgu_ref, wd_ref, o_ref, act_ref, d_ff, tf):
        thunk()


def _ffn(x, g, wgu, wd):
    t, d = x.shape
    d_ff = wd.shape[0]
    tm = _row_tile(t, ROWS_SINGLE)
    assert d_ff % FFN_TF == 0
    row = lambda n: pl.BlockSpec((tm, n), lambda i: (i, 0))
    return pl.pallas_call(
        functools.partial(_ffn_body, d_ff=d_ff, tf=FFN_TF),
        grid=(t // tm,),
        in_specs=[row(d), _resident((1, d)), _resident((d, 2 * d_ff)), _resident((d_ff, d))],
        out_specs=row(d),
        out_shape=jax.ShapeDtypeStruct((t, d), F32),
        scratch_shapes=[pltpu.VMEM((tm, d_ff), BF16)],
        compiler_params=_cparams(1),
        name="ffn",
    )(x, g.reshape(1, d), wgu, wd)


def _mlstm_proj_body(x_ref, g_ref, w_ref, wg_ref, bg_ref,
                     q_ref, k_ref, v_ref, og_ref, gcol_ref, grow_ref, *, qk, vd, dk, chunk):
    h_n = M_HEADS
    xn = _rms(x_ref[...], g_ref[...]).astype(BF16)
    tm = xn.shape[0]

    shift = chunk.bit_length() - 1
    r_i = lax.broadcasted_iota(jnp.int32, (LANES, LANES), 0)
    c_i = lax.broadcasted_iota(jnp.int32, (LANES, LANES), 1)
    same_chunk = lax.shift_right_logical(r_i, shift) == lax.shift_right_logical(c_i, shift)
    tri = jnp.where((r_i >= c_i) & same_chunk, 1.0, 0.0).astype(BF16)

    gates = _dot(xn, wg_ref[...]) + bg_ref[...]
    q_ref[...] = _dot(xn, w_ref[:, 0:qk]).astype(BF16)
    lane = lax.broadcasted_iota(jnp.int32, (tm, LANES), 1)
    log_f = jnp.where((lane >= h_n) & (lane < 2 * h_n), jax.nn.log_sigmoid(gates), 0.0)
    hi = log_f.astype(BF16)
    lo = (log_f - hi.astype(F32)).astype(BF16)
    k_ref[...] = (_dot(xn, w_ref[:, qk:2 * qk]) * (dk ** -0.5)).astype(BF16)
    hi_lo = jnp.concatenate([hi, lo], axis=1)
    csum = jnp.concatenate([_dot(tri, hi_lo[r:r + LANES]) for r in range(0, tm, LANES)], axis=0)
    gcol = jnp.where(lane < h_n, gates, csum[:, :LANES] + csum[:, LANES:])
    gcol_ref[...] = gcol
    og_ref[...] = jax.nn.sigmoid(_dot(xn, w_ref[:, 2 * qk + vd:2 * qk + 2 * vd])).astype(BF16)
    g_t = gcol.T
    grow_ref[0] = jnp.concatenate([g_t[0:h_n] - g_t[h_n:2 * h_n], g_t[h_n:2 * h_n]], axis=0)
    v_ref[...] = _dot(xn, w_ref[:, 2 * qk:2 * qk + vd]).astype(BF16)


def _mlstm_proj(x3, g, w_main, w_gates, b_gates, qk, vd, chunk):
    b, s, d = x3.shape
    t = b * s
    tm = _row_tile(s, ROWS_SINGLE)
    nt = s // tm
    assert chunk & (chunk - 1) == 0 and LANES % chunk == 0 and tm % LANES == 0
    row = lambda n: pl.BlockSpec((tm, n), lambda i: (i, 0))
    return pl.pallas_call(
        functools.partial(_mlstm_proj_body, qk=qk, vd=vd, dk=qk // M_HEADS, chunk=chunk),
        grid=(t // tm,),
        in_specs=[row(d), _resident((1, d)), _resident(w_main.shape),
                  _resident(w_gates.shape), _resident((1, LANES))],
        out_specs=[row(qk), row(qk), row(vd), row(vd), row(LANES),
                   pl.BlockSpec((1, 2 * M_HEADS, tm), lambda i: (i // nt, 0, i % nt))],
        out_shape=[jax.ShapeDtypeStruct((t, qk), BF16), jax.ShapeDtypeStruct((t, qk), BF16),
                   jax.ShapeDtypeStruct((t, vd), BF16), jax.ShapeDtypeStruct((t, vd), BF16),
                   jax.ShapeDtypeStruct((t, LANES), F32),
                   jax.ShapeDtypeStruct((b, 2 * M_HEADS, s), F32)],
        compiler_params=_cparams(1),
        name="mlstm_proj",
    )(x3.reshape(t, d), g.reshape(1, d), w_main, w_gates, b_gates)


def _mlstm_chunk_items(units, load, state, emit, rows, update_early):
    r_i = lax.broadcasted_iota(jnp.int32, (rows, rows), 0)
    c_i = lax.broadcasted_iota(jnp.int32, (rows, rows), 1)
    causal = r_i >= c_i
    us = range(len(units))
    st = {}

    def products():
        st["x"] = x = [load(u) for u in units]
        st["old"] = [state[u] for u in units]
        st["qk"] = [_dot_nt(x[i]["q"], x[i]["k"]) for i in us]
        st["qc"] = [_dot(x[i]["q"], st["old"][i][0].astype(BF16)) for i in us]

    def weights():
        x, old = st["x"], st["old"]
        log_d = [jnp.where(causal, x[i]["b_c"] + x[i]["imb"], -jnp.inf) for i in us]
        log_inter = [x[i]["b_c"] + old[i][2] for i in us]
        st["m_t"] = m_t = [jnp.maximum(log_inter[i], jnp.max(log_d[i], axis=-1, keepdims=True)) for i in us]
        st["inter"] = [jnp.exp(log_inter[i] - m_t[i]) for i in us]
        st["s"] = [st["qk"][i] * jnp.exp(log_d[i] - m_t[i]) for i in us]

    def hidden():
        x, old, s, inter, m_t = st["x"], st["old"], st["s"], st["inter"], st["m_t"]
        s_v = [_dot(s[i].astype(BF16), x[i]["v"]) for i in us]
        den = [inter[i] * jnp.sum(x[i]["q"].astype(F32) * old[i][1], axis=-1, keepdims=True)
               + jnp.sum(s[i], axis=-1, keepdims=True) for i in us]
        st["hid"] = [(inter[i] * st["qc"][i] + s_v[i]) / jnp.maximum(jnp.abs(den[i]), jnp.exp(-m_t[i]))
                     for i in us]

    def update():
        x, old = st["x"], st["old"]
        b_last = [x[i]["b_c"][rows - 1:rows, :] for i in us]
        log_w = [b_last[i] - x[i]["b_c"] + x[i]["i_c"] for i in us]
        m_new = [jnp.maximum(b_last[i] + old[i][2], jnp.max(log_w[i], axis=0, keepdims=True)) for i in us]
        decay = [jnp.exp(b_last[i] + old[i][2] - m_new[i]) for i in us]
        kw = [jnp.exp(log_w[i] - m_new[i]) * x[i]["k"].astype(F32) for i in us]
        kw_v = [_dot_tn(kw[i].astype(BF16), x[i]["v"]) for i in us]
        for i, u in enumerate(units):
            state[u] = (decay[i] * old[i][0] + kw_v[i],
                        decay[i] * old[i][1] + jnp.sum(kw[i], axis=0, keepdims=True), m_new[i])

    def output():
        hid = [st["hid"][i] * lax.rsqrt(jnp.mean(st["hid"][i] * st["hid"][i], axis=-1, keepdims=True) + RMS_EPS)
               for i in us]
        for i, u in enumerate(units):
            emit(u, (hid[i] * st["x"][i]["ng"] * st["x"][i]["og"].astype(F32)).astype(BF16))

    return [products, update, weights, hidden, output] if update_early else [
        products, weights, hidden, update, output]


def _mlstm_cell_body(q_ref, k_ref, v_ref, og_ref, gcol_ref, grow_ref, ng_ref, c0_ref, n0_ref, m0_ref,
                     hg_ref, c_ref, n_ref, m_ref, *, dk, dv, bt):
    h_n = M_HEADS
    rows = q_ref.shape[1]
    units = [(b, h) for b in range(bt) for h in range(h_n)]

    def load(u):
        b, h = u
        kc, vc = slice(h * dk, (h + 1) * dk), slice(h * dv, (h + 1) * dv)
        return dict(q=q_ref[b, :, kc], k=k_ref[b, :, kc], v=v_ref[b, :, vc], og=og_ref[b, :, vc],
                    ng=ng_ref[:, vc], b_c=gcol_ref[b, :, h_n + h:h_n + h + 1], i_c=gcol_ref[b, :, h:h + 1],
                    imb=grow_ref[b, h:h + 1, :])

    def emit(u, value):
        b, h = u
        hg_ref[b, :, h * dv:(h + 1) * dv] = value

    state = {(b, h): (c0_ref[b, h], n0_ref[b, h:h + 1, :], m0_ref[b, :, h:h + 1]) for b, h in units}
    for thunk in _mlstm_chunk_items(units, load, state, emit, rows, update_early=True):
        thunk()
    for b, h in units:
        c_ref[b, h], n_ref[b, h:h + 1, :], m_ref[b, :, h:h + 1] = state[(b, h)]


def _mlstm_cell(q, k, v, og, gcol, grow, norm_g, c0, n0, m0, bt):
    b, s, qk = q.shape
    vd = v.shape[-1]
    dk, dv = qk // M_HEADS, vd // M_HEADS
    assert b % bt == 0
    tok = lambda n: pl.BlockSpec((bt, s, n), lambda i: (i, 0, 0))
    st_c = pl.BlockSpec((bt, M_HEADS, dk, dv), lambda i: (i, 0, 0, 0))
    st_n = pl.BlockSpec((bt, M_HEADS, dk), lambda i: (i, 0, 0))
    st_m = pl.BlockSpec((bt, 1, M_HEADS), lambda i: (i, 0, 0))
    return pl.pallas_call(
        functools.partial(_mlstm_cell_body, dk=dk, dv=dv, bt=bt),
        grid=(b // bt,),
        in_specs=[tok(qk), tok(qk), tok(vd), tok(vd), tok(LANES),
                  pl.BlockSpec((bt, 2 * M_HEADS, s), lambda i: (i, 0, 0)),
                  pl.BlockSpec((1, vd), lambda i: (0, 0)), st_c, st_n, st_m],
        out_specs=[tok(vd), st_c, st_n, st_m],
        out_shape=[jax.ShapeDtypeStruct((b, s, vd), BF16),
                   jax.ShapeDtypeStruct((b, M_HEADS, dk, dv), F32),
                   jax.ShapeDtypeStruct((b, M_HEADS, dk), F32),
                   jax.ShapeDtypeStruct((b, 1, M_HEADS), F32)],
        compiler_params=_cparams(1),
        name="mlstm_cell",
    )(q, k, v, og, gcol, grow, norm_g.reshape(1, vd), c0, n0, m0)


def _mlstm_ffn_body(q_ref, k_ref, v_ref, og_ref, gcol_ref, grow_ref, ng_ref, wa_ref,
                    x_ref, g_ref, wgu_ref, wd_ref, o_ref, c_ref, n_ref, m_ref, act_ref, hg_ref,
                    *, d_ff, tf, dk, dv, chunk, tiles_per_seq):
    h_n = M_HEADS
    i = pl.program_id(0)
    last = pl.num_programs(0) - 1
    live = i < last
    first_of_seq = lax.rem(jnp.minimum(i, last - 1), tiles_per_seq) == 0

    @pl.when(i == 0)
    def _():
        hg_ref[...] = jnp.zeros_like(hg_ref)

    @pl.when(first_of_seq & live)
    def _():
        c_ref[...] = jnp.zeros_like(c_ref)
        n_ref[...] = jnp.zeros_like(n_ref)
        m_ref[...] = jnp.zeros_like(m_ref)

    cur = lax.rem(i, 2)
    heads = list(range(h_n))
    state = {h: (c_ref[0, h], n_ref[0, h:h + 1, :], m_ref[0, :, h:h + 1]) for h in heads}
    side = []
    for j in range(q_ref.shape[0] // chunk):
        rows = slice(j * chunk, (j + 1) * chunk)

        def load(h, rows=rows):
            kc, vc = slice(h * dk, (h + 1) * dk), slice(h * dv, (h + 1) * dv)
            return dict(q=q_ref[rows, kc], k=k_ref[rows, kc], v=v_ref[rows, vc], og=og_ref[rows, vc],
                        ng=ng_ref[:, vc], b_c=gcol_ref[rows, h_n + h:h_n + h + 1], i_c=gcol_ref[rows, h:h + 1],
                        imb=grow_ref[0, h:h + 1, rows])

        def emit(h, value, rows=rows):
            hg_ref[cur, rows, h * dv:(h + 1) * dv] = value

        side += _mlstm_chunk_items(heads, load, state, emit, chunk, update_early=False)
    get_x = lambda: x_ref[...] + _dot(hg_ref[1 - cur], wa_ref[...])
    _interleave(_ffn_items(get_x, g_ref, wgu_ref, wd_ref, o_ref, act_ref, d_ff, tf), side)

    @pl.when(live)
    def _():
        for h in heads:
            c_ref[0, h], n_ref[0, h:h + 1, :], m_ref[0, :, h:h + 1] = state[h]


def _mlstm_ffn(x, q, k, v, og, gcol, grow, norm_g, wa, g, wgu, wd, chunk):
    t, d = x.shape
    b, _, s = grow.shape
    d_ff = wd.shape[0]
    qk, vd = q.shape[1], v.shape[1]
    dk, dv = qk // M_HEADS, vd // M_HEADS
    tm = _row_tile(s, ROWS_FUSED)
    nt, tps = t // tm, s // tm
    assert tm % chunk == 0
    mix_tile = lambda i: jnp.minimum(i, nt - 1)
    ffn_row = lambda n: pl.BlockSpec((tm, n), lambda i: (jnp.maximum(i - 1, 0), 0))
    mix_row = lambda n: pl.BlockSpec((tm, n), lambda i: (mix_tile(i), 0))
    seq = lambda *tail: pl.BlockSpec((1,) + tail, lambda i: (mix_tile(i) // tps,) + (0,) * len(tail))
    return pl.pallas_call(
        functools.partial(_mlstm_ffn_body, d_ff=d_ff, tf=FFN_TF, dk=dk, dv=dv, chunk=chunk, tiles_per_seq=tps),
        grid=(nt + 1,),
        in_specs=[mix_row(qk), mix_row(qk), mix_row(vd), mix_row(vd), mix_row(LANES),
                  pl.BlockSpec((1, 2 * M_HEADS, tm), lambda i: (mix_tile(i) // tps, 0, mix_tile(i) % tps)),
                  _resident((1, vd)), _resident(wa.shape),
                  ffn_row(d), _resident((1, d)), _resident((d, 2 * d_ff)), _resident((d_ff, d))],
        out_specs=[ffn_row(d), seq(M_HEADS, dk, dv), seq(M_HEADS, dk), seq(1, M_HEADS)],
        out_shape=[jax.ShapeDtypeStruct((t, d), F32),
                   jax.ShapeDtypeStruct((b, M_HEADS, dk, dv), F32),
                   jax.ShapeDtypeStruct((b, M_HEADS, dk), F32),
                   jax.ShapeDtypeStruct((b, 1, M_HEADS), F32)],
        scratch_shapes=[pltpu.VMEM((tm, d_ff), BF16), pltpu.VMEM((2, tm, vd), BF16)],
        compiler_params=_cparams(1),
        name="mlstm_ffn",
    )(q, k, v, og, gcol, grow, norm_g.reshape(1, vd), wa, x, g.reshape(1, d), wgu, wd)


def _mlstm_layer(x3, state, p):
    b, s, d = x3.shape
    qk, vd = p["mlstm_qk"], p["mlstm_vd"]
    proj = functools.partial(_mlstm_proj, g=p["mix_norm0"], w_main=p["mlstm_w_main"],
                             w_gates=p["mlstm_w_gates"], b_gates=p["mlstm_b_gates"], qk=qk, vd=vd)
    if state is None:
        chunk = min(s, 128)
        q, k, v, og, gcol, grow = proj(x3, chunk=chunk)
        return dict(kind="mlstm", args=(q, k, v, og, gcol, grow, p["mlstm_out_norm"], p["mlstm_w_out"]),
                    chunk=chunk), None
    c0, n0, m0 = state
    q, k, v, og, gcol, grow = proj(x3.reshape(1, b * s, d), chunk=s)
    grow = jnp.swapaxes(grow.reshape(2 * M_HEADS, b, s), 0, 1)
    sh = lambda a: a.reshape(b, s, a.shape[-1])
    hg, c1, n1, m1 = _mlstm_cell(sh(q), sh(k), sh(v), sh(og), sh(gcol), grow, p["mlstm_out_norm"],
                                 c0, n0, m0.reshape(b, 1, M_HEADS), math.gcd(b, CELL_BATCH))
    return (hg.reshape(b * s, vd), p["mlstm_w_out"]), (c1, n1, m1.reshape(b, M_HEADS))


def _attn_proj_body(x_ref, g_ref, w_ref, wvt_ref, qg_ref, kg_ref, q_ref, k_ref, vt_ref, *, dh, nq, nk):
    xn = _rms(x_ref[...], g_ref[...]).astype(BF16)
    pair = 2 * LANES
    r_i = lax.broadcasted_iota(jnp.int32, (pair, pair), 0)
    c_i = lax.broadcasted_iota(jnp.int32, (pair, pair), 1)
    shift = dh.bit_length() - 1
    same_head = jnp.where(lax.shift_right_logical(r_i, shift) == lax.shift_right_logical(c_i, shift),
                          1.0, 0.0).astype(BF16)

    def head_norm(y, gain):
        sq = y * y
        hi = sq.astype(BF16)
        lo = (sq - hi.astype(F32)).astype(BF16)
        ms = (_dot(hi, same_head) + _dot(lo, same_head)) * (1.0 / dh)
        return y * lax.rsqrt(ms + RMS_EPS) * gain

    qg = jnp.concatenate([qg_ref[...], qg_ref[...]], axis=1)
    kg = jnp.concatenate([kg_ref[...], kg_ref[...]], axis=1)
    q = _dot(xn, w_ref[:, 0:nq])
    for s in range(nq // pair):
        sl = slice(s * pair, (s + 1) * pair)
        q_ref[:, sl] = head_norm(q[:, sl], qg).astype(BF16)
    k = _dot(xn, w_ref[:, nq:nq + nk])
    for s in range(nk // pair):
        sl = slice(s * pair, (s + 1) * pair)
        k_ref[:, sl] = head_norm(k[:, sl], kg)
    vt_ref[0] = _dot_nt(wvt_ref[...], xn)


def _attn_proj(x3, g, w, wvt, qg, kg, dh, nq, nk):
    b, s, d = x3.shape
    t = b * s
    nv = wvt.shape[0]
    tm = _row_tile(s, ROWS_SINGLE)
    nt = s // tm
    row = lambda n: pl.BlockSpec((tm, n), lambda i: (i, 0))
    return pl.pallas_call(
        functools.partial(_attn_proj_body, dh=dh, nq=nq, nk=nk),
        grid=(t // tm,),
        in_specs=[row(d), _resident((1, d)), _resident(w.shape), _resident(wvt.shape),
                  _resident((1, LANES)), _resident((1, LANES))],
        out_specs=[row(nq), row(nk), pl.BlockSpec((1, nv, tm), lambda i: (i // nt, 0, i % nt))],
        out_shape=[jax.ShapeDtypeStruct((t, nq), BF16), jax.ShapeDtypeStruct((t, nk), F32),
                   jax.ShapeDtypeStruct((b, nv, s), F32)],
        compiler_params=_cparams(1),
        name="attn_proj",
    )(x3.reshape(t, d), g.reshape(1, d), w, wvt, qg, kg)


def _bias_body(rel_ref, o_ref, *, tq, key_major):
    nk = 2 * WINDOW
    shape, q_ax = ((nk, tq), 1) if key_major else ((tq, nk), 0)
    qi = lax.broadcasted_iota(jnp.int32, shape, q_ax)
    ki = lax.broadcasted_iota(jnp.int32, shape, 1 - q_ax)
    dist = qi + WINDOW - ki
    max_exact = NUM_BUCKETS // 2
    d = jnp.maximum(dist, 0)
    log_ratio = (jnp.log(jnp.maximum(d, 1).astype(F32) / max_exact)
                 / math.log(MAX_DISTANCE / max_exact))
    large = jnp.minimum(max_exact + (log_ratio * (NUM_BUCKETS - max_exact)).astype(jnp.int32),
                        NUM_BUCKETS - 1)
    bucket = jnp.where(d < max_exact, d, large)
    in_window = (dist >= 0) & (dist < WINDOW)
    visible = (in_window & (ki >= WINDOW), in_window)
    for h in range(A_HEADS_Q):
        bias = jnp.zeros(shape, F32)
        for bkt in range(NUM_BUCKETS):
            bias = jnp.where(bucket == bkt, rel_ref[bkt, h], bias)
        for variant in range(2):
            tile = jnp.where(visible[variant], bias, -jnp.inf)
            if key_major:
                o_ref[variant, h // 2, :, (h % 2) * tq:(h % 2 + 1) * tq] = tile
            else:
                o_ref[variant, h] = tile


def _bias_table(rel_bias, tq, key_major):
    nk = 2 * WINDOW
    shape = (2, A_HEADS_Q // 2, nk, 2 * tq) if key_major else (2, A_HEADS_Q, tq, nk)
    return pl.pallas_call(
        functools.partial(_bias_body, tq=tq, key_major=key_major),
        grid=(1,),
        in_specs=[pl.BlockSpec(memory_space=pltpu.SMEM)],
        out_specs=pl.BlockSpec(shape, lambda i: (0, 0, 0, 0)),
        out_shape=jax.ShapeDtypeStruct(shape, F32),
        compiler_params=_cparams(1),
        name="rel_bias_table",
    )(rel_bias)


def _attn_tile_items(sink_ref, q_ref, kp_ref, ko_ref, vp_ref, vo_ref, bias_ref, first_of_seq, write, dh):
    blk = WINDOW
    n_blk = ko_ref.shape[0] // blk
    slots_per_kv = A_GROUP // 2
    n_slots = A_HEADS_Q // 2
    lane = lax.broadcasted_iota(jnp.int32, (blk, LANES), 1)
    lo_half = lane < dh
    left = lax.broadcasted_iota(jnp.int32, (1, 2 * blk), 1) < blk
    ones = jnp.ones((dh, 2 * blk), BF16)
    sinks = [jnp.where(left, sink_ref[2 * i], sink_ref[2 * i + 1]) for i in range(n_slots)]
    items = []
    for j in range(n_blk):
        st = {}
        own = slice(j * blk, (j + 1) * blk)
        before = slice((j - 1) * blk, j * blk)
        variant = jnp.where(first_of_seq, 0, 1) if j == 0 else 1

        def scores(st=st, j=j, own=own, before=before, variant=variant):
            st["vts"], st["sts"] = [], []
            for g in range(A_HEADS_KV):
                sl = slice(g * LANES, (g + 1) * LANES)
                rows = slice(g * dh, (g + 1) * dh)
                k_prev = kp_ref[:, sl] if j == 0 else ko_ref[before, sl]
                v_prev = vp_ref[0, rows, :] if j == 0 else vo_ref[0, rows, before]
                kk = jnp.concatenate([k_prev, ko_ref[own, sl]], axis=0).astype(BF16)
                vt = jnp.concatenate([v_prev, vo_ref[0, rows, own]], axis=1).astype(BF16)
                st["vts"].append(jnp.concatenate([vt, ones], axis=0))
                for pr in range(slots_per_kv):
                    slot_i = g * slots_per_kv + pr
                    slot = q_ref[own, slot_i * LANES:(slot_i + 1) * LANES]
                    zero = jnp.zeros_like(slot)
                    q2 = jnp.concatenate([jnp.where(lo_half, slot, zero), jnp.where(lo_half, zero, slot)],
                                         axis=0)
                    st["sts"].append(_dot_nt(kk, q2) + bias_ref[variant, slot_i])

        def softmax(st=st):
            st["ms"] = [jnp.maximum(jnp.max(st["sts"][i], axis=0, keepdims=True), sinks[i])
                        for i in range(n_slots)]
            st["pts"] = [jnp.exp(st["sts"][i] - st["ms"][i]).astype(BF16) for i in range(n_slots)]

        def values(st=st):
            st["oes"] = [_dot(st["vts"][i // slots_per_kv], st["pts"][i]) for i in range(n_slots)]

        def finish(st=st, j=j):
            heads_t = []
            for i in range(n_slots):
                oe = st["oes"][i]
                ot = oe[0:dh, :] / (oe[dh:dh + 1, :] + jnp.exp(sinks[i] - st["ms"][i]))
                heads_t += [ot[:, 0:blk], ot[:, blk:2 * blk]]
            write(j, jnp.concatenate(heads_t, axis=0).T.astype(BF16))

        items += [scores, softmax, values, finish]
    return items


def _attn_ffn_body(sink_ref, q_ref, kp_ref, ko_ref, vp_ref, vo_ref, bias_ref, wa_ref,
                   x_ref, g_ref, wgu_ref, wd_ref, o_ref, act_ref, att_ref, *, d_ff, tf, dh, tiles_per_seq):
    i = pl.program_id(0)

    @pl.when(i == 0)
    def _():
        att_ref[...] = jnp.zeros_like(att_ref)

    cur = lax.rem(i, 2)
    first_of_seq = lax.rem(jnp.minimum(i, pl.num_programs(0) - 2), tiles_per_seq) == 0

    def write(j, tile):
        att_ref[cur, j * WINDOW:(j + 1) * WINDOW, :] = tile

    side = _attn_tile_items(sink_ref, q_ref, kp_ref, ko_ref, vp_ref, vo_ref, bias_ref,
                            first_of_seq, write, dh)
    get_x = lambda: x_ref[...] + _dot(att_ref[1 - cur], wa_ref[...])
    _interleave(_ffn_items(get_x, g_ref, wgu_ref, wd_ref, o_ref, act_ref, d_ff, tf), side,
                side_before_last=True)


def _attn_ffn(x, q, k, vt, bias, sinks, wa, g, wgu, wd, dh):
    t, d = x.shape
    b, vd, s = vt.shape
    d_ff = wd.shape[0]
    qd, kd = q.shape[1], k.shape[1]
    blk = WINDOW
    tm = _row_tile(s, ROWS_FUSED)
    nt, tps, bpt = t // tm, s // tm, tm // blk
    att_tile = lambda i: jnp.minimum(i, nt - 1)
    ffn_tile = lambda i: jnp.maximum(i - 1, 0)
    ffn_row = lambda n: pl.BlockSpec((tm, n), lambda i: (ffn_tile(i), 0))
    att_row = lambda n: pl.BlockSpec((tm, n), lambda i: (att_tile(i), 0))
    return pl.pallas_call(
        functools.partial(_attn_ffn_body, d_ff=d_ff, tf=FFN_TF, dh=dh, tiles_per_seq=tps),
        grid=(nt + 1,),
        in_specs=[pl.BlockSpec(memory_space=pltpu.SMEM),
                  att_row(qd),
                  pl.BlockSpec((blk, kd), lambda i: (jnp.maximum(att_tile(i) * bpt - 1, 0), 0)),
                  att_row(kd),
                  pl.BlockSpec((1, vd, blk), lambda i: (att_tile(i) // tps,
                                                        0, jnp.maximum(att_tile(i) % tps * bpt - 1, 0))),
                  pl.BlockSpec((1, vd, tm), lambda i: (att_tile(i) // tps, 0, att_tile(i) % tps)),
                  _resident(bias.shape), _resident(wa.shape),
                  ffn_row(d), _resident((1, d)), _resident((d, 2 * d_ff)), _resident((d_ff, d))],
        out_specs=ffn_row(d),
        out_shape=jax.ShapeDtypeStruct((t, d), F32),
        scratch_shapes=[pltpu.VMEM((tm, d_ff), BF16), pltpu.VMEM((2, tm, qd), BF16)],
        compiler_params=_cparams(1),
        name="attn_ffn",
    )(sinks, q, k, k, vt, vt, bias, wa, x, g.reshape(1, d), wgu, wd)


def _attn_sample_body(sink_ref, q_ref, kc_ref, kn_ref, vc_ref, vn_ref, bias_ref,
                      o_ref, ko_ref, vo_ref, *, bt, tq):
    dh = q_ref.shape[-1]
    fresh = lax.broadcasted_iota(jnp.int32, (dh, WINDOW), 1) >= WINDOW - tq
    sinks = [jnp.concatenate([jnp.full((tq, 1), sink_ref[g * A_GROUP + j], F32) for j in range(A_GROUP)],
                             axis=0) for g in range(A_HEADS_KV)]

    def slide(win_ref, new_ref, b, g):
        new = pltpu.roll(new_ref[g], (WINDOW - tq - b * tq) % WINDOW, axis=1)
        return jnp.where(fresh, new, pltpu.roll(win_ref[b, g], WINDOW - tq, axis=1))

    units = [(b, g) for b in range(bt) for g in range(A_HEADS_KV)]
    us = range(len(units))
    q = [q_ref[b, g] for b, g in units]
    k_win = [slide(kc_ref, kn_ref, b, g) for b, g in units]
    v_win = [slide(vc_ref, vn_ref, b, g) for b, g in units]
    for u, (b, g) in enumerate(units):
        ko_ref[b, g] = k_win[u]
        vo_ref[b, g] = v_win[u]
    s_old = [_dot(q[u], kc_ref[b, g].astype(BF16)) + bias_ref[0, g] for u, (b, g) in enumerate(units)]
    s_new = [_dot(q[u], k_win[u].astype(BF16)) + bias_ref[1, g] for u, (b, g) in enumerate(units)]
    m = [jnp.maximum(jnp.maximum(jnp.max(s_old[u], axis=-1, keepdims=True),
                                 jnp.max(s_new[u], axis=-1, keepdims=True)), sinks[g])
         for u, (b, g) in enumerate(units)]
    p_old = [jnp.exp(s_old[u] - m[u]) for u in us]
    p_new = [jnp.exp(s_new[u] - m[u]) for u in us]
    den = [jnp.sum(p_old[u], axis=-1, keepdims=True) + jnp.sum(p_new[u], axis=-1, keepdims=True)
           + jnp.exp(sinks[g] - m[u]) for u, (b, g) in enumerate(units)]
    out = [_dot_nt((p_old[u] / den[u]).astype(BF16), vc_ref[b, g].astype(BF16))
           + _dot_nt((p_new[u] / den[u]).astype(BF16), v_win[u].astype(BF16))
           for u, (b, g) in enumerate(units)]
    for u, (b, g) in enumerate(units):
        o_ref[b, g] = out[u].astype(BF16)


def _attn_sample(q, kn, vn, k_win, v_win, bias, sinks, tq):
    b, n_kv, rows, dh = q.shape
    assert LANES % tq == 0
    bt = LANES // tq
    assert b % bt == 0
    lead = lambda *tail: pl.BlockSpec((bt,) + tail, lambda i: (i, 0, 0, 0))
    new = pl.BlockSpec((n_kv, dh, LANES), lambda i: (0, 0, i))
    return pl.pallas_call(
        functools.partial(_attn_sample_body, bt=bt, tq=tq),
        grid=(b // bt,),
        in_specs=[pl.BlockSpec(memory_space=pltpu.SMEM),
                  lead(n_kv, rows, dh), lead(n_kv, dh, WINDOW), new, lead(n_kv, dh, WINDOW), new,
                  pl.BlockSpec(bias.shape, lambda i: (0, 0, 0, 0))],
        out_specs=[lead(n_kv, rows, dh), lead(n_kv, dh, WINDOW), lead(n_kv, dh, WINDOW)],
        out_shape=[jax.ShapeDtypeStruct(q.shape, BF16),
                   jax.ShapeDtypeStruct(k_win.shape, F32),
                   jax.ShapeDtypeStruct(v_win.shape, F32)],
        compiler_params=_cparams(1),
        name="attn_sample",
    )(sinks, q, k_win, kn, v_win, vn, bias)


def _swa_layer(x3, buffers, p):
    b, s, d = x3.shape
    dh = p["attn_dh"]
    kd = A_HEADS_KV * dh
    x = x3.reshape(b * s, d)
    kv_shape = (b, WINDOW, A_HEADS_KV, dh)
    if buffers is None:
        nq, nk = A_HEADS_Q * dh, A_HEADS_KV * LANES
        q, k, vt = _attn_proj(x3, p["mix_norm1"], p["attn_w_prompt"], p["attn_wvt"],
                              p["attn_qg"], p["attn_kg"], dh, nq, nk)
        bias = _bias_table(p["rel_bias"], WINDOW, key_major=True)
        new_k = k.reshape(b, s, nk)[:, s - WINDOW:].reshape(b, WINDOW, A_HEADS_KV, LANES)[..., :dh]
        new_v = jnp.swapaxes(vt[:, :, s - WINDOW:], 1, 2)
        mixed = dict(kind="attn", args=(q, k, vt, bias, p["attn_sinks"], p["attn_w_out"]), dh=dh)
    else:
        k_buf, v_buf = buffers
        nq, nk = A_HEADS_Q * dh, kd
        q, k, vt = _attn_proj(x.reshape(1, b * s, d), p["mix_norm1"], p["attn_w_sample"], p["attn_wvt"],
                              p["attn_qg"], p["attn_kg"], dh, nq, nk)
        rows = A_GROUP * s
        q = jnp.transpose(q.reshape(b, s, A_HEADS_KV, A_GROUP, dh), (0, 2, 3, 1, 4))
        table = _bias_table(p["rel_bias"], s, key_major=False)[1].reshape(A_HEADS_KV, rows, 2 * WINDOW)
        own = jnp.pad(table[..., WINDOW:WINDOW + s], ((0, 0), (0, 0), (WINDOW - s, 0)),
                      constant_values=-jnp.inf)
        bias = jnp.stack([table[..., :WINDOW], own])
        to_lanes = lambda a: jnp.transpose(a, (0, 2, 3, 1))
        att, new_k, new_v = _attn_sample(q.reshape(b, A_HEADS_KV, rows, dh),
                                         k.T.reshape(A_HEADS_KV, dh, b * s), vt[0].reshape(A_HEADS_KV, dh, b * s),
                                         to_lanes(k_buf), to_lanes(v_buf), bias, p["attn_sinks"], s)
        new_k, new_v = (jnp.transpose(a, (0, 3, 1, 2)) for a in (new_k, new_v))
        att = jnp.transpose(att.reshape(b, A_HEADS_KV, A_GROUP, s, dh), (0, 3, 1, 2, 4))
        mixed = (att.reshape(b * s, nq), p["attn_w_out"])
    return mixed, (new_k.reshape(kv_shape), new_v.reshape(kv_shape))


def _attn_weights(w_in, w_out, q_norm, k_norm, dh):
    d = w_in.shape[0]
    qd, kd = A_HEADS_Q * dh, A_HEADS_KV * dh
    wk = w_in[:, qd:qd + kd].reshape(d, A_HEADS_KV, dh)
    k_dup = jnp.concatenate([wk, wk], axis=-1).reshape(d, A_HEADS_KV * LANES)
    w_prompt = jnp.concatenate([w_in[:, :qd], k_dup], axis=1).astype(BF16)
    qg = jnp.concatenate([q_norm, q_norm]).reshape(1, LANES) * (dh ** -0.5)
    kg = jnp.concatenate([k_norm, k_norm]).reshape(1, LANES)
    return {"attn_w_prompt": w_prompt, "attn_w_sample": w_in[:, :qd + kd].astype(BF16),
            "attn_wvt": w_in[:, qd + kd:].T.astype(BF16),
            "attn_w_out": w_out.astype(BF16),
            "attn_qg": qg, "attn_kg": kg}


def _trunk(x3, mlstm_state, swa_buffers, p, ffn_bf16):
    b, s, d = x3.shape
    stream = mlstm_state is not None

    def ffn(x, name, layer, mixed=None):
        norm = p[name + "_norm"][layer]
        if stream:
            y, *ffn_bf16[name, layer] = _ffn_stream(x, norm, *p[name + "_f32"], layer, pre=mixed)
            return y
        w = (norm, *ffn_bf16[name, layer])
        if mixed is None:
            return _ffn(x, *w)
        if mixed["kind"] == "attn":
            return _attn_ffn(x, *mixed["args"], *w, mixed["dh"])
        return _mlstm_ffn(x, *mixed["args"], *w, mixed["chunk"])

    x = ffn(x3.reshape(b * s, d), "ffn1", 0)
    mixed, new_mlstm = _mlstm_layer(x.reshape(b, s, d), mlstm_state, p)
    x = ffn(x, "ffn2", 0, mixed)
    if new_mlstm is None:
        x, c1, n1, m1 = x
        new_mlstm = (c1, n1, m1.reshape(b, M_HEADS))
    x = ffn(x, "ffn1", 1)
    mixed, new_swa = _swa_layer(x.reshape(b, s, d), swa_buffers, p)
    x = ffn(x, "ffn2", 1, mixed)
    return x.reshape(b, s, d), new_mlstm, new_swa


def kernel(x_prompt, x_sample, state_mlstm_C, state_mlstm_n, state_mlstm_m, cache_swa_k, cache_swa_v,
           ffn1_norm, ffn1_w_gate_up, ffn1_w_down, mix_norm, ffn2_norm, ffn2_w_gate_up, ffn2_w_down,
           mlstm_w_in, mlstm_b_gates, mlstm_out_norm, mlstm_w_out,
           attn_w_in, attn_q_norm, attn_k_norm, attn_sinks, rel_bias, attn_w_out):
    d = x_prompt.shape[-1]
    vd = mlstm_w_out.shape[0]
    n_gates = 2 * M_HEADS
    qk = (mlstm_w_in.shape[1] - 2 * vd - n_gates) // 2
    dh = attn_q_norm.shape[0]
    assert 2 * dh == LANES
    w_gates = jnp.zeros((d, LANES), BF16).at[:, :n_gates].set(mlstm_w_in[:, 2 * qk + 2 * vd:].astype(BF16))
    b_gates = jnp.zeros((1, LANES), F32).at[0, :n_gates].set(mlstm_b_gates)
    p = {
        "ffn1_norm": ffn1_norm, "ffn2_norm": ffn2_norm,
        "ffn1_f32": (ffn1_w_gate_up, ffn1_w_down), "ffn2_f32": (ffn2_w_gate_up, ffn2_w_down),
        "mix_norm0": mix_norm[0], "mix_norm1": mix_norm[1],
        "mlstm_qk": qk, "mlstm_vd": vd,
        "mlstm_w_main": mlstm_w_in[:, :2 * qk + 2 * vd].astype(BF16),
        "mlstm_w_gates": w_gates, "mlstm_b_gates": b_gates,
        "mlstm_out_norm": mlstm_out_norm, "mlstm_w_out": mlstm_w_out.astype(BF16),
        "attn_dh": dh, "attn_sinks": attn_sinks, "rel_bias": rel_bias,
    }
    p.update(_attn_weights(attn_w_in, attn_w_out, attn_q_norm, attn_k_norm, dh))
    ffn_bf16 = {}
    y_s, (c_s, n_s, m_s), (k_s, v_s) = _trunk(
        x_sample, (state_mlstm_C, state_mlstm_n, state_mlstm_m), (cache_swa_k, cache_swa_v), p, ffn_bf16)
    y_p, (c_p, n_p, m_p), (k_p, v_p) = _trunk(x_prompt, None, None, p, ffn_bf16)
    return (y_p, y_s, c_p, n_p, m_p, k_p, v_p, c_s, n_s, m_s, k_s, v_s)
```

```python
import functools
import math

import jax
import jax.numpy as jnp
from jax import lax
from jax.experimental import pallas as pl
from jax.experimental.pallas import tpu as pltpu

F32 = jnp.float32
BF16 = jnp.bfloat16

M_HEADS = 4
A_HEADS_Q = 16
A_HEADS_KV = 4
A_GROUP = A_HEADS_Q // A_HEADS_KV
WINDOW = 128
NUM_BUCKETS = 32
MAX_DISTANCE = 128
FFN_RESIDUAL = 0.5
RMS_EPS = 1e-6
LANES = 128
VMEM_LIMIT = 56 * 1024 * 1024
ROWS_SINGLE = 1024
ROWS_FUSED = 512
CELL_BATCH = 16


def _cparams(n_axes):
    return pltpu.CompilerParams(
        dimension_semantics=("arbitrary",) * n_axes, vmem_limit_bytes=VMEM_LIMIT)


def _resident(shape):
    nd = len(shape)
    return pl.BlockSpec(shape, lambda *_: (0,) * nd, pipeline_mode=pl.Buffered(1))


def _rms(x, g):
    return x * lax.rsqrt(jnp.mean(x * x, axis=-1, keepdims=True) + RMS_EPS) * g


def _dot(a, b):
    return jnp.dot(a, b, preferred_element_type=F32)


def _dot_nt(a, b):
    return lax.dot_general(a, b, (((1,), (1,)), ((), ())), preferred_element_type=F32)


def _dot_tn(a, b):
    return lax.dot_general(a, b, (((0,), (0,)), ((), ())), preferred_element_type=F32)


def _row_tile(t, pref):
    tm = min(t, pref)
    assert t % tm == 0
    return tm


FFN_TF = 256


STREAM_IN_BUFS = 3
STREAM_OUT_BUFS = 2


def _ffn_stream_body(*refs, has_pre, layer, d_ff, tf):
    if has_pre:
        a_ref, wa_ref, *refs = refs
    (x_ref, g_ref, wgu_hbm, wd_hbm, o_ref, wgu_out, wd_out,
     wg_buf, wu_buf, wd_buf, wgu_obuf, wd_obuf, in_sem, out_sem) = refs
    n_chunks = d_ff // tf

    def copies_in(c):
        s = c % STREAM_IN_BUFS
        cols = lambda lo: pl.ds(lo, tf)
        return (pltpu.make_async_copy(wgu_hbm.at[layer, :, cols(c * tf)], wg_buf.at[s], in_sem.at[0, s]),
                pltpu.make_async_copy(wgu_hbm.at[layer, :, cols(d_ff + c * tf)], wu_buf.at[s], in_sem.at[1, s]),
                pltpu.make_async_copy(wd_hbm.at[layer, cols(c * tf), :], wd_buf.at[s], in_sem.at[2, s]))

    def copies_out(c):
        s = c % STREAM_OUT_BUFS
        return (pltpu.make_async_copy(wgu_obuf.at[s], wgu_out.at[:, pl.ds(c * 2 * tf, 2 * tf)], out_sem.at[0, s]),
                pltpu.make_async_copy(wd_obuf.at[s], wd_out.at[pl.ds(c * tf, tf), :], out_sem.at[1, s]))

    def start_all(copies):
        for n, cp in enumerate(copies):
            cp.start(priority=n % 2)

    for c in range(min(STREAM_IN_BUFS, n_chunks)):
        start_all(copies_in(c))
    y = x_ref[...] + _dot(a_ref[...], wa_ref[...]) if has_pre else x_ref[...]
    xn = _rms(y, g_ref[...]).astype(BF16)
    o_ref[...] = y
    for c in range(n_chunks):
        for cp in copies_in(c):
            cp.wait()
        s_in, s_out = c % STREAM_IN_BUFS, c % STREAM_OUT_BUFS
        wg, wu, wd = (buf[s_in].astype(BF16) for buf in (wg_buf, wu_buf, wd_buf))
        if c + STREAM_IN_BUFS < n_chunks:
            start_all(copies_in(c + STREAM_IN_BUFS))
        if c >= STREAM_OUT_BUFS:
            for cp in copies_out(c - STREAM_OUT_BUFS):
                cp.wait()
        wgu_obuf[s_out] = jnp.concatenate([wg, wu], axis=1)
        wd_obuf[s_out] = wd
        start_all(copies_out(c))
        gate = _dot(xn, wg)
        up = _dot(xn, wu)
        act = (gate * jax.nn.sigmoid(gate) * up).astype(BF16)
        o_ref[...] = o_ref[...] + FFN_RESIDUAL * _dot(act, wd)
    for c in range(max(0, n_chunks - STREAM_OUT_BUFS), n_chunks):
        for cp in copies_out(c):
            cp.wait()


def _ffn_stream(x, g, wgu_f32, wd_f32, layer, pre=None):
    t, d = x.shape
    d_ff = wd_f32.shape[1]
    tf = FFN_TF
    assert d_ff % tf == 0
    whole = lambda r, c: pl.BlockSpec((r, c), lambda i: (0, 0))
    hbm = pl.BlockSpec(memory_space=pl.ANY)
    pre_specs, pre_args = [], []
    if pre is not None:
        a, wa = pre
        pre_specs, pre_args = [whole(t, a.shape[1]), whole(*wa.shape)], [a, wa]
    return pl.pallas_call(
        functools.partial(_ffn_stream_body, has_pre=pre is not None, layer=layer, d_ff=d_ff, tf=tf),
        grid=(1,),
        in_specs=pre_specs + [whole(t, d), whole(1, d), hbm, hbm],
        out_specs=[whole(t, d), hbm, hbm],
        out_shape=[jax.ShapeDtypeStruct((t, d), F32), jax.ShapeDtypeStruct((d, 2 * d_ff), BF16),
                   jax.ShapeDtypeStruct((d_ff, d), BF16)],
        scratch_shapes=[pltpu.VMEM((STREAM_IN_BUFS, d, tf), F32), pltpu.VMEM((STREAM_IN_BUFS, d, tf), F32),
                        pltpu.VMEM((STREAM_IN_BUFS, tf, d), F32),
                        pltpu.VMEM((STREAM_OUT_BUFS, d, 2 * tf), BF16), pltpu.VMEM((STREAM_OUT_BUFS, tf, d), BF16),
                        pltpu.SemaphoreType.DMA((3, STREAM_IN_BUFS)),
                        pltpu.SemaphoreType.DMA((2, STREAM_OUT_BUFS))],
        compiler_params=_cparams(1),
        name="ffn_stream",
    )(*pre_args, x, g.reshape(1, d), wgu_f32, wd_f32)

def _interleave(main, side, side_before_last=False):
    slots = len(main) - 1 if side_before_last else len(main)
    done = 0
    for j, thunk in enumerate(main):
        thunk()
        upto = min(len(side), -(-(j + 1) * len(side) // slots)) if side_before_last else (
            (j + 1) * len(side) // slots)
        for other in side[done:upto]:
            other()
        done = upto


def _ffn_items(get_x, g_ref, wgu_ref, wd_ref, o_ref, act_ref, d_ff, tf):
    st = {}

    def head():
        y = get_x()
        o_ref[...] = y
        st["xn"] = _rms(y, g_ref[...]).astype(BF16)

    def chunk(lo):
        gate = _dot(st["xn"], wgu_ref[:, 2 * lo:2 * lo + tf])
        up = _dot(st["xn"], wgu_ref[:, 2 * lo + tf:2 * lo + 2 * tf])
        act_ref[:, lo:lo + tf] = (gate * jax.nn.sigmoid(gate) * up).astype(BF16)

    def tail():
        o_ref[...] = o_ref[...] + FFN_RESIDUAL * _dot(act_ref[...], wd_ref[...])

    return [head] + [functools.partial(chunk, lo) for lo in range(0, d_ff, tf)] + [tail]


def _ffn_body(x_ref, g_ref, wgu_ref, wd_ref, o_ref, act_ref, *, d_ff, tf):
    for thunk in _ffn_items(lambda: x_ref[...], g_ref, wgu_ref, wd_ref, o_ref, act_ref, d_ff, tf):
        thunk()


def _ffn(x, g, wgu, wd):
    t, d = x.shape
    d_ff = wd.shape[0]
    tm = _row_tile(t, ROWS_SINGLE)
    assert d_ff % FFN_TF == 0
    row = lambda n: pl.BlockSpec((tm, n), lambda i: (i, 0))
    return pl.pallas_call(
        functools.partial(_ffn_body, d_ff=d_ff, tf=FFN_TF),
        grid=(t // tm,),
        in_specs=[row(d), _resident((1, d)), _resident((d, 2 * d_ff)), _resident((d_ff, d))],
        out_specs=row(d),
        out_shape=jax.ShapeDtypeStruct((t, d), F32),
        scratch_shapes=[pltpu.VMEM((tm, d_ff), BF16)],
        compiler_params=_cparams(1),
        name="ffn",
    )(x, g.reshape(1, d), wgu, wd)


def _mlstm_proj_body(x_ref, g_ref, w_ref, wg_ref, bg_ref,
                     q_ref, k_ref, v_ref, og_ref, gcol_ref, grow_ref, *, qk, vd, dk, chunk):
    h_n = M_HEADS
    xn = _rms(x_ref[...], g_ref[...]).astype(BF16)
    tm = xn.shape[0]

    shift = chunk.bit_length() - 1
    r_i = lax.broadcasted_iota(jnp.int32, (LANES, LANES), 0)
    c_i = lax.broadcasted_iota(jnp.int32, (LANES, LANES), 1)
    same_chunk = lax.shift_right_logical(r_i, shift) == lax.shift_right_logical(c_i, shift)
    tri = jnp.where((r_i >= c_i) & same_chunk, 1.0, 0.0).astype(BF16)

    gates = _dot(xn, wg_ref[...]) + bg_ref[...]
    q_ref[...] = _dot(xn, w_ref[:, 0:qk]).astype(BF16)
    lane = lax.broadcasted_iota(jnp.int32, (tm, LANES), 1)
    log_f = jnp.where((lane >= h_n) & (lane < 2 * h_n), jax.nn.log_sigmoid(gates), 0.0)
    hi = log_f.astype(BF16)
    lo = (log_f - hi.astype(F32)).astype(BF16)
    k_ref[...] = (_dot(xn, w_ref[:, qk:2 * qk]) * (dk ** -0.5)).astype(BF16)
    hi_lo = jnp.concatenate([hi, lo], axis=1)
    csum = jnp.concatenate([_dot(tri, hi_lo[r:r + LANES]) for r in range(0, tm, LANES)], axis=0)
    gcol = jnp.where(lane < h_n, gates, csum[:, :LANES] + csum[:, LANES:])
    gcol_ref[...] = gcol
    og_ref[...] = jax.nn.sigmoid(_dot(xn, w_ref[:, 2 * qk + vd:2 * qk + 2 * vd])).astype(BF16)
    g_t = gcol.T
    grow_ref[0] = jnp.concatenate([g_t[0:h_n] - g_t[h_n:2 * h_n], g_t[h_n:2 * h_n]], axis=0)
    v_ref[...] = _dot(xn, w_ref[:, 2 * qk:2 * qk + vd]).astype(BF16)


def _mlstm_proj(x3, g, w_main, w_gates, b_gates, qk, vd, chunk):
    b, s, d = x3.shape
    t = b * s
    tm = _row_tile(s, ROWS_SINGLE)
    nt = s // tm
    assert chunk & (chunk - 1) == 0 and LANES % chunk == 0 and tm % LANES == 0
    row = lambda n: pl.BlockSpec((tm, n), lambda i: (i, 0))
    return pl.pallas_call(
        functools.partial(_mlstm_proj_body, qk=qk, vd=vd, dk=qk // M_HEADS, chunk=chunk),
        grid=(t // tm,),
        in_specs=[row(d), _resident((1, d)), _resident(w_main.shape),
                  _resident(w_gates.shape), _resident((1, LANES))],
        out_specs=[row(qk), row(qk), row(vd), row(vd), row(LANES),
                   pl.BlockSpec((1, 2 * M_HEADS, tm), lambda i: (i // nt, 0, i % nt))],
        out_shape=[jax.ShapeDtypeStruct((t, qk), BF16), jax.ShapeDtypeStruct((t, qk), BF16),
                   jax.ShapeDtypeStruct((t, vd), BF16), jax.ShapeDtypeStruct((t, vd), BF16),
                   jax.ShapeDtypeStruct((t, LANES), F32),
                   jax.ShapeDtypeStruct((b, 2 * M_HEADS, s), F32)],
        compiler_params=_cparams(1),
        name="mlstm_proj",
    )(x3.reshape(t, d), g.reshape(1, d), w_main, w_gates, b_gates)


def _mlstm_chunk_items(units, load, state, emit, rows, update_early):
    r_i = lax.broadcasted_iota(jnp.int32, (rows, rows), 0)
    c_i = lax.broadcasted_iota(jnp.int32, (rows, rows), 1)
    causal = r_i >= c_i
    us = range(len(units))
    st = {}

    def products():
        st["x"] = x = [load(u) for u in units]
        st["old"] = [state[u] for u in units]
        st["qk"] = [_dot_nt(x[i]["q"], x[i]["k"]) for i in us]
        st["qc"] = [_dot(x[i]["q"], st["old"][i][0].astype(BF16)) for i in us]

    def weights():
        x, old = st["x"], st["old"]
        log_d = [jnp.where(causal, x[i]["b_c"] + x[i]["imb"], -jnp.inf) for i in us]
        log_inter = [x[i]["b_c"] + old[i][2] for i in us]
        st["m_t"] = m_t = [jnp.maximum(log_inter[i], jnp.max(log_d[i], axis=-1, keepdims=True)) for i in us]
        st["inter"] = [jnp.exp(log_inter[i] - m_t[i]) for i in us]
        st["s"] = [st["qk"][i] * jnp.exp(log_d[i] - m_t[i]) for i in us]

    def hidden():
        x, old, s, inter, m_t = st["x"], st["old"], st["s"], st["inter"], st["m_t"]
        s_v = [_dot(s[i].astype(BF16), x[i]["v"]) for i in us]
        den = [inter[i] * jnp.sum(x[i]["q"].astype(F32) * old[i][1], axis=-1, keepdims=True)
               + jnp.sum(s[i], axis=-1, keepdims=True) for i in us]
        st["hid"] = [(inter[i] * st["qc"][i] + s_v[i]) / jnp.maximum(jnp.abs(den[i]), jnp.exp(-m_t[i]))
                     for i in us]

    def update():
        x, old = st["x"], st["old"]
        b_last = [x[i]["b_c"][rows - 1:rows, :] for i in us]
        log_w = [b_last[i] - x[i]["b_c"] + x[i]["i_c"] for i in us]
        m_new = [jnp.maximum(b_last[i] + old[i][2], jnp.max(log_w[i], axis=0, keepdims=True)) for i in us]
        decay = [jnp.exp(b_last[i] + old[i][2] - m_new[i]) for i in us]
        kw = [jnp.exp(log_w[i] - m_new[i]) * x[i]["k"].astype(F32) for i in us]
        kw_v = [_dot_tn(kw[i].astype(BF16), x[i]["v"]) for i in us]
        for i, u in enumerate(units):
            state[u] = (decay[i] * old[i][0] + kw_v[i],
                        decay[i] * old[i][1] + jnp.sum(kw[i], axis=0, keepdims=True), m_new[i])

    def output():
        hid = [st["hid"][i] * lax.rsqrt(jnp.mean(st["hid"][i] * st["hid"][i], axis=-1, keepdims=True) + RMS_EPS)
               for i in us]
        for i, u in enumerate(units):
            emit(u, (hid[i] * st["x"][i]["ng"] * st["x"][i]["og"].astype(F32)).astype(BF16))

    return [products, update, weights, hidden, output] if update_early else [
        products, weights, hidden, update, output]


def _mlstm_cell_body(q_ref, k_ref, v_ref, og_ref, gcol_ref, grow_ref, ng_ref, c0_ref, n0_ref, m0_ref,
                     hg_ref, c_ref, n_ref, m_ref, *, dk, dv, bt):
    h_n = M_HEADS
    rows = q_ref.shape[1]
    units = [(b, h) for b in range(bt) for h in range(h_n)]

    def load(u):
        b, h = u
        kc, vc = slice(h * dk, (h + 1) * dk), slice(h * dv, (h + 1) * dv)
        return dict(q=q_ref[b, :, kc], k=k_ref[b, :, kc], v=v_ref[b, :, vc], og=og_ref[b, :, vc],
                    ng=ng_ref[:, vc], b_c=gcol_ref[b, :, h_n + h:h_n + h + 1], i_c=gcol_ref[b, :, h:h + 1],
                    imb=grow_ref[b, h:h + 1, :])

    def emit(u, value):
        b, h = u
        hg_ref[b, :, h * dv:(h + 1) * dv] = value

    state = {(b, h): (c0_ref[b, h], n0_ref[b, h:h + 1, :], m0_ref[b, :, h:h + 1]) for b, h in units}
    for thunk in _mlstm_chunk_items(units, load, state, emit, rows, update_early=True):
        thunk()
    for b, h in units:
        c_ref[b, h], n_ref[b, h:h + 1, :], m_ref[b, :, h:h + 1] = state[(b, h)]


def _mlstm_cell(q, k, v, og, gcol, grow, norm_g, c0, n0, m0, bt):
    b, s, qk = q.shape
    vd = v.shape[-1]
    dk, dv = qk // M_HEADS, vd // M_HEADS
    assert b % bt == 0
    tok = lambda n: pl.BlockSpec((bt, s, n), lambda i: (i, 0, 0))
    st_c = pl.BlockSpec((bt, M_HEADS, dk, dv), lambda i: (i, 0, 0, 0))
    st_n = pl.BlockSpec((bt, M_HEADS, dk), lambda i: (i, 0, 0))
    st_m = pl.BlockSpec((bt, 1, M_HEADS), lambda i: (i, 0, 0))
    return pl.pallas_call(
        functools.partial(_mlstm_cell_body, dk=dk, dv=dv, bt=bt),
        grid=(b // bt,),
        in_specs=[tok(qk), tok(qk), tok(vd), tok(vd), tok(LANES),
                  pl.BlockSpec((bt, 2 * M_HEADS, s), lambda i: (i, 0, 0)),
                  pl.BlockSpec((1, vd), lambda i: (0, 0)), st_c, st_n, st_m],
        out_specs=[tok(vd), st_c, st_n, st_m],
        out_shape=[jax.ShapeDtypeStruct((b, s, vd), BF16),
                   jax.ShapeDtypeStruct((b, M_HEADS, dk, dv), F32),
                   jax.ShapeDtypeStruct((b, M_HEADS, dk), F32),
                   jax.ShapeDtypeStruct((b, 1, M_HEADS), F32)],
        compiler_params=_cparams(1),
        name="mlstm_cell",
    )(q, k, v, og, gcol, grow, norm_g.reshape(1, vd), c0, n0, m0)


def _mlstm_ffn_body(q_ref, k_ref, v_ref, og_ref, gcol_ref, grow_ref, ng_ref, wa_ref,
                    x_ref, g_ref, wgu_ref, wd_ref, o_ref, c_ref, n_ref, m_ref, act_ref, hg_ref,
                    *, d_ff, tf, dk, dv, chunk, tiles_per_seq):
    h_n = M_HEADS
    i = pl.program_id(0)
    last = pl.num_programs(0) - 1
    live = i < last
    first_of_seq = lax.rem(jnp.minimum(i, last - 1), tiles_per_seq) == 0

    @pl.when(i == 0)
    def _():
        hg_ref[...] = jnp.zeros_like(hg_ref)

    @pl.when(first_of_seq & live)
    def _():
        c_ref[...] = jnp.zeros_like(c_ref)
        n_ref[...] = jnp.zeros_like(n_ref)
        m_ref[...] = jnp.zeros_like(m_ref)

    cur = lax.rem(i, 2)
    heads = list(range(h_n))
    state = {h: (c_ref[0, h], n_ref[0, h:h + 1, :], m_ref[0, :, h:h + 1]) for h in heads}
    side = []
    for j in range(q_ref.shape[0] // chunk):
        rows = slice(j * chunk, (j + 1) * chunk)

        def load(h, rows=rows):
            kc, vc = slice(h * dk, (h + 1) * dk), slice(h * dv, (h + 1) * dv)
            return dict(q=q_ref[rows, kc], k=k_ref[rows, kc], v=v_ref[rows, vc], og=og_ref[rows, vc],
                        ng=ng_ref[:, vc], b_c=gcol_ref[rows, h_n + h:h_n + h + 1], i_c=gcol_ref[rows, h:h + 1],
                        imb=grow_ref[0, h:h + 1, rows])

        def emit(h, value, rows=rows):
            hg_ref[cur, rows, h * dv:(h + 1) * dv] = value

        side += _mlstm_chunk_items(heads, load, state, emit, chunk, update_early=False)
    get_x = lambda: x_ref[...] + _dot(hg_ref[1 - cur], wa_ref[...])
    _interleave(_ffn_items(get_x, g_ref, wgu_ref, wd_ref, o_ref, act_ref, d_ff, tf), side)

    @pl.when(live)
    def _():
        for h in heads:
            c_ref[0, h], n_ref[0, h:h + 1, :], m_ref[0, :, h:h + 1] = state[h]


def _mlstm_ffn(x, q, k, v, og, gcol, grow, norm_g, wa, g, wgu, wd, chunk):
    t, d = x.shape
    b, _, s = grow.shape
    d_ff = wd.shape[0]
    qk, vd = q.shape[1], v.shape[1]
    dk, dv = qk // M_HEADS, vd // M_HEADS
    tm = _row_tile(s, ROWS_FUSED)
    nt, tps = t // tm, s // tm
    assert tm % chunk == 0
    mix_tile = lambda i: jnp.minimum(i, nt - 1)
    ffn_row = lambda n: pl.BlockSpec((tm, n), lambda i: (jnp.maximum(i - 1, 0), 0))
    mix_row = lambda n: pl.BlockSpec((tm, n), lambda i: (mix_tile(i), 0))
    seq = lambda *tail: pl.BlockSpec((1,) + tail, lambda i: (mix_tile(i) // tps,) + (0,) * len(tail))
    return pl.pallas_call(
        functools.partial(_mlstm_ffn_body, d_ff=d_ff, tf=FFN_TF, dk=dk, dv=dv, chunk=chunk, tiles_per_seq=tps),
        grid=(nt + 1,),
        in_specs=[mix_row(qk), mix_row(qk), mix_row(vd), mix_row(vd), mix_row(LANES),
                  pl.BlockSpec((1, 2 * M_HEADS, tm), lambda i: (mix_tile(i) // tps, 0, mix_tile(i) % tps)),
                  _resident((1, vd)), _resident(wa.shape),
                  ffn_row(d), _resident((1, d)), _resident((d, 2 * d_ff)), _resident((d_ff, d))],
        out_specs=[ffn_row(d), seq(M_HEADS, dk, dv), seq(M_HEADS, dk), seq(1, M_HEADS)],
        out_shape=[jax.ShapeDtypeStruct((t, d), F32),
                   jax.ShapeDtypeStruct((b, M_HEADS, dk, dv), F32),
                   jax.ShapeDtypeStruct((b, M_HEADS, dk), F32),
                   jax.ShapeDtypeStruct((b, 1, M_HEADS), F32)],
        scratch_shapes=[pltpu.VMEM((tm, d_ff), BF16), pltpu.VMEM((2, tm, vd), BF16)],
        compiler_params=_cparams(1),
        name="mlstm_ffn",
    )(q, k, v, og, gcol, grow, norm_g.reshape(1, vd), wa, x, g.reshape(1, d), wgu, wd)


def _mlstm_layer(x3, state, p):
    b, s, d = x3.shape
    qk, vd = p["mlstm_qk"], p["mlstm_vd"]
    proj = functools.partial(_mlstm_proj, g=p["mix_norm0"], w_main=p["mlstm_w_main"],
                             w_gates=p["mlstm_w_gates"], b_gates=p["mlstm_b_gates"], qk=qk, vd=vd)
    if state is None:
        chunk = min(s, 128)
        q, k, v, og, gcol, grow = proj(x3, chunk=chunk)
        return dict(kind="mlstm", args=(q, k, v, og, gcol, grow, p["mlstm_out_norm"], p["mlstm_w_out"]),
                    chunk=chunk), None
    c0, n0, m0 = state
    q, k, v, og, gcol, grow = proj(x3.reshape(1, b * s, d), chunk=s)
    grow = jnp.swapaxes(grow.reshape(2 * M_HEADS, b, s), 0, 1)
    sh = lambda a: a.reshape(b, s, a.shape[-1])
    hg, c1, n1, m1 = _mlstm_cell(sh(q), sh(k), sh(v), sh(og), sh(gcol), grow, p["mlstm_out_norm"],
                                 c0, n0, m0.reshape(b, 1, M_HEADS), math.gcd(b, CELL_BATCH))
    return (hg.reshape(b * s, vd), p["mlstm_w_out"]), (c1, n1, m1.reshape(b, M_HEADS))


def _attn_proj_body(x_ref, g_ref, w_ref, wvt_ref, qg_ref, kg_ref, q_ref, k_ref, vt_ref, *, dh, nq, nk):
    xn = _rms(x_ref[...], g_ref[...]).astype(BF16)
    pair = 2 * LANES
    r_i = lax.broadcasted_iota(jnp.int32, (pair, pair), 0)
    c_i = lax.broadcasted_iota(jnp.int32, (pair, pair), 1)
    shift = dh.bit_length() - 1
    same_head = jnp.where(lax.shift_right_logical(r_i, shift) == lax.shift_right_logical(c_i, shift),
                          1.0, 0.0).astype(BF16)

    def head_norm(y, gain):
        sq = y * y
        hi = sq.astype(BF16)
        lo = (sq - hi.astype(F32)).astype(BF16)
        ms = (_dot(hi, same_head) + _dot(lo, same_head)) * (1.0 / dh)
        return y * lax.rsqrt(ms + RMS_EPS) * gain

    qg = jnp.concatenate([qg_ref[...], qg_ref[...]], axis=1)
    kg = jnp.concatenate([kg_ref[...], kg_ref[...]], axis=1)
    q = _dot(xn, w_ref[:, 0:nq])
    for s in range(nq // pair):
        sl = slice(s * pair, (s + 1) * pair)
        q_ref[:, sl] = head_norm(q[:, sl], qg).astype(BF16)
    k = _dot(xn, w_ref[:, nq:nq + nk])
    for s in range(nk // pair):
        sl = slice(s * pair, (s + 1) * pair)
        k_ref[:, sl] = head_norm(k[:, sl], kg)
    vt_ref[0] = _dot_nt(wvt_ref[...], xn)


def _attn_proj(x3, g, w, wvt, qg, kg, dh, nq, nk):
    b, s, d = x3.shape
    t = b * s
    nv = wvt.shape[0]
    tm = _row_tile(s, ROWS_SINGLE)
    nt = s // tm
    row = lambda n: pl.BlockSpec((tm, n), lambda i: (i, 0))
    return pl.pallas_call(
        functools.partial(_attn_proj_body, dh=dh, nq=nq, nk=nk),
        grid=(t // tm,),
        in_specs=[row(d), _resident((1, d)), _resident(w.shape), _resident(wvt.shape),
                  _resident((1, LANES)), _resident((1, LANES))],
        out_specs=[row(nq), row(nk), pl.BlockSpec((1, nv, tm), lambda i: (i // nt, 0, i % nt))],
        out_shape=[jax.ShapeDtypeStruct((t, nq), BF16), jax.ShapeDtypeStruct((t, nk), F32),
                   jax.ShapeDtypeStruct((b, nv, s), F32)],
        compiler_params=_cparams(1),
        name="attn_proj",
    )(x3.reshape(t, d), g.reshape(1, d), w, wvt, qg, kg)


def _bias_body(rel_ref, o_ref, *, tq, key_major):
    nk = 2 * WINDOW
    shape, q_ax = ((nk, tq), 1) if key_major else ((tq, nk), 0)
    qi = lax.broadcasted_iota(jnp.int32, shape, q_ax)
    ki = lax.broadcasted_iota(jnp.int32, shape, 1 - q_ax)
    dist = qi + WINDOW - ki
    max_exact = NUM_BUCKETS // 2
    d = jnp.maximum(dist, 0)
    log_ratio = (jnp.log(jnp.maximum(d, 1).astype(F32) / max_exact)
                 / math.log(MAX_DISTANCE / max_exact))
    large = jnp.minimum(max_exact + (log_ratio * (NUM_BUCKETS - max_exact)).astype(jnp.int32),
                        NUM_BUCKETS - 1)
    bucket = jnp.where(d < max_exact, d, large)
    in_window = (dist >= 0) & (dist < WINDOW)
    visible = (in_window & (ki >= WINDOW), in_window)
    for h in range(A_HEADS_Q):
        bias = jnp.zeros(shape, F32)
        for bkt in range(NUM_BUCKETS):
            bias = jnp.where(bucket == bkt, rel_ref[bkt, h], bias)
        for variant in range(2):
            tile = jnp.where(visible[variant], bias, -jnp.inf)
            if key_major:
                o_ref[variant, h // 2, :, (h % 2) * tq:(h % 2 + 1) * tq] = tile
            else:
                o_ref[variant, h] = tile


def _bias_table(rel_bias, tq, key_major):
    nk = 2 * WINDOW
    shape = (2, A_HEADS_Q // 2, nk, 2 * tq) if key_major else (2, A_HEADS_Q, tq, nk)
    return pl.pallas_call(
        functools.partial(_bias_body, tq=tq, key_major=key_major),
        grid=(1,),
        in_specs=[pl.BlockSpec(memory_space=pltpu.SMEM)],
        out_specs=pl.BlockSpec(shape, lambda i: (0, 0, 0, 0)),
        out_shape=jax.ShapeDtypeStruct(shape, F32),
        compiler_params=_cparams(1),
        name="rel_bias_table",
    )(rel_bias)


def _attn_tile_items(sink_ref, q_ref, kp_ref, ko_ref, vp_ref, vo_ref, bias_ref, first_of_seq, write, dh):
    blk = WINDOW
    n_blk = ko_ref.shape[0] // blk
    slots_per_kv = A_GROUP // 2
    n_slots = A_HEADS_Q // 2
    lane = lax.broadcasted_iota(jnp.int32, (blk, LANES), 1)
    lo_half = lane < dh
    left = lax.broadcasted_iota(jnp.int32, (1, 2 * blk), 1) < blk
    ones = jnp.ones((dh, 2 * blk), BF16)
    sinks = [jnp.where(left, sink_ref[2 * i], sink_ref[2 * i + 1]) for i in range(n_slots)]
    items = []
    for j in range(n_blk):
        st = {}
        own = slice(j * blk, (j + 1) * blk)
        before = slice((j - 1) * blk, j * blk)
        variant = jnp.where(first_of_seq, 0, 1) if j == 0 else 1

        def scores(st=st, j=j, own=own, before=before, variant=variant):
            st["vts"], st["sts"] = [], []
            for g in range(A_HEADS_KV):
                sl = slice(g * LANES, (g + 1) * LANES)
                rows = slice(g * dh, (g + 1) * dh)
                k_prev = kp_ref[:, sl] if j == 0 else ko_ref[before, sl]
                v_prev = vp_ref[0, rows, :] if j == 0 else vo_ref[0, rows, before]
                kk = jnp.concatenate([k_prev, ko_ref[own, sl]], axis=0).astype(BF16)
                vt = jnp.concatenate([v_prev, vo_ref[0, rows, own]], axis=1).astype(BF16)
                st["vts"].append(jnp.concatenate([vt, ones], axis=0))
                for pr in range(slots_per_kv):
                    slot_i = g * slots_per_kv + pr
                    slot = q_ref[own, slot_i * LANES:(slot_i + 1) * LANES]
                    zero = jnp.zeros_like(slot)
                    q2 = jnp.concatenate([jnp.where(lo_half, slot, zero), jnp.where(lo_half, zero, slot)],
                                         axis=0)
                    st["sts"].append(_dot_nt(kk, q2) + bias_ref[variant, slot_i])

        def softmax(st=st):
            st["ms"] = [jnp.maximum(jnp.max(st["sts"][i], axis=0, keepdims=True), sinks[i])
                        for i in range(n_slots)]
            st["pts"] = [jnp.exp(st["sts"][i] - st["ms"][i]).astype(BF16) for i in range(n_slots)]

        def values(st=st):
            st["oes"] = [_dot(st["vts"][i // slots_per_kv], st["pts"][i]) for i in range(n_slots)]

        def finish(st=st, j=j):
            heads_t = []
            for i in range(n_slots):
                oe = st["oes"][i]
                ot = oe[0:dh, :] / (oe[dh:dh + 1, :] + jnp.exp(sinks[i] - st["ms"][i]))
                heads_t += [ot[:, 0:blk], ot[:, blk:2 * blk]]
            write(j, jnp.concatenate(heads_t, axis=0).T.astype(BF16))

        items += [scores, softmax, values, finish]
    return items


def _attn_ffn_body(sink_ref, q_ref, kp_ref, ko_ref, vp_ref, vo_ref, bias_ref, wa_ref,
                   x_ref, g_ref, wgu_ref, wd_ref, o_ref, act_ref, att_ref, *, d_ff, tf, dh, tiles_per_seq):
    i = pl.program_id(0)

    @pl.when(i == 0)
    def _():
        att_ref[...] = jnp.zeros_like(att_ref)

    cur = lax.rem(i, 2)
    first_of_seq = lax.rem(jnp.minimum(i, pl.num_programs(0) - 2), tiles_per_seq) == 0

    def write(j, tile):
        att_ref[cur, j * WINDOW:(j + 1) * WINDOW, :] = tile

    side = _attn_tile_items(sink_ref, q_ref, kp_ref, ko_ref, vp_ref, vo_ref, bias_ref,
                            first_of_seq, write, dh)
    get_x = lambda: x_ref[...] + _dot(att_ref[1 - cur], wa_ref[...])
    _interleave(_ffn_items(get_x, g_ref, wgu_ref, wd_ref, o_ref, act_ref, d_ff, tf), side,
                side_before_last=True)


def _attn_ffn(x, q, k, vt, bias, sinks, wa, g, wgu, wd, dh):
    t, d = x.shape
    b, vd, s = vt.shape
    d_ff = wd.shape[0]
    qd, kd = q.shape[1], k.shape[1]
    blk = WINDOW
    tm = _row_tile(s, ROWS_FUSED)
    nt, tps, bpt = t // tm, s // tm, tm // blk
    att_tile = lambda i: jnp.minimum(i, nt - 1)
    ffn_tile = lambda i: jnp.maximum(i - 1, 0)
    ffn_row = lambda n: pl.BlockSpec((tm, n), lambda i: (ffn_tile(i), 0))
    att_row = lambda n: pl.BlockSpec((tm, n), lambda i: (att_tile(i), 0))
    return pl.pallas_call(
        functools.partial(_attn_ffn_body, d_ff=d_ff, tf=FFN_TF, dh=dh, tiles_per_seq=tps),
        grid=(nt + 1,),
        in_specs=[pl.BlockSpec(memory_space=pltpu.SMEM),
                  att_row(qd),
                  pl.BlockSpec((blk, kd), lambda i: (jnp.maximum(att_tile(i) * bpt - 1, 0), 0)),
                  att_row(kd),
                  pl.BlockSpec((1, vd, blk), lambda i: (att_tile(i) // tps,
                                                        0, jnp.maximum(att_tile(i) % tps * bpt - 1, 0))),
                  pl.BlockSpec((1, vd, tm), lambda i: (att_tile(i) // tps, 0, att_tile(i) % tps)),
                  _resident(bias.shape), _resident(wa.shape),
                  ffn_row(d), _resident((1, d)), _resident((d, 2 * d_ff)), _resident((d_ff, d))],
        out_specs=ffn_row(d),
        out_shape=jax.ShapeDtypeStruct((t, d), F32),
        scratch_shapes=[pltpu.VMEM((tm, d_ff), BF16), pltpu.VMEM((2, tm, qd), BF16)],
        compiler_params=_cparams(1),
        name="attn_ffn",
    )(sinks, q, k, k, vt, vt, bias, wa, x, g.reshape(1, d), wgu, wd)


def _attn_sample_body(sink_ref, q_ref, kc_ref, kn_ref, vc_ref, vn_ref, bias_ref,
                      o_ref, ko_ref, vo_ref, *, bt, tq):
    dh = q_ref.shape[-1]
    fresh = lax.broadcasted_iota(jnp.int32, (dh, WINDOW), 1) >= WINDOW - tq
    sinks = [jnp.concatenate([jnp.full((tq, 1), sink_ref[g * A_GROUP + j], F32) for j in range(A_GROUP)],
                             axis=0) for g in range(A_HEADS_KV)]

    def slide(win_ref, new_ref, b, g):
        new = pltpu.roll(new_ref[g], (WINDOW - tq - b * tq) % WINDOW, axis=1)
        return jnp.where(fresh, new, pltpu.roll(win_ref[b, g], WINDOW - tq, axis=1))

    units = [(b, g) for b in range(bt) for g in range(A_HEADS_KV)]
    us = range(len(units))
    q = [q_ref[b, g] for b, g in units]
    k_win = [slide(kc_ref, kn_ref, b, g) for b, g in units]
    v_win = [slide(vc_ref, vn_ref, b, g) for b, g in units]
    for u, (b, g) in enumerate(units):
        ko_ref[b, g] = k_win[u]
        vo_ref[b, g] = v_win[u]
    s_old = [_dot(q[u], kc_ref[b, g].astype(BF16)) + bias_ref[0, g] for u, (b, g) in enumerate(units)]
    s_new = [_dot(q[u], k_win[u].astype(BF16)) + bias_ref[1, g] for u, (b, g) in enumerate(units)]
    m = [jnp.maximum(jnp.maximum(jnp.max(s_old[u], axis=-1, keepdims=True),
                                 jnp.max(s_new[u], axis=-1, keepdims=True)), sinks[g])
         for u, (b, g) in enumerate(units)]
    p_old = [jnp.exp(s_old[u] - m[u]) for u in us]
    p_new = [jnp.exp(s_new[u] - m[u]) for u in us]
    den = [jnp.sum(p_old[u], axis=-1, keepdims=True) + jnp.sum(p_new[u], axis=-1, keepdims=True)
           + jnp.exp(sinks[g] - m[u]) for u, (b, g) in enumerate(units)]
    out = [_dot_nt((p_old[u] / den[u]).astype(BF16), vc_ref[b, g].astype(BF16))
           + _dot_nt((p_new[u] / den[u]).astype(BF16), v_win[u].astype(BF16))
           for u, (b, g) in enumerate(units)]
    for u, (b, g) in enumerate(units):
        o_ref[b, g] = out[u].astype(BF16)


def _attn_sample(q, kn, vn, k_win, v_win, bias, sinks, tq):
    b, n_kv, rows, dh = q.shape
    assert LANES % tq == 0
    bt = LANES // tq
    assert b % bt == 0
    lead = lambda *tail: pl.BlockSpec((bt,) + tail, lambda i: (i, 0, 0, 0))
    new = pl.BlockSpec((n_kv, dh, LANES), lambda i: (0, 0, i))
    return pl.pallas_call(
        functools.partial(_attn_sample_body, bt=bt, tq=tq),
        grid=(b // bt,),
        in_specs=[pl.BlockSpec(memory_space=pltpu.SMEM),
                  lead(n_kv, rows, dh), lead(n_kv, dh, WINDOW), new, lead(n_kv, dh, WINDOW), new,
                  pl.BlockSpec(bias.shape, lambda i: (0, 0, 0, 0))],
        out_specs=[lead(n_kv, rows, dh), lead(n_kv, dh, WINDOW), lead(n_kv, dh, WINDOW)],
        out_shape=[jax.ShapeDtypeStruct(q.shape, BF16),
                   jax.ShapeDtypeStruct(k_win.shape, F32),
                   jax.ShapeDtypeStruct(v_win.shape, F32)],
        compiler_params=_cparams(1),
        name="attn_sample",
    )(sinks, q, k_win, kn, v_win, vn, bias)


def _swa_layer(x3, buffers, p):
    b, s, d = x3.shape
    dh = p["attn_dh"]
    kd = A_HEADS_KV * dh
    x = x3.reshape(b * s, d)
    kv_shape = (b, WINDOW, A_HEADS_KV, dh)
    if buffers is None:
        nq, nk = A_HEADS_Q * dh, A_HEADS_KV * LANES
        q, k, vt = _attn_proj(x3, p["mix_norm1"], p["attn_w_prompt"], p["attn_wvt"],
                              p["attn_qg"], p["attn_kg"], dh, nq, nk)
        bias = _bias_table(p["rel_bias"], WINDOW, key_major=True)
        new_k = k.reshape(b, s, nk)[:, s - WINDOW:].reshape(b, WINDOW, A_HEADS_KV, LANES)[..., :dh]
        new_v = jnp.swapaxes(vt[:, :, s - WINDOW:], 1, 2)
        mixed = dict(kind="attn", args=(q, k, vt, bias, p["attn_sinks"], p["attn_w_out"]), dh=dh)
    else:
        k_buf, v_buf = buffers
        nq, nk = A_HEADS_Q * dh, kd
        q, k, vt = _attn_proj(x.reshape(1, b * s, d), p["mix_norm1"], p["attn_w_sample"], p["attn_wvt"],
                              p["attn_qg"], p["attn_kg"], dh, nq, nk)
        rows = A_GROUP * s
        q = jnp.transpose(q.reshape(b, s, A_HEADS_KV, A_GROUP, dh), (0, 2, 3, 1, 4))
        table = _bias_table(p["rel_bias"], s, key_major=False)[1].reshape(A_HEADS_KV, rows, 2 * WINDOW)
        own = jnp.pad(table[..., WINDOW:WINDOW + s], ((0, 0), (0, 0), (WINDOW - s, 0)),
                      constant_values=-jnp.inf)
        bias = jnp.stack([table[..., :WINDOW], own])
        to_lanes = lambda a: jnp.transpose(a, (0, 2, 3, 1))
        att, new_k, new_v = _attn_sample(q.reshape(b, A_HEADS_KV, rows, dh),
                                         k.T.reshape(A_HEADS_KV, dh, b * s), vt[0].reshape(A_HEADS_KV, dh, b * s),
                                         to_lanes(k_buf), to_lanes(v_buf), bias, p["attn_sinks"], s)
        new_k, new_v = (jnp.transpose(a, (0, 3, 1, 2)) for a in (new_k, new_v))
        att = jnp.transpose(att.reshape(b, A_HEADS_KV, A_GROUP, s, dh), (0, 3, 1, 2, 4))
        mixed = (att.reshape(b * s, nq), p["attn_w_out"])
    return mixed, (new_k.reshape(kv_shape), new_v.reshape(kv_shape))


def _attn_weights(w_in, w_out, q_norm, k_norm, dh):
    d = w_in.shape[0]
    qd, kd = A_HEADS_Q * dh, A_HEADS_KV * dh
    wk = w_in[:, qd:qd + kd].reshape(d, A_HEADS_KV, dh)
    k_dup = jnp.concatenate([wk, wk], axis=-1).reshape(d, A_HEADS_KV * LANES)
    w_prompt = jnp.concatenate([w_in[:, :qd], k_dup], axis=1).astype(BF16)
    qg = jnp.concatenate([q_norm, q_norm]).reshape(1, LANES) * (dh ** -0.5)
    kg = jnp.concatenate([k_norm, k_norm]).reshape(1, LANES)
    return {"attn_w_prompt": w_prompt, "attn_w_sample": w_in[:, :qd + kd].astype(BF16),
            "attn_wvt": w_in[:, qd + kd:].T.astype(BF16),
            "attn_w_out": w_out.astype(BF16),
            "attn_qg": qg, "attn_kg": kg}


def _trunk(x3, mlstm_state, swa_buffers, p, ffn_bf16):
    b, s, d = x3.shape
    stream = mlstm_state is not None

    def ffn(x, name, layer, mixed=None):
        norm = p[name + "_norm"][layer]
        if stream:
            y, *ffn_bf16[name, layer] = _ffn_stream(x, norm, *p[name + "_f32"], layer, pre=mixed)
            return y
        w = (norm, *ffn_bf16[name, layer])
        if mixed is None:
            return _ffn(x, *w)
        if mixed["kind"] == "attn":
            return _attn_ffn(x, *mixed["args"], *w, mixed["dh"])
        return _mlstm_ffn(x, *mixed["args"], *w, mixed["chunk"])

    x = ffn(x3.reshape(b * s, d), "ffn1", 0)
    mixed, new_mlstm = _mlstm_layer(x.reshape(b, s, d), mlstm_state, p)
    x = ffn(x, "ffn2", 0, mixed)
    if new_mlstm is None:
        x, c1, n1, m1 = x
        new_mlstm = (c1, n1, m1.reshape(b, M_HEADS))
    x = ffn(x, "ffn1", 1)
    mixed, new_swa = _swa_layer(x.reshape(b, s, d), swa_buffers, p)
    x = ffn(x, "ffn2", 1, mixed)
    return x.reshape(b, s, d), new_mlstm, new_swa


def kernel(x_prompt, x_sample, state_mlstm_C, state_mlstm_n, state_mlstm_m, cache_swa_k, cache_swa_v,
           ffn1_norm, ffn1_w_gate_up, ffn1_w_down, mix_norm, ffn2_norm, ffn2_w_gate_up, ffn2_w_down,
           mlstm_w_in, mlstm_b_gates, mlstm_out_norm, mlstm_w_out,
           attn_w_in, attn_q_norm, attn_k_norm, attn_sinks, rel_bias, attn_w_out):
    d = x_prompt.shape[-1]
    vd = mlstm_w_out.shape[0]
    n_gates = 2 * M_HEADS
    qk = (mlstm_w_in.shape[1] - 2 * vd - n_gates) // 2
    dh = attn_q_norm.shape[0]
    assert 2 * dh == LANES
    w_gates = jnp.zeros((d, LANES), BF16).at[:, :n_gates].set(mlstm_w_in[:, 2 * qk + 2 * vd:].astype(BF16))
    b_gates = jnp.zeros((1, LANES), F32).at[0, :n_gates].set(mlstm_b_gates)
    p = {
        "ffn1_norm": ffn1_norm, "ffn2_norm": ffn2_norm,
        "ffn1_f32": (ffn1_w_gate_up, ffn1_w_down), "ffn2_f32": (ffn2_w_gate_up, ffn2_w_down),
        "mix_norm0": mix_norm[0], "mix_norm1": mix_norm[1],
        "mlstm_qk": qk, "mlstm_vd": vd,
        "mlstm_w_main": mlstm_w_in[:, :2 * qk + 2 * vd].astype(BF16),
        "mlstm_w_gates": w_gates, "mlstm_b_gates": b_gates,
        "mlstm_out_norm": mlstm_out_norm, "mlstm_w_out": mlstm_w_out.astype(BF16),
        "attn_dh": dh, "attn_sinks": attn_sinks, "rel_bias": rel_bias,
    }
    p.update(_attn_weights(attn_w_in, attn_w_out, attn_q_norm, attn_k_norm, dh))
    ffn_bf16 = {}
    y_s, (c_s, n_s, m_s), (k_s, v_s) = _trunk(
        x_sample, (state_mlstm_C, state_mlstm_n, state_mlstm_m), (cache_swa_k, cache_swa_v), p, ffn_bf16)
    y_p, (c_p, n_p, m_p), (k_p, v_p) = _trunk(x_prompt, None, None, p, ffn_bf16)
    return (y_p, y_s, c_p, n_p, m_p, k_p, v_p, c_s, n_s, m_s, k_s, v_s)
```
